```python
import jax
import jax.numpy as jnp
from jax import lax
import numpy as np

D_MODEL = 1024
BATCH = 16
SEQ = 256
DEPTH = 2
DEC_BATCH = 2
DEC_SEQ = 4096
PAST_LEN = 512

GRID_W = 64
HEAD_DIM = 64
H_GLA = 4
GLA_DK = 64
GLA_DV = 64
GLA_LOWRANK = 16
GLA_TAU = 16.0
GLA_CHUNK = 32
H_NAT = 4
NAT_KH = 8
NAT_KW = 16
H_GQA = 8
KV_GQA = 2
ROPE_THETA = 10000.0
Q_BLOCK = 128
N_EXPERTS = 64
TOP_K = 8
N_GROUPS = 8
TOPK_GROUPS = 4
D_EXPERT = 256
D_SHARED = 256
ROUTE_SCALE = 2.5
MOE_BLOCK = 128
EPS = 1e-6
ALPHA = (2 * DEPTH) ** 0.25
BETA = (8 * DEPTH) ** -0.25
IN_SPLITS = (H_GLA * GLA_DK, H_GLA * GLA_DK, H_GLA * GLA_DV, H_GLA * GLA_DV, 2 * GLA_LOWRANK,
             H_NAT * HEAD_DIM, H_NAT * HEAD_DIM, H_NAT * HEAD_DIM,
             H_GQA * HEAD_DIM, KV_GQA * HEAD_DIM, KV_GQA * HEAD_DIM)
D_IN = sum(IN_SPLITS)
MIX_WIDTH = H_GLA * GLA_DV + H_NAT * HEAD_DIM + H_GQA * HEAD_DIM
F32 = jnp.float32

kernel_name = 'hybrid_gla_natten_gqa_moe_diffusion_step'


def _ln_stats(x):
    xf = x.astype(F32)
    xc = xf - jnp.mean(xf, -1, keepdims=True)
    return xc * lax.rsqrt(jnp.mean(xc * xc, -1, keepdims=True) + EPS)


def layer_norm(x, g, b):
    return (_ln_stats(x) * g + b).astype(x.dtype)


def rms_norm(x, g):
    xf = x.astype(F32)
    return (xf * lax.rsqrt(jnp.mean(xf * xf, -1, keepdims=True) + EPS) * g).astype(x.dtype)


def modulation(cond, w_mod, b_mod):
    m = jax.nn.silu(cond) @ w_mod + b_mod
    return jnp.split(m[:, None, :], 6, axis=-1)


def modulate(x, shift, scale):
    return _ln_stats(x).astype(x.dtype) * (1 + scale) + shift


def split_proj(p):
    cuts = [int(v) for v in np.cumsum(IN_SPLITS)[:-1]]
    return jnp.split(p, cuts, axis=-1)


def axial_rope(T):
    t = jnp.arange(T)
    row = (t // GRID_W).astype(F32)
    col = (t % GRID_W).astype(F32)
    n_freq = HEAD_DIM // 4
    freqs = ROPE_THETA ** (-jnp.arange(n_freq, dtype=F32) / n_freq)
    ang = jnp.concatenate([row[:, None] * freqs, col[:, None] * freqs], -1)
    return jnp.cos(ang), jnp.sin(ang)


def apply_rope(x, cos, sin):
    x1, x2 = jnp.split(x.astype(F32), 2, axis=-1)
    c = cos[None, :, None, :]
    s = sin[None, :, None, :]
    return jnp.concatenate([x1 * c - x2 * s, x1 * s + x2 * c], -1).astype(x.dtype)


def block_attention(q, k, v):
    B, Tq, Hk, G, hd = q.shape
    nb = Tq // Q_BLOCK
    qb = jnp.moveaxis(q.reshape(B, nb, Q_BLOCK, Hk, G, hd), 1, 0)
    scale = hd ** -0.5

    def one(qi):
        s = jnp.einsum('bqhgd,bshd->bhgqs', qi, k).astype(F32) * scale
        p = jax.nn.softmax(s, axis=-1).astype(v.dtype)
        return jnp.einsum('bhgqs,bshd->bqhgd', p, v)

    o = lax.map(one, qb)
    return jnp.moveaxis(o, 0, 1).reshape(B, Tq, Hk, G, hd)


def gla_chunk_scan(q, k, v, log_a, s0):
    B, H, T, DK = q.shape
    C = GLA_CHUNK
    N = T // C
    r = lambda z: z.reshape(B, H, N, C, z.shape[-1])
    q, k, v, log_a = r(q), r(k), r(v), r(log_a)
    b = jnp.cumsum(log_a, axis=3)
    causal = jnp.tril(jnp.ones((C, C), bool))[:, :, None]
    diff = b[:, :, :, :, None, :] - b[:, :, :, None, :, :]
    decay = jnp.exp(jnp.where(causal, diff, -jnp.inf))
    att = jnp.einsum('bhnid,bhnjd,bhnijd->bhnij', q, k, decay)
    o_intra = jnp.einsum('bhnij,bhnje->bhnie', att, v)
    b_last = b[:, :, :, -1:, :]
    q_dec = q * jnp.exp(b)
    k_dec = k * jnp.exp(b_last - b)
    u = jnp.einsum('bhncd,bhnce->bhnde', k_dec, v)
    g = jnp.exp(b_last[:, :, :, 0, :])

    def step(s, inp):
        g_n, u_n = inp
        return g_n[..., None] * s + u_n, s

    s_final, s_prev = lax.scan(step, s0, (jnp.moveaxis(g, 2, 0), jnp.moveaxis(u, 2, 0)))
    s_prev = jnp.moveaxis(s_prev, 0, 2)
    o_inter = jnp.einsum('bhncd,bhnde->bhnce', q_dec, s_prev)
    return (o_intra + o_inter).reshape(B, H, T, -1), s_final


def gla_mixer(gq, gk, gv, gg, ga, w_a2, b_a, norm_g, s0_f, s0_b):
    B, T, _ = gq.shape
    heads = lambda z, d: z.reshape(B, T, H_GLA, d).transpose(0, 2, 1, 3).astype(F32)
    q = heads(gq, GLA_DK) * (GLA_DK ** -0.5)
    k = heads(gk, GLA_DK)
    v = heads(gv, GLA_DV)
    a_f, a_b = jnp.split(ga, 2, axis=-1)
    la_f = heads(jax.nn.log_sigmoid((a_f @ w_a2[0] + b_a[0]).astype(F32)), GLA_DK) / GLA_TAU
    la_b = heads(jax.nn.log_sigmoid((a_b @ w_a2[1] + b_a[1]).astype(F32)), GLA_DK) / GLA_TAU
    flip = lambda z: jnp.flip(z, axis=2)
    o_f, s_f = gla_chunk_scan(q, k, v, la_f, s0_f.astype(F32))
    o_b, s_b = gla_chunk_scan(flip(q), flip(k), flip(v), flip(la_b), s0_b.astype(F32))
    o = (o_f + flip(o_b)).transpose(0, 2, 1, 3)
    o = rms_norm(o, norm_g).reshape(B, T, -1) * jax.nn.silu(gg.astype(F32))
    return o.astype(gq.dtype), s_f, s_b


def nat_context(nq, nk, nv):
    B, T, _ = nq.shape
    q = nq.reshape(B, T, H_NAT, 1, HEAD_DIM)
    k = nk.reshape(B, T, H_NAT, HEAD_DIM)
    v = nv.reshape(B, T, H_NAT, HEAD_DIM)
    return block_attention(q, k, v).reshape(B, T, -1), k, v


def nat_latent(nq, nk, nv, k_ctx, v_ctx, rpb):
    B, T, _ = nq.shape
    rows = T // GRID_W
    kh = min(NAT_KH, rows)
    scale = HEAD_DIM ** -0.5
    r = jnp.arange(rows)
    cidx = jnp.arange(GRID_W)
    row_start = jnp.clip(r - kh // 2, 0, rows - kh)
    col_start = jnp.clip(cidx - NAT_KW // 2, 0, GRID_W - NAT_KW)
    key_rows = row_start[:, None] + jnp.arange(kh)[None, :]
    qg = nq.reshape(B, rows, GRID_W, H_NAT, HEAD_DIM)
    kg = nk.reshape(B, rows, GRID_W, H_NAT, HEAD_DIM)[:, key_rows]
    vg = nv.reshape(B, rows, GRID_W, H_NAT, HEAD_DIM)[:, key_rows]
    s_loc = jnp.einsum('brqhd,brkwhd->bhrqkw', qg, kg).astype(F32) * scale
    dr = key_rows - r[:, None]
    dc = cidx[None, :] - cidx[:, None]
    dc_idx = jnp.clip(dc + NAT_KW - 1, 0, 2 * NAT_KW - 2)
    bias = rpb[:, (dr + NAT_KH - 1)[:, None, :, None], dc_idx[None, :, None, :]]
    col_in = (cidx[None, :] >= col_start[:, None]) & (cidx[None, :] < col_start[:, None] + NAT_KW)
    s_loc = jnp.where(col_in[:, None, :], s_loc + bias[None].astype(F32), -jnp.inf)
    s_ctx = jnp.einsum('brqhd,blhd->bhrql', qg, k_ctx).astype(F32) * scale
    n_loc = kh * GRID_W
    s = jnp.concatenate([s_loc.reshape(B, H_NAT, rows, GRID_W, n_loc), s_ctx], -1)
    p = jax.nn.softmax(s, axis=-1).astype(nv.dtype)
    p_loc = p[..., :n_loc].reshape(B, H_NAT, rows, GRID_W, kh, GRID_W)
    p_ctx = p[..., n_loc:]
    o = jnp.einsum('bhrqkw,brkwhd->brqhd', p_loc, vg) + jnp.einsum('bhrql,blhd->brqhd', p_ctx, v_ctx)
    return o.reshape(B, T, -1)


def gqa_context(aq, ak, av, qn, kn):
    B, T, _ = aq.shape
    q = rms_norm(aq.reshape(B, T, H_GQA, HEAD_DIM), qn)
    k = rms_norm(ak.reshape(B, T, KV_GQA, HEAD_DIM), kn)
    v = av.reshape(B, T, KV_GQA, HEAD_DIM)
    o = block_attention(q.reshape(B, T, KV_GQA, H_GQA // KV_GQA, HEAD_DIM), k, v)
    return o.reshape(B, T, -1), k, v


def gqa_latent(aq, ak, av, k_ctx, v_ctx, qn, kn, cos, sin):
    B, T, _ = aq.shape
    q = apply_rope(rms_norm(aq.reshape(B, T, H_GQA, HEAD_DIM), qn), cos, sin)
    k = apply_rope(rms_norm(ak.reshape(B, T, KV_GQA, HEAD_DIM), kn), cos, sin)
    v = av.reshape(B, T, KV_GQA, HEAD_DIM)
    k_all = jnp.concatenate([k, k_ctx.astype(k.dtype)], axis=1)
    v_all = jnp.concatenate([v, v_ctx.astype(v.dtype)], axis=1)
    o = block_attention(q.reshape(B, T, KV_GQA, H_GQA // KV_GQA, HEAD_DIM), k_all, v_all)
    return o.reshape(B, T, -1)


def moe_ffn(x, w_router, router_bias, w_gu, w_down, ws_gu, ws_down):
    B, T, D = x.shape
    xt = x.reshape(-1, D)
    n = xt.shape[0]
    per = N_EXPERTS // N_GROUPS
    scores = jax.nn.sigmoid((xt @ w_router).astype(F32))
    biased = scores + router_bias.astype(F32)
    grp_score = lax.top_k(biased.reshape(n, N_GROUPS, per), 2)[0].sum(-1)
    top_g = lax.top_k(grp_score, TOPK_GROUPS)[1]
    keep = jnp.any(top_g[:, :, None] == jnp.arange(N_GROUPS), axis=1)
    keep = jnp.repeat(keep, per, axis=1)
    idx = lax.top_k(jnp.where(keep, biased, -jnp.inf), TOP_K)[1]
    w = jnp.take_along_axis(scores, idx, axis=-1)
    w = w / jnp.sum(w, -1, keepdims=True) * ROUTE_SCALE
    gates = jnp.einsum('nk,nke->ne', w, jax.nn.one_hot(idx, N_EXPERTS, dtype=F32)).astype(x.dtype)

    def block(args):
        xb, gb = args
        a, b = jnp.split(jnp.einsum('nd,edf->nef', xb, w_gu), 2, axis=-1)
        hb = jax.nn.silu(a) * b * gb[:, :, None]
        return jnp.einsum('nef,efd->nd', hb, w_down)

    nb = n // MOE_BLOCK
    routed = lax.map(block, (xt.reshape(nb, MOE_BLOCK, D), gates.reshape(nb, MOE_BLOCK, N_EXPERTS))).reshape(n, D)
    sa, sb = jnp.split(xt @ ws_gu, 2, axis=-1)
    shared = (jax.nn.silu(sa) * sb) @ ws_down
    return (routed + shared).reshape(B, T, D)


def setup_inputs(seed: int = 0) -> dict:
    key = jax.random.key(seed)
    ks = iter(jax.random.split(key, 32))
    L = DEPTH
    D = D_MODEL

    def nrm(shape, scale=1.0):
        return jax.random.normal(next(ks), shape, F32) * scale

    return {
        'x_prompt': nrm((BATCH, SEQ, D)),
        'x_sample': nrm((DEC_BATCH, DEC_SEQ, D)),
        'state_gla': nrm((DEC_BATCH, L, 2, H_GLA, GLA_DK, GLA_DV)),
        'cache_nat_k': nrm((DEC_BATCH, L, PAST_LEN, H_NAT, HEAD_DIM)),
        'cache_nat_v': nrm((DEC_BATCH, L, PAST_LEN, H_NAT, HEAD_DIM)),
        'cache_gqa_k': nrm((DEC_BATCH, L, PAST_LEN, KV_GQA, HEAD_DIM)),
        'cache_gqa_v': nrm((DEC_BATCH, L, PAST_LEN, KV_GQA, HEAD_DIM)),
        'c': nrm((DEC_BATCH, D)),
        'c_ctx': nrm((D,)),
        'w_mod': nrm((L, D, 6 * D), D ** -0.5),
        'b_mod': nrm((L, 6 * D), 0.02),
        'w_in': nrm((L, D, D_IN), D ** -0.5),
        'gla_w_a2': nrm((L, 2, GLA_LOWRANK, H_GLA * GLA_DK), GLA_LOWRANK ** -0.5),
        'gla_b_a': nrm((L, 2, H_GLA * GLA_DK), 0.1),
        'gla_norm_g': 1.0 + nrm((L, GLA_DV), 0.02),
        'nat_rpb': nrm((L, H_NAT, 2 * NAT_KH - 1, 2 * NAT_KW - 1), 0.1),
        'gqa_q_norm_g': 1.0 + nrm((L, HEAD_DIM), 0.02),
        'gqa_k_norm_g': 1.0 + nrm((L, HEAD_DIM), 0.02),
        'w_out': nrm((L, MIX_WIDTH, D), MIX_WIDTH ** -0.5 * BETA),
        'ln1_g': 1.0 + nrm((L, D), 0.02),
        'ln1_b': nrm((L, D), 0.02),
        'w_router': nrm((L, D, N_EXPERTS), D ** -0.5),
        'router_bias': nrm((L, N_EXPERTS), 0.01),
        'w_expert_gu': nrm((L, N_EXPERTS, D, 2 * D_EXPERT), D ** -0.5),
        'w_expert_down': nrm((L, N_EXPERTS, D_EXPERT, D), D_EXPERT ** -0.5 * BETA),
        'w_shared_gu': nrm((L, D, 2 * D_SHARED), D ** -0.5),
        'w_shared_down': nrm((L, D_SHARED, D), D_SHARED ** -0.5 * BETA),
        'ln2_g': 1.0 + nrm((L, D), 0.02),
        'ln2_b': nrm((L, D), 0.02),
    }


def reference(x_prompt, x_sample, state_gla, cache_nat_k, cache_nat_v, cache_gqa_k, cache_gqa_v, c, c_ctx,
              w_mod, b_mod, w_in, gla_w_a2, gla_b_a, gla_norm_g, nat_rpb, gqa_q_norm_g, gqa_k_norm_g, w_out,
              ln1_g, ln1_b, w_router, router_bias, w_expert_gu, w_expert_down, w_shared_gu, w_shared_down,
              ln2_g, ln2_b):
    x = x_prompt
    Bp = x.shape[0]
    st_gla, st_nk, st_nv, st_gk, st_gv = [], [], [], [], []
    for l in range(DEPTH):
        sh1, sc1, g1, sh2, sc2, g2 = modulation(c_ctx[None, :], w_mod[l], b_mod[l])
        gq, gk, gv, gg, ga, nq, nk, nv, aq, ak, av = split_proj(modulate(x, sh1, sc1) @ w_in[l])
        zero = jnp.zeros((Bp, H_GLA, GLA_DK, GLA_DV), F32)
        o_gla, s_f, s_b = gla_mixer(gq, gk, gv, gg, ga, gla_w_a2[l], gla_b_a[l], gla_norm_g[l], zero, zero)
        o_nat, k_nat, v_nat = nat_context(nq, nk, nv)
        o_gqa, k_gqa, v_gqa = gqa_context(aq, ak, av, gqa_q_norm_g[l], gqa_k_norm_g[l])
        mix = jnp.concatenate([o_gla, o_nat, o_gqa], axis=-1) @ w_out[l]
        x = layer_norm(ALPHA * x + g1 * mix, ln1_g[l], ln1_b[l])
        ffn = moe_ffn(modulate(x, sh2, sc2), w_router[l], router_bias[l], w_expert_gu[l], w_expert_down[l],
                      w_shared_gu[l], w_shared_down[l])
        x = layer_norm(ALPHA * x + g2 * ffn, ln2_g[l], ln2_b[l])
        st_gla.append(jnp.stack([s_f, s_b], axis=1))
        st_nk.append(k_nat)
        st_nv.append(v_nat)
        st_gk.append(k_gqa)
        st_gv.append(v_gqa)
    y_prompt = x

    x = x_sample
    cos, sin = axial_rope(x.shape[1])
    for l in range(DEPTH):
        sh1, sc1, g1, sh2, sc2, g2 = modulation(c, w_mod[l], b_mod[l])
        gq, gk, gv, gg, ga, nq, nk, nv, aq, ak, av = split_proj(modulate(x, sh1, sc1) @ w_in[l])
        o_gla, _, _ = gla_mixer(gq, gk, gv, gg, ga, gla_w_a2[l], gla_b_a[l], gla_norm_g[l],
                                state_gla[:, l, 0], state_gla[:, l, 1])
        o_nat = nat_latent(nq, nk, nv, cache_nat_k[:, l], cache_nat_v[:, l], nat_rpb[l])
        o_gqa = gqa_latent(aq, ak, av, cache_gqa_k[:, l], cache_gqa_v[:, l], gqa_q_norm_g[l], gqa_k_norm_g[l],
                           cos, sin)
        mix = jnp.concatenate([o_gla, o_nat, o_gqa], axis=-1) @ w_out[l]
        x = layer_norm(ALPHA * x + g1 * mix, ln1_g[l], ln1_b[l])
        ffn = moe_ffn(modulate(x, sh2, sc2), w_router[l], router_bias[l], w_expert_gu[l], w_expert_down[l],
                      w_shared_gu[l], w_shared_down[l])
        x = layer_norm(ALPHA * x + g2 * ffn, ln2_g[l], ln2_b[l])
    y_sample = x

    return (y_prompt, y_sample, jnp.stack(st_gla, axis=1), jnp.stack(st_nk, axis=1), jnp.stack(st_nv, axis=1),
            jnp.stack(st_gk, axis=1), jnp.stack(st_gv, axis=1))
```

```python
import functools

import numpy as np
import jax
import jax.numpy as jnp
from jax import lax
from jax.experimental import pallas as pl
from jax.experimental.pallas import tpu as pltpu

D_MODEL = 1024
BATCH = 16
SEQ = 256
DEPTH = 2
DEC_BATCH = 2
DEC_SEQ = 4096
PAST_LEN = 512
GRID_W = 64
HEAD_DIM = 64
H_GLA = 4
GLA_DK = 64
GLA_DV = 64
GLA_LOWRANK = 16
GLA_TAU = 16.0
H_NAT = 4
NAT_KH = 8
NAT_KW = 16
H_GQA = 8
KV_GQA = 2
ROPE_THETA = 10000.0
N_EXPERTS = 64
TOP_K = 8
N_GROUPS = 8
TOPK_GROUPS = 4
D_EXPERT = 256
D_SHARED = 256
ROUTE_SCALE = 2.5
EPS = 1e-6
ALPHA = (2 * DEPTH) ** 0.25
IN_SPLITS = (H_GLA * GLA_DK, H_GLA * GLA_DK, H_GLA * GLA_DV, H_GLA * GLA_DV, 2 * GLA_LOWRANK,
             H_NAT * HEAD_DIM, H_NAT * HEAD_DIM, H_NAT * HEAD_DIM,
             H_GQA * HEAD_DIM, KV_GQA * HEAD_DIM, KV_GQA * HEAD_DIM)

F32 = jnp.float32
BF16 = jnp.bfloat16

N_CTX = BATCH * SEQ
N_LAT = DEC_BATCH * DEC_SEQ
N_TOK = N_CTX + N_LAT
N_COND = 8
GLA_W = H_GLA * GLA_DK
NAT_W = H_NAT * HEAD_DIM
GQA_QW = H_GQA * HEAD_DIM
GQA_KW = KV_GQA * HEAD_DIM
W_IN_COLS = 4 * GLA_W + 3 * NAT_W + GQA_QW + 2 * GQA_KW + 128
GA_COL = W_IN_COLS - 128
GQA_SLOT_HEADS = (0, 4, 1, 5, 2, 6, 3, 7)
GLA_CHUNK = 32
GLA_TB = 256
ROWS = DEC_SEQ // GRID_W
NAT_ROWS_PER_STEP = 8
VMEM_LIMIT = 56 * 1024 * 1024

TM_PROJ = 256
TM_MOE = 1024
TQ_GQA = 256


def _dot(a, b):
    return jnp.dot(a.astype(BF16), b.astype(BF16), preferred_element_type=F32)


def _dot_nt(a, b):
    return lax.dot_general(a.astype(BF16), b.astype(BF16), (((1,), (1,)), ((), ())),
                           preferred_element_type=F32)


def _dot_tn(a, b):
    return lax.dot_general(a.astype(BF16), b.astype(BF16), (((0,), (0,)), ((), ())),
                           preferred_element_type=F32)


def _dot_split(a, b_bf16):
    hi = a.astype(BF16)
    lo = (a - hi.astype(F32)).astype(BF16)
    return (jnp.dot(hi, b_bf16, preferred_element_type=F32)
            + jnp.dot(lo, b_bf16, preferred_element_type=F32))


def _sigmoid(x):
    return 1.0 / (1.0 + jnp.exp(-x))


def _silu(x):
    return x * _sigmoid(x)


def _ln(x):
    xc = x - jnp.mean(x, axis=-1, keepdims=True)
    return xc * lax.rsqrt(jnp.mean(xc * xc, axis=-1, keepdims=True) + EPS)


def _lane_group(shape, axis, width):
    return lax.shift_right_logical(lax.broadcasted_iota(jnp.int32, shape, axis), int(np.log2(width)))


def _head_mean_sq(x):
    w = x.shape[-1]
    bmat = jnp.where(_lane_group((w, w), 0, HEAD_DIM) == _lane_group((w, w), 1, HEAD_DIM),
                     1.0 / HEAD_DIM, 0.0).astype(BF16)
    return _dot_split(x * x, bmat)


def _swap_half_heads(x):
    w = x.shape[-1]
    lane = lax.broadcasted_iota(jnp.int32, x.shape, x.ndim - 1)
    first = (lane & (HEAD_DIM - 1)) < HEAD_DIM // 2
    return jnp.where(first, pltpu.roll(x, w - HEAD_DIM // 2, x.ndim - 1),
                     pltpu.roll(x, HEAD_DIM // 2, x.ndim - 1))


def _cond_row(tile, tm):
    r = tile * tm
    return jnp.where(r < N_CTX, 0, 1 + (r - N_CTX) // DEC_SEQ)


def _stack_heads(x, n, width):
    head = _lane_group(x.shape, 1, width)
    return jnp.concatenate([jnp.where(head == h, x, jnp.zeros_like(x)) for h in range(n)], axis=0)


def _unstack_heads(o, n, width):
    m = o.shape[0] // n
    head = _lane_group((m, o.shape[1]), 1, width)
    out = jnp.zeros((m, o.shape[1]), o.dtype)
    for h in range(n):
        out = jnp.where(head == h, o[h * m:(h + 1) * m], out)
    return out


def _mod_kernel(c_ref, w_ref, b_ref, o_ref):
    o_ref[0] = _dot(_silu(c_ref[...]), w_ref[0]) + b_ref[0]


def _modulation(cond, w_mod, b_mod):
    tn = 1536
    return pl.pallas_call(
        _mod_kernel,
        grid=(DEPTH, 6 * D_MODEL // tn),
        in_specs=[pl.BlockSpec((N_COND, D_MODEL), lambda l, j: (0, 0)),
                  pl.BlockSpec((1, D_MODEL, tn), lambda l, j: (l, 0, j)),
                  pl.BlockSpec((1, 1, tn), lambda l, j: (l, 0, j))],
        out_specs=pl.BlockSpec((1, N_COND, tn), lambda l, j: (l, 0, j)),
        out_shape=jax.ShapeDtypeStruct((DEPTH, N_COND, 6 * D_MODEL), F32),
        compiler_params=pltpu.CompilerParams(dimension_semantics=("arbitrary", "arbitrary"),
                                             vmem_limit_bytes=VMEM_LIMIT),
        name="modulation",
    )(cond, w_mod, b_mod.reshape(DEPTH, 1, 6 * D_MODEL))


def _inproj_kernel(x_ref, sh_ref, sc_ref, w_ref, wa_ref, ba_ref, qg_ref, kg_ref, cos_ref, sin_ref,
                   pgla_ref, la_ref, pnat_ref, q_ref, k_ref, v_ref):
    ci = _cond_row(pl.program_id(0), TM_PROJ)
    xn = _ln(x_ref[...])
    xm = xn * (1.0 + sc_ref[pl.ds(ci, 1), :]) + sh_ref[pl.ds(ci, 1), :]
    acc = _dot(xm, w_ref[...])
    pgla_ref[...] = acc[:, 0:4 * GLA_W]
    pnat_ref[...] = acc[:, 4 * GLA_W:4 * GLA_W + 3 * NAT_W]
    z = _dot(acc[:, GA_COL:GA_COL + 128], wa_ref[...]) + ba_ref[...]
    la_ref[...] = (jnp.minimum(z, 0.0) - jnp.log(1.0 + jnp.exp(-jnp.abs(z)))) * (1.0 / GLA_TAU)

    c0 = 4 * GLA_W + 3 * NAT_W
    cos = cos_ref[...]
    sin = sin_ref[...]

    def norm_rope(a, g):
        an = a * lax.rsqrt(_head_mean_sq(a) + EPS) * g
        return an * cos + _swap_half_heads(an) * sin

    qs = [norm_rope(acc[:, c0 + j * 128:c0 + (j + 1) * 128], qg_ref[...]) for j in range(GQA_QW // 128)]
    q_ref[...] = (jnp.concatenate(qs, axis=1) * (HEAD_DIM ** -0.5)).astype(BF16)
    k_ref[...] = norm_rope(acc[:, c0 + GQA_QW:c0 + GQA_QW + GQA_KW], kg_ref[...])
    v_ref[...] = acc[:, c0 + GQA_QW + GQA_KW:c0 + GQA_QW + 2 * GQA_KW]


def _in_proj(x, sh, sc, w, wa, ba, qg, kg, cos_t, sin_t):
    tm = TM_PROJ
    full = lambda a: pl.BlockSpec(a.shape, lambda i: (0,) * a.ndim)
    rows = lambda w_: pl.BlockSpec((tm, w_), lambda i: (i, 0))
    return pl.pallas_call(
        _inproj_kernel,
        grid=(N_TOK // tm,),
        in_specs=[rows(D_MODEL), full(sh), full(sc), full(w), full(wa), full(ba), full(qg), full(kg),
                  rows(128), rows(128)],
        out_specs=[rows(4 * GLA_W), rows(2 * GLA_W), rows(3 * NAT_W), rows(GQA_QW), rows(GQA_KW),
                   rows(GQA_KW)],
        out_shape=[jax.ShapeDtypeStruct((N_TOK, 4 * GLA_W), F32),
                   jax.ShapeDtypeStruct((N_TOK, 2 * GLA_W), F32),
                   jax.ShapeDtypeStruct((N_TOK, 3 * NAT_W), F32),
                   jax.ShapeDtypeStruct((N_TOK, GQA_QW), BF16),
                   jax.ShapeDtypeStruct((N_TOK, GQA_KW), F32),
                   jax.ShapeDtypeStruct((N_TOK, GQA_KW), F32)],
        compiler_params=pltpu.CompilerParams(dimension_semantics=("arbitrary",),
                                             vmem_limit_bytes=VMEM_LIMIT),
        name="in_proj",
    )(x, sh, sc, w, wa, ba, qg, kg, cos_t, sin_t)


def _gla_kernel(qf_ref, kf_ref, vf_ref, laf_ref, qb_ref, kb_ref, vb_ref, lab_ref, st0_ref,
                of_ref, ob_ref, stout_ref, st_ref):
    t = pl.program_id(1)
    last_t = pl.num_programs(1) - 1
    C = GLA_CHUNK
    nchunk = GLA_TB // C

    @pl.when(t == 0)
    def _():
        st_ref[...] = st0_ref[0]

    ri = lax.broadcasted_iota(jnp.int32, (C, C), 0)
    cj = lax.broadcasted_iota(jnp.int32, (C, C), 1)
    rj = lax.broadcasted_iota(jnp.int32, (C, H_GLA * C), 0)
    cc = lax.broadcasted_iota(jnp.int32, (C, H_GLA * C), 1) & (C - 1)
    blk = _lane_group((H_GLA * C, GLA_W), 0, C) == _lane_group((H_GLA * C, GLA_W), 1, GLA_DK)
    diag = _lane_group((GLA_W, GLA_W), 0, GLA_DV) == _lane_group((GLA_W, GLA_W), 1, GLA_DK)

    def chunk(q_ref, k_ref, v_ref, la_ref, o_ref, d, c, causal):
        rows = pl.ds(pl.multiple_of(c * C, C), C)
        q = q_ref[rows, :] * (GLA_DK ** -0.5)
        k = k_ref[rows, :]
        v = v_ref[rows, :]
        la = la_ref[rows, :]
        tri = jnp.where((ri >= cj) if causal else (ri <= cj), 1.0, 0.0).astype(BF16)
        la_hi = la.astype(BF16)
        la_lo = (la - la_hi.astype(F32)).astype(BF16)
        b = (jnp.dot(tri, la_hi, preferred_element_type=F32)
             + jnp.dot(tri, la_lo, preferred_element_type=F32))
        btot = jnp.sum(la, axis=0, keepdims=True)
        bref = b[C // 2:C // 2 + 1, :]
        q_att = q * jnp.exp(b - bref)
        k_att = k * jnp.exp(bref - b)
        kblk = jnp.where(blk, jnp.concatenate([k_att] * H_GLA, axis=0), 0.0)
        att = _dot_nt(q_att, kblk)
        att = jnp.where((rj >= cc) if causal else (rj <= cc), att, 0.0)
        vblk = jnp.where(blk, jnp.concatenate([v] * H_GLA, axis=0), 0.0)
        st = st_ref[d]
        o_ref[rows, :] = _dot(att, vblk) + _dot_nt(q * jnp.exp(b), st)
        u = _dot_tn(v, k * jnp.exp(btot - b))
        st_ref[d] = st * jnp.exp(btot) + jnp.where(diag, u, 0.0)

    def body(c, carry):
        chunk(qf_ref, kf_ref, vf_ref, laf_ref, of_ref, 0, c, True)
        chunk(qb_ref, kb_ref, vb_ref, lab_ref, ob_ref, 1, nchunk - 1 - c, False)
        return carry

    lax.fori_loop(0, nchunk, body, 0)

    @pl.when(t == last_t)
    def _():
        stout_ref[0] = st_ref[...]


def _gla(pgla, la, st0, row0, n_seq, seq_len):
    tb = GLA_TB
    nt = seq_len // tb
    b0 = row0 // tb
    fwd = lambda col: pl.BlockSpec((tb, GLA_W), lambda s, t: (b0 + s * nt + t, col))
    bwd = lambda col: pl.BlockSpec((tb, GLA_W), lambda s, t: (b0 + s * nt + nt - 1 - t, col))
    st_spec = pl.BlockSpec((1, 2, GLA_W, GLA_W), lambda s, t: (s, 0, 0, 0))
    n = n_seq * seq_len
    return pl.pallas_call(
        _gla_kernel,
        grid=(n_seq, nt),
        in_specs=[fwd(0), fwd(1), fwd(2), fwd(0), bwd(0), bwd(1), bwd(2), bwd(1), st_spec],
        out_specs=[pl.BlockSpec((tb, GLA_W), lambda s, t: (s * nt + t, 0)),
                   pl.BlockSpec((tb, GLA_W), lambda s, t: (s * nt + nt - 1 - t, 0)),
                   st_spec],
        out_shape=[jax.ShapeDtypeStruct((n, GLA_W), F32), jax.ShapeDtypeStruct((n, GLA_W), F32),
                   jax.ShapeDtypeStruct((n_seq, 2, GLA_W, GLA_W), F32)],
        scratch_shapes=[pltpu.VMEM((2, GLA_W, GLA_W), F32)],
        compiler_params=pltpu.CompilerParams(dimension_semantics=("arbitrary", "arbitrary"),
                                             vmem_limit_bytes=VMEM_LIMIT),
        name="gla_scan",
    )(pgla, pgla, pgla, la, pgla, pgla, pgla, la, st0)


def _softmax_pv(s_list, v_list):
    m = s_list[0].max(axis=-1, keepdims=True)
    for s in s_list[1:]:
        m = jnp.maximum(m, s.max(axis=-1, keepdims=True))
    acc = None
    l = None
    for s, v in zip(s_list, v_list):
        p = jnp.exp(s - m)
        pl_ = p.sum(axis=-1, keepdims=True)
        pv = _dot(p, v)
        acc = pv if acc is None else acc + pv
        l = pl_ if l is None else l + pl_
    return acc * (1.0 / l)


def _ctx_attn_kernel(pnat_ref, q_ref, k_ref, v_ref, onat_ref, ogqa_ref):
    nq = pnat_ref[:, 0:NAT_W] * (HEAD_DIM ** -0.5)
    nk = pnat_ref[:, NAT_W:2 * NAT_W].astype(BF16)
    nv = pnat_ref[:, 2 * NAT_W:3 * NAT_W].astype(BF16)
    o = _softmax_pv([_dot_nt(_stack_heads(nq, H_NAT, HEAD_DIM), nk)], [nv])
    onat_ref[...] = _unstack_heads(o, H_NAT, HEAD_DIM).astype(BF16)

    k = k_ref[...].astype(BF16)
    v = v_ref[...].astype(BF16)
    k2 = jnp.concatenate([k, k], axis=1)
    v2 = jnp.concatenate([v, v], axis=1)
    outs = []
    for half in range(2):
        q = q_ref[:, half * 256:(half + 1) * 256]
        o = _softmax_pv([_dot_nt(_stack_heads(q, 4, HEAD_DIM), k2)], [v2])
        outs.append(_unstack_heads(o, 4, HEAD_DIM))
    ogqa_ref[...] = jnp.concatenate(outs, axis=1).astype(BF16)


def _ctx_attention(pnat, q, k, v):
    rows = lambda w_: pl.BlockSpec((SEQ, w_), lambda i: (i, 0))
    return pl.pallas_call(
        _ctx_attn_kernel,
        grid=(BATCH,),
        in_specs=[rows(3 * NAT_W), rows(GQA_QW), rows(GQA_KW), rows(GQA_KW)],
        out_specs=[rows(NAT_W), rows(GQA_QW)],
        out_shape=[jax.ShapeDtypeStruct((N_CTX, NAT_W), BF16),
                   jax.ShapeDtypeStruct((N_CTX, GQA_QW), BF16)],
        compiler_params=pltpu.CompilerParams(dimension_semantics=("arbitrary",),
                                             vmem_limit_bytes=VMEM_LIMIT),
        name="ctx_attention",
    )(pnat, q, k, v)


def _gqa_lat_kernel(q_ref, k_ref, v_ref, o_ref):
    k = k_ref[0]
    v = v_ref[0]
    k2 = jnp.concatenate([k, k], axis=1)
    v2 = jnp.concatenate([v, v], axis=1)
    slot = _lane_group((TQ_GQA, 256), 1, HEAD_DIM)
    outs = []
    for half in range(2):
        q = q_ref[:, half * 256:(half + 1) * 256]
        out = jnp.zeros((TQ_GQA, 256), F32)
        for s in range(4):
            qs = jnp.where(slot == s, q, jnp.zeros_like(q))
            o = _softmax_pv([_dot_nt(qs, k2)], [v2])
            out = jnp.where(slot == s, o, out)
        outs.append(out)
    o_ref[...] = jnp.concatenate(outs, axis=1).astype(BF16)


def _gqa_latent(q, k_all, v_all):
    tq = TQ_GQA
    nq = DEC_SEQ // tq
    tk = k_all.shape[1]
    return pl.pallas_call(
        _gqa_lat_kernel,
        grid=(DEC_BATCH, nq),
        in_specs=[pl.BlockSpec((tq, GQA_QW), lambda b, i: (N_CTX // tq + b * nq + i, 0)),
                  pl.BlockSpec((1, tk, GQA_KW), lambda b, i: (b, 0, 0)),
                  pl.BlockSpec((1, tk, GQA_KW), lambda b, i: (b, 0, 0))],
        out_specs=pl.BlockSpec((tq, GQA_QW), lambda b, i: (b * nq + i, 0)),
        out_shape=jax.ShapeDtypeStruct((N_LAT, GQA_QW), BF16),
        compiler_params=pltpu.CompilerParams(dimension_semantics=("arbitrary", "arbitrary"),
                                             vmem_limit_bytes=VMEM_LIMIT),
        name="gqa_latent",
    )(q, k_all, v_all)


def _nat_lat_kernel(q_ref, k_ref, v_ref, kc_ref, vc_ref, bias_ref, o_ref):
    j = pl.program_id(1)
    kc = kc_ref[0].astype(BF16)
    vc = vc_ref[0].astype(BF16)
    for i in range(NAT_ROWS_PER_STEP):
        r = j * NAT_ROWS_PER_STEP + i
        r0 = jnp.clip(r - NAT_KH // 2, 0, ROWS - NAT_KH)
        win = pl.ds(pl.multiple_of(r0 * GRID_W, GRID_W), NAT_KH * GRID_W)
        q = q_ref[i * GRID_W:(i + 1) * GRID_W, :] * (HEAD_DIM ** -0.5)
        qs = _stack_heads(q, H_NAT, HEAD_DIM)
        s_loc = _dot_nt(qs, k_ref[win, :]) + bias_ref[r - r0]
        s_ctx = _dot_nt(qs, kc)
        o = _softmax_pv([s_loc, s_ctx], [v_ref[win, :], vc])
        o_ref[i * GRID_W:(i + 1) * GRID_W, :] = _unstack_heads(o, H_NAT, HEAD_DIM).astype(BF16)


def _nat_latent(pnat, kc, vc, bias):
    tq = NAT_ROWS_PER_STEP * GRID_W
    nq = DEC_SEQ // tq
    lat_blk = N_CTX // DEC_SEQ
    return pl.pallas_call(
        _nat_lat_kernel,
        grid=(DEC_BATCH, nq),
        in_specs=[pl.BlockSpec((tq, NAT_W), lambda b, j: (N_CTX // tq + b * nq + j, 0)),
                  pl.BlockSpec((DEC_SEQ, NAT_W), lambda b, j: (lat_blk + b, 1)),
                  pl.BlockSpec((DEC_SEQ, NAT_W), lambda b, j: (lat_blk + b, 2)),
                  pl.BlockSpec((1, PAST_LEN, NAT_W), lambda b, j: (b, 0, 0)),
                  pl.BlockSpec((1, PAST_LEN, NAT_W), lambda b, j: (b, 0, 0)),
                  pl.BlockSpec(bias.shape, lambda b, j: (0, 0, 0))],
        out_specs=pl.BlockSpec((tq, NAT_W), lambda b, j: (b * nq + j, 0)),
        out_shape=jax.ShapeDtypeStruct((N_LAT, NAT_W), BF16),
        compiler_params=pltpu.CompilerParams(dimension_semantics=("arbitrary", "arbitrary"),
                                             vmem_limit_bytes=VMEM_LIMIT),
        name="nat_latent",
    )(pnat, pnat, pnat, kc, vc, bias)


def _outproj_kernel(x_ref, of_ref, ob_ref, gg_ref, onat_ref, ogqa_ref, w_ref, ng_ref, g1_ref, lng_ref,
                    lnb_ref, sh2_ref, sc2_ref, wr_ref, x1_ref, xm_ref, lg_ref):
    ci = _cond_row(pl.program_id(0), TM_PROJ)
    og = of_ref[...] + ob_ref[...]
    halves = [og[:, j * 128:(j + 1) * 128] for j in range(GLA_W // 128)]
    ms = jnp.concatenate([_head_mean_sq(h) for h in halves], axis=1)
    ogla = og * lax.rsqrt(ms + EPS) * ng_ref[...] * _silu(gg_ref[...])
    mix = (_dot(ogla, w_ref[0:GLA_W, :]) + _dot(onat_ref[...], w_ref[GLA_W:GLA_W + NAT_W, :])
           + _dot(ogqa_ref[...], w_ref[GLA_W + NAT_W:, :]))
    x1 = _ln(ALPHA * x_ref[...] + g1_ref[pl.ds(ci, 1), :] * mix) * lng_ref[...] + lnb_ref[...]
    x1_ref[...] = x1
    xm = (_ln(x1) * (1.0 + sc2_ref[pl.ds(ci, 1), :]) + sh2_ref[pl.ds(ci, 1), :]).astype(BF16)
    xm_ref[...] = xm
    lg_ref[...] = jnp.dot(xm, wr_ref[...], preferred_element_type=F32)


def _out_proj(x, of, ob, pgla, onat, ogqa, w, ng, g1, lng, lnb, sh2, sc2, wr):
    tm = TM_PROJ
    full = lambda a: pl.BlockSpec(a.shape, lambda i: (0,) * a.ndim)
    rows = lambda w_: pl.BlockSpec((tm, w_), lambda i: (i, 0))
    return pl.pallas_call(
        _outproj_kernel,
        grid=(N_TOK // tm,),
        in_specs=[rows(D_MODEL), rows(GLA_W), rows(GLA_W), pl.BlockSpec((tm, GLA_W), lambda i: (i, 3)),
                  rows(NAT_W), rows(GQA_QW), full(w), full(ng), full(g1), full(lng), full(lnb),
                  full(sh2), full(sc2), full(wr)],
        out_specs=[rows(D_MODEL), rows(D_MODEL), rows(128)],
        out_shape=[jax.ShapeDtypeStruct((N_TOK, D_MODEL), F32),
                   jax.ShapeDtypeStruct((N_TOK, D_MODEL), BF16),
                   jax.ShapeDtypeStruct((N_TOK, 128), F32)],
        compiler_params=pltpu.CompilerParams(dimension_semantics=("arbitrary",),
                                             vmem_limit_bytes=VMEM_LIMIT),
        name="out_proj",
    )(x, of, ob, pgla, onat, ogqa, w, ng, g1, lng, lnb, sh2, sc2, wr)


def _moe_kernel(xm_ref, gates_ref, x1_ref, g2_ref, wgu_ref, wd_ref, wsgu_ref, wsd_ref, lng_ref, lnb_ref,
                out_ref, acc_ref):
    e = pl.program_id(1)
    ci = _cond_row(pl.program_id(0), TM_MOE)
    xm = xm_ref[...]

    def expert(ab, gate):
        h = _silu(ab[:, :D_EXPERT]) * ab[:, D_EXPERT:]
        return h if gate is None else h * gate

    @pl.when(e == 0)
    def _():
        acc_ref[...] = _dot(expert(jnp.dot(xm, wsgu_ref[...], preferred_element_type=F32), None),
                            wsd_ref[...])

    lane = lax.broadcasted_iota(jnp.int32, gates_ref.shape, 1)
    gate = jnp.sum(jnp.where(lane == e, gates_ref[...], 0.0), axis=1, keepdims=True)
    ab = jnp.dot(xm, wgu_ref[0].astype(BF16), preferred_element_type=F32)
    acc_ref[...] += _dot(expert(ab, gate), wd_ref[0])

    @pl.when(e == N_EXPERTS - 1)
    def _():
        y = ALPHA * x1_ref[...] + g2_ref[pl.ds(ci, 1), :] * acc_ref[...]
        out_ref[...] = _ln(y) * lng_ref[...] + lnb_ref[...]


def _moe(xm, gates, x1, g2, wgu, wd, wsgu, wsd, lng, lnb):
    tm = TM_MOE
    full = lambda a: pl.BlockSpec(a.shape, lambda i, e: (0,) * a.ndim)
    rows = lambda w_: pl.BlockSpec((tm, w_), lambda i, e: (i, 0))
    return pl.pallas_call(
        _moe_kernel,
        grid=(N_TOK // tm, N_EXPERTS),
        in_specs=[rows(D_MODEL), rows(N_EXPERTS), rows(D_MODEL), full(g2),
                  pl.BlockSpec((1, D_MODEL, 2 * D_EXPERT), lambda i, e: (e, 0, 0)),
                  pl.BlockSpec((1, D_EXPERT, D_MODEL), lambda i, e: (e, 0, 0)),
                  full(wsgu), full(wsd), full(lng), full(lnb)],
        out_specs=rows(D_MODEL),
        out_shape=jax.ShapeDtypeStruct((N_TOK, D_MODEL), F32),
        scratch_shapes=[pltpu.VMEM((tm, D_MODEL), F32)],
        compiler_params=pltpu.CompilerParams(dimension_semantics=("arbitrary", "arbitrary"),
                                             vmem_limit_bytes=VMEM_LIMIT),
        name="moe",
    )(xm, gates, x1, g2, wgu, wd, wsgu, wsd, lng, lnb)


def _route(logits, router_bias):
    n = logits.shape[0]
    per = N_EXPERTS // N_GROUPS
    scores = jax.nn.sigmoid(logits[:, :N_EXPERTS])
    biased = scores + router_bias
    grp_score = lax.top_k(biased.reshape(n, N_GROUPS, per), 2)[0].sum(-1)
    top_g = lax.top_k(grp_score, TOPK_GROUPS)[1]
    keep = jnp.any(top_g[:, :, None] == jnp.arange(N_GROUPS), axis=1)
    keep = jnp.repeat(keep, per, axis=1)
    idx = lax.top_k(jnp.where(keep, biased, -jnp.inf), TOP_K)[1]
    w = jnp.take_along_axis(scores, idx, axis=-1)
    w = w / jnp.sum(w, -1, keepdims=True) * ROUTE_SCALE
    return jnp.einsum('nk,nke->ne', w, jax.nn.one_hot(idx, N_EXPERTS, dtype=F32))


def _rope_tables():
    t = jnp.arange(DEC_SEQ)
    n_freq = HEAD_DIM // 4
    freqs = ROPE_THETA ** (-jnp.arange(n_freq, dtype=F32) / n_freq)
    ang = jnp.concatenate([(t // GRID_W).astype(F32)[:, None] * freqs,
                           (t % GRID_W).astype(F32)[:, None] * freqs], -1)
    cos, sin = jnp.cos(ang), jnp.sin(ang)
    cos_h = jnp.concatenate([cos, cos], -1)
    sin_h = jnp.concatenate([-sin, sin], -1)
    lat = lambda a: jnp.tile(a, (DEC_BATCH, 128 // HEAD_DIM))
    cos_t = jnp.concatenate([jnp.ones((N_CTX, 128), F32), lat(cos_h)], 0)
    sin_t = jnp.concatenate([jnp.zeros((N_CTX, 128), F32), lat(sin_h)], 0)
    return cos_t, sin_t


def _nat_bias_table(rpb):
    pat = np.arange(NAT_KH)
    dr_idx = np.arange(NAT_KH)[None, :] - pat[:, None] + NAT_KH - 1
    cidx = np.arange(GRID_W)
    dc_idx = np.clip(cidx[None, :] - cidx[:, None] + NAT_KW - 1, 0, 2 * NAT_KW - 2)
    col_start = np.clip(cidx - NAT_KW // 2, 0, GRID_W - NAT_KW)
    col_in = (cidx[None, :] >= col_start[:, None]) & (cidx[None, :] < col_start[:, None] + NAT_KW)
    bias = rpb[:, dr_idx[:, None, :, None], dc_idx[None, :, None, :]]
    bias = jnp.where(col_in[None, None, :, None, :], bias, -jnp.inf)
    return bias.transpose(1, 0, 2, 3, 4).reshape(NAT_KH, H_NAT * GRID_W, NAT_KH * GRID_W)


def _prep_w_in(w_in):
    cuts = np.cumsum((0,) + IN_SPLITS)
    seg = [w_in[:, cuts[i]:cuts[i + 1]] for i in range(len(IN_SPLITS))]
    gq, gk, gv, gg, ga, nq, nk, nv, aq, ak, av = seg
    aq = aq.reshape(D_MODEL, H_GQA, HEAD_DIM)[:, np.array(GQA_SLOT_HEADS), :].reshape(D_MODEL, GQA_QW)
    ga = jnp.pad(ga, ((0, 0), (0, 128 - 2 * GLA_LOWRANK)))
    return jnp.concatenate([gq, gk, gv, gg, nq, nk, nv, aq, ak, av, ga], axis=1).astype(BF16)


def _prep_w_a(w_a2, b_a):
    wa = jnp.zeros((128, 2 * GLA_W), F32)
    wa = wa.at[0:GLA_LOWRANK, 0:GLA_W].set(w_a2[0])
    wa = wa.at[GLA_LOWRANK:2 * GLA_LOWRANK, GLA_W:].set(w_a2[1])
    return wa.astype(BF16), b_a.reshape(1, 2 * GLA_W)


def _prep_w_out(w_out):
    gqa = w_out[GLA_W + NAT_W:].reshape(H_GQA, HEAD_DIM, D_MODEL)[np.array(GQA_SLOT_HEADS)]
    return jnp.concatenate([w_out[:GLA_W + NAT_W], gqa.reshape(GQA_QW, D_MODEL)], 0).astype(BF16)


def _state_to_blockdiag(s):
    out = jnp.zeros(s.shape[:-3] + (GLA_W, GLA_W), F32)
    for h in range(H_GLA):
        out = out.at[..., h * GLA_DV:(h + 1) * GLA_DV, h * GLA_DK:(h + 1) * GLA_DK].set(
            jnp.swapaxes(s[..., h, :, :], -1, -2))
    return out


def _blockdiag_to_state(st):
    blocks = [st[..., h * GLA_DV:(h + 1) * GLA_DV, h * GLA_DK:(h + 1) * GLA_DK] for h in range(H_GLA)]
    return jnp.swapaxes(jnp.stack(blocks, axis=-3), -1, -2)


def kernel(x_prompt, x_sample, state_gla, cache_nat_k, cache_nat_v, cache_gqa_k, cache_gqa_v, c, c_ctx, w_mod, b_mod, w_in, gla_w_a2, gla_b_a, gla_norm_g, nat_rpb, gqa_q_norm_g, gqa_k_norm_g, w_out, ln1_g, ln1_b, w_router, router_bias, w_expert_gu, w_expert_down, w_shared_gu, w_shared_down, ln2_g, ln2_b):
    x = jnp.concatenate([x_prompt.reshape(N_CTX, D_MODEL), x_sample.reshape(N_LAT, D_MODEL)], axis=0)
    cond = jnp.concatenate([c_ctx[None, :], c, jnp.zeros((N_COND - 1 - DEC_BATCH, D_MODEL), F32)], axis=0)
    mods = _modulation(cond, w_mod, b_mod)
    cos_t, sin_t = _rope_tables()
    row = lambda a: a.reshape(1, -1)

    st_gla, st_nk, st_nv, st_gk, st_gv = [], [], [], [], []
    for l in range(DEPTH):
        sh1, sc1, g1, sh2, sc2, g2 = [mods[l, :, j * D_MODEL:(j + 1) * D_MODEL] for j in range(6)]
        wa, ba = _prep_w_a(gla_w_a2[l], gla_b_a[l])
        qg = row(jnp.tile(gqa_q_norm_g[l], 128 // HEAD_DIM))
        kg = row(jnp.tile(gqa_k_norm_g[l], 128 // HEAD_DIM))
        pgla, la, pnat, q, k, v = _in_proj(x, sh1, sc1, _prep_w_in(w_in[l]), wa, ba, qg, kg, cos_t, sin_t)

        zero_st = jnp.zeros((BATCH, 2, GLA_W, GLA_W), F32)
        of_c, ob_c, st_c = _gla(pgla, la, zero_st, 0, BATCH, SEQ)
        of_l, ob_l, _ = _gla(pgla, la, _state_to_blockdiag(state_gla[:, l]), N_CTX, DEC_BATCH, DEC_SEQ)

        onat_c, ogqa_c = _ctx_attention(pnat, q, k, v)
        k_all = jnp.concatenate([k[N_CTX:].reshape(DEC_BATCH, DEC_SEQ, GQA_KW),
                                 cache_gqa_k[:, l].reshape(DEC_BATCH, PAST_LEN, GQA_KW)], axis=1).astype(BF16)
        v_all = jnp.concatenate([v[N_CTX:].reshape(DEC_BATCH, DEC_SEQ, GQA_KW),
                                 cache_gqa_v[:, l].reshape(DEC_BATCH, PAST_LEN, GQA_KW)], axis=1).astype(BF16)
        ogqa_l = _gqa_latent(q, k_all, v_all)
        onat_l = _nat_latent(pnat, cache_nat_k[:, l].reshape(DEC_BATCH, PAST_LEN, NAT_W),
                             cache_nat_v[:, l].reshape(DEC_BATCH, PAST_LEN, NAT_W),
                             _nat_bias_table(nat_rpb[l]))

        wr = jnp.pad(w_router[l], ((0, 0), (0, 128 - N_EXPERTS))).astype(BF16)
        x1, xm, logits = _out_proj(
            x, jnp.concatenate([of_c, of_l], 0), jnp.concatenate([ob_c, ob_l], 0), pgla,
            jnp.concatenate([onat_c, onat_l], 0), jnp.concatenate([ogqa_c, ogqa_l], 0),
            _prep_w_out(w_out[l]), row(jnp.tile(gla_norm_g[l], H_GLA)), g1, row(ln1_g[l]), row(ln1_b[l]),
            sh2, sc2, wr)
        gates = _route(logits, router_bias[l])
        x = _moe(xm, gates, x1, g2, w_expert_gu[l], w_expert_down[l], w_shared_gu[l].astype(BF16),
                 w_shared_down[l].astype(BF16), row(ln2_g[l]), row(ln2_b[l]))

        st_gla.append(_blockdiag_to_state(st_c))
        st_nk.append(pnat[:N_CTX, NAT_W:2 * NAT_W].reshape(BATCH, SEQ, H_NAT, HEAD_DIM))
        st_nv.append(pnat[:N_CTX, 2 * NAT_W:].reshape(BATCH, SEQ, H_NAT, HEAD_DIM))
        st_gk.append(k[:N_CTX].reshape(BATCH, SEQ, KV_GQA, HEAD_DIM))
        st_gv.append(v[:N_CTX].reshape(BATCH, SEQ, KV_GQA, HEAD_DIM))

    y_prompt = x[:N_CTX].reshape(BATCH, SEQ, D_MODEL)
    y_sample = x[N_CTX:].reshape(DEC_BATCH, DEC_SEQ, D_MODEL)
    return (y_prompt, y_sample, jnp.stack(st_gla, axis=1), jnp.stack(st_nk, axis=1), jnp.stack(st_nv, axis=1),
            jnp.stack(st_gk, axis=1), jnp.stack(st_gv, axis=1))
```

```python
import functools

import numpy as np
import jax
import jax.numpy as jnp
from jax import lax
from jax.experimental import pallas as pl
from jax.experimental.pallas import tpu as pltpu

D_MODEL = 1024
BATCH = 16
SEQ = 256
DEPTH = 2
DEC_BATCH = 2
DEC_SEQ = 4096
PAST_LEN = 512
GRID_W = 64
HEAD_DIM = 64
H_GLA = 4
GLA_DK = 64
GLA_DV = 64
GLA_LOWRANK = 16
GLA_TAU = 16.0
H_NAT = 4
NAT_KH = 8
NAT_KW = 16
H_GQA = 8
KV_GQA = 2
ROPE_THETA = 10000.0
N_EXPERTS = 64
TOP_K = 8
N_GROUPS = 8
TOPK_GROUPS = 4
D_EXPERT = 256
D_SHARED = 256
ROUTE_SCALE = 2.5
EPS = 1e-6
ALPHA = (2 * DEPTH) ** 0.25
IN_SPLITS = (H_GLA * GLA_DK, H_GLA * GLA_DK, H_GLA * GLA_DV, H_GLA * GLA_DV, 2 * GLA_LOWRANK,
             H_NAT * HEAD_DIM, H_NAT * HEAD_DIM, H_NAT * HEAD_DIM,
             H_GQA * HEAD_DIM, KV_GQA * HEAD_DIM, KV_GQA * HEAD_DIM)

F32 = jnp.float32
BF16 = jnp.bfloat16

N_CTX = BATCH * SEQ
N_LAT = DEC_BATCH * DEC_SEQ
N_TOK = N_CTX + N_LAT
N_COND = 8
GLA_W = H_GLA * GLA_DK
NAT_W = H_NAT * HEAD_DIM
GQA_QW = H_GQA * HEAD_DIM
GQA_KW = KV_GQA * HEAD_DIM
W_IN_COLS = 4 * GLA_W + 3 * NAT_W + GQA_QW + 2 * GQA_KW + 128
GA_COL = W_IN_COLS - 128
GQA_SLOT_HEADS = (0, 4, 1, 5, 2, 6, 3, 7)
GLA_CHUNK = 32
GLA_TB = 256
ROWS = DEC_SEQ // GRID_W
NAT_ROWS_PER_STEP = 8
VMEM_LIMIT = 56 * 1024 * 1024

TM_PROJ = 256
TM_MOE = 1024
TQ_GQA = 256


def _dot(a, b):
    return jnp.dot(a.astype(BF16), b.astype(BF16), preferred_element_type=F32)


def _dot_nt(a, b):
    return lax.dot_general(a.astype(BF16), b.astype(BF16), (((1,), (1,)), ((), ())),
                           preferred_element_type=F32)


def _dot_tn(a, b):
    return lax.dot_general(a.astype(BF16), b.astype(BF16), (((0,), (0,)), ((), ())),
                           preferred_element_type=F32)


def _dot_split(a, b_bf16):
    hi = a.astype(BF16)
    lo = (a - hi.astype(F32)).astype(BF16)
    return (jnp.dot(hi, b_bf16, preferred_element_type=F32)
            + jnp.dot(lo, b_bf16, preferred_element_type=F32))


def _sigmoid(x):
    return 1.0 / (1.0 + jnp.exp(-x))


def _silu(x):
    return x * _sigmoid(x)


def _ln(x):
    xc = x - jnp.mean(x, axis=-1, keepdims=True)
    return xc * lax.rsqrt(jnp.mean(xc * xc, axis=-1, keepdims=True) + EPS)


def _lane_group(shape, axis, width):
    return lax.shift_right_logical(lax.broadcasted_iota(jnp.int32, shape, axis), int(np.log2(width)))


def _head_mean_sq(x):
    w = x.shape[-1]
    bmat = jnp.where(_lane_group((w, w), 0, HEAD_DIM) == _lane_group((w, w), 1, HEAD_DIM),
                     1.0 / HEAD_DIM, 0.0).astype(BF16)
    return _dot_split(x * x, bmat)


def _swap_half_heads(x):
    w = x.shape[-1]
    lane = lax.broadcasted_iota(jnp.int32, x.shape, x.ndim - 1)
    first = (lane & (HEAD_DIM - 1)) < HEAD_DIM // 2
    return jnp.where(first, pltpu.roll(x, w - HEAD_DIM // 2, x.ndim - 1),
                     pltpu.roll(x, HEAD_DIM // 2, x.ndim - 1))


def _cond_row(tile, tm):
    r = tile * tm
    return jnp.where(r < N_CTX, 0, 1 + (r - N_CTX) // DEC_SEQ)


def _stack_heads(x, n, width):
    head = _lane_group(x.shape, 1, width)
    return jnp.concatenate([jnp.where(head == h, x, jnp.zeros_like(x)) for h in range(n)], axis=0)


def _unstack_heads(o, n, width):
    m = o.shape[0] // n
    head = _lane_group((m, o.shape[1]), 1, width)
    out = jnp.zeros((m, o.shape[1]), o.dtype)
    for h in range(n):
        out = jnp.where(head == h, o[h * m:(h + 1) * m], out)
    return out


def _mod_kernel(c_ref, w_ref, b_ref, o_ref):
    o_ref[0] = _dot(_silu(c_ref[...]), w_ref[0]) + b_ref[0]


def _modulation(cond, w_mod, b_mod):
    tn = 1536
    return pl.pallas_call(
        _mod_kernel,
        grid=(DEPTH, 6 * D_MODEL // tn),
        in_specs=[pl.BlockSpec((N_COND, D_MODEL), lambda l, j: (0, 0)),
                  pl.BlockSpec((1, D_MODEL, tn), lambda l, j: (l, 0, j)),
                  pl.BlockSpec((1, 1, tn), lambda l, j: (l, 0, j))],
        out_specs=pl.BlockSpec((1, N_COND, tn), lambda l, j: (l, 0, j)),
        out_shape=jax.ShapeDtypeStruct((DEPTH, N_COND, 6 * D_MODEL), F32),
        compiler_params=pltpu.CompilerParams(dimension_semantics=("arbitrary", "arbitrary"),
                                             vmem_limit_bytes=VMEM_LIMIT),
        name="modulation",
    )(cond, w_mod, b_mod.reshape(DEPTH, 1, 6 * D_MODEL))


def _inproj_kernel(x_ref, sh_ref, sc_ref, w_ref, wa_ref, ba_ref, qg_ref, kg_ref, cos_ref, sin_ref,
                   pgla_ref, la_ref, pnat_ref, q_ref, k_ref, v_ref):
    ci = _cond_row(pl.program_id(0), TM_PROJ)
    xn = _ln(x_ref[...])
    xm = xn * (1.0 + sc_ref[pl.ds(ci, 1), :]) + sh_ref[pl.ds(ci, 1), :]
    acc = _dot(xm, w_ref[...])
    pgla_ref[...] = acc[:, 0:4 * GLA_W]
    pnat_ref[...] = acc[:, 4 * GLA_W:4 * GLA_W + 3 * NAT_W]
    z = _dot(acc[:, GA_COL:GA_COL + 128], wa_ref[...]) + ba_ref[...]
    la_ref[...] = (jnp.minimum(z, 0.0) - jnp.log(1.0 + jnp.exp(-jnp.abs(z)))) * (1.0 / GLA_TAU)

    c0 = 4 * GLA_W + 3 * NAT_W
    cos = cos_ref[...]
    sin = sin_ref[...]

    def norm_rope(a, g):
        an = a * lax.rsqrt(_head_mean_sq(a) + EPS) * g
        return an * cos + _swap_half_heads(an) * sin

    qs = [norm_rope(acc[:, c0 + j * 128:c0 + (j + 1) * 128], qg_ref[...]) for j in range(GQA_QW // 128)]
    q_ref[...] = (jnp.concatenate(qs, axis=1) * (HEAD_DIM ** -0.5)).astype(BF16)
    k_ref[...] = norm_rope(acc[:, c0 + GQA_QW:c0 + GQA_QW + GQA_KW], kg_ref[...])
    v_ref[...] = acc[:, c0 + GQA_QW + GQA_KW:c0 + GQA_QW + 2 * GQA_KW]


def _in_proj(x, sh, sc, w, wa, ba, qg, kg, cos_t, sin_t):
    tm = TM_PROJ
    full = lambda a: pl.BlockSpec(a.shape, lambda i: (0,) * a.ndim)
    rows = lambda w_: pl.BlockSpec((tm, w_), lambda i: (i, 0))
    return pl.pallas_call(
        _inproj_kernel,
        grid=(N_TOK // tm,),
        in_specs=[rows(D_MODEL), full(sh), full(sc), full(w), full(wa), full(ba), full(qg), full(kg),
                  rows(128), rows(128)],
        out_specs=[rows(4 * GLA_W), rows(2 * GLA_W), rows(3 * NAT_W), rows(GQA_QW), rows(GQA_KW),
                   rows(GQA_KW)],
        out_shape=[jax.ShapeDtypeStruct((N_TOK, 4 * GLA_W), F32),
                   jax.ShapeDtypeStruct((N_TOK, 2 * GLA_W), F32),
                   jax.ShapeDtypeStruct((N_TOK, 3 * NAT_W), F32),
                   jax.ShapeDtypeStruct((N_TOK, GQA_QW), BF16),
                   jax.ShapeDtypeStruct((N_TOK, GQA_KW), F32),
                   jax.ShapeDtypeStruct((N_TOK, GQA_KW), F32)],
        compiler_params=pltpu.CompilerParams(dimension_semantics=("arbitrary",),
                                             vmem_limit_bytes=VMEM_LIMIT),
        name="in_proj",
    )(x, sh, sc, w, wa, ba, qg, kg, cos_t, sin_t)


def _gla_kernel(qf_ref, kf_ref, vf_ref, laf_ref, qb_ref, kb_ref, vb_ref, lab_ref, st0_ref,
                of_ref, ob_ref, stout_ref, st_ref):
    t = pl.program_id(1)
    last_t = pl.num_programs(1) - 1
    C = GLA_CHUNK
    nchunk = GLA_TB // C

    @pl.when(t == 0)
    def _():
        st_ref[...] = st0_ref[0]

    ri = lax.broadcasted_iota(jnp.int32, (C, C), 0)
    cj = lax.broadcasted_iota(jnp.int32, (C, C), 1)
    rj = lax.broadcasted_iota(jnp.int32, (C, H_GLA * C), 0)
    cc = lax.broadcasted_iota(jnp.int32, (C, H_GLA * C), 1) & (C - 1)
    blk = _lane_group((H_GLA * C, GLA_W), 0, C) == _lane_group((H_GLA * C, GLA_W), 1, GLA_DK)
    diag = _lane_group((GLA_W, GLA_W), 0, GLA_DV) == _lane_group((GLA_W, GLA_W), 1, GLA_DK)

    def chunk(q_ref, k_ref, v_ref, la_ref, o_ref, d, c, causal):
        rows = pl.ds(pl.multiple_of(c * C, C), C)
        q = q_ref[rows, :] * (GLA_DK ** -0.5)
        k = k_ref[rows, :]
        v = v_ref[rows, :]
        la = la_ref[rows, :]
        tri = jnp.where((ri >= cj) if causal else (ri <= cj), 1.0, 0.0).astype(BF16)
        la_hi = la.astype(BF16)
        la_lo = (la - la_hi.astype(F32)).astype(BF16)
        b = (jnp.dot(tri, la_hi, preferred_element_type=F32)
             + jnp.dot(tri, la_lo, preferred_element_type=F32))
        btot = jnp.sum(la, axis=0, keepdims=True)
        bref = b[C // 2:C // 2 + 1, :]
        q_att = q * jnp.exp(b - bref)
        k_att = k * jnp.exp(bref - b)
        kblk = jnp.where(blk, jnp.concatenate([k_att] * H_GLA, axis=0), 0.0)
        att = _dot_nt(q_att, kblk)
        att = jnp.where((rj >= cc) if causal else (rj <= cc), att, 0.0)
        vblk = jnp.where(blk, jnp.concatenate([v] * H_GLA, axis=0), 0.0)
        st = st_ref[d]
        o_ref[rows, :] = _dot(att, vblk) + _dot_nt(q * jnp.exp(b), st)
        u = _dot_tn(v, k * jnp.exp(btot - b))
        st_ref[d] = st * jnp.exp(btot) + jnp.where(diag, u, 0.0)

    def body(c, carry):
        chunk(qf_ref, kf_ref, vf_ref, laf_ref, of_ref, 0, c, True)
        chunk(qb_ref, kb_ref, vb_ref, lab_ref, ob_ref, 1, nchunk - 1 - c, False)
        return carry

    lax.fori_loop(0, nchunk, body, 0)

    @pl.when(t == last_t)
    def _():
        stout_ref[0] = st_ref[...]


def _gla(pgla, la, st0, row0, n_seq, seq_len):
    tb = GLA_TB
    nt = seq_len // tb
    b0 = row0 // tb
    fwd = lambda col: pl.BlockSpec((tb, GLA_W), lambda s, t: (b0 + s * nt + t, col))
    bwd = lambda col: pl.BlockSpec((tb, GLA_W), lambda s, t: (b0 + s * nt + nt - 1 - t, col))
    st_spec = pl.BlockSpec((1, 2, GLA_W, GLA_W), lambda s, t: (s, 0, 0, 0))
    n = n_seq * seq_len
    return pl.pallas_call(
        _gla_kernel,
        grid=(n_seq, nt),
        in_specs=[fwd(0), fwd(1), fwd(2), fwd(0), bwd(0), bwd(1), bwd(2), bwd(1), st_spec],
        out_specs=[pl.BlockSpec((tb, GLA_W), lambda s, t: (s * nt + t, 0)),
                   pl.BlockSpec((tb, GLA_W), lambda s, t: (s * nt + nt - 1 - t, 0)),
                   st_spec],
        out_shape=[jax.ShapeDtypeStruct((n, GLA_W), F32), jax.ShapeDtypeStruct((n, GLA_W), F32),
                   jax.ShapeDtypeStruct((n_seq, 2, GLA_W, GLA_W), F32)],
        scratch_shapes=[pltpu.VMEM((2, GLA_W, GLA_W), F32)],
        compiler_params=pltpu.CompilerParams(dimension_semantics=("arbitrary", "arbitrary"),
                                             vmem_limit_bytes=VMEM_LIMIT),
        name="gla_scan",
    )(pgla, pgla, pgla, la, pgla, pgla, pgla, la, st0)


def _softmax_pv(s_list, v_list):
    m = s_list[0].max(axis=-1, keepdims=True)
    for s in s_list[1:]:
        m = jnp.maximum(m, s.max(axis=-1, keepdims=True))
    acc = None
    l = None
    for s, v in zip(s_list, v_list):
        p = jnp.exp(s - m)
        pl_ = p.sum(axis=-1, keepdims=True)
        pv = _dot(p, v)
        acc = pv if acc is None else acc + pv
        l = pl_ if l is None else l + pl_
    return acc * (1.0 / l)


def _ctx_attn_kernel(pnat_ref, q_ref, k_ref, v_ref, onat_ref, ogqa_ref):
    nq = pnat_ref[:, 0:NAT_W] * (HEAD_DIM ** -0.5)
    nk = pnat_ref[:, NAT_W:2 * NAT_W].astype(BF16)
    nv = pnat_ref[:, 2 * NAT_W:3 * NAT_W].astype(BF16)
    o = _softmax_pv([_dot_nt(_stack_heads(nq, H_NAT, HEAD_DIM), nk)], [nv])
    onat_ref[...] = _unstack_heads(o, H_NAT, HEAD_DIM).astype(BF16)

    k = k_ref[...].astype(BF16)
    v = v_ref[...].astype(BF16)
    k2 = jnp.concatenate([k, k], axis=1)
    v2 = jnp.concatenate([v, v], axis=1)
    outs = []
    for half in range(2):
        q = q_ref[:, half * 256:(half + 1) * 256]
        o = _softmax_pv([_dot_nt(_stack_heads(q, 4, HEAD_DIM), k2)], [v2])
        outs.append(_unstack_heads(o, 4, HEAD_DIM))
    ogqa_ref[...] = jnp.concatenate(outs, axis=1).astype(BF16)


def _ctx_attention(pnat, q, k, v):
    rows = lambda w_: pl.BlockSpec((SEQ, w_), lambda i: (i, 0))
    return pl.pallas_call(
        _ctx_attn_kernel,
        grid=(BATCH,),
        in_specs=[rows(3 * NAT_W), rows(GQA_QW), rows(GQA_KW), rows(GQA_KW)],
        out_specs=[rows(NAT_W), rows(GQA_QW)],
        out_shape=[jax.ShapeDtypeStruct((N_CTX, NAT_W), BF16),
                   jax.ShapeDtypeStruct((N_CTX, GQA_QW), BF16)],
        compiler_params=pltpu.CompilerParams(dimension_semantics=("arbitrary",),
                                             vmem_limit_bytes=VMEM_LIMIT),
        name="ctx_attention",
    )(pnat, q, k, v)


def _gqa_lat_kernel(q_ref, k_ref, v_ref, o_ref):
    k = k_ref[0]
    v = v_ref[0]
    k2 = jnp.concatenate([k, k], axis=1)
    v2 = jnp.concatenate([v, v], axis=1)
    slot = _lane_group((TQ_GQA, 256), 1, HEAD_DIM)
    outs = []
    for half in range(2):
        q = q_ref[:, half * 256:(half + 1) * 256]
        out = jnp.zeros((TQ_GQA, 256), F32)
        for s in range(4):
            qs = jnp.where(slot == s, q, jnp.zeros_like(q))
            o = _softmax_pv([_dot_nt(qs, k2)], [v2])
            out = jnp.where(slot == s, o, out)
        outs.append(out)
    o_ref[...] = jnp.concatenate(outs, axis=1).astype(BF16)


def _gqa_latent(q, k_all, v_all):
    tq = TQ_GQA
    nq = DEC_SEQ // tq
    tk = k_all.shape[1]
    return pl.pallas_call(
        _gqa_lat_kernel,
        grid=(DEC_BATCH, nq),
        in_specs=[pl.BlockSpec((tq, GQA_QW), lambda b, i: (N_CTX // tq + b * nq + i, 0)),
                  pl.BlockSpec((1, tk, GQA_KW), lambda b, i: (b, 0, 0)),
                  pl.BlockSpec((1, tk, GQA_KW), lambda b, i: (b, 0, 0))],
        out_specs=pl.BlockSpec((tq, GQA_QW), lambda b, i: (b * nq + i, 0)),
        out_shape=jax.ShapeDtypeStruct((N_LAT, GQA_QW), BF16),
        compiler_params=pltpu.CompilerParams(dimension_semantics=("arbitrary", "arbitrary"),
                                             vmem_limit_bytes=VMEM_LIMIT),
        name="gqa_latent",
    )(q, k_all, v_all)


def _nat_lat_kernel(q_ref, k_ref, v_ref, kc_ref, vc_ref, bias_ref, o_ref):
    j = pl.program_id(1)
    kc = kc_ref[0].astype(BF16)
    vc = vc_ref[0].astype(BF16)
    for i in range(NAT_ROWS_PER_STEP):
        r = j * NAT_ROWS_PER_STEP + i
        r0 = jnp.clip(r - NAT_KH // 2, 0, ROWS - NAT_KH)
        win = pl.ds(pl.multiple_of(r0 * GRID_W, GRID_W), NAT_KH * GRID_W)
        q = q_ref[i * GRID_W:(i + 1) * GRID_W, :] * (HEAD_DIM ** -0.5)
        qs = _stack_heads(q, H_NAT, HEAD_DIM)
        s_loc = _dot_nt(qs, k_ref[win, :]) + bias_ref[r - r0]
        s_ctx = _dot_nt(qs, kc)
        o = _softmax_pv([s_loc, s_ctx], [v_ref[win, :], vc])
        o_ref[i * GRID_W:(i + 1) * GRID_W, :] = _unstack_heads(o, H_NAT, HEAD_DIM).astype(BF16)


def _nat_latent(pnat, kc, vc, bias):
    tq = NAT_ROWS_PER_STEP * GRID_W
    nq = DEC_SEQ // tq
    lat_blk = N_CTX // DEC_SEQ
    return pl.pallas_call(
        _nat_lat_kernel,
        grid=(DEC_BATCH, nq),
        in_specs=[pl.BlockSpec((tq, NAT_W), lambda b, j: (N_CTX // tq + b * nq + j, 0)),
                  pl.BlockSpec((DEC_SEQ, NAT_W), lambda b, j: (lat_blk + b, 1)),
                  pl.BlockSpec((DEC_SEQ, NAT_W), lambda b, j: (lat_blk + b, 2)),
                  pl.BlockSpec((1, PAST_LEN, NAT_W), lambda b, j: (b, 0, 0)),
                  pl.BlockSpec((1, PAST_LEN, NAT_W), lambda b, j: (b, 0, 0)),
                  pl.BlockSpec(bias.shape, lambda b, j: (0, 0, 0))],
        out_specs=pl.BlockSpec((tq, NAT_W), lambda b, j: (b * nq + j, 0)),
        out_shape=jax.ShapeDtypeStruct((N_LAT, NAT_W), BF16),
        compiler_params=pltpu.CompilerParams(dimension_semantics=("arbitrary", "arbitrary"),
                                             vmem_limit_bytes=VMEM_LIMIT),
        name="nat_latent",
    )(pnat, pnat, pnat, kc, vc, bias)


def _route_gates(logits_t, bias_ref):
    per = N_EXPERTS // N_GROUPS
    t = logits_t.shape[1]
    neg = -jnp.inf
    pos = lax.broadcasted_iota(jnp.int32, (per, t), 0)
    scores = [_sigmoid(logits_t[g * per:(g + 1) * per, :]) for g in range(N_GROUPS)]
    biased = [scores[g] + bias_ref[g * per:(g + 1) * per, :] for g in range(N_GROUPS)]

    grp = []
    for v in biased:
        m1 = jnp.max(v, axis=0, keepdims=True)
        i1 = jnp.min(jnp.where(v == m1, pos, per), axis=0, keepdims=True)
        m2 = jnp.max(jnp.where(pos == i1, neg, v), axis=0, keepdims=True)
        grp.append(m1 + m2)

    keep = [jnp.zeros((1, t), jnp.bool_) for _ in range(N_GROUPS)]
    for _ in range(TOPK_GROUPS):
        best = functools.reduce(jnp.maximum, grp)
        first = jnp.full((1, t), N_GROUPS, jnp.int32)
        for g in reversed(range(N_GROUPS)):
            first = jnp.where(grp[g] == best, g, first)
        for g in range(N_GROUPS):
            hit = first == g
            keep[g] = keep[g] | hit
            grp[g] = jnp.where(hit, neg, grp[g])

    cand = [jnp.where(keep[g], biased[g], neg) for g in range(N_GROUPS)]
    flat = [pos + g * per for g in range(N_GROUPS)]
    sel = [jnp.zeros((per, t), jnp.bool_) for _ in range(N_GROUPS)]
    for _ in range(TOP_K):
        best = functools.reduce(jnp.maximum, [jnp.max(v, axis=0, keepdims=True) for v in cand])
        first = functools.reduce(jnp.minimum, [
            jnp.min(jnp.where(cand[g] == best, flat[g], N_EXPERTS), axis=0, keepdims=True)
            for g in range(N_GROUPS)])
        for g in range(N_GROUPS):
            hit = flat[g] == first
            sel[g] = sel[g] | hit
            cand[g] = jnp.where(hit, neg, cand[g])

    w = [jnp.where(sel[g], scores[g], 0.0) for g in range(N_GROUPS)]
    total = functools.reduce(lambda a, b: a + b, [jnp.sum(v, axis=0, keepdims=True) for v in w])
    return jnp.concatenate([v / total * ROUTE_SCALE for v in w], axis=0)


def _outproj_kernel(x_ref, of_ref, ob_ref, gg_ref, onat_ref, ogqa_ref, w_ref, ng_ref, g1_ref, lng_ref,
                    lnb_ref, sh2_ref, sc2_ref, wr_ref, rb_ref, x1_ref, xm_ref, gates_ref):
    ci = _cond_row(pl.program_id(0), TM_PROJ)
    og = of_ref[...] + ob_ref[...]
    halves = [og[:, j * 128:(j + 1) * 128] for j in range(GLA_W // 128)]
    ms = jnp.concatenate([_head_mean_sq(h) for h in halves], axis=1)
    ogla = og * lax.rsqrt(ms + EPS) * ng_ref[...] * _silu(gg_ref[...])
    mix = (_dot(ogla, w_ref[0:GLA_W, :]) + _dot(onat_ref[...], w_ref[GLA_W:GLA_W + NAT_W, :])
           + _dot(ogqa_ref[...], w_ref[GLA_W + NAT_W:, :]))
    x1 = _ln(ALPHA * x_ref[...] + g1_ref[pl.ds(ci, 1), :] * mix) * lng_ref[...] + lnb_ref[...]
    x1_ref[...] = x1
    xm = (_ln(x1) * (1.0 + sc2_ref[pl.ds(ci, 1), :]) + sh2_ref[pl.ds(ci, 1), :]).astype(BF16)
    xm_ref[...] = xm
    gates_t = _route_gates(_dot_nt(wr_ref[...], xm), rb_ref)
    gates_ref[...] = jnp.concatenate([gates_t, jnp.zeros_like(gates_t)], axis=0).T


def _out_proj(x, of, ob, pgla, onat, ogqa, w, ng, g1, lng, lnb, sh2, sc2, wr, rb):
    tm = TM_PROJ
    full = lambda a: pl.BlockSpec(a.shape, lambda i: (0,) * a.ndim)
    rows = lambda w_: pl.BlockSpec((tm, w_), lambda i: (i, 0))
    return pl.pallas_call(
        _outproj_kernel,
        grid=(N_TOK // tm,),
        in_specs=[rows(D_MODEL), rows(GLA_W), rows(GLA_W), pl.BlockSpec((tm, GLA_W), lambda i: (i, 3)),
                  rows(NAT_W), rows(GQA_QW), full(w), full(ng), full(g1), full(lng), full(lnb),
                  full(sh2), full(sc2), full(wr), full(rb)],
        out_specs=[rows(D_MODEL), rows(D_MODEL), rows(128)],
        out_shape=[jax.ShapeDtypeStruct((N_TOK, D_MODEL), F32),
                   jax.ShapeDtypeStruct((N_TOK, D_MODEL), BF16),
                   jax.ShapeDtypeStruct((N_TOK, 128), F32)],
        compiler_params=pltpu.CompilerParams(dimension_semantics=("arbitrary",),
                                             vmem_limit_bytes=VMEM_LIMIT),
        name="out_proj",
    )(x, of, ob, pgla, onat, ogqa, w, ng, g1, lng, lnb, sh2, sc2, wr, rb)


def _moe_kernel(xm_ref, gates_ref, x1_ref, g2_ref, wgu_ref, wd_ref, wsgu_ref, wsd_ref, lng_ref, lnb_ref,
                out_ref, acc_ref):
    e = pl.program_id(1)
    ci = _cond_row(pl.program_id(0), TM_MOE)
    xm = xm_ref[...]

    def expert(ab, gate):
        h = _silu(ab[:, :D_EXPERT]) * ab[:, D_EXPERT:]
        return h if gate is None else h * gate

    @pl.when(e == 0)
    def _():
        acc_ref[...] = _dot(expert(jnp.dot(xm, wsgu_ref[...], preferred_element_type=F32), None),
                            wsd_ref[...])

    lane = lax.broadcasted_iota(jnp.int32, gates_ref.shape, 1)
    gate = jnp.sum(jnp.where(lane == e, gates_ref[...], 0.0), axis=1, keepdims=True)
    ab = jnp.dot(xm, wgu_ref[0].astype(BF16), preferred_element_type=F32)
    acc_ref[...] += _dot(expert(ab, gate), wd_ref[0])

    @pl.when(e == N_EXPERTS - 1)
    def _():
        y = ALPHA * x1_ref[...] + g2_ref[pl.ds(ci, 1), :] * acc_ref[...]
        out_ref[...] = _ln(y) * lng_ref[...] + lnb_ref[...]


def _moe(xm, gates, x1, g2, wgu, wd, wsgu, wsd, lng, lnb):
    tm = TM_MOE
    full = lambda a: pl.BlockSpec(a.shape, lambda i, e: (0,) * a.ndim)
    rows = lambda w_: pl.BlockSpec((tm, w_), lambda i, e: (i, 0))
    return pl.pallas_call(
        _moe_kernel,
        grid=(N_TOK // tm, N_EXPERTS),
        in_specs=[rows(D_MODEL), rows(128), rows(D_MODEL), full(g2),
                  pl.BlockSpec((1, D_MODEL, 2 * D_EXPERT), lambda i, e: (e, 0, 0)),
                  pl.BlockSpec((1, D_EXPERT, D_MODEL), lambda i, e: (e, 0, 0)),
                  full(wsgu), full(wsd), full(lng), full(lnb)],
        out_specs=rows(D_MODEL),
        out_shape=jax.ShapeDtypeStruct((N_TOK, D_MODEL), F32),
        scratch_shapes=[pltpu.VMEM((tm, D_MODEL), F32)],
        compiler_params=pltpu.CompilerParams(dimension_semantics=("arbitrary", "arbitrary"),
                                             vmem_limit_bytes=VMEM_LIMIT),
        name="moe",
    )(xm, gates, x1, g2, wgu, wd, wsgu, wsd, lng, lnb)


def _rope_tables():
    t = jnp.arange(DEC_SEQ)
    n_freq = HEAD_DIM // 4
    freqs = ROPE_THETA ** (-jnp.arange(n_freq, dtype=F32) / n_freq)
    ang = jnp.concatenate([(t // GRID_W).astype(F32)[:, None] * freqs,
                           (t % GRID_W).astype(F32)[:, None] * freqs], -1)
    cos, sin = jnp.cos(ang), jnp.sin(ang)
    cos_h = jnp.concatenate([cos, cos], -1)
    sin_h = jnp.concatenate([-sin, sin], -1)
    lat = lambda a: jnp.tile(a, (DEC_BATCH, 128 // HEAD_DIM))
    cos_t = jnp.concatenate([jnp.ones((N_CTX, 128), F32), lat(cos_h)], 0)
    sin_t = jnp.concatenate([jnp.zeros((N_CTX, 128), F32), lat(sin_h)], 0)
    return cos_t, sin_t


def _nat_bias_table(rpb):
    pat = np.arange(NAT_KH)
    dr_idx = np.arange(NAT_KH)[None, :] - pat[:, None] + NAT_KH - 1
    cidx = np.arange(GRID_W)
    dc_idx = np.clip(cidx[None, :] - cidx[:, None] + NAT_KW - 1, 0, 2 * NAT_KW - 2)
    col_start = np.clip(cidx - NAT_KW // 2, 0, GRID_W - NAT_KW)
    col_in = (cidx[None, :] >= col_start[:, None]) & (cidx[None, :] < col_start[:, None] + NAT_KW)
    bias = rpb[:, dr_idx[:, None, :, None], dc_idx[None, :, None, :]]
    bias = jnp.where(col_in[None, None, :, None, :], bias, -jnp.inf)
    return bias.transpose(1, 0, 2, 3, 4).reshape(NAT_KH, H_NAT * GRID_W, NAT_KH * GRID_W)


def _prep_w_in(w_in):
    cuts = np.cumsum((0,) + IN_SPLITS)
    seg = [w_in[:, cuts[i]:cuts[i + 1]] for i in range(len(IN_SPLITS))]
    gq, gk, gv, gg, ga, nq, nk, nv, aq, ak, av = seg
    aq = aq.reshape(D_MODEL, H_GQA, HEAD_DIM)[:, np.array(GQA_SLOT_HEADS), :].reshape(D_MODEL, GQA_QW)
    ga = jnp.pad(ga, ((0, 0), (0, 128 - 2 * GLA_LOWRANK)))
    return jnp.concatenate([gq, gk, gv, gg, nq, nk, nv, aq, ak, av, ga], axis=1).astype(BF16)


def _prep_w_a(w_a2, b_a):
    wa = jnp.zeros((128, 2 * GLA_W), F32)
    wa = wa.at[0:GLA_LOWRANK, 0:GLA_W].set(w_a2[0])
    wa = wa.at[GLA_LOWRANK:2 * GLA_LOWRANK, GLA_W:].set(w_a2[1])
    return wa.astype(BF16), b_a.reshape(1, 2 * GLA_W)


def _prep_w_out(w_out):
    gqa = w_out[GLA_W + NAT_W:].reshape(H_GQA, HEAD_DIM, D_MODEL)[np.array(GQA_SLOT_HEADS)]
    return jnp.concatenate([w_out[:GLA_W + NAT_W], gqa.reshape(GQA_QW, D_MODEL)], 0).astype(BF16)


def _state_to_blockdiag(s):
    out = jnp.zeros(s.shape[:-3] + (GLA_W, GLA_W), F32)
    for h in range(H_GLA):
        out = out.at[..., h * GLA_DV:(h + 1) * GLA_DV, h * GLA_DK:(h + 1) * GLA_DK].set(
            jnp.swapaxes(s[..., h, :, :], -1, -2))
    return out


def _blockdiag_to_state(st):
    blocks = [st[..., h * GLA_DV:(h + 1) * GLA_DV, h * GLA_DK:(h + 1) * GLA_DK] for h in range(H_GLA)]
    return jnp.swapaxes(jnp.stack(blocks, axis=-3), -1, -2)


def kernel(x_prompt, x_sample, state_gla, cache_nat_k, cache_nat_v, cache_gqa_k, cache_gqa_v, c, c_ctx, w_mod, b_mod, w_in, gla_w_a2, gla_b_a, gla_norm_g, nat_rpb, gqa_q_norm_g, gqa_k_norm_g, w_out, ln1_g, ln1_b, w_router, router_bias, w_expert_gu, w_expert_down, w_shared_gu, w_shared_down, ln2_g, ln2_b):
    x = jnp.concatenate([x_prompt.reshape(N_CTX, D_MODEL), x_sample.reshape(N_LAT, D_MODEL)], axis=0)
    cond = jnp.concatenate([c_ctx[None, :], c, jnp.zeros((N_COND - 1 - DEC_BATCH, D_MODEL), F32)], axis=0)
    mods = _modulation(cond, w_mod, b_mod)
    cos_t, sin_t = _rope_tables()
    row = lambda a: a.reshape(1, -1)

    st_gla, st_nk, st_nv, st_gk, st_gv = [], [], [], [], []
    for l in range(DEPTH):
        sh1, sc1, g1, sh2, sc2, g2 = [mods[l, :, j * D_MODEL:(j + 1) * D_MODEL] for j in range(6)]
        wa, ba = _prep_w_a(gla_w_a2[l], gla_b_a[l])
        qg = row(jnp.tile(gqa_q_norm_g[l], 128 // HEAD_DIM))
        kg = row(jnp.tile(gqa_k_norm_g[l], 128 // HEAD_DIM))
        pgla, la, pnat, q, k, v = _in_proj(x, sh1, sc1, _prep_w_in(w_in[l]), wa, ba, qg, kg, cos_t, sin_t)

        zero_st = jnp.zeros((BATCH, 2, GLA_W, GLA_W), F32)
        of_c, ob_c, st_c = _gla(pgla, la, zero_st, 0, BATCH, SEQ)
        of_l, ob_l, _ = _gla(pgla, la, _state_to_blockdiag(state_gla[:, l]), N_CTX, DEC_BATCH, DEC_SEQ)

        onat_c, ogqa_c = _ctx_attention(pnat, q, k, v)
        k_all = jnp.concatenate([k[N_CTX:].reshape(DEC_BATCH, DEC_SEQ, GQA_KW),
                                 cache_gqa_k[:, l].reshape(DEC_BATCH, PAST_LEN, GQA_KW)], axis=1).astype(BF16)
        v_all = jnp.concatenate([v[N_CTX:].reshape(DEC_BATCH, DEC_SEQ, GQA_KW),
                                 cache_gqa_v[:, l].reshape(DEC_BATCH, PAST_LEN, GQA_KW)], axis=1).astype(BF16)
        ogqa_l = _gqa_latent(q, k_all, v_all)
        onat_l = _nat_latent(pnat, cache_nat_k[:, l].reshape(DEC_BATCH, PAST_LEN, NAT_W),
                             cache_nat_v[:, l].reshape(DEC_BATCH, PAST_LEN, NAT_W),
                             _nat_bias_table(nat_rpb[l]))

        x1, xm, gates = _out_proj(
            x, jnp.concatenate([of_c, of_l], 0), jnp.concatenate([ob_c, ob_l], 0), pgla,
            jnp.concatenate([onat_c, onat_l], 0), jnp.concatenate([ogqa_c, ogqa_l], 0),
            _prep_w_out(w_out[l]), row(jnp.tile(gla_norm_g[l], H_GLA)), g1, row(ln1_g[l]), row(ln1_b[l]),
            sh2, sc2, w_router[l].T.astype(BF16), router_bias[l].reshape(N_EXPERTS, 1))
        x = _moe(xm, gates, x1, g2, w_expert_gu[l], w_expert_down[l], w_shared_gu[l].astype(BF16),
                 w_shared_down[l].astype(BF16), row(ln2_g[l]), row(ln2_b[l]))

        st_gla.append(_blockdiag_to_state(st_c))
        st_nk.append(pnat[:N_CTX, NAT_W:2 * NAT_W].reshape(BATCH, SEQ, H_NAT, HEAD_DIM))
        st_nv.append(pnat[:N_CTX, 2 * NAT_W:].reshape(BATCH, SEQ, H_NAT, HEAD_DIM))
        st_gk.append(k[:N_CTX].reshape(BATCH, SEQ, KV_GQA, HEAD_DIM))
        st_gv.append(v[:N_CTX].reshape(BATCH, SEQ, KV_GQA, HEAD_DIM))

    y_prompt = x[:N_CTX].reshape(BATCH, SEQ, D_MODEL)
    y_sample = x[N_CTX:].reshape(DEC_BATCH, DEC_SEQ, D_MODEL)
    return (y_prompt, y_sample, jnp.stack(st_gla, axis=1), jnp.stack(st_nk, axis=1), jnp.stack(st_nv, axis=1),
            jnp.stack(st_gk, axis=1), jnp.stack(st_gv, axis=1))
```

```python
import functools

import numpy as np
import jax
import jax.numpy as jnp
from jax import lax
from jax.experimental import pallas as pl
from jax.experimental.pallas import tpu as pltpu

D_MODEL = 1024
BATCH = 16
SEQ = 256
DEPTH = 2
DEC_BATCH = 2
DEC_SEQ = 4096
PAST_LEN = 512
GRID_W = 64
HEAD_DIM = 64
H_GLA = 4
GLA_DK = 64
GLA_DV = 64
GLA_LOWRANK = 16
GLA_TAU = 16.0
H_NAT = 4
NAT_KH = 8
NAT_KW = 16
H_GQA = 8
KV_GQA = 2
ROPE_THETA = 10000.0
N_EXPERTS = 64
TOP_K = 8
N_GROUPS = 8
TOPK_GROUPS = 4
D_EXPERT = 256
D_SHARED = 256
ROUTE_SCALE = 2.5
EPS = 1e-6
ALPHA = (2 * DEPTH) ** 0.25
IN_SPLITS = (H_GLA * GLA_DK, H_GLA * GLA_DK, H_GLA * GLA_DV, H_GLA * GLA_DV, 2 * GLA_LOWRANK,
             H_NAT * HEAD_DIM, H_NAT * HEAD_DIM, H_NAT * HEAD_DIM,
             H_GQA * HEAD_DIM, KV_GQA * HEAD_DIM, KV_GQA * HEAD_DIM)

F32 = jnp.float32
BF16 = jnp.bfloat16

N_CTX = BATCH * SEQ
N_LAT = DEC_BATCH * DEC_SEQ
N_TOK = N_CTX + N_LAT
N_COND = 8
GLA_W = H_GLA * GLA_DK
NAT_W = H_NAT * HEAD_DIM
GQA_QW = H_GQA * HEAD_DIM
GQA_KW = KV_GQA * HEAD_DIM
W_IN_COLS = 4 * GLA_W + 3 * NAT_W + GQA_QW + 2 * GQA_KW + 128
GA_COL = W_IN_COLS - 128
GQA_SLOT_HEADS = (0, 4, 1, 5, 2, 6, 3, 7)
GLA_CHUNK = 32
GLA_TB = 256
ROWS = DEC_SEQ // GRID_W
NAT_ROWS_PER_STEP = 8
VMEM_LIMIT = 56 * 1024 * 1024

TM_PROJ = 256
TM_MOE = 2048
TQ_GQA = 256


def _dot(a, b):
    return jnp.dot(a.astype(BF16), b.astype(BF16), preferred_element_type=F32)


def _dot_nt(a, b):
    return lax.dot_general(a.astype(BF16), b.astype(BF16), (((1,), (1,)), ((), ())),
                           preferred_element_type=F32)


def _dot_tn(a, b):
    return lax.dot_general(a.astype(BF16), b.astype(BF16), (((0,), (0,)), ((), ())),
                           preferred_element_type=F32)


def _dot_split(a, b_bf16):
    hi = a.astype(BF16)
    lo = (a - hi.astype(F32)).astype(BF16)
    return (jnp.dot(hi, b_bf16, preferred_element_type=F32)
            + jnp.dot(lo, b_bf16, preferred_element_type=F32))


def _sigmoid(x):
    return 1.0 / (1.0 + jnp.exp(-x))


def _silu(x):
    return x * _sigmoid(x)


def _ln(x):
    xc = x - jnp.mean(x, axis=-1, keepdims=True)
    return xc * lax.rsqrt(jnp.mean(xc * xc, axis=-1, keepdims=True) + EPS)


def _lane_group(shape, axis, width):
    return lax.shift_right_logical(lax.broadcasted_iota(jnp.int32, shape, axis), int(np.log2(width)))


def _head_mean_sq(x):
    w = x.shape[-1]
    bmat = jnp.where(_lane_group((w, w), 0, HEAD_DIM) == _lane_group((w, w), 1, HEAD_DIM),
                     1.0 / HEAD_DIM, 0.0).astype(BF16)
    return _dot_split(x * x, bmat)


def _swap_half_heads(x):
    w = x.shape[-1]
    lane = lax.broadcasted_iota(jnp.int32, x.shape, x.ndim - 1)
    first = (lane & (HEAD_DIM - 1)) < HEAD_DIM // 2
    return jnp.where(first, pltpu.roll(x, w - HEAD_DIM // 2, x.ndim - 1),
                     pltpu.roll(x, HEAD_DIM // 2, x.ndim - 1))


def _cond_row(tile, tm):
    r = tile * tm
    return jnp.where(r < N_CTX, 0, 1 + (r - N_CTX) // DEC_SEQ)


def _stack_heads(x, n, width):
    head = _lane_group(x.shape, 1, width)
    return jnp.concatenate([jnp.where(head == h, x, jnp.zeros_like(x)) for h in range(n)], axis=0)


def _unstack_heads(o, n, width):
    m = o.shape[0] // n
    head = _lane_group((m, o.shape[1]), 1, width)
    out = jnp.zeros((m, o.shape[1]), o.dtype)
    for h in range(n):
        out = jnp.where(head == h, o[h * m:(h + 1) * m], out)
    return out


def _mod_kernel(c_ref, w_ref, b_ref, o_ref):
    o_ref[0] = _dot(_silu(c_ref[...]), w_ref[0]) + b_ref[0]


def _modulation(cond, w_mod, b_mod):
    tn = 1536
    return pl.pallas_call(
        _mod_kernel,
        grid=(DEPTH, 6 * D_MODEL // tn),
        in_specs=[pl.BlockSpec((N_COND, D_MODEL), lambda l, j: (0, 0)),
                  pl.BlockSpec((1, D_MODEL, tn), lambda l, j: (l, 0, j)),
                  pl.BlockSpec((1, 1, tn), lambda l, j: (l, 0, j))],
        out_specs=pl.BlockSpec((1, N_COND, tn), lambda l, j: (l, 0, j)),
        out_shape=jax.ShapeDtypeStruct((DEPTH, N_COND, 6 * D_MODEL), F32),
        compiler_params=pltpu.CompilerParams(dimension_semantics=("arbitrary", "arbitrary"),
                                             vmem_limit_bytes=VMEM_LIMIT),
        name="modulation",
    )(cond, w_mod, b_mod.reshape(DEPTH, 1, 6 * D_MODEL))


def _inproj_kernel(x_ref, sh_ref, sc_ref, w_ref, wa_ref, ba_ref, qg_ref, kg_ref, cos_ref, sin_ref,
                   pgla_ref, la_ref, pnat_ref, q_ref, k_ref, v_ref):
    ci = _cond_row(pl.program_id(0), TM_PROJ)
    xn = _ln(x_ref[...])
    xm = xn * (1.0 + sc_ref[pl.ds(ci, 1), :]) + sh_ref[pl.ds(ci, 1), :]
    acc = _dot(xm, w_ref[...])
    pgla_ref[...] = acc[:, 0:4 * GLA_W]
    pnat_ref[...] = acc[:, 4 * GLA_W:4 * GLA_W + 3 * NAT_W]
    z = _dot(acc[:, GA_COL:GA_COL + 128], wa_ref[...]) + ba_ref[...]
    la_ref[...] = (jnp.minimum(z, 0.0) - jnp.log(1.0 + jnp.exp(-jnp.abs(z)))) * (1.0 / GLA_TAU)

    c0 = 4 * GLA_W + 3 * NAT_W
    cos = cos_ref[...]
    sin = sin_ref[...]

    def norm_rope(a, g):
        an = a * lax.rsqrt(_head_mean_sq(a) + EPS) * g
        return an * cos + _swap_half_heads(an) * sin

    qs = [norm_rope(acc[:, c0 + j * 128:c0 + (j + 1) * 128], qg_ref[...]) for j in range(GQA_QW // 128)]
    q_ref[...] = (jnp.concatenate(qs, axis=1) * (HEAD_DIM ** -0.5)).astype(BF16)
    k_ref[...] = norm_rope(acc[:, c0 + GQA_QW:c0 + GQA_QW + GQA_KW], kg_ref[...])
    v_ref[...] = acc[:, c0 + GQA_QW + GQA_KW:c0 + GQA_QW + 2 * GQA_KW]


def _in_proj(x, sh, sc, w, wa, ba, qg, kg, cos_t, sin_t):
    tm = TM_PROJ
    full = lambda a: pl.BlockSpec(a.shape, lambda i: (0,) * a.ndim)
    rows = lambda w_: pl.BlockSpec((tm, w_), lambda i: (i, 0))
    return pl.pallas_call(
        _inproj_kernel,
        grid=(N_TOK // tm,),
        in_specs=[rows(D_MODEL), full(sh), full(sc), full(w), full(wa), full(ba), full(qg), full(kg),
                  rows(128), rows(128)],
        out_specs=[rows(4 * GLA_W), rows(2 * GLA_W), rows(3 * NAT_W), rows(GQA_QW), rows(GQA_KW),
                   rows(GQA_KW)],
        out_shape=[jax.ShapeDtypeStruct((N_TOK, 4 * GLA_W), F32),
                   jax.ShapeDtypeStruct((N_TOK, 2 * GLA_W), F32),
                   jax.ShapeDtypeStruct((N_TOK, 3 * NAT_W), F32),
                   jax.ShapeDtypeStruct((N_TOK, GQA_QW), BF16),
                   jax.ShapeDtypeStruct((N_TOK, GQA_KW), F32),
                   jax.ShapeDtypeStruct((N_TOK, GQA_KW), F32)],
        compiler_params=pltpu.CompilerParams(dimension_semantics=("arbitrary",),
                                             vmem_limit_bytes=VMEM_LIMIT),
        name="in_proj",
    )(x, sh, sc, w, wa, ba, qg, kg, cos_t, sin_t)


def _gla_kernel(qf_ref, kf_ref, vf_ref, laf_ref, qb_ref, kb_ref, vb_ref, lab_ref, st0_ref,
                of_ref, ob_ref, stout_ref, st_ref):
    t = pl.program_id(1)
    last_t = pl.num_programs(1) - 1
    C = GLA_CHUNK
    nchunk = GLA_TB // C

    @pl.when(t == 0)
    def _():
        st_ref[...] = st0_ref[0]

    tb = GLA_TB
    shift = int(np.log2(C))
    r = lax.broadcasted_iota(jnp.int32, (tb, tb), 0)
    c = lax.broadcasted_iota(jnp.int32, (tb, tb), 1)
    same = lax.shift_right_logical(r, shift) == lax.shift_right_logical(c, shift)
    ra = lax.broadcasted_iota(jnp.int32, (tb, H_GLA * tb), 0)
    ca = lax.broadcasted_iota(jnp.int32, (tb, H_GLA * tb), 1) & (tb - 1)
    same_a = lax.shift_right_logical(ra, shift) == lax.shift_right_logical(ca, shift)
    blk = _lane_group((H_GLA * tb, GLA_W), 0, tb) == _lane_group((H_GLA * tb, GLA_W), 1, GLA_DK)
    diag = _lane_group((GLA_W, GLA_W), 0, GLA_DV) == _lane_group((GLA_W, GLA_W), 1, GLA_DK)
    one = lambda m: jnp.where(m, 1.0, 0.0).astype(BF16)
    chunk_ones = one(same)
    ref_pick = one(c == (r & ~(C - 1)) + C // 2)

    def direction(q_ref, k_ref, v_ref, la_ref, o_ref, d, causal):
        q = q_ref[...] * (GLA_DK ** -0.5)
        k = k_ref[...]
        v = v_ref[...]
        la = la_ref[...]
        la_hi = la.astype(BF16)
        la_lo = (la - la_hi.astype(F32)).astype(BF16)
        csum = lambda m: (jnp.dot(m, la_hi, preferred_element_type=F32)
                          + jnp.dot(m, la_lo, preferred_element_type=F32))
        b = csum(one(same & ((r >= c) if causal else (r <= c))))
        btot = csum(chunk_ones)
        bref = jnp.dot(ref_pick, b.astype(BF16), preferred_element_type=F32)
        q_att = q * jnp.exp(b - bref)
        k_att = k * jnp.exp(bref - b)
        kblk = jnp.where(blk, jnp.concatenate([k_att] * H_GLA, axis=0), 0.0)
        att = _dot_nt(q_att, kblk)
        att = jnp.where(same_a & ((ra >= ca) if causal else (ra <= ca)), att, 0.0)
        vblk = jnp.where(blk, jnp.concatenate([v] * H_GLA, axis=0), 0.0)
        o_intra = _dot(att, vblk)

        q_dec = (q * jnp.exp(b)).astype(BF16)
        k_dec = (k * jnp.exp(btot - b)).astype(BF16)
        g = jnp.exp(btot)
        vb = v.astype(BF16)
        st = st_ref[d]
        o_inter = [None] * nchunk
        for ci in (range(nchunk) if causal else reversed(range(nchunk))):
            rows = slice(ci * C, (ci + 1) * C)
            o_inter[ci] = _dot_nt(q_dec[rows], st)
            u = _dot_tn(vb[rows], k_dec[rows])
            st = st * g[ci * C:ci * C + 1, :] + jnp.where(diag, u, 0.0)
        st_ref[d] = st
        o_ref[...] = o_intra + jnp.concatenate(o_inter, axis=0)

    direction(qf_ref, kf_ref, vf_ref, laf_ref, of_ref, 0, True)
    direction(qb_ref, kb_ref, vb_ref, lab_ref, ob_ref, 1, False)

    @pl.when(t == last_t)
    def _():
        stout_ref[0] = st_ref[...]


def _gla(pgla, la, st0, row0, n_seq, seq_len):
    tb = GLA_TB
    nt = seq_len // tb
    b0 = row0 // tb
    fwd = lambda col: pl.BlockSpec((tb, GLA_W), lambda s, t: (b0 + s * nt + t, col))
    bwd = lambda col: pl.BlockSpec((tb, GLA_W), lambda s, t: (b0 + s * nt + nt - 1 - t, col))
    st_spec = pl.BlockSpec((1, 2, GLA_W, GLA_W), lambda s, t: (s, 0, 0, 0))
    n = n_seq * seq_len
    return pl.pallas_call(
        _gla_kernel,
        grid=(n_seq, nt),
        in_specs=[fwd(0), fwd(1), fwd(2), fwd(0), bwd(0), bwd(1), bwd(2), bwd(1), st_spec],
        out_specs=[pl.BlockSpec((tb, GLA_W), lambda s, t: (s * nt + t, 0)),
                   pl.BlockSpec((tb, GLA_W), lambda s, t: (s * nt + nt - 1 - t, 0)),
                   st_spec],
        out_shape=[jax.ShapeDtypeStruct((n, GLA_W), F32), jax.ShapeDtypeStruct((n, GLA_W), F32),
                   jax.ShapeDtypeStruct((n_seq, 2, GLA_W, GLA_W), F32)],
        scratch_shapes=[pltpu.VMEM((2, GLA_W, GLA_W), F32)],
        compiler_params=pltpu.CompilerParams(dimension_semantics=("arbitrary", "arbitrary"),
                                             vmem_limit_bytes=VMEM_LIMIT),
        name="gla_scan",
    )(pgla, pgla, pgla, la, pgla, pgla, pgla, la, st0)


def _softmax_pv(s_list, v_list):
    m = s_list[0].max(axis=-1, keepdims=True)
    for s in s_list[1:]:
        m = jnp.maximum(m, s.max(axis=-1, keepdims=True))
    acc = None
    l = None
    for s, v in zip(s_list, v_list):
        p = jnp.exp(s - m)
        pl_ = p.sum(axis=-1, keepdims=True)
        pv = _dot(p, v)
        acc = pv if acc is None else acc + pv
        l = pl_ if l is None else l + pl_
    return acc * (1.0 / l)


def _ctx_attn_kernel(pnat_ref, q_ref, k_ref, v_ref, onat_ref, ogqa_ref):
    nq = pnat_ref[:, 0:NAT_W] * (HEAD_DIM ** -0.5)
    nk = pnat_ref[:, NAT_W:2 * NAT_W].astype(BF16)
    nv = pnat_ref[:, 2 * NAT_W:3 * NAT_W].astype(BF16)
    o = _softmax_pv([_dot_nt(_stack_heads(nq, H_NAT, HEAD_DIM), nk)], [nv])
    onat_ref[...] = _unstack_heads(o, H_NAT, HEAD_DIM).astype(BF16)

    k = k_ref[...].astype(BF16)
    v = v_ref[...].astype(BF16)
    k2 = jnp.concatenate([k, k], axis=1)
    v2 = jnp.concatenate([v, v], axis=1)
    outs = []
    for half in range(2):
        q = q_ref[:, half * 256:(half + 1) * 256]
        o = _softmax_pv([_dot_nt(_stack_heads(q, 4, HEAD_DIM), k2)], [v2])
        outs.append(_unstack_heads(o, 4, HEAD_DIM))
    ogqa_ref[...] = jnp.concatenate(outs, axis=1).astype(BF16)


def _ctx_attention(pnat, q, k, v):
    rows = lambda w_: pl.BlockSpec((SEQ, w_), lambda i: (i, 0))
    return pl.pallas_call(
        _ctx_attn_kernel,
        grid=(BATCH,),
        in_specs=[rows(3 * NAT_W), rows(GQA_QW), rows(GQA_KW), rows(GQA_KW)],
        out_specs=[rows(NAT_W), rows(GQA_QW)],
        out_shape=[jax.ShapeDtypeStruct((N_CTX, NAT_W), BF16),
                   jax.ShapeDtypeStruct((N_CTX, GQA_QW), BF16)],
        compiler_params=pltpu.CompilerParams(dimension_semantics=("arbitrary",),
                                             vmem_limit_bytes=VMEM_LIMIT),
        name="ctx_attention",
    )(pnat, q, k, v)


def _gqa_lat_kernel(q_ref, k_ref, v_ref, o_ref):
    k = k_ref[0]
    v = v_ref[0]
    k2 = jnp.concatenate([k, k], axis=1)
    v2 = jnp.concatenate([v, v], axis=1)
    slot = _lane_group((TQ_GQA, 256), 1, HEAD_DIM)
    outs = []
    for half in range(2):
        q = q_ref[:, half * 256:(half + 1) * 256]
        out = jnp.zeros((TQ_GQA, 256), F32)
        for s in range(4):
            qs = jnp.where(slot == s, q, jnp.zeros_like(q))
            o = _softmax_pv([_dot_nt(qs, k2)], [v2])
            out = jnp.where(slot == s, o, out)
        outs.append(out)
    o_ref[...] = jnp.concatenate(outs, axis=1).astype(BF16)


def _gqa_latent(q, k_all, v_all):
    tq = TQ_GQA
    nq = DEC_SEQ // tq
    tk = k_all.shape[1]
    return pl.pallas_call(
        _gqa_lat_kernel,
        grid=(DEC_BATCH, nq),
        in_specs=[pl.BlockSpec((tq, GQA_QW), lambda b, i: (N_CTX // tq + b * nq + i, 0)),
                  pl.BlockSpec((1, tk, GQA_KW), lambda b, i: (b, 0, 0)),
                  pl.BlockSpec((1, tk, GQA_KW), lambda b, i: (b, 0, 0))],
        out_specs=pl.BlockSpec((tq, GQA_QW), lambda b, i: (b * nq + i, 0)),
        out_shape=jax.ShapeDtypeStruct((N_LAT, GQA_QW), BF16),
        compiler_params=pltpu.CompilerParams(dimension_semantics=("arbitrary", "arbitrary"),
                                             vmem_limit_bytes=VMEM_LIMIT),
        name="gqa_latent",
    )(q, k_all, v_all)


def _nat_lat_kernel(q_ref, k_ref, v_ref, kc_ref, vc_ref, bias_ref, o_ref):
    j = pl.program_id(1)
    kc = kc_ref[0].astype(BF16)
    vc = vc_ref[0].astype(BF16)
    for i in range(NAT_ROWS_PER_STEP):
        r = j * NAT_ROWS_PER_STEP + i
        r0 = jnp.clip(r - NAT_KH // 2, 0, ROWS - NAT_KH)
        win = pl.ds(pl.multiple_of(r0 * GRID_W, GRID_W), NAT_KH * GRID_W)
        q = q_ref[i * GRID_W:(i + 1) * GRID_W, :] * (HEAD_DIM ** -0.5)
        qs = _stack_heads(q, H_NAT, HEAD_DIM)
        s_loc = _dot_nt(qs, k_ref[win, :]) + bias_ref[r - r0]
        s_ctx = _dot_nt(qs, kc)
        o = _softmax_pv([s_loc, s_ctx], [v_ref[win, :], vc])
        o_ref[i * GRID_W:(i + 1) * GRID_W, :] = _unstack_heads(o, H_NAT, HEAD_DIM).astype(BF16)


def _nat_latent(pnat, kc, vc, bias):
    tq = NAT_ROWS_PER_STEP * GRID_W
    nq = DEC_SEQ // tq
    lat_blk = N_CTX // DEC_SEQ
    return pl.pallas_call(
        _nat_lat_kernel,
        grid=(DEC_BATCH, nq),
        in_specs=[pl.BlockSpec((tq, NAT_W), lambda b, j: (N_CTX // tq + b * nq + j, 0)),
                  pl.BlockSpec((DEC_SEQ, NAT_W), lambda b, j: (lat_blk + b, 1)),
                  pl.BlockSpec((DEC_SEQ, NAT_W), lambda b, j: (lat_blk + b, 2)),
                  pl.BlockSpec((1, PAST_LEN, NAT_W), lambda b, j: (b, 0, 0)),
                  pl.BlockSpec((1, PAST_LEN, NAT_W), lambda b, j: (b, 0, 0)),
                  pl.BlockSpec(bias.shape, lambda b, j: (0, 0, 0))],
        out_specs=pl.BlockSpec((tq, NAT_W), lambda b, j: (b * nq + j, 0)),
        out_shape=jax.ShapeDtypeStruct((N_LAT, NAT_W), BF16),
        compiler_params=pltpu.CompilerParams(dimension_semantics=("arbitrary", "arbitrary"),
                                             vmem_limit_bytes=VMEM_LIMIT),
        name="nat_latent",
    )(pnat, pnat, pnat, kc, vc, bias)


def _route_gates(logits_t, bias_ref):
    per = N_EXPERTS // N_GROUPS
    t = logits_t.shape[1]
    neg = -jnp.inf
    pos = lax.broadcasted_iota(jnp.int32, (per, t), 0)
    scores = [_sigmoid(logits_t[g * per:(g + 1) * per, :]) for g in range(N_GROUPS)]
    biased = [scores[g] + bias_ref[g * per:(g + 1) * per, :] for g in range(N_GROUPS)]

    grp = []
    for v in biased:
        m1 = jnp.max(v, axis=0, keepdims=True)
        i1 = jnp.min(jnp.where(v == m1, pos, per), axis=0, keepdims=True)
        m2 = jnp.max(jnp.where(pos == i1, neg, v), axis=0, keepdims=True)
        grp.append(m1 + m2)

    keep = [jnp.zeros((1, t), jnp.bool_) for _ in range(N_GROUPS)]
    for _ in range(TOPK_GROUPS):
        best = functools.reduce(jnp.maximum, grp)
        first = jnp.full((1, t), N_GROUPS, jnp.int32)
        for g in reversed(range(N_GROUPS)):
            first = jnp.where(grp[g] == best, g, first)
        for g in range(N_GROUPS):
            hit = first == g
            keep[g] = keep[g] | hit
            grp[g] = jnp.where(hit, neg, grp[g])

    cand = [jnp.where(keep[g], biased[g], neg) for g in range(N_GROUPS)]
    flat = [pos + g * per for g in range(N_GROUPS)]
    sel = [jnp.zeros((per, t), jnp.bool_) for _ in range(N_GROUPS)]
    for _ in range(TOP_K):
        best = functools.reduce(jnp.maximum, [jnp.max(v, axis=0, keepdims=True) for v in cand])
        first = functools.reduce(jnp.minimum, [
            jnp.min(jnp.where(cand[g] == best, flat[g], N_EXPERTS), axis=0, keepdims=True)
            for g in range(N_GROUPS)])
        for g in range(N_GROUPS):
            hit = flat[g] == first
            sel[g] = sel[g] | hit
            cand[g] = jnp.where(hit, neg, cand[g])

    w = [jnp.where(sel[g], scores[g], 0.0) for g in range(N_GROUPS)]
    total = functools.reduce(lambda a, b: a + b, [jnp.sum(v, axis=0, keepdims=True) for v in w])
    return jnp.concatenate([v / total * ROUTE_SCALE for v in w], axis=0)


def _outproj_kernel(x_ref, of_ref, ob_ref, gg_ref, onat_ref, ogqa_ref, w_ref, ng_ref, g1_ref, lng_ref,
                    lnb_ref, sh2_ref, sc2_ref, wr_ref, rb_ref, x1_ref, xm_ref, gates_ref):
    ci = _cond_row(pl.program_id(0), TM_PROJ)
    og = of_ref[...] + ob_ref[...]
    halves = [og[:, j * 128:(j + 1) * 128] for j in range(GLA_W // 128)]
    ms = jnp.concatenate([_head_mean_sq(h) for h in halves], axis=1)
    ogla = og * lax.rsqrt(ms + EPS) * ng_ref[...] * _silu(gg_ref[...])
    mix = (_dot(ogla, w_ref[0:GLA_W, :]) + _dot(onat_ref[...], w_ref[GLA_W:GLA_W + NAT_W, :])
           + _dot(ogqa_ref[...], w_ref[GLA_W + NAT_W:, :]))
    x1 = _ln(ALPHA * x_ref[...] + g1_ref[pl.ds(ci, 1), :] * mix) * lng_ref[...] + lnb_ref[...]
    x1_ref[...] = x1
    xm = (_ln(x1) * (1.0 + sc2_ref[pl.ds(ci, 1), :]) + sh2_ref[pl.ds(ci, 1), :]).astype(BF16)
    xm_ref[...] = xm
    gates_t = _route_gates(_dot_nt(wr_ref[...], xm), rb_ref)
    gates_ref[...] = jnp.concatenate([gates_t, jnp.zeros_like(gates_t)], axis=0).T


def _out_proj(x, of, ob, pgla, onat, ogqa, w, ng, g1, lng, lnb, sh2, sc2, wr, rb):
    tm = TM_PROJ
    full = lambda a: pl.BlockSpec(a.shape, lambda i: (0,) * a.ndim)
    rows = lambda w_: pl.BlockSpec((tm, w_), lambda i: (i, 0))
    return pl.pallas_call(
        _outproj_kernel,
        grid=(N_TOK // tm,),
        in_specs=[rows(D_MODEL), rows(GLA_W), rows(GLA_W), pl.BlockSpec((tm, GLA_W), lambda i: (i, 3)),
                  rows(NAT_W), rows(GQA_QW), full(w), full(ng), full(g1), full(lng), full(lnb),
                  full(sh2), full(sc2), full(wr), full(rb)],
        out_specs=[rows(D_MODEL), rows(D_MODEL), rows(128)],
        out_shape=[jax.ShapeDtypeStruct((N_TOK, D_MODEL), F32),
                   jax.ShapeDtypeStruct((N_TOK, D_MODEL), BF16),
                   jax.ShapeDtypeStruct((N_TOK, 128), F32)],
        compiler_params=pltpu.CompilerParams(dimension_semantics=("arbitrary",),
                                             vmem_limit_bytes=VMEM_LIMIT),
        name="out_proj",
    )(x, of, ob, pgla, onat, ogqa, w, ng, g1, lng, lnb, sh2, sc2, wr, rb)


def _moe_kernel(xm_ref, gates_ref, x1_ref, g2_ref, wgu_ref, wd_ref, wsgu_ref, wsd_ref, lng_ref, lnb_ref,
                out_ref, acc_ref):
    e = pl.program_id(1)
    ci = _cond_row(pl.program_id(0), TM_MOE)
    xm = xm_ref[...]

    def expert(ab, gate):
        h = _silu(ab[:, :D_EXPERT]) * ab[:, D_EXPERT:]
        return h if gate is None else h * gate

    @pl.when(e == 0)
    def _():
        acc_ref[...] = _dot(expert(jnp.dot(xm, wsgu_ref[...], preferred_element_type=F32), None),
                            wsd_ref[...])

    lane = lax.broadcasted_iota(jnp.int32, gates_ref.shape, 1)
    gate = jnp.sum(jnp.where(lane == e, gates_ref[...], 0.0), axis=1, keepdims=True)
    ab = jnp.dot(xm, wgu_ref[0].astype(BF16), preferred_element_type=F32)
    acc_ref[...] += _dot(expert(ab, gate), wd_ref[0])

    @pl.when(e == N_EXPERTS - 1)
    def _():
        y = ALPHA * x1_ref[...] + g2_ref[pl.ds(ci, 1), :] * acc_ref[...]
        out_ref[...] = _ln(y) * lng_ref[...] + lnb_ref[...]


def _moe(layer, xm, gates, x1, g2, wgu, wd, wsgu, wsd, lng, lnb):
    tm = TM_MOE
    full = lambda a: pl.BlockSpec(a.shape, lambda i, e: (0,) * a.ndim)
    rows = lambda w_: pl.BlockSpec((tm, w_), lambda i, e: (i, 0), pipeline_mode=pl.Buffered(1))
    return pl.pallas_call(
        _moe_kernel,
        grid=(N_TOK // tm, N_EXPERTS),
        in_specs=[rows(D_MODEL), rows(128), rows(D_MODEL), full(g2),
                  pl.BlockSpec((None, 1, D_MODEL, 2 * D_EXPERT), lambda i, e: (layer, e, 0, 0)),
                  pl.BlockSpec((None, 1, D_EXPERT, D_MODEL), lambda i, e: (layer, e, 0, 0)),
                  full(wsgu), full(wsd), full(lng), full(lnb)],
        out_specs=rows(D_MODEL),
        out_shape=jax.ShapeDtypeStruct((N_TOK, D_MODEL), F32),
        scratch_shapes=[pltpu.VMEM((tm, D_MODEL), F32)],
        compiler_params=pltpu.CompilerParams(dimension_semantics=("arbitrary", "arbitrary"),
                                             vmem_limit_bytes=VMEM_LIMIT),
        name="moe",
    )(xm, gates, x1, g2, wgu, wd, wsgu, wsd, lng, lnb)


def _rope_tables():
    t = jnp.arange(DEC_SEQ)
    n_freq = HEAD_DIM // 4
    freqs = ROPE_THETA ** (-jnp.arange(n_freq, dtype=F32) / n_freq)
    ang = jnp.concatenate([(t // GRID_W).astype(F32)[:, None] * freqs,
                           (t % GRID_W).astype(F32)[:, None] * freqs], -1)
    cos, sin = jnp.cos(ang), jnp.sin(ang)
    cos_h = jnp.concatenate([cos, cos], -1)
    sin_h = jnp.concatenate([-sin, sin], -1)
    lat = lambda a: jnp.tile(a, (DEC_BATCH, 128 // HEAD_DIM))
    cos_t = jnp.concatenate([jnp.ones((N_CTX, 128), F32), lat(cos_h)], 0)
    sin_t = jnp.concatenate([jnp.zeros((N_CTX, 128), F32), lat(sin_h)], 0)
    return cos_t, sin_t


def _nat_bias_table(rpb):
    n_dr, n_dc = 2 * NAT_KH - 1, 2 * NAT_KW - 1
    cidx = np.arange(GRID_W)
    dc_idx = np.clip(cidx[None, :] - cidx[:, None] + NAT_KW - 1, 0, n_dc - 1)
    col_start = np.clip(cidx - NAT_KW // 2, 0, GRID_W - NAT_KW)
    col_in = (cidx[None, :] >= col_start[:, None]) & (cidx[None, :] < col_start[:, None] + NAT_KW)
    onehot = (dc_idx.reshape(1, -1) == np.arange(n_dc)[:, None]).astype(np.float32)
    t = jnp.dot(rpb.reshape(H_NAT * n_dr, n_dc), onehot, precision=lax.Precision.HIGHEST)
    t = jnp.where(col_in[None, None], t.reshape(H_NAT, n_dr, GRID_W, GRID_W), -jnp.inf)
    bias = jnp.stack([t[:, NAT_KH - 1 - p:2 * NAT_KH - 1 - p] for p in range(NAT_KH)], axis=0)
    return bias.transpose(0, 1, 3, 2, 4).reshape(NAT_KH, H_NAT * GRID_W, NAT_KH * GRID_W)


def _prep_w_in(w_in):
    cuts = np.cumsum((0,) + IN_SPLITS)
    seg = [w_in[:, cuts[i]:cuts[i + 1]] for i in range(len(IN_SPLITS))]
    gq, gk, gv, gg, ga, nq, nk, nv, aq, ak, av = seg
    aq = aq.reshape(D_MODEL, H_GQA, HEAD_DIM)[:, np.array(GQA_SLOT_HEADS), :].reshape(D_MODEL, GQA_QW)
    ga = jnp.pad(ga, ((0, 0), (0, 128 - 2 * GLA_LOWRANK)))
    return jnp.concatenate([gq, gk, gv, gg, nq, nk, nv, aq, ak, av, ga], axis=1).astype(BF16)


def _prep_w_a(w_a2, b_a):
    wa = jnp.zeros((128, 2 * GLA_W), F32)
    wa = wa.at[0:GLA_LOWRANK, 0:GLA_W].set(w_a2[0])
    wa = wa.at[GLA_LOWRANK:2 * GLA_LOWRANK, GLA_W:].set(w_a2[1])
    return wa.astype(BF16), b_a.reshape(1, 2 * GLA_W)


def _prep_w_out(w_out):
    gqa = w_out[GLA_W + NAT_W:].reshape(H_GQA, HEAD_DIM, D_MODEL)[np.array(GQA_SLOT_HEADS)]
    return jnp.concatenate([w_out[:GLA_W + NAT_W], gqa.reshape(GQA_QW, D_MODEL)], 0).astype(BF16)


def _state_to_blockdiag(s):
    out = jnp.zeros(s.shape[:-3] + (GLA_W, GLA_W), F32)
    for h in range(H_GLA):
        out = out.at[..., h * GLA_DV:(h + 1) * GLA_DV, h * GLA_DK:(h + 1) * GLA_DK].set(
            jnp.swapaxes(s[..., h, :, :], -1, -2))
    return out


def _blockdiag_to_state(st):
    blocks = [st[..., h * GLA_DV:(h + 1) * GLA_DV, h * GLA_DK:(h + 1) * GLA_DK] for h in range(H_GLA)]
    return jnp.swapaxes(jnp.stack(blocks, axis=-3), -1, -2)


def kernel(x_prompt, x_sample, state_gla, cache_nat_k, cache_nat_v, cache_gqa_k, cache_gqa_v, c, c_ctx, w_mod, b_mod, w_in, gla_w_a2, gla_b_a, gla_norm_g, nat_rpb, gqa_q_norm_g, gqa_k_norm_g, w_out, ln1_g, ln1_b, w_router, router_bias, w_expert_gu, w_expert_down, w_shared_gu, w_shared_down, ln2_g, ln2_b):
    x = jnp.concatenate([x_prompt.reshape(N_CTX, D_MODEL), x_sample.reshape(N_LAT, D_MODEL)], axis=0)
    cond = jnp.concatenate([c_ctx[None, :], c, jnp.zeros((N_COND - 1 - DEC_BATCH, D_MODEL), F32)], axis=0)
    mods = _modulation(cond, w_mod, b_mod)
    cos_t, sin_t = _rope_tables()
    row = lambda a: a.reshape(1, -1)

    st_gla, st_nk, st_nv, st_gk, st_gv = [], [], [], [], []
    for l in range(DEPTH):
        sh1, sc1, g1, sh2, sc2, g2 = [mods[l, :, j * D_MODEL:(j + 1) * D_MODEL] for j in range(6)]
        wa, ba = _prep_w_a(gla_w_a2[l], gla_b_a[l])
        qg = row(jnp.tile(gqa_q_norm_g[l], 128 // HEAD_DIM))
        kg = row(jnp.tile(gqa_k_norm_g[l], 128 // HEAD_DIM))
        pgla, la, pnat, q, k, v = _in_proj(x, sh1, sc1, _prep_w_in(w_in[l]), wa, ba, qg, kg, cos_t, sin_t)

        zero_st = jnp.zeros((BATCH, 2, GLA_W, GLA_W), F32)
        of_c, ob_c, st_c = _gla(pgla, la, zero_st, 0, BATCH, SEQ)
        of_l, ob_l, _ = _gla(pgla, la, _state_to_blockdiag(state_gla[:, l]), N_CTX, DEC_BATCH, DEC_SEQ)

        onat_c, ogqa_c = _ctx_attention(pnat, q, k, v)
        k_all = jnp.concatenate([k[N_CTX:].reshape(DEC_BATCH, DEC_SEQ, GQA_KW),
                                 cache_gqa_k[:, l].reshape(DEC_BATCH, PAST_LEN, GQA_KW)], axis=1).astype(BF16)
        v_all = jnp.concatenate([v[N_CTX:].reshape(DEC_BATCH, DEC_SEQ, GQA_KW),
                                 cache_gqa_v[:, l].reshape(DEC_BATCH, PAST_LEN, GQA_KW)], axis=1).astype(BF16)
        ogqa_l = _gqa_latent(q, k_all, v_all)
        onat_l = _nat_latent(pnat, cache_nat_k[:, l].reshape(DEC_BATCH, PAST_LEN, NAT_W),
                             cache_nat_v[:, l].reshape(DEC_BATCH, PAST_LEN, NAT_W),
                             _nat_bias_table(nat_rpb[l]))

        x1, xm, gates = _out_proj(
            x, jnp.concatenate([of_c, of_l], 0), jnp.concatenate([ob_c, ob_l], 0), pgla,
            jnp.concatenate([onat_c, onat_l], 0), jnp.concatenate([ogqa_c, ogqa_l], 0),
            _prep_w_out(w_out[l]), row(jnp.tile(gla_norm_g[l], H_GLA)), g1, row(ln1_g[l]), row(ln1_b[l]),
            sh2, sc2, w_router[l].T.astype(BF16), router_bias[l].reshape(N_EXPERTS, 1))
        x = _moe(l, xm, gates, x1, g2, w_expert_gu, w_expert_down, w_shared_gu[l].astype(BF16),
                 w_shared_down[l].astype(BF16), row(ln2_g[l]), row(ln2_b[l]))

        st_gla.append(_blockdiag_to_state(st_c))
        st_nk.append(pnat[:N_CTX, NAT_W:2 * NAT_W].reshape(BATCH, SEQ, H_NAT, HEAD_DIM))
        st_nv.append(pnat[:N_CTX, 2 * NAT_W:].reshape(BATCH, SEQ, H_NAT, HEAD_DIM))
        st_gk.append(k[:N_CTX].reshape(BATCH, SEQ, KV_GQA, HEAD_DIM))
        st_gv.append(v[:N_CTX].reshape(BATCH, SEQ, KV_GQA, HEAD_DIM))

    y_prompt = x[:N_CTX].reshape(BATCH, SEQ, D_MODEL)
    y_sample = x[N_CTX:].reshape(DEC_BATCH, DEC_SEQ, D_MODEL)
    return (y_prompt, y_sample, jnp.stack(st_gla, axis=1), jnp.stack(st_nk, axis=1), jnp.stack(st_nv, axis=1),
            jnp.stack(st_gk, axis=1), jnp.stack(st_gv, axis=1))
```

```python
import functools

import numpy as np
import jax
import jax.numpy as jnp
from jax import lax
from jax.experimental import pallas as pl
from jax.experimental.pallas import tpu as pltpu

D_MODEL = 1024
BATCH = 16
SEQ = 256
DEPTH = 2
DEC_BATCH = 2
DEC_SEQ = 4096
PAST_LEN = 512
GRID_W = 64
HEAD_DIM = 64
H_GLA = 4
GLA_DK = 64
GLA_DV = 64
GLA_LOWRANK = 16
GLA_TAU = 16.0
H_NAT = 4
NAT_KH = 8
NAT_KW = 16
H_GQA = 8
KV_GQA = 2
ROPE_THETA = 10000.0
N_EXPERTS = 64
TOP_K = 8
N_GROUPS = 8
TOPK_GROUPS = 4
D_EXPERT = 256
D_SHARED = 256
ROUTE_SCALE = 2.5
EPS = 1e-6
ALPHA = (2 * DEPTH) ** 0.25
IN_SPLITS = (H_GLA * GLA_DK, H_GLA * GLA_DK, H_GLA * GLA_DV, H_GLA * GLA_DV, 2 * GLA_LOWRANK,
             H_NAT * HEAD_DIM, H_NAT * HEAD_DIM, H_NAT * HEAD_DIM,
             H_GQA * HEAD_DIM, KV_GQA * HEAD_DIM, KV_GQA * HEAD_DIM)

F32 = jnp.float32
BF16 = jnp.bfloat16

N_CTX = BATCH * SEQ
N_LAT = DEC_BATCH * DEC_SEQ
N_TOK = N_CTX + N_LAT
N_COND = 8
GLA_W = H_GLA * GLA_DK
NAT_W = H_NAT * HEAD_DIM
GQA_QW = H_GQA * HEAD_DIM
GQA_KW = KV_GQA * HEAD_DIM
W_IN_COLS = 4 * GLA_W + 3 * NAT_W + GQA_QW + 2 * GQA_KW + 128
GA_COL = W_IN_COLS - 128
GQA_SLOT_HEADS = (0, 4, 1, 5, 2, 6, 3, 7)
GLA_CHUNK = 32
GLA_TB = 256
ROWS = DEC_SEQ // GRID_W
NAT_ROWS_PER_STEP = 8
VMEM_LIMIT = 56 * 1024 * 1024

TM_PROJ = 256
MOE_TILE = 256
MOE_NT = N_TOK // MOE_TILE
PIECE = 8
MOE_CAP = MOE_TILE * TOP_K + N_EXPERTS * PIECE
ROW_TILE = 256
XS_ROWS = N_TOK * TOP_K + MOE_NT * N_EXPERTS * (PIECE - 1)
assert XS_ROWS % ROW_TILE == 0
N_ITEMS = XS_ROWS // ROW_TILE + N_EXPERTS
TQ_GQA = 256


def _dot(a, b):
    return jnp.dot(a.astype(BF16), b.astype(BF16), preferred_element_type=F32)


def _dot_nt(a, b):
    return lax.dot_general(a.astype(BF16), b.astype(BF16), (((1,), (1,)), ((), ())),
                           preferred_element_type=F32)


def _dot_tn(a, b):
    return lax.dot_general(a.astype(BF16), b.astype(BF16), (((0,), (0,)), ((), ())),
                           preferred_element_type=F32)


def _dot_split(a, b_bf16):
    hi = a.astype(BF16)
    lo = (a - hi.astype(F32)).astype(BF16)
    return (jnp.dot(hi, b_bf16, preferred_element_type=F32)
            + jnp.dot(lo, b_bf16, preferred_element_type=F32))


def _sigmoid(x):
    return 1.0 / (1.0 + jnp.exp(-x))


def _silu(x):
    return x * _sigmoid(x)


def _ln(x):
    xc = x - jnp.mean(x, axis=-1, keepdims=True)
    return xc * lax.rsqrt(jnp.mean(xc * xc, axis=-1, keepdims=True) + EPS)


def _lane_group(shape, axis, width):
    return lax.shift_right_logical(lax.broadcasted_iota(jnp.int32, shape, axis), int(np.log2(width)))


def _head_mean_sq(x):
    w = x.shape[-1]
    bmat = jnp.where(_lane_group((w, w), 0, HEAD_DIM) == _lane_group((w, w), 1, HEAD_DIM),
                     1.0 / HEAD_DIM, 0.0).astype(BF16)
    return _dot_split(x * x, bmat)


def _swap_half_heads(x):
    w = x.shape[-1]
    lane = lax.broadcasted_iota(jnp.int32, x.shape, x.ndim - 1)
    first = (lane & (HEAD_DIM - 1)) < HEAD_DIM // 2
    return jnp.where(first, pltpu.roll(x, w - HEAD_DIM // 2, x.ndim - 1),
                     pltpu.roll(x, HEAD_DIM // 2, x.ndim - 1))


def _cond_row(tile, tm):
    r = tile * tm
    return jnp.where(r < N_CTX, 0, 1 + (r - N_CTX) // DEC_SEQ)


def _stack_heads(x, n, width):
    head = _lane_group(x.shape, 1, width)
    return jnp.concatenate([jnp.where(head == h, x, jnp.zeros_like(x)) for h in range(n)], axis=0)


def _unstack_heads(o, n, width):
    m = o.shape[0] // n
    head = _lane_group((m, o.shape[1]), 1, width)
    out = jnp.zeros((m, o.shape[1]), o.dtype)
    for h in range(n):
        out = jnp.where(head == h, o[h * m:(h + 1) * m], out)
    return out


def _mod_kernel(c_ref, w_ref, b_ref, o_ref):
    o_ref[0] = _dot(_silu(c_ref[...]), w_ref[0]) + b_ref[0]


def _modulation(cond, w_mod, b_mod):
    tn = 1536
    return pl.pallas_call(
        _mod_kernel,
        grid=(DEPTH, 6 * D_MODEL // tn),
        in_specs=[pl.BlockSpec((N_COND, D_MODEL), lambda l, j: (0, 0)),
                  pl.BlockSpec((1, D_MODEL, tn), lambda l, j: (l, 0, j)),
                  pl.BlockSpec((1, 1, tn), lambda l, j: (l, 0, j))],
        out_specs=pl.BlockSpec((1, N_COND, tn), lambda l, j: (l, 0, j)),
        out_shape=jax.ShapeDtypeStruct((DEPTH, N_COND, 6 * D_MODEL), F32),
        compiler_params=pltpu.CompilerParams(dimension_semantics=("arbitrary", "arbitrary"),
                                             vmem_limit_bytes=VMEM_LIMIT),
        name="modulation",
    )(cond, w_mod, b_mod.reshape(DEPTH, 1, 6 * D_MODEL))


def _inproj_kernel(x_ref, sh_ref, sc_ref, w_ref, wa_ref, ba_ref, qg_ref, kg_ref, cos_ref, sin_ref,
                   pgla_ref, la_ref, pnat_ref, q_ref, k_ref, v_ref):
    ci = _cond_row(pl.program_id(0), TM_PROJ)
    xn = _ln(x_ref[...])
    xm = xn * (1.0 + sc_ref[pl.ds(ci, 1), :]) + sh_ref[pl.ds(ci, 1), :]
    acc = _dot(xm, w_ref[...])
    pgla_ref[...] = acc[:, 0:4 * GLA_W]
    pnat_ref[...] = acc[:, 4 * GLA_W:4 * GLA_W + 3 * NAT_W]
    z = _dot(acc[:, GA_COL:GA_COL + 128], wa_ref[...]) + ba_ref[...]
    la_ref[...] = (jnp.minimum(z, 0.0) - jnp.log(1.0 + jnp.exp(-jnp.abs(z)))) * (1.0 / GLA_TAU)

    c0 = 4 * GLA_W + 3 * NAT_W
    cos = cos_ref[...]
    sin = sin_ref[...]

    def norm_rope(a, g):
        an = a * lax.rsqrt(_head_mean_sq(a) + EPS) * g
        return an * cos + _swap_half_heads(an) * sin

    qs = [norm_rope(acc[:, c0 + j * 128:c0 + (j + 1) * 128], qg_ref[...]) for j in range(GQA_QW // 128)]
    q_ref[...] = (jnp.concatenate(qs, axis=1) * (HEAD_DIM ** -0.5)).astype(BF16)
    k_ref[...] = norm_rope(acc[:, c0 + GQA_QW:c0 + GQA_QW + GQA_KW], kg_ref[...])
    v_ref[...] = acc[:, c0 + GQA_QW + GQA_KW:c0 + GQA_QW + 2 * GQA_KW]


def _in_proj(x, sh, sc, w, wa, ba, qg, kg, cos_t, sin_t):
    tm = TM_PROJ
    full = lambda a: pl.BlockSpec(a.shape, lambda i: (0,) * a.ndim)
    rows = lambda w_: pl.BlockSpec((tm, w_), lambda i: (i, 0))
    return pl.pallas_call(
        _inproj_kernel,
        grid=(N_TOK // tm,),
        in_specs=[rows(D_MODEL), full(sh), full(sc), full(w), full(wa), full(ba), full(qg), full(kg),
                  rows(128), rows(128)],
        out_specs=[rows(4 * GLA_W), rows(2 * GLA_W), rows(3 * NAT_W), rows(GQA_QW), rows(GQA_KW),
                   rows(GQA_KW)],
        out_shape=[jax.ShapeDtypeStruct((N_TOK, 4 * GLA_W), F32),
                   jax.ShapeDtypeStruct((N_TOK, 2 * GLA_W), F32),
                   jax.ShapeDtypeStruct((N_TOK, 3 * NAT_W), F32),
                   jax.ShapeDtypeStruct((N_TOK, GQA_QW), BF16),
                   jax.ShapeDtypeStruct((N_TOK, GQA_KW), F32),
                   jax.ShapeDtypeStruct((N_TOK, GQA_KW), F32)],
        compiler_params=pltpu.CompilerParams(dimension_semantics=("arbitrary",),
                                             vmem_limit_bytes=VMEM_LIMIT),
        name="in_proj",
    )(x, sh, sc, w, wa, ba, qg, kg, cos_t, sin_t)


def _gla_kernel(qf_ref, kf_ref, vf_ref, laf_ref, qb_ref, kb_ref, vb_ref, lab_ref, st0_ref,
                of_ref, ob_ref, stout_ref, st_ref):
    t = pl.program_id(1)
    last_t = pl.num_programs(1) - 1
    C = GLA_CHUNK
    nchunk = GLA_TB // C

    @pl.when(t == 0)
    def _():
        st_ref[...] = st0_ref[0]

    tb = GLA_TB
    shift = int(np.log2(C))
    r = lax.broadcasted_iota(jnp.int32, (tb, tb), 0)
    c = lax.broadcasted_iota(jnp.int32, (tb, tb), 1)
    same = lax.shift_right_logical(r, shift) == lax.shift_right_logical(c, shift)
    ra = lax.broadcasted_iota(jnp.int32, (tb, H_GLA * tb), 0)
    ca = lax.broadcasted_iota(jnp.int32, (tb, H_GLA * tb), 1) & (tb - 1)
    same_a = lax.shift_right_logical(ra, shift) == lax.shift_right_logical(ca, shift)
    blk = _lane_group((H_GLA * tb, GLA_W), 0, tb) == _lane_group((H_GLA * tb, GLA_W), 1, GLA_DK)
    diag = _lane_group((GLA_W, GLA_W), 0, GLA_DV) == _lane_group((GLA_W, GLA_W), 1, GLA_DK)
    one = lambda m: jnp.where(m, 1.0, 0.0).astype(BF16)
    chunk_ones = one(same)
    ref_pick = one(c == (r & ~(C - 1)) + C // 2)

    def direction(q_ref, k_ref, v_ref, la_ref, o_ref, d, causal):
        q = q_ref[...] * (GLA_DK ** -0.5)
        k = k_ref[...]
        v = v_ref[...]
        la = la_ref[...]
        la_hi = la.astype(BF16)
        la_lo = (la - la_hi.astype(F32)).astype(BF16)
        csum = lambda m: (jnp.dot(m, la_hi, preferred_element_type=F32)
                          + jnp.dot(m, la_lo, preferred_element_type=F32))
        b = csum(one(same & ((r >= c) if causal else (r <= c))))
        btot = csum(chunk_ones)
        bref = jnp.dot(ref_pick, b.astype(BF16), preferred_element_type=F32)
        q_att = q * jnp.exp(b - bref)
        k_att = k * jnp.exp(bref - b)
        kblk = jnp.where(blk, jnp.concatenate([k_att] * H_GLA, axis=0), 0.0)
        att = _dot_nt(q_att, kblk)
        att = jnp.where(same_a & ((ra >= ca) if causal else (ra <= ca)), att, 0.0)
        vblk = jnp.where(blk, jnp.concatenate([v] * H_GLA, axis=0), 0.0)
        o_intra = _dot(att, vblk)

        q_dec = (q * jnp.exp(b)).astype(BF16)
        k_dec = (k * jnp.exp(btot - b)).astype(BF16)
        g = jnp.exp(btot)
        vb = v.astype(BF16)
        st = st_ref[d]
        o_inter = [None] * nchunk
        for ci in (range(nchunk) if causal else reversed(range(nchunk))):
            rows = slice(ci * C, (ci + 1) * C)
            o_inter[ci] = _dot_nt(q_dec[rows], st)
            u = _dot_tn(vb[rows], k_dec[rows])
            st = st * g[ci * C:ci * C + 1, :] + jnp.where(diag, u, 0.0)
        st_ref[d] = st
        o_ref[...] = o_intra + jnp.concatenate(o_inter, axis=0)

    direction(qf_ref, kf_ref, vf_ref, laf_ref, of_ref, 0, True)
    direction(qb_ref, kb_ref, vb_ref, lab_ref, ob_ref, 1, False)

    @pl.when(t == last_t)
    def _():
        stout_ref[0] = st_ref[...]


def _gla(pgla, la, st0, row0, n_seq, seq_len):
    tb = GLA_TB
    nt = seq_len // tb
    b0 = row0 // tb
    fwd = lambda col: pl.BlockSpec((tb, GLA_W), lambda s, t: (b0 + s * nt + t, col))
    bwd = lambda col: pl.BlockSpec((tb, GLA_W), lambda s, t: (b0 + s * nt + nt - 1 - t, col))
    st_spec = pl.BlockSpec((1, 2, GLA_W, GLA_W), lambda s, t: (s, 0, 0, 0))
    n = n_seq * seq_len
    return pl.pallas_call(
        _gla_kernel,
        grid=(n_seq, nt),
        in_specs=[fwd(0), fwd(1), fwd(2), fwd(0), bwd(0), bwd(1), bwd(2), bwd(1), st_spec],
        out_specs=[pl.BlockSpec((tb, GLA_W), lambda s, t: (s * nt + t, 0)),
                   pl.BlockSpec((tb, GLA_W), lambda s, t: (s * nt + nt - 1 - t, 0)),
                   st_spec],
        out_shape=[jax.ShapeDtypeStruct((n, GLA_W), F32), jax.ShapeDtypeStruct((n, GLA_W), F32),
                   jax.ShapeDtypeStruct((n_seq, 2, GLA_W, GLA_W), F32)],
        scratch_shapes=[pltpu.VMEM((2, GLA_W, GLA_W), F32)],
        compiler_params=pltpu.CompilerParams(dimension_semantics=("arbitrary", "arbitrary"),
                                             vmem_limit_bytes=VMEM_LIMIT),
        name="gla_scan",
    )(pgla, pgla, pgla, la, pgla, pgla, pgla, la, st0)


def _softmax_pv(s_list, v_list):
    m = s_list[0].max(axis=-1, keepdims=True)
    for s in s_list[1:]:
        m = jnp.maximum(m, s.max(axis=-1, keepdims=True))
    acc = None
    l = None
    for s, v in zip(s_list, v_list):
        p = jnp.exp(s - m)
        pl_ = p.sum(axis=-1, keepdims=True)
        pv = _dot(p, v)
        acc = pv if acc is None else acc + pv
        l = pl_ if l is None else l + pl_
    return acc * (1.0 / l)


def _ctx_attn_kernel(pnat_ref, q_ref, k_ref, v_ref, onat_ref, ogqa_ref):
    nq = pnat_ref[:, 0:NAT_W] * (HEAD_DIM ** -0.5)
    nk = pnat_ref[:, NAT_W:2 * NAT_W].astype(BF16)
    nv = pnat_ref[:, 2 * NAT_W:3 * NAT_W].astype(BF16)
    o = _softmax_pv([_dot_nt(_stack_heads(nq, H_NAT, HEAD_DIM), nk)], [nv])
    onat_ref[...] = _unstack_heads(o, H_NAT, HEAD_DIM).astype(BF16)

    k = k_ref[...].astype(BF16)
    v = v_ref[...].astype(BF16)
    k2 = jnp.concatenate([k, k], axis=1)
    v2 = jnp.concatenate([v, v], axis=1)
    outs = []
    for half in range(2):
        q = q_ref[:, half * 256:(half + 1) * 256]
        o = _softmax_pv([_dot_nt(_stack_heads(q, 4, HEAD_DIM), k2)], [v2])
        outs.append(_unstack_heads(o, 4, HEAD_DIM))
    ogqa_ref[...] = jnp.concatenate(outs, axis=1).astype(BF16)


def _ctx_attention(pnat, q, k, v):
    rows = lambda w_: pl.BlockSpec((SEQ, w_), lambda i: (i, 0))
    return pl.pallas_call(
        _ctx_attn_kernel,
        grid=(BATCH,),
        in_specs=[rows(3 * NAT_W), rows(GQA_QW), rows(GQA_KW), rows(GQA_KW)],
        out_specs=[rows(NAT_W), rows(GQA_QW)],
        out_shape=[jax.ShapeDtypeStruct((N_CTX, NAT_W), BF16),
                   jax.ShapeDtypeStruct((N_CTX, GQA_QW), BF16)],
        compiler_params=pltpu.CompilerParams(dimension_semantics=("arbitrary",),
                                             vmem_limit_bytes=VMEM_LIMIT),
        name="ctx_attention",
    )(pnat, q, k, v)


def _gqa_lat_kernel(q_ref, k_ref, v_ref, o_ref):
    k = k_ref[0]
    v = v_ref[0]
    k2 = jnp.concatenate([k, k], axis=1)
    v2 = jnp.concatenate([v, v], axis=1)
    slot = _lane_group((TQ_GQA, 256), 1, HEAD_DIM)
    outs = []
    for half in range(2):
        q = q_ref[:, half * 256:(half + 1) * 256]
        out = jnp.zeros((TQ_GQA, 256), F32)
        for s in range(4):
            qs = jnp.where(slot == s, q, jnp.zeros_like(q))
            o = _softmax_pv([_dot_nt(qs, k2)], [v2])
            out = jnp.where(slot == s, o, out)
        outs.append(out)
    o_ref[...] = jnp.concatenate(outs, axis=1).astype(BF16)


def _gqa_latent(q, k_all, v_all):
    tq = TQ_GQA
    nq = DEC_SEQ // tq
    tk = k_all.shape[1]
    return pl.pallas_call(
        _gqa_lat_kernel,
        grid=(DEC_BATCH, nq),
        in_specs=[pl.BlockSpec((tq, GQA_QW), lambda b, i: (N_CTX // tq + b * nq + i, 0)),
                  pl.BlockSpec((1, tk, GQA_KW), lambda b, i: (b, 0, 0)),
                  pl.BlockSpec((1, tk, GQA_KW), lambda b, i: (b, 0, 0))],
        out_specs=pl.BlockSpec((tq, GQA_QW), lambda b, i: (b * nq + i, 0)),
        out_shape=jax.ShapeDtypeStruct((N_LAT, GQA_QW), BF16),
        compiler_params=pltpu.CompilerParams(dimension_semantics=("arbitrary", "arbitrary"),
                                             vmem_limit_bytes=VMEM_LIMIT),
        name="gqa_latent",
    )(q, k_all, v_all)


def _nat_lat_kernel(q_ref, k_ref, v_ref, kc_ref, vc_ref, bias_ref, o_ref):
    j = pl.program_id(1)
    kc = kc_ref[0].astype(BF16)
    vc = vc_ref[0].astype(BF16)
    for i in range(NAT_ROWS_PER_STEP):
        r = j * NAT_ROWS_PER_STEP + i
        r0 = jnp.clip(r - NAT_KH // 2, 0, ROWS - NAT_KH)
        win = pl.ds(pl.multiple_of(r0 * GRID_W, GRID_W), NAT_KH * GRID_W)
        q = q_ref[i * GRID_W:(i + 1) * GRID_W, :] * (HEAD_DIM ** -0.5)
        qs = _stack_heads(q, H_NAT, HEAD_DIM)
        s_loc = _dot_nt(qs, k_ref[win, :]) + bias_ref[r - r0]
        s_ctx = _dot_nt(qs, kc)
        o = _softmax_pv([s_loc, s_ctx], [v_ref[win, :], vc])
        o_ref[i * GRID_W:(i + 1) * GRID_W, :] = _unstack_heads(o, H_NAT, HEAD_DIM).astype(BF16)


def _nat_latent(pnat, kc, vc, bias):
    tq = NAT_ROWS_PER_STEP * GRID_W
    nq = DEC_SEQ // tq
    lat_blk = N_CTX // DEC_SEQ
    return pl.pallas_call(
        _nat_lat_kernel,
        grid=(DEC_BATCH, nq),
        in_specs=[pl.BlockSpec((tq, NAT_W), lambda b, j: (N_CTX // tq + b * nq + j, 0)),
                  pl.BlockSpec((DEC_SEQ, NAT_W), lambda b, j: (lat_blk + b, 1)),
                  pl.BlockSpec((DEC_SEQ, NAT_W), lambda b, j: (lat_blk + b, 2)),
                  pl.BlockSpec((1, PAST_LEN, NAT_W), lambda b, j: (b, 0, 0)),
                  pl.BlockSpec((1, PAST_LEN, NAT_W), lambda b, j: (b, 0, 0)),
                  pl.BlockSpec(bias.shape, lambda b, j: (0, 0, 0))],
        out_specs=pl.BlockSpec((tq, NAT_W), lambda b, j: (b * nq + j, 0)),
        out_shape=jax.ShapeDtypeStruct((N_LAT, NAT_W), BF16),
        compiler_params=pltpu.CompilerParams(dimension_semantics=("arbitrary", "arbitrary"),
                                             vmem_limit_bytes=VMEM_LIMIT),
        name="nat_latent",
    )(pnat, pnat, pnat, kc, vc, bias)


def _route_gates(logits_t, bias_ref):
    per = N_EXPERTS // N_GROUPS
    t = logits_t.shape[1]
    neg = -jnp.inf
    pos = lax.broadcasted_iota(jnp.int32, (per, t), 0)
    scores = [_sigmoid(logits_t[g * per:(g + 1) * per, :]) for g in range(N_GROUPS)]
    biased = [scores[g] + bias_ref[g * per:(g + 1) * per, :] for g in range(N_GROUPS)]

    grp = []
    for v in biased:
        m1 = jnp.max(v, axis=0, keepdims=True)
        i1 = jnp.min(jnp.where(v == m1, pos, per), axis=0, keepdims=True)
        m2 = jnp.max(jnp.where(pos == i1, neg, v), axis=0, keepdims=True)
        grp.append(m1 + m2)

    keep = [jnp.zeros((1, t), jnp.bool_) for _ in range(N_GROUPS)]
    for _ in range(TOPK_GROUPS):
        best = functools.reduce(jnp.maximum, grp)
        first = jnp.full((1, t), N_GROUPS, jnp.int32)
        for g in reversed(range(N_GROUPS)):
            first = jnp.where(grp[g] == best, g, first)
        for g in range(N_GROUPS):
            hit = first == g
            keep[g] = keep[g] | hit
            grp[g] = jnp.where(hit, neg, grp[g])

    cand = [jnp.where(keep[g], biased[g], neg) for g in range(N_GROUPS)]
    flat = [pos + g * per for g in range(N_GROUPS)]
    picks, weights = [], []
    for _ in range(TOP_K):
        best = functools.reduce(jnp.maximum, [jnp.max(v, axis=0, keepdims=True) for v in cand])
        first = functools.reduce(jnp.minimum, [
            jnp.min(jnp.where(cand[g] == best, flat[g], N_EXPERTS), axis=0, keepdims=True)
            for g in range(N_GROUPS)])
        score = jnp.zeros((1, t), F32)
        for g in range(N_GROUPS):
            hit = flat[g] == first
            score = score + jnp.sum(jnp.where(hit, scores[g], 0.0), axis=0, keepdims=True)
            cand[g] = jnp.where(hit, neg, cand[g])
        picks.append(first)
        weights.append(score)

    total = functools.reduce(lambda a, b: a + b, weights)
    return (jnp.concatenate(picks, axis=0),
            jnp.concatenate([v / total * ROUTE_SCALE for v in weights], axis=0))


def _outproj_kernel(x_ref, of_ref, ob_ref, gg_ref, onat_ref, ogqa_ref, w_ref, ng_ref, g1_ref, lng_ref,
                    lnb_ref, sh2_ref, sc2_ref, wr_ref, rb_ref, x1_ref, xm_ref, idx_ref, gate_ref):
    ci = _cond_row(pl.program_id(0), TM_PROJ)
    og = of_ref[...] + ob_ref[...]
    halves = [og[:, j * 128:(j + 1) * 128] for j in range(GLA_W // 128)]
    ms = jnp.concatenate([_head_mean_sq(h) for h in halves], axis=1)
    ogla = og * lax.rsqrt(ms + EPS) * ng_ref[...] * _silu(gg_ref[...])
    mix = (_dot(ogla, w_ref[0:GLA_W, :]) + _dot(onat_ref[...], w_ref[GLA_W:GLA_W + NAT_W, :])
           + _dot(ogqa_ref[...], w_ref[GLA_W + NAT_W:, :]))
    x1 = _ln(ALPHA * x_ref[...] + g1_ref[pl.ds(ci, 1), :] * mix) * lng_ref[...] + lnb_ref[...]
    x1_ref[...] = x1
    xm = (_ln(x1) * (1.0 + sc2_ref[pl.ds(ci, 1), :]) + sh2_ref[pl.ds(ci, 1), :]).astype(BF16)
    xm_ref[...] = xm
    idx_ref[...], gate_ref[...] = _route_gates(_dot_nt(wr_ref[...], xm), rb_ref)


def _out_proj(x, of, ob, pgla, onat, ogqa, w, ng, g1, lng, lnb, sh2, sc2, wr, rb):
    tm = TM_PROJ
    full = lambda a: pl.BlockSpec(a.shape, lambda i: (0,) * a.ndim)
    rows = lambda w_: pl.BlockSpec((tm, w_), lambda i: (i, 0))
    return pl.pallas_call(
        _outproj_kernel,
        grid=(N_TOK // tm,),
        in_specs=[rows(D_MODEL), rows(GLA_W), rows(GLA_W), pl.BlockSpec((tm, GLA_W), lambda i: (i, 3)),
                  rows(NAT_W), rows(GQA_QW), full(w), full(ng), full(g1), full(lng), full(lnb),
                  full(sh2), full(sc2), full(wr), full(rb)],
        out_specs=[rows(D_MODEL), rows(D_MODEL), pl.BlockSpec((TOP_K, tm), lambda i: (0, i)),
                   pl.BlockSpec((TOP_K, tm), lambda i: (0, i))],
        out_shape=[jax.ShapeDtypeStruct((N_TOK, D_MODEL), F32),
                   jax.ShapeDtypeStruct((N_TOK, D_MODEL), BF16),
                   jax.ShapeDtypeStruct((TOP_K, N_TOK), jnp.int32),
                   jax.ShapeDtypeStruct((TOP_K, N_TOK), F32)],
        compiler_params=pltpu.CompilerParams(dimension_semantics=("arbitrary",),
                                             vmem_limit_bytes=VMEM_LIMIT),
        name="out_proj",
    )(x, of, ob, pgla, onat, ogqa, w, ng, g1, lng, lnb, sh2, sc2, wr, rb)


def _plan_specs(n_prefetch, **kw):
    return pltpu.PrefetchScalarGridSpec(num_scalar_prefetch=n_prefetch, **kw)


def _for_each_piece(t, npc_ref, lo_ref, hs_ref, fn):
    def per_expert(e, carry):
        j = t * N_EXPERTS + e

        def per_piece(i, c):
            fn(pl.multiple_of(lo_ref[j] + i * PIECE, PIECE), pl.multiple_of(hs_ref[j] + i * PIECE, PIECE))
            return c

        return lax.fori_loop(0, npc_ref[j], per_piece, carry)

    lax.fori_loop(0, N_EXPERTS, per_expert, 0)


def _dispatch_kernel(hs_ref, npc_ref, lo_ref, tp_ref, tot_ref, xm_ref, ldt_ref, xs_ref, stage_ref, zero_ref,
                     sem, tail_sem):
    t = pl.program_id(0)
    nt = pl.num_programs(0)
    slot = lax.rem(t, 2)

    def piece_copy(local_row, xs_row, s):
        return pltpu.make_async_copy(stage_ref.at[s, pl.ds(local_row, PIECE)],
                                     xs_ref.at[pl.ds(xs_row, PIECE)], sem.at[s])

    def wait_tile(tt, s):
        def one(i, c):
            piece_copy(0, 0, s).wait()
            return c
        lax.fori_loop(0, tp_ref[tt], one, 0)

    @pl.when(t >= 2)
    def _():
        wait_tile(t - 2, slot)

    p = lax.broadcasted_iota(jnp.int32, (MOE_CAP, MOE_TILE), 0)
    hit = p == ldt_ref[0:1, :]
    for k in range(1, TOP_K):
        hit = hit | (p == ldt_ref[k:k + 1, :])
    rows = jnp.dot(jnp.where(hit, 1.0, 0.0).astype(BF16), xm_ref[...], preferred_element_type=F32)
    bits = lax.bitcast_convert_type(rows, jnp.uint32)
    stage_ref[slot] = (bits[:, :D_MODEL // 2] >> jnp.uint32(16)) | (bits[:, D_MODEL // 2:] & jnp.uint32(0xFFFF0000))
    _for_each_piece(t, npc_ref, lo_ref, hs_ref, lambda a, b: piece_copy(a, b, slot).start())

    @pl.when(t == nt - 1)
    def _():
        zero_ref[...] = jnp.zeros_like(zero_ref)
        total = tot_ref[0]
        first_free_tile = (total + ROW_TILE - 1) // ROW_TILE
        n_piece = (first_free_tile * ROW_TILE - total) // PIECE
        n_tile = XS_ROWS // ROW_TILE - first_free_tile

        def zero_piece(i):
            return pltpu.make_async_copy(zero_ref.at[pl.ds(0, PIECE)],
                                         xs_ref.at[pl.ds(pl.multiple_of(total + i * PIECE, PIECE), PIECE)],
                                         tail_sem.at[0])

        def zero_tile(i):
            row = pl.multiple_of((first_free_tile + i) * ROW_TILE, ROW_TILE)
            return pltpu.make_async_copy(zero_ref, xs_ref.at[pl.ds(row, ROW_TILE)], tail_sem.at[1])

        def run(n, make, op):
            def body(i, c):
                op(make(i))
                return c
            lax.fori_loop(0, n, body, 0)

        run(n_piece, zero_piece, lambda cp: cp.start())
        run(n_tile, zero_tile, lambda cp: cp.start())
        run(n_piece, zero_piece, lambda cp: cp.wait())
        run(n_tile, zero_tile, lambda cp: cp.wait())
        wait_tile(t - 1, 1 - slot)
        wait_tile(t, slot)


def _moe_dispatch(plan, xm):
    hs, npc, lo, tp, tot, ldt = plan["hs"], plan["npc"], plan["lo"], plan["tile_pieces"], plan["total"], plan["ld_t"]
    return pl.pallas_call(
        _dispatch_kernel,
        grid_spec=_plan_specs(
            5, grid=(MOE_NT,),
            in_specs=[pl.BlockSpec((MOE_TILE, D_MODEL), lambda t, *_: (t, 0)),
                      pl.BlockSpec((TOP_K, MOE_TILE), lambda t, *_: (0, t))],
            out_specs=pl.BlockSpec(memory_space=pl.ANY),
            scratch_shapes=[pltpu.VMEM((2, MOE_CAP, D_MODEL // 2), jnp.uint32),
                            pltpu.VMEM((ROW_TILE, D_MODEL // 2), jnp.uint32),
                            pltpu.SemaphoreType.DMA((2,)), pltpu.SemaphoreType.DMA((2,))]),
        out_shape=jax.ShapeDtypeStruct((XS_ROWS, D_MODEL // 2), jnp.uint32),
        compiler_params=pltpu.CompilerParams(dimension_semantics=("arbitrary",),
                                             vmem_limit_bytes=VMEM_LIMIT),
        name="moe_dispatch",
    )(hs, npc, lo, tp, tot, xm, ldt)


def _experts_kernel(tile_ref, exp_ref, lo_ref, hi_ref, first_ref, xs_ref, wgu_ref, wd_ref, y_ref):
    w = pl.program_id(0)
    lo, hi, first = lo_ref[w], hi_ref[w], first_ref[w]

    def contribution():
        packed = xs_ref[...]
        half = D_MODEL // 2
        x_lo = lax.bitcast_convert_type(packed << jnp.uint32(16), F32)
        x_hi = lax.bitcast_convert_type(packed & jnp.uint32(0xFFFF0000), F32)
        wgu = wgu_ref[0]
        ab = _dot(x_lo, wgu[:half]) + _dot(x_hi, wgu[half:])
        h = _silu(ab[:, :D_EXPERT]) * ab[:, D_EXPERT:]
        row = lax.broadcasted_iota(jnp.int32, (ROW_TILE, 1), 0)
        return _dot(jnp.where((row >= lo) & (row < hi), h, 0.0), wd_ref[0])

    @pl.when((hi > lo) & (first == 1))
    def _():
        y_ref[...] = contribution()

    @pl.when((hi > lo) & (first == 0))
    def _():
        y_ref[...] += contribution()

    @pl.when((hi <= lo) & (first == 1))
    def _():
        y_ref[...] = jnp.zeros_like(y_ref)


def _moe_experts(layer, plan, xs, wgu, wd):
    return pl.pallas_call(
        _experts_kernel,
        grid_spec=_plan_specs(
            5, grid=(N_ITEMS,),
            in_specs=[pl.BlockSpec((ROW_TILE, D_MODEL // 2), lambda w, tile, *_: (tile[w], 0)),
                      pl.BlockSpec((None, 1, D_MODEL, 2 * D_EXPERT), lambda w, tile, ex, *_: (layer, ex[w], 0, 0)),
                      pl.BlockSpec((None, 1, D_EXPERT, D_MODEL), lambda w, tile, ex, *_: (layer, ex[w], 0, 0))],
            out_specs=pl.BlockSpec((ROW_TILE, D_MODEL), lambda w, tile, *_: (tile[w], 0))),
        out_shape=jax.ShapeDtypeStruct((XS_ROWS, D_MODEL), F32),
        compiler_params=pltpu.CompilerParams(dimension_semantics=("arbitrary",),
                                             vmem_limit_bytes=VMEM_LIMIT),
        name="moe_experts",
    )(plan["item_tile"], plan["item_expert"], plan["item_lo"], plan["item_hi"], plan["item_first"], xs, wgu, wd)


def _combine_kernel(hs_ref, npc_ref, lo_ref, tp_ref, y_ref, ld_ref, gate_ref, xm_ref, x1_ref, g2_ref, wsgu_ref,
                    wsd_ref, lng_ref, lnb_ref, out_ref, ybuf_ref, sem):
    t = pl.program_id(0)
    nt = pl.num_programs(0)
    slot = lax.rem(t, 2)
    ci = _cond_row(t, MOE_TILE)

    def piece_copy(local_row, y_row, s):
        return pltpu.make_async_copy(y_ref.at[pl.ds(y_row, PIECE)], ybuf_ref.at[s, pl.ds(local_row, PIECE)],
                                     sem.at[s])

    def fetch(tt, s):
        _for_each_piece(tt, npc_ref, lo_ref, hs_ref, lambda a, b: piece_copy(a, b, s).start())

    @pl.when(t == 0)
    def _():
        ybuf_ref[...] = jnp.zeros_like(ybuf_ref)
        fetch(0, 0)

    @pl.when(t + 1 < nt)
    def _():
        fetch(t + 1, 1 - slot)

    def one(i, c):
        piece_copy(0, 0, slot).wait()
        return c
    lax.fori_loop(0, tp_ref[t], one, 0)

    p = lax.broadcasted_iota(jnp.int32, (MOE_TILE, MOE_CAP), 1)
    wmat = jnp.zeros((MOE_TILE, MOE_CAP), F32)
    for k in range(TOP_K):
        wmat = wmat + jnp.where(p == ld_ref[:, k:k + 1], gate_ref[:, k:k + 1], 0.0)
    y = ybuf_ref[slot]
    w_hi = wmat.astype(BF16)
    w_lo = (wmat - w_hi.astype(F32)).astype(BF16)
    y_hi = y.astype(BF16)
    y_lo = (y - y_hi.astype(F32)).astype(BF16)
    routed = (jnp.dot(w_hi, y_hi, preferred_element_type=F32) + jnp.dot(w_lo, y_hi, preferred_element_type=F32)
              + jnp.dot(w_hi, y_lo, preferred_element_type=F32))

    sab = jnp.dot(xm_ref[...], wsgu_ref[...], preferred_element_type=F32)
    shared = _dot(_silu(sab[:, :D_SHARED]) * sab[:, D_SHARED:], wsd_ref[...])
    z = ALPHA * x1_ref[...] + g2_ref[pl.ds(ci, 1), :] * (routed + shared)
    out_ref[...] = _ln(z) * lng_ref[...] + lnb_ref[...]


def _moe_combine(plan, y, gate, xm, x1, g2, wsgu, wsd, lng, lnb):
    full = lambda a: pl.BlockSpec(a.shape, lambda t, *_: (0,) * a.ndim)
    rows = lambda w_: pl.BlockSpec((MOE_TILE, w_), lambda t, *_: (t, 0))
    return pl.pallas_call(
        _combine_kernel,
        grid_spec=_plan_specs(
            4, grid=(MOE_NT,),
            in_specs=[pl.BlockSpec(memory_space=pl.ANY), rows(TOP_K), rows(TOP_K), rows(D_MODEL), rows(D_MODEL),
                      full(g2), full(wsgu), full(wsd), full(lng), full(lnb)],
            out_specs=rows(D_MODEL),
            scratch_shapes=[pltpu.VMEM((2, MOE_CAP, D_MODEL), F32), pltpu.SemaphoreType.DMA((2,))]),
        out_shape=jax.ShapeDtypeStruct((N_TOK, D_MODEL), F32),
        compiler_params=pltpu.CompilerParams(dimension_semantics=("arbitrary",),
                                             vmem_limit_bytes=VMEM_LIMIT),
        name="moe_combine",
    )(plan["hs"], plan["npc"], plan["lo"], plan["tile_pieces"], y, plan["ld"], gate, xm, x1, g2, wsgu, wsd, lng,
      lnb)


def _moe_plan(idx_t):
    i32 = jnp.int32
    idx = idx_t.T
    onehot = idx[:, :, None] == jnp.arange(N_EXPERTS, dtype=i32)[None, None, :]
    sel = jnp.any(onehot, axis=1).astype(i32).reshape(MOE_NT, MOE_TILE, N_EXPERTS)
    csum = jnp.cumsum(sel, axis=1)
    cnt = csum[:, -1, :]
    c8 = (cnt + PIECE - 1) // PIECE * PIECE
    lo = jnp.cumsum(c8, axis=1) - c8
    ld_e = (lo[:, None, :] + csum - sel).reshape(N_TOK, N_EXPERTS)
    ld = jnp.sum(jnp.where(onehot, ld_e[:, None, :], 0), axis=2)
    tot_e = jnp.sum(c8, axis=0)
    ends = jnp.cumsum(tot_e)
    off = ends - tot_e
    hs = off[None, :] + jnp.cumsum(c8, axis=0) - c8
    npc = c8 // PIECE

    first_tile = off // ROW_TILE
    n_tile = jnp.where(tot_e > 0, (ends - 1) // ROW_TILE - first_tile + 1, 0)
    item_end = jnp.cumsum(n_tile)
    item_start = item_end - n_tile
    w = jnp.arange(N_ITEMS, dtype=i32)
    valid = w < item_end[-1]
    wc = jnp.minimum(w, item_end[-1] - 1)
    ex = jnp.sum((wc[:, None] >= item_end[None, :]).astype(i32), axis=1)
    tile = first_tile[ex] + wc - item_start[ex]
    r_lo = jnp.where(valid, jnp.clip(off[ex] - tile * ROW_TILE, 0, ROW_TILE), 0)
    r_hi = jnp.where(valid, jnp.clip(ends[ex] - tile * ROW_TILE, 0, ROW_TILE), 0)
    n_tiles = XS_ROWS // ROW_TILE
    spare = (ends[-1] + ROW_TILE - 1) // ROW_TILE + w - item_end[-1]
    tile = jnp.where(valid, tile, jnp.minimum(spare, n_tiles - 1))
    valid = valid | (spare < n_tiles)
    prev = jnp.concatenate([jnp.full((1,), -1, i32), tile[:-1]])
    return dict(hs=hs.reshape(-1).astype(i32), npc=npc.reshape(-1).astype(i32), lo=lo.reshape(-1).astype(i32),
                tile_pieces=jnp.sum(npc, axis=1).astype(i32), total=ends[-1:].astype(i32),
                ld=ld.astype(i32), ld_t=ld.T.astype(i32),
                item_tile=tile.astype(i32), item_expert=ex.astype(i32), item_lo=r_lo.astype(i32),
                item_hi=r_hi.astype(i32), item_first=(valid & (tile != prev)).astype(i32))


def _rope_tables():
    t = jnp.arange(DEC_SEQ)
    n_freq = HEAD_DIM // 4
    freqs = ROPE_THETA ** (-jnp.arange(n_freq, dtype=F32) / n_freq)
    ang = jnp.concatenate([(t // GRID_W).astype(F32)[:, None] * freqs,
                           (t % GRID_W).astype(F32)[:, None] * freqs], -1)
    cos, sin = jnp.cos(ang), jnp.sin(ang)
    cos_h = jnp.concatenate([cos, cos], -1)
    sin_h = jnp.concatenate([-sin, sin], -1)
    lat = lambda a: jnp.tile(a, (DEC_BATCH, 128 // HEAD_DIM))
    cos_t = jnp.concatenate([jnp.ones((N_CTX, 128), F32), lat(cos_h)], 0)
    sin_t = jnp.concatenate([jnp.zeros((N_CTX, 128), F32), lat(sin_h)], 0)
    return cos_t, sin_t


def _nat_bias_table(rpb):
    n_dr, n_dc = 2 * NAT_KH - 1, 2 * NAT_KW - 1
    cidx = np.arange(GRID_W)
    dc_idx = np.clip(cidx[None, :] - cidx[:, None] + NAT_KW - 1, 0, n_dc - 1)
    col_start = np.clip(cidx - NAT_KW // 2, 0, GRID_W - NAT_KW)
    col_in = (cidx[None, :] >= col_start[:, None]) & (cidx[None, :] < col_start[:, None] + NAT_KW)
    onehot = (dc_idx.reshape(1, -1) == np.arange(n_dc)[:, None]).astype(np.float32)
    t = jnp.dot(rpb.reshape(H_NAT * n_dr, n_dc), onehot, precision=lax.Precision.HIGHEST)
    t = jnp.where(col_in[None, None], t.reshape(H_NAT, n_dr, GRID_W, GRID_W), -jnp.inf)
    bias = jnp.stack([t[:, NAT_KH - 1 - p:2 * NAT_KH - 1 - p] for p in range(NAT_KH)], axis=0)
    return bias.transpose(0, 1, 3, 2, 4).reshape(NAT_KH, H_NAT * GRID_W, NAT_KH * GRID_W)


def _prep_w_in(w_in):
    cuts = np.cumsum((0,) + IN_SPLITS)
    seg = [w_in[:, cuts[i]:cuts[i + 1]] for i in range(len(IN_SPLITS))]
    gq, gk, gv, gg, ga, nq, nk, nv, aq, ak, av = seg
    aq = aq.reshape(D_MODEL, H_GQA, HEAD_DIM)[:, np.array(GQA_SLOT_HEADS), :].reshape(D_MODEL, GQA_QW)
    ga = jnp.pad(ga, ((0, 0), (0, 128 - 2 * GLA_LOWRANK)))
    return jnp.concatenate([gq, gk, gv, gg, nq, nk, nv, aq, ak, av, ga], axis=1).astype(BF16)


def _prep_w_a(w_a2, b_a):
    wa = jnp.zeros((128, 2 * GLA_W), F32)
    wa = wa.at[0:GLA_LOWRANK, 0:GLA_W].set(w_a2[0])
    wa = wa.at[GLA_LOWRANK:2 * GLA_LOWRANK, GLA_W:].set(w_a2[1])
    return wa.astype(BF16), b_a.reshape(1, 2 * GLA_W)


def _prep_w_out(w_out):
    gqa = w_out[GLA_W + NAT_W:].reshape(H_GQA, HEAD_DIM, D_MODEL)[np.array(GQA_SLOT_HEADS)]
    return jnp.concatenate([w_out[:GLA_W + NAT_W], gqa.reshape(GQA_QW, D_MODEL)], 0).astype(BF16)


def _state_to_blockdiag(s):
    out = jnp.zeros(s.shape[:-3] + (GLA_W, GLA_W), F32)
    for h in range(H_GLA):
        out = out.at[..., h * GLA_DV:(h + 1) * GLA_DV, h * GLA_DK:(h + 1) * GLA_DK].set(
            jnp.swapaxes(s[..., h, :, :], -1, -2))
    return out


def _blockdiag_to_state(st):
    blocks = [st[..., h * GLA_DV:(h + 1) * GLA_DV, h * GLA_DK:(h + 1) * GLA_DK] for h in range(H_GLA)]
    return jnp.swapaxes(jnp.stack(blocks, axis=-3), -1, -2)


def kernel(x_prompt, x_sample, state_gla, cache_nat_k, cache_nat_v, cache_gqa_k, cache_gqa_v, c, c_ctx, w_mod, b_mod, w_in, gla_w_a2, gla_b_a, gla_norm_g, nat_rpb, gqa_q_norm_g, gqa_k_norm_g, w_out, ln1_g, ln1_b, w_router, router_bias, w_expert_gu, w_expert_down, w_shared_gu, w_shared_down, ln2_g, ln2_b):
    x = jnp.concatenate([x_prompt.reshape(N_CTX, D_MODEL), x_sample.reshape(N_LAT, D_MODEL)], axis=0)
    cond = jnp.concatenate([c_ctx[None, :], c, jnp.zeros((N_COND - 1 - DEC_BATCH, D_MODEL), F32)], axis=0)
    mods = _modulation(cond, w_mod, b_mod)
    cos_t, sin_t = _rope_tables()
    row = lambda a: a.reshape(1, -1)

    st_gla, st_nk, st_nv, st_gk, st_gv = [], [], [], [], []
    for l in range(DEPTH):
        sh1, sc1, g1, sh2, sc2, g2 = [mods[l, :, j * D_MODEL:(j + 1) * D_MODEL] for j in range(6)]
        wa, ba = _prep_w_a(gla_w_a2[l], gla_b_a[l])
        qg = row(jnp.tile(gqa_q_norm_g[l], 128 // HEAD_DIM))
        kg = row(jnp.tile(gqa_k_norm_g[l], 128 // HEAD_DIM))
        pgla, la, pnat, q, k, v = _in_proj(x, sh1, sc1, _prep_w_in(w_in[l]), wa, ba, qg, kg, cos_t, sin_t)

        zero_st = jnp.zeros((BATCH, 2, GLA_W, GLA_W), F32)
        of_c, ob_c, st_c = _gla(pgla, la, zero_st, 0, BATCH, SEQ)
        of_l, ob_l, _ = _gla(pgla, la, _state_to_blockdiag(state_gla[:, l]), N_CTX, DEC_BATCH, DEC_SEQ)

        onat_c, ogqa_c = _ctx_attention(pnat, q, k, v)
        k_all = jnp.concatenate([k[N_CTX:].reshape(DEC_BATCH, DEC_SEQ, GQA_KW),
                                 cache_gqa_k[:, l].reshape(DEC_BATCH, PAST_LEN, GQA_KW)], axis=1).astype(BF16)
        v_all = jnp.concatenate([v[N_CTX:].reshape(DEC_BATCH, DEC_SEQ, GQA_KW),
                                 cache_gqa_v[:, l].reshape(DEC_BATCH, PAST_LEN, GQA_KW)], axis=1).astype(BF16)
        ogqa_l = _gqa_latent(q, k_all, v_all)
        onat_l = _nat_latent(pnat, cache_nat_k[:, l].reshape(DEC_BATCH, PAST_LEN, NAT_W),
                             cache_nat_v[:, l].reshape(DEC_BATCH, PAST_LEN, NAT_W),
                             _nat_bias_table(nat_rpb[l]))

        x1, xm, idx_t, gate_t = _out_proj(
            x, jnp.concatenate([of_c, of_l], 0), jnp.concatenate([ob_c, ob_l], 0), pgla,
            jnp.concatenate([onat_c, onat_l], 0), jnp.concatenate([ogqa_c, ogqa_l], 0),
            _prep_w_out(w_out[l]), row(jnp.tile(gla_norm_g[l], H_GLA)), g1, row(ln1_g[l]), row(ln1_b[l]),
            sh2, sc2, w_router[l].T.astype(BF16), router_bias[l].reshape(N_EXPERTS, 1))
        plan = _moe_plan(idx_t)
        y = _moe_experts(l, plan, _moe_dispatch(plan, xm), w_expert_gu, w_expert_down)
        x = _moe_combine(plan, y, gate_t.T, xm, x1, g2, w_shared_gu[l].astype(BF16),
                         w_shared_down[l].astype(BF16), row(ln2_g[l]), row(ln2_b[l]))

        st_gla.append(_blockdiag_to_state(st_c))
        st_nk.append(pnat[:N_CTX, NAT_W:2 * NAT_W].reshape(BATCH, SEQ, H_NAT, HEAD_DIM))
        st_nv.append(pnat[:N_CTX, 2 * NAT_W:].reshape(BATCH, SEQ, H_NAT, HEAD_DIM))
        st_gk.append(k[:N_CTX].reshape(BATCH, SEQ, KV_GQA, HEAD_DIM))
        st_gv.append(v[:N_CTX].reshape(BATCH, SEQ, KV_GQA, HEAD_DIM))

    y_prompt = x[:N_CTX].reshape(BATCH, SEQ, D_MODEL)
    y_sample = x[N_CTX:].reshape(DEC_BATCH, DEC_SEQ, D_MODEL)
    return (y_prompt, y_sample, jnp.stack(st_gla, axis=1), jnp.stack(st_nk, axis=1), jnp.stack(st_nv, axis=1),
            jnp.stack(st_gk, axis=1), jnp.stack(st_gv, axis=1))
```

```python
import functools

import numpy as np
import jax
import jax.numpy as jnp
from jax import lax
from jax.experimental import pallas as pl
from jax.experimental.pallas import tpu as pltpu

D_MODEL = 1024
BATCH = 16
SEQ = 256
DEPTH = 2
DEC_BATCH = 2
DEC_SEQ = 4096
PAST_LEN = 512
GRID_W = 64
HEAD_DIM = 64
H_GLA = 4
GLA_DK = 64
GLA_DV = 64
GLA_LOWRANK = 16
GLA_TAU = 16.0
H_NAT = 4
NAT_KH = 8
NAT_KW = 16
H_GQA = 8
KV_GQA = 2
ROPE_THETA = 10000.0
N_EXPERTS = 64
TOP_K = 8
N_GROUPS = 8
TOPK_GROUPS = 4
D_EXPERT = 256
D_SHARED = 256
ROUTE_SCALE = 2.5
EPS = 1e-6
ALPHA = (2 * DEPTH) ** 0.25
IN_SPLITS = (H_GLA * GLA_DK, H_GLA * GLA_DK, H_GLA * GLA_DV, H_GLA * GLA_DV, 2 * GLA_LOWRANK,
             H_NAT * HEAD_DIM, H_NAT * HEAD_DIM, H_NAT * HEAD_DIM,
             H_GQA * HEAD_DIM, KV_GQA * HEAD_DIM, KV_GQA * HEAD_DIM)

F32 = jnp.float32
BF16 = jnp.bfloat16

N_CTX = BATCH * SEQ
N_LAT = DEC_BATCH * DEC_SEQ
N_TOK = N_CTX + N_LAT
N_COND = 8
GLA_W = H_GLA * GLA_DK
NAT_W = H_NAT * HEAD_DIM
GQA_QW = H_GQA * HEAD_DIM
GQA_KW = KV_GQA * HEAD_DIM
W_IN_COLS = 4 * GLA_W + 3 * NAT_W + GQA_QW + 2 * GQA_KW + 128
GA_COL = W_IN_COLS - 128
GQA_SLOT_HEADS = (0, 4, 1, 5, 2, 6, 3, 7)
GLA_CHUNK = 32
GLA_TB = 256
ROWS = DEC_SEQ // GRID_W
NAT_ROWS_PER_STEP = 8
VMEM_LIMIT = 56 * 1024 * 1024

TM_PROJ = 256
MOE_TILE = 256
MOE_NT = N_TOK // MOE_TILE
PIECE = 16
COPY_ROWS = 2 * PIECE
MOE_CAP = MOE_TILE * TOP_K + N_EXPERTS * PIECE
ROW_TILE = 512
XS_ROWS = N_TOK * TOP_K + MOE_NT * N_EXPERTS * (PIECE - 1)
assert XS_ROWS % ROW_TILE == 0
N_ITEMS = XS_ROWS // ROW_TILE + N_EXPERTS
TQ_GQA = 256


def _dot(a, b):
    return jnp.dot(a.astype(BF16), b.astype(BF16), preferred_element_type=F32)


def _dot_nt(a, b):
    return lax.dot_general(a.astype(BF16), b.astype(BF16), (((1,), (1,)), ((), ())),
                           preferred_element_type=F32)


def _dot_tn(a, b):
    return lax.dot_general(a.astype(BF16), b.astype(BF16), (((0,), (0,)), ((), ())),
                           preferred_element_type=F32)


def _dot_split(a, b_bf16):
    hi = a.astype(BF16)
    lo = (a - hi.astype(F32)).astype(BF16)
    return (jnp.dot(hi, b_bf16, preferred_element_type=F32)
            + jnp.dot(lo, b_bf16, preferred_element_type=F32))


def _sigmoid(x):
    return 1.0 / (1.0 + jnp.exp(-x))


def _silu(x):
    return x * _sigmoid(x)


def _ln(x):
    xc = x - jnp.mean(x, axis=-1, keepdims=True)
    return xc * lax.rsqrt(jnp.mean(xc * xc, axis=-1, keepdims=True) + EPS)


def _lane_group(shape, axis, width):
    return lax.shift_right_logical(lax.broadcasted_iota(jnp.int32, shape, axis), int(np.log2(width)))


def _head_mean_sq(x):
    w = x.shape[-1]
    bmat = jnp.where(_lane_group((w, w), 0, HEAD_DIM) == _lane_group((w, w), 1, HEAD_DIM),
                     1.0 / HEAD_DIM, 0.0).astype(BF16)
    return _dot_split(x * x, bmat)


def _swap_half_heads(x):
    w = x.shape[-1]
    lane = lax.broadcasted_iota(jnp.int32, x.shape, x.ndim - 1)
    first = (lane & (HEAD_DIM - 1)) < HEAD_DIM // 2
    return jnp.where(first, pltpu.roll(x, w - HEAD_DIM // 2, x.ndim - 1),
                     pltpu.roll(x, HEAD_DIM // 2, x.ndim - 1))


def _cond_row(tile, tm):
    r = tile * tm
    return jnp.where(r < N_CTX, 0, 1 + (r - N_CTX) // DEC_SEQ)


def _stack_heads(x, n, width):
    head = _lane_group(x.shape, 1, width)
    return jnp.concatenate([jnp.where(head == h, x, jnp.zeros_like(x)) for h in range(n)], axis=0)


def _unstack_heads(o, n, width):
    m = o.shape[0] // n
    head = _lane_group((m, o.shape[1]), 1, width)
    out = jnp.zeros((m, o.shape[1]), o.dtype)
    for h in range(n):
        out = jnp.where(head == h, o[h * m:(h + 1) * m], out)
    return out


def _mod_kernel(c_ref, w_ref, b_ref, o_ref):
    o_ref[0] = _dot(_silu(c_ref[...]), w_ref[0]) + b_ref[0]


def _modulation(cond, w_mod, b_mod):
    tn = 1536
    return pl.pallas_call(
        _mod_kernel,
        grid=(DEPTH, 6 * D_MODEL // tn),
        in_specs=[pl.BlockSpec((N_COND, D_MODEL), lambda l, j: (0, 0)),
                  pl.BlockSpec((1, D_MODEL, tn), lambda l, j: (l, 0, j)),
                  pl.BlockSpec((1, 1, tn), lambda l, j: (l, 0, j))],
        out_specs=pl.BlockSpec((1, N_COND, tn), lambda l, j: (l, 0, j)),
        out_shape=jax.ShapeDtypeStruct((DEPTH, N_COND, 6 * D_MODEL), F32),
        compiler_params=pltpu.CompilerParams(dimension_semantics=("arbitrary", "arbitrary"),
                                             vmem_limit_bytes=VMEM_LIMIT),
        name="modulation",
    )(cond, w_mod, b_mod.reshape(DEPTH, 1, 6 * D_MODEL))


def _inproj_kernel(x_ref, sh_ref, sc_ref, w_ref, wa_ref, ba_ref, qg_ref, kg_ref, cos_ref, sin_ref,
                   pgla_ref, la_ref, pnat_ref, q_ref, k_ref, v_ref):
    ci = _cond_row(pl.program_id(0), TM_PROJ)
    xn = _ln(x_ref[...])
    xm = xn * (1.0 + sc_ref[pl.ds(ci, 1), :]) + sh_ref[pl.ds(ci, 1), :]
    acc = _dot(xm, w_ref[...])
    pgla_ref[...] = acc[:, 0:4 * GLA_W]
    pnat_ref[...] = acc[:, 4 * GLA_W:4 * GLA_W + 3 * NAT_W]
    z = _dot(acc[:, GA_COL:GA_COL + 128], wa_ref[...]) + ba_ref[...]
    la_ref[...] = (jnp.minimum(z, 0.0) - jnp.log(1.0 + jnp.exp(-jnp.abs(z)))) * (1.0 / GLA_TAU)

    c0 = 4 * GLA_W + 3 * NAT_W
    cos = cos_ref[...]
    sin = sin_ref[...]

    def norm_rope(a, g):
        an = a * lax.rsqrt(_head_mean_sq(a) + EPS) * g
        return an * cos + _swap_half_heads(an) * sin

    qs = [norm_rope(acc[:, c0 + j * 128:c0 + (j + 1) * 128], qg_ref[...]) for j in range(GQA_QW // 128)]
    q_ref[...] = (jnp.concatenate(qs, axis=1) * (HEAD_DIM ** -0.5)).astype(BF16)
    k_ref[...] = norm_rope(acc[:, c0 + GQA_QW:c0 + GQA_QW + GQA_KW], kg_ref[...])
    v_ref[...] = acc[:, c0 + GQA_QW + GQA_KW:c0 + GQA_QW + 2 * GQA_KW]


def _in_proj(x, sh, sc, w, wa, ba, qg, kg, cos_t, sin_t):
    tm = TM_PROJ
    full = lambda a: pl.BlockSpec(a.shape, lambda i: (0,) * a.ndim)
    rows = lambda w_: pl.BlockSpec((tm, w_), lambda i: (i, 0))
    return pl.pallas_call(
        _inproj_kernel,
        grid=(N_TOK // tm,),
        in_specs=[rows(D_MODEL), full(sh), full(sc), full(w), full(wa), full(ba), full(qg), full(kg),
                  rows(128), rows(128)],
        out_specs=[rows(4 * GLA_W), rows(2 * GLA_W), rows(3 * NAT_W), rows(GQA_QW), rows(GQA_KW),
                   rows(GQA_KW)],
        out_shape=[jax.ShapeDtypeStruct((N_TOK, 4 * GLA_W), F32),
                   jax.ShapeDtypeStruct((N_TOK, 2 * GLA_W), F32),
                   jax.ShapeDtypeStruct((N_TOK, 3 * NAT_W), F32),
                   jax.ShapeDtypeStruct((N_TOK, GQA_QW), BF16),
                   jax.ShapeDtypeStruct((N_TOK, GQA_KW), F32),
                   jax.ShapeDtypeStruct((N_TOK, GQA_KW), F32)],
        compiler_params=pltpu.CompilerParams(dimension_semantics=("arbitrary",),
                                             vmem_limit_bytes=VMEM_LIMIT),
        name="in_proj",
    )(x, sh, sc, w, wa, ba, qg, kg, cos_t, sin_t)


def _gla_kernel(qf_ref, kf_ref, vf_ref, laf_ref, qb_ref, kb_ref, vb_ref, lab_ref, st0_ref,
                of_ref, ob_ref, stout_ref, st_ref):
    t = pl.program_id(1)
    last_t = pl.num_programs(1) - 1
    C = GLA_CHUNK
    nchunk = GLA_TB // C

    @pl.when(t == 0)
    def _():
        st_ref[...] = st0_ref[0]

    tb = GLA_TB
    shift = int(np.log2(C))
    r = lax.broadcasted_iota(jnp.int32, (tb, tb), 0)
    c = lax.broadcasted_iota(jnp.int32, (tb, tb), 1)
    same = lax.shift_right_logical(r, shift) == lax.shift_right_logical(c, shift)
    ra = lax.broadcasted_iota(jnp.int32, (tb, H_GLA * tb), 0)
    ca = lax.broadcasted_iota(jnp.int32, (tb, H_GLA * tb), 1) & (tb - 1)
    same_a = lax.shift_right_logical(ra, shift) == lax.shift_right_logical(ca, shift)
    blk = _lane_group((H_GLA * tb, GLA_W), 0, tb) == _lane_group((H_GLA * tb, GLA_W), 1, GLA_DK)
    diag = _lane_group((GLA_W, GLA_W), 0, GLA_DV) == _lane_group((GLA_W, GLA_W), 1, GLA_DK)
    one = lambda m: jnp.where(m, 1.0, 0.0).astype(BF16)
    chunk_ones = one(same)
    ref_pick = one(c == (r & ~(C - 1)) + C // 2)

    def direction(q_ref, k_ref, v_ref, la_ref, o_ref, d, causal):
        q = q_ref[...] * (GLA_DK ** -0.5)
        k = k_ref[...]
        v = v_ref[...]
        la = la_ref[...]
        la_hi = la.astype(BF16)
        la_lo = (la - la_hi.astype(F32)).astype(BF16)
        csum = lambda m: (jnp.dot(m, la_hi, preferred_element_type=F32)
                          + jnp.dot(m, la_lo, preferred_element_type=F32))
        b = csum(one(same & ((r >= c) if causal else (r <= c))))
        btot = csum(chunk_ones)
        bref = jnp.dot(ref_pick, b.astype(BF16), preferred_element_type=F32)
        q_att = q * jnp.exp(b - bref)
        k_att = k * jnp.exp(bref - b)
        kblk = jnp.where(blk, jnp.concatenate([k_att] * H_GLA, axis=0), 0.0)
        att = _dot_nt(q_att, kblk)
        att = jnp.where(same_a & ((ra >= ca) if causal else (ra <= ca)), att, 0.0)
        vblk = jnp.where(blk, jnp.concatenate([v] * H_GLA, axis=0), 0.0)
        o_intra = _dot(att, vblk)

        q_dec = (q * jnp.exp(b)).astype(BF16)
        k_dec = (k * jnp.exp(btot - b)).astype(BF16)
        g = jnp.exp(btot)
        vb = v.astype(BF16)
        st = st_ref[d]
        o_inter = [None] * nchunk
        for ci in (range(nchunk) if causal else reversed(range(nchunk))):
            rows = slice(ci * C, (ci + 1) * C)
            o_inter[ci] = _dot_nt(q_dec[rows], st)
            u = _dot_tn(vb[rows], k_dec[rows])
            st = st * g[ci * C:ci * C + 1, :] + jnp.where(diag, u, 0.0)
        st_ref[d] = st
        o_ref[...] = o_intra + jnp.concatenate(o_inter, axis=0)

    direction(qf_ref, kf_ref, vf_ref, laf_ref, of_ref, 0, True)
    direction(qb_ref, kb_ref, vb_ref, lab_ref, ob_ref, 1, False)

    @pl.when(t == last_t)
    def _():
        stout_ref[0] = st_ref[...]


def _gla(pgla, la, st0, row0, n_seq, seq_len):
    tb = GLA_TB
    nt = seq_len // tb
    b0 = row0 // tb
    fwd = lambda col: pl.BlockSpec((tb, GLA_W), lambda s, t: (b0 + s * nt + t, col))
    bwd = lambda col: pl.BlockSpec((tb, GLA_W), lambda s, t: (b0 + s * nt + nt - 1 - t, col))
    st_spec = pl.BlockSpec((1, 2, GLA_W, GLA_W), lambda s, t: (s, 0, 0, 0))
    n = n_seq * seq_len
    return pl.pallas_call(
        _gla_kernel,
        grid=(n_seq, nt),
        in_specs=[fwd(0), fwd(1), fwd(2), fwd(0), bwd(0), bwd(1), bwd(2), bwd(1), st_spec],
        out_specs=[pl.BlockSpec((tb, GLA_W), lambda s, t: (s * nt + t, 0)),
                   pl.BlockSpec((tb, GLA_W), lambda s, t: (s * nt + nt - 1 - t, 0)),
                   st_spec],
        out_shape=[jax.ShapeDtypeStruct((n, GLA_W), F32), jax.ShapeDtypeStruct((n, GLA_W), F32),
                   jax.ShapeDtypeStruct((n_seq, 2, GLA_W, GLA_W), F32)],
        scratch_shapes=[pltpu.VMEM((2, GLA_W, GLA_W), F32)],
        compiler_params=pltpu.CompilerParams(dimension_semantics=("arbitrary", "arbitrary"),
                                             vmem_limit_bytes=VMEM_LIMIT),
        name="gla_scan",
    )(pgla, pgla, pgla, la, pgla, pgla, pgla, la, st0)


def _softmax_pv(s_list, v_list):
    m = s_list[0].max(axis=-1, keepdims=True)
    for s in s_list[1:]:
        m = jnp.maximum(m, s.max(axis=-1, keepdims=True))
    acc = None
    l = None
    for s, v in zip(s_list, v_list):
        p = jnp.exp(s - m)
        pl_ = p.sum(axis=-1, keepdims=True)
        pv = _dot(p, v)
        acc = pv if acc is None else acc + pv
        l = pl_ if l is None else l + pl_
    return acc * (1.0 / l)


def _ctx_attn_kernel(pnat_ref, q_ref, k_ref, v_ref, onat_ref, ogqa_ref):
    nq = pnat_ref[:, 0:NAT_W] * (HEAD_DIM ** -0.5)
    nk = pnat_ref[:, NAT_W:2 * NAT_W].astype(BF16)
    nv = pnat_ref[:, 2 * NAT_W:3 * NAT_W].astype(BF16)
    o = _softmax_pv([_dot_nt(_stack_heads(nq, H_NAT, HEAD_DIM), nk)], [nv])
    onat_ref[...] = _unstack_heads(o, H_NAT, HEAD_DIM).astype(BF16)

    k = k_ref[...].astype(BF16)
    v = v_ref[...].astype(BF16)
    k2 = jnp.concatenate([k, k], axis=1)
    v2 = jnp.concatenate([v, v], axis=1)
    outs = []
    for half in range(2):
        q = q_ref[:, half * 256:(half + 1) * 256]
        o = _softmax_pv([_dot_nt(_stack_heads(q, 4, HEAD_DIM), k2)], [v2])
        outs.append(_unstack_heads(o, 4, HEAD_DIM))
    ogqa_ref[...] = jnp.concatenate(outs, axis=1).astype(BF16)


def _ctx_attention(pnat, q, k, v):
    rows = lambda w_: pl.BlockSpec((SEQ, w_), lambda i: (i, 0))
    return pl.pallas_call(
        _ctx_attn_kernel,
        grid=(BATCH,),
        in_specs=[rows(3 * NAT_W), rows(GQA_QW), rows(GQA_KW), rows(GQA_KW)],
        out_specs=[rows(NAT_W), rows(GQA_QW)],
        out_shape=[jax.ShapeDtypeStruct((N_CTX, NAT_W), BF16),
                   jax.ShapeDtypeStruct((N_CTX, GQA_QW), BF16)],
        compiler_params=pltpu.CompilerParams(dimension_semantics=("arbitrary",),
                                             vmem_limit_bytes=VMEM_LIMIT),
        name="ctx_attention",
    )(pnat, q, k, v)


def _gqa_lat_kernel(q_ref, k_ref, v_ref, o_ref):
    k = k_ref[0]
    v = v_ref[0]
    k2 = jnp.concatenate([k, k], axis=1)
    v2 = jnp.concatenate([v, v], axis=1)
    slot = _lane_group((TQ_GQA, 256), 1, HEAD_DIM)
    outs = []
    for half in range(2):
        q = q_ref[:, half * 256:(half + 1) * 256]
        out = jnp.zeros((TQ_GQA, 256), F32)
        for s in range(4):
            qs = jnp.where(slot == s, q, jnp.zeros_like(q))
            o = _softmax_pv([_dot_nt(qs, k2)], [v2])
            out = jnp.where(slot == s, o, out)
        outs.append(out)
    o_ref[...] = jnp.concatenate(outs, axis=1).astype(BF16)


def _gqa_latent(q, k_all, v_all):
    tq = TQ_GQA
    nq = DEC_SEQ // tq
    tk = k_all.shape[1]
    return pl.pallas_call(
        _gqa_lat_kernel,
        grid=(DEC_BATCH, nq),
        in_specs=[pl.BlockSpec((tq, GQA_QW), lambda b, i: (N_CTX // tq + b * nq + i, 0)),
                  pl.BlockSpec((1, tk, GQA_KW), lambda b, i: (b, 0, 0)),
                  pl.BlockSpec((1, tk, GQA_KW), lambda b, i: (b, 0, 0))],
        out_specs=pl.BlockSpec((tq, GQA_QW), lambda b, i: (b * nq + i, 0)),
        out_shape=jax.ShapeDtypeStruct((N_LAT, GQA_QW), BF16),
        compiler_params=pltpu.CompilerParams(dimension_semantics=("arbitrary", "arbitrary"),
                                             vmem_limit_bytes=VMEM_LIMIT),
        name="gqa_latent",
    )(q, k_all, v_all)


def _nat_lat_kernel(q_ref, k_ref, v_ref, kc_ref, vc_ref, bias_ref, o_ref):
    j = pl.program_id(1)
    kc = kc_ref[0].astype(BF16)
    vc = vc_ref[0].astype(BF16)
    for i in range(NAT_ROWS_PER_STEP):
        r = j * NAT_ROWS_PER_STEP + i
        r0 = jnp.clip(r - NAT_KH // 2, 0, ROWS - NAT_KH)
        win = pl.ds(pl.multiple_of(r0 * GRID_W, GRID_W), NAT_KH * GRID_W)
        q = q_ref[i * GRID_W:(i + 1) * GRID_W, :] * (HEAD_DIM ** -0.5)
        qs = _stack_heads(q, H_NAT, HEAD_DIM)
        s_loc = _dot_nt(qs, k_ref[win, :]) + bias_ref[r - r0]
        s_ctx = _dot_nt(qs, kc)
        o = _softmax_pv([s_loc, s_ctx], [v_ref[win, :], vc])
        o_ref[i * GRID_W:(i + 1) * GRID_W, :] = _unstack_heads(o, H_NAT, HEAD_DIM).astype(BF16)


def _nat_latent(pnat, kc, vc, bias):
    tq = NAT_ROWS_PER_STEP * GRID_W
    nq = DEC_SEQ // tq
    lat_blk = N_CTX // DEC_SEQ
    return pl.pallas_call(
        _nat_lat_kernel,
        grid=(DEC_BATCH, nq),
        in_specs=[pl.BlockSpec((tq, NAT_W), lambda b, j: (N_CTX // tq + b * nq + j, 0)),
                  pl.BlockSpec((DEC_SEQ, NAT_W), lambda b, j: (lat_blk + b, 1)),
                  pl.BlockSpec((DEC_SEQ, NAT_W), lambda b, j: (lat_blk + b, 2)),
                  pl.BlockSpec((1, PAST_LEN, NAT_W), lambda b, j: (b, 0, 0)),
                  pl.BlockSpec((1, PAST_LEN, NAT_W), lambda b, j: (b, 0, 0)),
                  pl.BlockSpec(bias.shape, lambda b, j: (0, 0, 0))],
        out_specs=pl.BlockSpec((tq, NAT_W), lambda b, j: (b * nq + j, 0)),
        out_shape=jax.ShapeDtypeStruct((N_LAT, NAT_W), BF16),
        compiler_params=pltpu.CompilerParams(dimension_semantics=("arbitrary", "arbitrary"),
                                             vmem_limit_bytes=VMEM_LIMIT),
        name="nat_latent",
    )(pnat, pnat, pnat, kc, vc, bias)


def _route_gates(logits_t, bias_ref):
    per = N_EXPERTS // N_GROUPS
    t = logits_t.shape[1]
    neg = -jnp.inf
    pos = lax.broadcasted_iota(jnp.int32, (per, t), 0)
    scores = [_sigmoid(logits_t[g * per:(g + 1) * per, :]) for g in range(N_GROUPS)]
    biased = [scores[g] + bias_ref[g * per:(g + 1) * per, :] for g in range(N_GROUPS)]

    grp = []
    for v in biased:
        m1 = jnp.max(v, axis=0, keepdims=True)
        i1 = jnp.min(jnp.where(v == m1, pos, per), axis=0, keepdims=True)
        m2 = jnp.max(jnp.where(pos == i1, neg, v), axis=0, keepdims=True)
        grp.append(m1 + m2)

    keep = [jnp.zeros((1, t), jnp.bool_) for _ in range(N_GROUPS)]
    for _ in range(TOPK_GROUPS):
        best = functools.reduce(jnp.maximum, grp)
        first = jnp.full((1, t), N_GROUPS, jnp.int32)
        for g in reversed(range(N_GROUPS)):
            first = jnp.where(grp[g] == best, g, first)
        for g in range(N_GROUPS):
            hit = first == g
            keep[g] = keep[g] | hit
            grp[g] = jnp.where(hit, neg, grp[g])

    cand = [jnp.where(keep[g], biased[g], neg) for g in range(N_GROUPS)]
    flat = [pos + g * per for g in range(N_GROUPS)]
    picks, weights = [], []
    for _ in range(TOP_K):
        best = functools.reduce(jnp.maximum, [jnp.max(v, axis=0, keepdims=True) for v in cand])
        first = functools.reduce(jnp.minimum, [
            jnp.min(jnp.where(cand[g] == best, flat[g], N_EXPERTS), axis=0, keepdims=True)
            for g in range(N_GROUPS)])
        score = jnp.zeros((1, t), F32)
        for g in range(N_GROUPS):
            hit = flat[g] == first
            score = score + jnp.sum(jnp.where(hit, scores[g], 0.0), axis=0, keepdims=True)
            cand[g] = jnp.where(hit, neg, cand[g])
        picks.append(first)
        weights.append(score)

    total = functools.reduce(lambda a, b: a + b, weights)
    gates = jnp.concatenate([v / total * ROUTE_SCALE for v in weights], axis=0)

    sel = [functools.reduce(lambda a, b: a | b, [flat[g] == k for k in picks]) for g in range(N_GROUPS)]
    sel = jnp.where(jnp.concatenate(sel, axis=0), 1.0, 0.0).astype(BF16)
    before = (lax.broadcasted_iota(jnp.int32, (t, t), 0) < lax.broadcasted_iota(jnp.int32, (t, t), 1))
    rank = jnp.dot(sel, jnp.where(before, 1.0, 0.0).astype(BF16), preferred_element_type=F32)
    count = jnp.sum(sel.astype(F32), axis=1, keepdims=True)
    pieces = jnp.floor((count + (PIECE - 1)) * (1.0 / PIECE))
    lower = (lax.broadcasted_iota(jnp.int32, (N_EXPERTS, N_EXPERTS), 1)
             < lax.broadcasted_iota(jnp.int32, (N_EXPERTS, N_EXPERTS), 0))
    start = PIECE * jnp.dot(jnp.where(lower, 1.0, 0.0).astype(BF16),
                            jnp.broadcast_to(pieces, (N_EXPERTS, 128)).astype(BF16),
                            preferred_element_type=F32)[:, 0:1]
    row = rank + start
    rows = []
    for k in picks:
        r = jnp.zeros((1, t), F32)
        for g in range(N_GROUPS):
            r = r + jnp.sum(jnp.where(flat[g] == k, row[g * per:(g + 1) * per, :], 0.0), axis=0, keepdims=True)
        rows.append(r)
    return (jnp.concatenate(rows, axis=0).astype(jnp.int32), gates,
            jnp.broadcast_to(pieces, (N_EXPERTS, 128)).astype(jnp.int32))


def _outproj_kernel(x_ref, of_ref, ob_ref, gg_ref, onat_ref, ogqa_ref, w_ref, ng_ref, g1_ref, lng_ref,
                    lnb_ref, sh2_ref, sc2_ref, wr_ref, rb_ref, x1_ref, xm_ref, row_ref, gate_ref, pieces_ref):
    ci = _cond_row(pl.program_id(0), TM_PROJ)
    og = of_ref[...] + ob_ref[...]
    halves = [og[:, j * 128:(j + 1) * 128] for j in range(GLA_W // 128)]
    ms = jnp.concatenate([_head_mean_sq(h) for h in halves], axis=1)
    ogla = og * lax.rsqrt(ms + EPS) * ng_ref[...] * _silu(gg_ref[...])
    mix = (_dot(ogla, w_ref[0:GLA_W, :]) + _dot(onat_ref[...], w_ref[GLA_W:GLA_W + NAT_W, :])
           + _dot(ogqa_ref[...], w_ref[GLA_W + NAT_W:, :]))
    x1 = _ln(ALPHA * x_ref[...] + g1_ref[pl.ds(ci, 1), :] * mix) * lng_ref[...] + lnb_ref[...]
    x1_ref[...] = x1
    xm = (_ln(x1) * (1.0 + sc2_ref[pl.ds(ci, 1), :]) + sh2_ref[pl.ds(ci, 1), :]).astype(BF16)
    xm_ref[...] = xm
    row_ref[...], gate_ref[...], pieces_ref[...] = _route_gates(_dot_nt(wr_ref[...], xm), rb_ref)


def _out_proj(x, of, ob, pgla, onat, ogqa, w, ng, g1, lng, lnb, sh2, sc2, wr, rb):
    tm = TM_PROJ
    full = lambda a: pl.BlockSpec(a.shape, lambda i: (0,) * a.ndim)
    rows = lambda w_: pl.BlockSpec((tm, w_), lambda i: (i, 0))
    return pl.pallas_call(
        _outproj_kernel,
        grid=(N_TOK // tm,),
        in_specs=[rows(D_MODEL), rows(GLA_W), rows(GLA_W), pl.BlockSpec((tm, GLA_W), lambda i: (i, 3)),
                  rows(NAT_W), rows(GQA_QW), full(w), full(ng), full(g1), full(lng), full(lnb),
                  full(sh2), full(sc2), full(wr), full(rb)],
        out_specs=[rows(D_MODEL), rows(D_MODEL), pl.BlockSpec((TOP_K, tm), lambda i: (0, i)),
                   pl.BlockSpec((TOP_K, tm), lambda i: (0, i)),
                   pl.BlockSpec((N_EXPERTS, 128), lambda i: (0, i))],
        out_shape=[jax.ShapeDtypeStruct((N_TOK, D_MODEL), F32),
                   jax.ShapeDtypeStruct((N_TOK, D_MODEL), BF16),
                   jax.ShapeDtypeStruct((TOP_K, N_TOK), jnp.int32),
                   jax.ShapeDtypeStruct((TOP_K, N_TOK), F32),
                   jax.ShapeDtypeStruct((N_EXPERTS, MOE_NT * 128), jnp.int32)],
        compiler_params=pltpu.CompilerParams(dimension_semantics=("arbitrary",),
                                             vmem_limit_bytes=VMEM_LIMIT),
        name="out_proj",
    )(x, of, ob, pgla, onat, ogqa, w, ng, g1, lng, lnb, sh2, sc2, wr, rb)


def _plan_specs(n_prefetch, **kw):
    return pltpu.PrefetchScalarGridSpec(num_scalar_prefetch=n_prefetch, **kw)


def _for_each_copy(t, npc_ref, lo_ref, hs_ref, fn):
    def per_expert(e, carry):
        j = t * N_EXPERTS + e
        n_big = npc_ref[j] // (COPY_ROWS // PIECE)

        def big(i, c):
            fn(pl.multiple_of(lo_ref[j] + i * COPY_ROWS, PIECE), pl.multiple_of(hs_ref[j] + i * COPY_ROWS, PIECE),
               COPY_ROWS)
            return c

        def small(i, c):
            fn(pl.multiple_of(lo_ref[j] + n_big * COPY_ROWS, PIECE),
               pl.multiple_of(hs_ref[j] + n_big * COPY_ROWS, PIECE), PIECE)
            return c

        lax.fori_loop(0, n_big, big, carry)
        return lax.fori_loop(0, npc_ref[j] - n_big * (COPY_ROWS // PIECE), small, carry)

    lax.fori_loop(0, N_EXPERTS, per_expert, 0)


def _wait_copies(n_big, n_small, make):
    for n, rows in ((n_big, COPY_ROWS), (n_small, PIECE)):
        def one(i, c, rows=rows):
            make(rows).wait()
            return c
        lax.fori_loop(0, n, one, 0)


def _dispatch_kernel(hs_ref, npc_ref, lo_ref, nbig_ref, nsmall_ref, tot_ref, xm_ref, ldt_ref, xs_ref, stage_ref,
                     zero_ref, sem_big, sem_small, tail_sem):
    t = pl.program_id(0)
    nt = pl.num_programs(0)
    slot = lax.rem(t, 2)

    def run_copy(local_row, xs_row, rows, s):
        sem = sem_big if rows == COPY_ROWS else sem_small
        return pltpu.make_async_copy(stage_ref.at[s, pl.ds(local_row, rows)], xs_ref.at[pl.ds(xs_row, rows)],
                                     sem.at[s])

    def wait_tile(tt, s):
        _wait_copies(nbig_ref[tt], nsmall_ref[tt], lambda rows: run_copy(0, 0, rows, s))

    @pl.when(t >= 2)
    def _():
        wait_tile(t - 2, slot)

    p = lax.broadcasted_iota(jnp.int32, (MOE_CAP, MOE_TILE), 0)
    hit = p == ldt_ref[0:1, :]
    for k in range(1, TOP_K):
        hit = hit | (p == ldt_ref[k:k + 1, :])
    stage_ref[slot] = jnp.dot(jnp.where(hit, 1.0, 0.0).astype(BF16), xm_ref[...],
                              preferred_element_type=F32).astype(BF16)
    _for_each_copy(t, npc_ref, lo_ref, hs_ref, lambda a, b, rows: run_copy(a, b, rows, slot).start())

    @pl.when(t == nt - 1)
    def _():
        zero_ref[...] = jnp.zeros_like(zero_ref)
        total = tot_ref[0]
        first_free_tile = (total + ROW_TILE - 1) // ROW_TILE
        n_piece = (first_free_tile * ROW_TILE - total) // PIECE
        n_tile = XS_ROWS // ROW_TILE - first_free_tile

        def zero_piece(i):
            return pltpu.make_async_copy(zero_ref.at[pl.ds(0, PIECE)],
                                         xs_ref.at[pl.ds(pl.multiple_of(total + i * PIECE, PIECE), PIECE)],
                                         tail_sem.at[0])

        def zero_tile(i):
            row = pl.multiple_of((first_free_tile + i) * ROW_TILE, ROW_TILE)
            return pltpu.make_async_copy(zero_ref, xs_ref.at[pl.ds(row, ROW_TILE)], tail_sem.at[1])

        def run(n, make, op):
            def body(i, c):
                op(make(i))
                return c
            lax.fori_loop(0, n, body, 0)

        run(n_piece, zero_piece, lambda cp: cp.start())
        run(n_tile, zero_tile, lambda cp: cp.start())
        run(n_piece, zero_piece, lambda cp: cp.wait())
        run(n_tile, zero_tile, lambda cp: cp.wait())
        wait_tile(t - 1, 1 - slot)
        wait_tile(t, slot)


def _moe_dispatch(plan, xm, row_t):
    return pl.pallas_call(
        _dispatch_kernel,
        grid_spec=_plan_specs(
            6, grid=(MOE_NT,),
            in_specs=[pl.BlockSpec((MOE_TILE, D_MODEL), lambda t, *_: (t, 0)),
                      pl.BlockSpec((TOP_K, MOE_TILE), lambda t, *_: (0, t))],
            out_specs=pl.BlockSpec(memory_space=pl.ANY),
            scratch_shapes=[pltpu.VMEM((2, MOE_CAP, D_MODEL), BF16), pltpu.VMEM((ROW_TILE, D_MODEL), BF16),
                            pltpu.SemaphoreType.DMA((2,)), pltpu.SemaphoreType.DMA((2,)),
                            pltpu.SemaphoreType.DMA((2,))]),
        out_shape=jax.ShapeDtypeStruct((XS_ROWS, D_MODEL), BF16),
        compiler_params=pltpu.CompilerParams(dimension_semantics=("arbitrary",),
                                             vmem_limit_bytes=VMEM_LIMIT),
        name="moe_dispatch",
    )(plan["hs"], plan["npc"], plan["lo"], plan["tile_big"], plan["tile_small"], plan["total"], xm, row_t)


def _experts_kernel(tile_ref, exp_ref, lo_ref, hi_ref, first_ref, fresh_ref, xs_ref, wgu_ref, wd_ref, y_ref,
                    wgu_bf_ref, wd_bf_ref):
    w = pl.program_id(0)
    lo, hi, first = lo_ref[w], hi_ref[w], first_ref[w]

    @pl.when(fresh_ref[w] == 1)
    def _():
        wgu_bf_ref[...] = wgu_ref[0].astype(BF16)
        wd_bf_ref[...] = wd_ref[0].astype(BF16)

    def rows_of_this_expert():
        ab = jnp.dot(xs_ref[...], wgu_bf_ref[...], preferred_element_type=F32)
        h = (_silu(ab[:, :D_EXPERT]) * ab[:, D_EXPERT:]).astype(BF16)
        out = jnp.dot(h, wd_bf_ref[...], preferred_element_type=F32).astype(BF16)
        row = lax.broadcasted_iota(jnp.int32, (ROW_TILE, 1), 0)
        return (row >= lo) & (row < hi), out

    @pl.when((hi > lo) & (first == 1))
    def _():
        mine, out = rows_of_this_expert()
        y_ref[...] = jnp.where(mine, out, jnp.zeros_like(out))

    @pl.when((hi > lo) & (first == 0))
    def _():
        mine, out = rows_of_this_expert()
        y_ref[...] = jnp.where(mine, out, y_ref[...])

    @pl.when((hi <= lo) & (first == 1))
    def _():
        y_ref[...] = jnp.zeros_like(y_ref)


def _moe_experts(layer, plan, xs, wgu, wd):
    return pl.pallas_call(
        _experts_kernel,
        grid_spec=_plan_specs(
            6, grid=(N_ITEMS,),
            in_specs=[pl.BlockSpec((ROW_TILE, D_MODEL), lambda w, tile, *_: (tile[w], 0)),
                      pl.BlockSpec((None, 1, D_MODEL, 2 * D_EXPERT), lambda w, tile, ex, *_: (layer, ex[w], 0, 0)),
                      pl.BlockSpec((None, 1, D_EXPERT, D_MODEL), lambda w, tile, ex, *_: (layer, ex[w], 0, 0))],
            out_specs=pl.BlockSpec((ROW_TILE, D_MODEL), lambda w, tile, *_: (tile[w], 0)),
            scratch_shapes=[pltpu.VMEM((D_MODEL, 2 * D_EXPERT), BF16), pltpu.VMEM((D_EXPERT, D_MODEL), BF16)]),
        out_shape=jax.ShapeDtypeStruct((XS_ROWS, D_MODEL), BF16),
        compiler_params=pltpu.CompilerParams(dimension_semantics=("arbitrary",),
                                             vmem_limit_bytes=VMEM_LIMIT),
        name="moe_experts",
    )(plan["item_tile"], plan["item_expert"], plan["item_lo"], plan["item_hi"], plan["item_first"],
      plan["item_fresh"], xs, wgu, wd)


def _combine_kernel(hs_ref, npc_ref, lo_ref, nbig_ref, nsmall_ref, y_ref, ld_ref, gate_ref, xm_ref, x1_ref, g2_ref,
                    wsgu_ref, wsd_ref, lng_ref, lnb_ref, out_ref, ybuf_ref, sem_big, sem_small):
    t = pl.program_id(0)
    nt = pl.num_programs(0)
    slot = lax.rem(t, 2)
    ci = _cond_row(t, MOE_TILE)

    def run_copy(local_row, y_row, rows, s):
        sem = sem_big if rows == COPY_ROWS else sem_small
        return pltpu.make_async_copy(y_ref.at[pl.ds(y_row, rows)], ybuf_ref.at[s, pl.ds(local_row, rows)], sem.at[s])

    def fetch(tt, s):
        _for_each_copy(tt, npc_ref, lo_ref, hs_ref, lambda a, b, rows: run_copy(a, b, rows, s).start())

    @pl.when(t == 0)
    def _():
        ybuf_ref[...] = jnp.zeros_like(ybuf_ref)
        fetch(0, 0)

    @pl.when(t + 1 < nt)
    def _():
        fetch(t + 1, 1 - slot)

    _wait_copies(nbig_ref[t], nsmall_ref[t], lambda rows: run_copy(0, 0, rows, slot))

    p = lax.broadcasted_iota(jnp.int32, (MOE_TILE, MOE_CAP), 1)
    wmat = jnp.zeros((MOE_TILE, MOE_CAP), F32)
    for k in range(TOP_K):
        wmat = wmat + jnp.where(p == ld_ref[:, k:k + 1], gate_ref[:, k:k + 1], 0.0)
    routed = _dot_split(wmat, ybuf_ref[slot])

    sab = jnp.dot(xm_ref[...], wsgu_ref[...], preferred_element_type=F32)
    shared = _dot(_silu(sab[:, :D_SHARED]) * sab[:, D_SHARED:], wsd_ref[...])
    z = ALPHA * x1_ref[...] + g2_ref[pl.ds(ci, 1), :] * (routed + shared)
    out_ref[...] = _ln(z) * lng_ref[...] + lnb_ref[...]


def _moe_combine(plan, y, row, gate, xm, x1, g2, wsgu, wsd, lng, lnb):
    full = lambda a: pl.BlockSpec(a.shape, lambda t, *_: (0,) * a.ndim)
    rows = lambda w_: pl.BlockSpec((MOE_TILE, w_), lambda t, *_: (t, 0))
    return pl.pallas_call(
        _combine_kernel,
        grid_spec=_plan_specs(
            5, grid=(MOE_NT,),
            in_specs=[pl.BlockSpec(memory_space=pl.ANY), rows(TOP_K), rows(TOP_K), rows(D_MODEL), rows(D_MODEL),
                      full(g2), full(wsgu), full(wsd), full(lng), full(lnb)],
            out_specs=rows(D_MODEL),
            scratch_shapes=[pltpu.VMEM((2, MOE_CAP, D_MODEL), BF16), pltpu.SemaphoreType.DMA((2,)),
                            pltpu.SemaphoreType.DMA((2,))]),
        out_shape=jax.ShapeDtypeStruct((N_TOK, D_MODEL), F32),
        compiler_params=pltpu.CompilerParams(dimension_semantics=("arbitrary",),
                                             vmem_limit_bytes=VMEM_LIMIT),
        name="moe_combine",
    )(plan["hs"], plan["npc"], plan["lo"], plan["tile_big"], plan["tile_small"], y, row, gate, xm, x1, g2, wsgu,
      wsd, lng, lnb)


def _moe_plan(npc):
    i32 = jnp.int32
    run = npc * PIECE
    lo = jnp.cumsum(run, axis=1) - run
    tot_e = jnp.sum(run, axis=0)
    ends = jnp.cumsum(tot_e)
    off = ends - tot_e
    hs = off[None, :] + jnp.cumsum(run, axis=0) - run
    n_big = npc // (COPY_ROWS // PIECE)

    first_tile = off // ROW_TILE
    n_tile = jnp.where(tot_e > 0, (ends - 1) // ROW_TILE - first_tile + 1, 0)
    item_end = jnp.cumsum(n_tile)
    item_start = item_end - n_tile
    w = jnp.arange(N_ITEMS, dtype=i32)
    valid = w < item_end[-1]
    wc = jnp.minimum(w, item_end[-1] - 1)
    ex = jnp.sum((wc[:, None] >= item_end[None, :]).astype(i32), axis=1)
    tile = first_tile[ex] + wc - item_start[ex]
    r_lo = jnp.where(valid, jnp.clip(off[ex] - tile * ROW_TILE, 0, ROW_TILE), 0)
    r_hi = jnp.where(valid, jnp.clip(ends[ex] - tile * ROW_TILE, 0, ROW_TILE), 0)
    n_tiles = XS_ROWS // ROW_TILE
    spare = (ends[-1] + ROW_TILE - 1) // ROW_TILE + w - item_end[-1]
    tile = jnp.where(valid, tile, jnp.minimum(spare, n_tiles - 1))
    valid = valid | (spare < n_tiles)
    prev = jnp.concatenate([jnp.full((1,), -1, i32), tile[:-1]])
    prev_ex = jnp.concatenate([jnp.full((1,), -1, i32), ex[:-1]])
    return dict(hs=hs.reshape(-1).astype(i32), npc=npc.reshape(-1).astype(i32), lo=lo.reshape(-1).astype(i32),
                tile_big=jnp.sum(n_big, axis=1).astype(i32),
                tile_small=jnp.sum(npc - n_big * (COPY_ROWS // PIECE), axis=1).astype(i32),
                total=ends[-1:].astype(i32),
                item_tile=tile.astype(i32), item_expert=ex.astype(i32), item_lo=r_lo.astype(i32),
                item_hi=r_hi.astype(i32), item_first=(valid & (tile != prev)).astype(i32),
                item_fresh=(ex != prev_ex).astype(i32))


def _rope_tables():
    t = jnp.arange(DEC_SEQ)
    n_freq = HEAD_DIM // 4
    freqs = ROPE_THETA ** (-jnp.arange(n_freq, dtype=F32) / n_freq)
    ang = jnp.concatenate([(t // GRID_W).astype(F32)[:, None] * freqs,
                           (t % GRID_W).astype(F32)[:, None] * freqs], -1)
    cos, sin = jnp.cos(ang), jnp.sin(ang)
    cos_h = jnp.concatenate([cos, cos], -1)
    sin_h = jnp.concatenate([-sin, sin], -1)
    lat = lambda a: jnp.tile(a, (DEC_BATCH, 128 // HEAD_DIM))
    cos_t = jnp.concatenate([jnp.ones((N_CTX, 128), F32), lat(cos_h)], 0)
    sin_t = jnp.concatenate([jnp.zeros((N_CTX, 128), F32), lat(sin_h)], 0)
    return cos_t, sin_t


def _nat_bias_table(rpb):
    n_dr, n_dc = 2 * NAT_KH - 1, 2 * NAT_KW - 1
    cidx = np.arange(GRID_W)
    dc_idx = np.clip(cidx[None, :] - cidx[:, None] + NAT_KW - 1, 0, n_dc - 1)
    col_start = np.clip(cidx - NAT_KW // 2, 0, GRID_W - NAT_KW)
    col_in = (cidx[None, :] >= col_start[:, None]) & (cidx[None, :] < col_start[:, None] + NAT_KW)
    onehot = (dc_idx.reshape(1, -1) == np.arange(n_dc)[:, None]).astype(np.float32)
    t = jnp.dot(rpb.reshape(H_NAT * n_dr, n_dc), onehot, precision=lax.Precision.HIGHEST)
    t = jnp.where(col_in[None, None], t.reshape(H_NAT, n_dr, GRID_W, GRID_W), -jnp.inf)
    bias = jnp.stack([t[:, NAT_KH - 1 - p:2 * NAT_KH - 1 - p] for p in range(NAT_KH)], axis=0)
    return bias.transpose(0, 1, 3, 2, 4).reshape(NAT_KH, H_NAT * GRID_W, NAT_KH * GRID_W)


def _prep_w_in(w_in):
    cuts = np.cumsum((0,) + IN_SPLITS)
    seg = [w_in[:, cuts[i]:cuts[i + 1]] for i in range(len(IN_SPLITS))]
    gq, gk, gv, gg, ga, nq, nk, nv, aq, ak, av = seg
    aq = aq.reshape(D_MODEL, H_GQA, HEAD_DIM)[:, np.array(GQA_SLOT_HEADS), :].reshape(D_MODEL, GQA_QW)
    ga = jnp.pad(ga, ((0, 0), (0, 128 - 2 * GLA_LOWRANK)))
    return jnp.concatenate([gq, gk, gv, gg, nq, nk, nv, aq, ak, av, ga], axis=1).astype(BF16)


def _prep_w_a(w_a2, b_a):
    wa = jnp.zeros((128, 2 * GLA_W), F32)
    wa = wa.at[0:GLA_LOWRANK, 0:GLA_W].set(w_a2[0])
    wa = wa.at[GLA_LOWRANK:2 * GLA_LOWRANK, GLA_W:].set(w_a2[1])
    return wa.astype(BF16), b_a.reshape(1, 2 * GLA_W)


def _prep_w_out(w_out):
    gqa = w_out[GLA_W + NAT_W:].reshape(H_GQA, HEAD_DIM, D_MODEL)[np.array(GQA_SLOT_HEADS)]
    return jnp.concatenate([w_out[:GLA_W + NAT_W], gqa.reshape(GQA_QW, D_MODEL)], 0).astype(BF16)


def _state_to_blockdiag(s):
    out = jnp.zeros(s.shape[:-3] + (GLA_W, GLA_W), F32)
    for h in range(H_GLA):
        out = out.at[..., h * GLA_DV:(h + 1) * GLA_DV, h * GLA_DK:(h + 1) * GLA_DK].set(
            jnp.swapaxes(s[..., h, :, :], -1, -2))
    return out


def _blockdiag_to_state(st):
    blocks = [st[..., h * GLA_DV:(h + 1) * GLA_DV, h * GLA_DK:(h + 1) * GLA_DK] for h in range(H_GLA)]
    return jnp.swapaxes(jnp.stack(blocks, axis=-3), -1, -2)


def kernel(x_prompt, x_sample, state_gla, cache_nat_k, cache_nat_v, cache_gqa_k, cache_gqa_v, c, c_ctx, w_mod, b_mod, w_in, gla_w_a2, gla_b_a, gla_norm_g, nat_rpb, gqa_q_norm_g, gqa_k_norm_g, w_out, ln1_g, ln1_b, w_router, router_bias, w_expert_gu, w_expert_down, w_shared_gu, w_shared_down, ln2_g, ln2_b):
    x = jnp.concatenate([x_prompt.reshape(N_CTX, D_MODEL), x_sample.reshape(N_LAT, D_MODEL)], axis=0)
    cond = jnp.concatenate([c_ctx[None, :], c, jnp.zeros((N_COND - 1 - DEC_BATCH, D_MODEL), F32)], axis=0)
    mods = _modulation(cond, w_mod, b_mod)
    cos_t, sin_t = _rope_tables()
    row = lambda a: a.reshape(1, -1)

    st_gla, st_nk, st_nv, st_gk, st_gv = [], [], [], [], []
    for l in range(DEPTH):
        sh1, sc1, g1, sh2, sc2, g2 = [mods[l, :, j * D_MODEL:(j + 1) * D_MODEL] for j in range(6)]
        wa, ba = _prep_w_a(gla_w_a2[l], gla_b_a[l])
        qg = row(jnp.tile(gqa_q_norm_g[l], 128 // HEAD_DIM))
        kg = row(jnp.tile(gqa_k_norm_g[l], 128 // HEAD_DIM))
        pgla, la, pnat, q, k, v = _in_proj(x, sh1, sc1, _prep_w_in(w_in[l]), wa, ba, qg, kg, cos_t, sin_t)

        zero_st = jnp.zeros((BATCH, 2, GLA_W, GLA_W), F32)
        of_c, ob_c, st_c = _gla(pgla, la, zero_st, 0, BATCH, SEQ)
        of_l, ob_l, _ = _gla(pgla, la, _state_to_blockdiag(state_gla[:, l]), N_CTX, DEC_BATCH, DEC_SEQ)

        onat_c, ogqa_c = _ctx_attention(pnat, q, k, v)
        k_all = jnp.concatenate([k[N_CTX:].reshape(DEC_BATCH, DEC_SEQ, GQA_KW),
                                 cache_gqa_k[:, l].reshape(DEC_BATCH, PAST_LEN, GQA_KW)], axis=1).astype(BF16)
        v_all = jnp.concatenate([v[N_CTX:].reshape(DEC_BATCH, DEC_SEQ, GQA_KW),
                                 cache_gqa_v[:, l].reshape(DEC_BATCH, PAST_LEN, GQA_KW)], axis=1).astype(BF16)
        ogqa_l = _gqa_latent(q, k_all, v_all)
        onat_l = _nat_latent(pnat, cache_nat_k[:, l].reshape(DEC_BATCH, PAST_LEN, NAT_W),
                             cache_nat_v[:, l].reshape(DEC_BATCH, PAST_LEN, NAT_W),
                             _nat_bias_table(nat_rpb[l]))

        x1, xm, row_t, gate_t, pieces = _out_proj(
            x, jnp.concatenate([of_c, of_l], 0), jnp.concatenate([ob_c, ob_l], 0), pgla,
            jnp.concatenate([onat_c, onat_l], 0), jnp.concatenate([ogqa_c, ogqa_l], 0),
            _prep_w_out(w_out[l]), row(jnp.tile(gla_norm_g[l], H_GLA)), g1, row(ln1_g[l]), row(ln1_b[l]),
            sh2, sc2, w_router[l].T.astype(BF16), router_bias[l].reshape(N_EXPERTS, 1))
        plan = _moe_plan(pieces[:, ::128].T)
        y = _moe_experts(l, plan, _moe_dispatch(plan, xm, row_t), w_expert_gu, w_expert_down)
        x = _moe_combine(plan, y, row_t.T, gate_t.T, xm, x1, g2, w_shared_gu[l].astype(BF16),
                         w_shared_down[l].astype(BF16), row(ln2_g[l]), row(ln2_b[l]))

        st_gla.append(_blockdiag_to_state(st_c))
        st_nk.append(pnat[:N_CTX, NAT_W:2 * NAT_W].reshape(BATCH, SEQ, H_NAT, HEAD_DIM))
        st_nv.append(pnat[:N_CTX, 2 * NAT_W:].reshape(BATCH, SEQ, H_NAT, HEAD_DIM))
        st_gk.append(k[:N_CTX].reshape(BATCH, SEQ, KV_GQA, HEAD_DIM))
        st_gv.append(v[:N_CTX].reshape(BATCH, SEQ, KV_GQA, HEAD_DIM))

    y_prompt = x[:N_CTX].reshape(BATCH, SEQ, D_MODEL)
    y_sample = x[N_CTX:].reshape(DEC_BATCH, DEC_SEQ, D_MODEL)
    return (y_prompt, y_sample, jnp.stack(st_gla, axis=1), jnp.stack(st_nk, axis=1), jnp.stack(st_nv, axis=1),
            jnp.stack(st_gk, axis=1), jnp.stack(st_gv, axis=1))
```

```python
import functools

import numpy as np
import jax
import jax.numpy as jnp
from jax import lax
from jax.experimental import pallas as pl
from jax.experimental.pallas import tpu as pltpu

D_MODEL = 1024
BATCH = 16
SEQ = 256
DEPTH = 2
DEC_BATCH = 2
DEC_SEQ = 4096
PAST_LEN = 512
GRID_W = 64
HEAD_DIM = 64
H_GLA = 4
GLA_DK = 64
GLA_DV = 64
GLA_LOWRANK = 16
GLA_TAU = 16.0
H_NAT = 4
NAT_KH = 8
NAT_KW = 16
H_GQA = 8
KV_GQA = 2
ROPE_THETA = 10000.0
N_EXPERTS = 64
TOP_K = 8
N_GROUPS = 8
TOPK_GROUPS = 4
D_EXPERT = 256
D_SHARED = 256
ROUTE_SCALE = 2.5
EPS = 1e-6
ALPHA = (2 * DEPTH) ** 0.25
IN_SPLITS = (H_GLA * GLA_DK, H_GLA * GLA_DK, H_GLA * GLA_DV, H_GLA * GLA_DV, 2 * GLA_LOWRANK,
             H_NAT * HEAD_DIM, H_NAT * HEAD_DIM, H_NAT * HEAD_DIM,
             H_GQA * HEAD_DIM, KV_GQA * HEAD_DIM, KV_GQA * HEAD_DIM)

F32 = jnp.float32
BF16 = jnp.bfloat16

N_CTX = BATCH * SEQ
N_LAT = DEC_BATCH * DEC_SEQ
N_TOK = N_CTX + N_LAT
N_COND = 8
GLA_W = H_GLA * GLA_DK
NAT_W = H_NAT * HEAD_DIM
GQA_QW = H_GQA * HEAD_DIM
GQA_KW = KV_GQA * HEAD_DIM
W_IN_COLS = 4 * GLA_W + 3 * NAT_W + GQA_QW + 2 * GQA_KW + 128
GA_COL = W_IN_COLS - 128
GQA_SLOT_HEADS = (0, 4, 1, 5, 2, 6, 3, 7)
GLA_CHUNK = 32
GLA_TB = 256
ROWS = DEC_SEQ // GRID_W
NAT_ROWS_PER_STEP = 8
VMEM_LIMIT = 56 * 1024 * 1024

TM_PROJ = 256
MOE_TILE = 256
MOE_NT = N_TOK // MOE_TILE
PIECE = 16
COPY_ROWS = 2 * PIECE
MOE_CAP = MOE_TILE * TOP_K + N_EXPERTS * PIECE
MAX_BIG = MOE_CAP // COPY_ROWS
ROW_TILE = 512
XS_ROWS = N_TOK * TOP_K + MOE_NT * N_EXPERTS * (PIECE - 1)
assert XS_ROWS % ROW_TILE == 0
N_ITEMS = XS_ROWS // ROW_TILE + N_EXPERTS
TQ_GQA = 256


def _dot(a, b):
    return jnp.dot(a.astype(BF16), b.astype(BF16), preferred_element_type=F32)


def _dot_nt(a, b):
    return lax.dot_general(a.astype(BF16), b.astype(BF16), (((1,), (1,)), ((), ())),
                           preferred_element_type=F32)


def _dot_tn(a, b):
    return lax.dot_general(a.astype(BF16), b.astype(BF16), (((0,), (0,)), ((), ())),
                           preferred_element_type=F32)


def _dot_split(a, b_bf16):
    hi = a.astype(BF16)
    lo = (a - hi.astype(F32)).astype(BF16)
    return (jnp.dot(hi, b_bf16, preferred_element_type=F32)
            + jnp.dot(lo, b_bf16, preferred_element_type=F32))


def _sigmoid(x):
    return 1.0 / (1.0 + jnp.exp(-x))


def _silu(x):
    return x * _sigmoid(x)


def _ln(x):
    xc = x - jnp.mean(x, axis=-1, keepdims=True)
    return xc * lax.rsqrt(jnp.mean(xc * xc, axis=-1, keepdims=True) + EPS)


def _lane_group(shape, axis, width):
    return lax.shift_right_logical(lax.broadcasted_iota(jnp.int32, shape, axis), int(np.log2(width)))


def _head_mean_sq(x):
    w = x.shape[-1]
    bmat = jnp.where(_lane_group((w, w), 0, HEAD_DIM) == _lane_group((w, w), 1, HEAD_DIM),
                     1.0 / HEAD_DIM, 0.0).astype(BF16)
    return _dot_split(x * x, bmat)


def _swap_half_heads(x):
    w = x.shape[-1]
    lane = lax.broadcasted_iota(jnp.int32, x.shape, x.ndim - 1)
    first = (lane & (HEAD_DIM - 1)) < HEAD_DIM // 2
    return jnp.where(first, pltpu.roll(x, w - HEAD_DIM // 2, x.ndim - 1),
                     pltpu.roll(x, HEAD_DIM // 2, x.ndim - 1))


def _cond_row(tile, tm):
    r = tile * tm
    return jnp.where(r < N_CTX, 0, 1 + (r - N_CTX) // DEC_SEQ)


def _stack_heads(x, n, width):
    head = _lane_group(x.shape, 1, width)
    return jnp.concatenate([jnp.where(head == h, x, jnp.zeros_like(x)) for h in range(n)], axis=0)


def _unstack_heads(o, n, width):
    m = o.shape[0] // n
    head = _lane_group((m, o.shape[1]), 1, width)
    out = jnp.zeros((m, o.shape[1]), o.dtype)
    for h in range(n):
        out = jnp.where(head == h, o[h * m:(h + 1) * m], out)
    return out


def _mod_kernel(c_ref, w_ref, b_ref, o_ref):
    o_ref[0] = _dot(_silu(c_ref[...]), w_ref[0]) + b_ref[0]


def _modulation(cond, w_mod, b_mod):
    tn = 1536
    return pl.pallas_call(
        _mod_kernel,
        grid=(DEPTH, 6 * D_MODEL // tn),
        in_specs=[pl.BlockSpec((N_COND, D_MODEL), lambda l, j: (0, 0)),
                  pl.BlockSpec((1, D_MODEL, tn), lambda l, j: (l, 0, j)),
                  pl.BlockSpec((1, 1, tn), lambda l, j: (l, 0, j))],
        out_specs=pl.BlockSpec((1, N_COND, tn), lambda l, j: (l, 0, j)),
        out_shape=jax.ShapeDtypeStruct((DEPTH, N_COND, 6 * D_MODEL), F32),
        compiler_params=pltpu.CompilerParams(dimension_semantics=("arbitrary", "arbitrary"),
                                             vmem_limit_bytes=VMEM_LIMIT),
        name="modulation",
    )(cond, w_mod, b_mod.reshape(DEPTH, 1, 6 * D_MODEL))


def _inproj_kernel(x_ref, sh_ref, sc_ref, w_ref, wa_ref, ba_ref, qg_ref, kg_ref, cos_ref, sin_ref,
                   pgla_ref, la_ref, pnat_ref, q_ref, k_ref, v_ref):
    ci = _cond_row(pl.program_id(0), TM_PROJ)
    xn = _ln(x_ref[...])
    xm = xn * (1.0 + sc_ref[pl.ds(ci, 1), :]) + sh_ref[pl.ds(ci, 1), :]
    acc = _dot(xm, w_ref[...])
    pgla_ref[...] = acc[:, 0:4 * GLA_W]
    pnat_ref[...] = acc[:, 4 * GLA_W:4 * GLA_W + 3 * NAT_W]
    z = _dot(acc[:, GA_COL:GA_COL + 128], wa_ref[...]) + ba_ref[...]
    la_ref[...] = (jnp.minimum(z, 0.0) - jnp.log(1.0 + jnp.exp(-jnp.abs(z)))) * (1.0 / GLA_TAU)

    c0 = 4 * GLA_W + 3 * NAT_W
    cos = cos_ref[...]
    sin = sin_ref[...]

    def norm_rope(a, g):
        an = a * lax.rsqrt(_head_mean_sq(a) + EPS) * g
        return an * cos + _swap_half_heads(an) * sin

    qs = [norm_rope(acc[:, c0 + j * 128:c0 + (j + 1) * 128], qg_ref[...]) for j in range(GQA_QW // 128)]
    q_ref[...] = (jnp.concatenate(qs, axis=1) * (HEAD_DIM ** -0.5)).astype(BF16)
    k_ref[...] = norm_rope(acc[:, c0 + GQA_QW:c0 + GQA_QW + GQA_KW], kg_ref[...])
    v_ref[...] = acc[:, c0 + GQA_QW + GQA_KW:c0 + GQA_QW + 2 * GQA_KW]


def _in_proj(x, sh, sc, w, wa, ba, qg, kg, cos_t, sin_t):
    tm = TM_PROJ
    full = lambda a: pl.BlockSpec(a.shape, lambda i: (0,) * a.ndim)
    rows = lambda w_: pl.BlockSpec((tm, w_), lambda i: (i, 0))
    return pl.pallas_call(
        _inproj_kernel,
        grid=(N_TOK // tm,),
        in_specs=[rows(D_MODEL), full(sh), full(sc), full(w), full(wa), full(ba), full(qg), full(kg),
                  rows(128), rows(128)],
        out_specs=[rows(4 * GLA_W), rows(2 * GLA_W), rows(3 * NAT_W), rows(GQA_QW), rows(GQA_KW),
                   rows(GQA_KW)],
        out_shape=[jax.ShapeDtypeStruct((N_TOK, 4 * GLA_W), F32),
                   jax.ShapeDtypeStruct((N_TOK, 2 * GLA_W), F32),
                   jax.ShapeDtypeStruct((N_TOK, 3 * NAT_W), F32),
                   jax.ShapeDtypeStruct((N_TOK, GQA_QW), BF16),
                   jax.ShapeDtypeStruct((N_TOK, GQA_KW), F32),
                   jax.ShapeDtypeStruct((N_TOK, GQA_KW), F32)],
        compiler_params=pltpu.CompilerParams(dimension_semantics=("arbitrary",),
                                             vmem_limit_bytes=VMEM_LIMIT),
        name="in_proj",
    )(x, sh, sc, w, wa, ba, qg, kg, cos_t, sin_t)


def _gla_kernel(qf_ref, kf_ref, vf_ref, laf_ref, qb_ref, kb_ref, vb_ref, lab_ref, st0_ref,
                of_ref, ob_ref, stout_ref, st_ref):
    t = pl.program_id(1)
    last_t = pl.num_programs(1) - 1
    C = GLA_CHUNK
    nchunk = GLA_TB // C

    @pl.when(t == 0)
    def _():
        st_ref[...] = st0_ref[0]

    tb = GLA_TB
    shift = int(np.log2(C))
    r = lax.broadcasted_iota(jnp.int32, (tb, tb), 0)
    c = lax.broadcasted_iota(jnp.int32, (tb, tb), 1)
    same = lax.shift_right_logical(r, shift) == lax.shift_right_logical(c, shift)
    ra = lax.broadcasted_iota(jnp.int32, (tb, H_GLA * tb), 0)
    ca = lax.broadcasted_iota(jnp.int32, (tb, H_GLA * tb), 1) & (tb - 1)
    same_a = lax.shift_right_logical(ra, shift) == lax.shift_right_logical(ca, shift)
    blk = _lane_group((H_GLA * tb, GLA_W), 0, tb) == _lane_group((H_GLA * tb, GLA_W), 1, GLA_DK)
    diag = _lane_group((GLA_W, GLA_W), 0, GLA_DV) == _lane_group((GLA_W, GLA_W), 1, GLA_DK)
    one = lambda m: jnp.where(m, 1.0, 0.0).astype(BF16)
    chunk_ones = one(same)
    ref_pick = one(c == (r & ~(C - 1)) + C // 2)

    def direction(q_ref, k_ref, v_ref, la_ref, o_ref, d, causal):
        q = q_ref[...] * (GLA_DK ** -0.5)
        k = k_ref[...]
        v = v_ref[...]
        la = la_ref[...]
        la_hi = la.astype(BF16)
        la_lo = (la - la_hi.astype(F32)).astype(BF16)
        csum = lambda m: (jnp.dot(m, la_hi, preferred_element_type=F32)
                          + jnp.dot(m, la_lo, preferred_element_type=F32))
        b = csum(one(same & ((r >= c) if causal else (r <= c))))
        btot = csum(chunk_ones)
        bref = jnp.dot(ref_pick, b.astype(BF16), preferred_element_type=F32)
        q_att = q * jnp.exp(b - bref)
        k_att = k * jnp.exp(bref - b)
        kblk = jnp.where(blk, jnp.concatenate([k_att] * H_GLA, axis=0), 0.0)
        att = _dot_nt(q_att, kblk)
        att = jnp.where(same_a & ((ra >= ca) if causal else (ra <= ca)), att, 0.0)
        vblk = jnp.where(blk, jnp.concatenate([v] * H_GLA, axis=0), 0.0)
        o_intra = _dot(att, vblk)

        q_dec = (q * jnp.exp(b)).astype(BF16)
        k_dec = (k * jnp.exp(btot - b)).astype(BF16)
        g = jnp.exp(btot)
        vb = v.astype(BF16)
        st = st_ref[d]
        o_inter = [None] * nchunk
        for ci in (range(nchunk) if causal else reversed(range(nchunk))):
            rows = slice(ci * C, (ci + 1) * C)
            o_inter[ci] = _dot_nt(q_dec[rows], st)
            u = _dot_tn(vb[rows], k_dec[rows])
            st = st * g[ci * C:ci * C + 1, :] + jnp.where(diag, u, 0.0)
        st_ref[d] = st
        o_ref[...] = o_intra + jnp.concatenate(o_inter, axis=0)

    direction(qf_ref, kf_ref, vf_ref, laf_ref, of_ref, 0, True)
    direction(qb_ref, kb_ref, vb_ref, lab_ref, ob_ref, 1, False)

    @pl.when(t == last_t)
    def _():
        stout_ref[0] = st_ref[...]


def _gla(pgla, la, st0, row0, n_seq, seq_len):
    tb = GLA_TB
    nt = seq_len // tb
    b0 = row0 // tb
    fwd = lambda col: pl.BlockSpec((tb, GLA_W), lambda s, t: (b0 + s * nt + t, col))
    bwd = lambda col: pl.BlockSpec((tb, GLA_W), lambda s, t: (b0 + s * nt + nt - 1 - t, col))
    st_spec = pl.BlockSpec((1, 2, GLA_W, GLA_W), lambda s, t: (s, 0, 0, 0))
    n = n_seq * seq_len
    return pl.pallas_call(
        _gla_kernel,
        grid=(n_seq, nt),
        in_specs=[fwd(0), fwd(1), fwd(2), fwd(0), bwd(0), bwd(1), bwd(2), bwd(1), st_spec],
        out_specs=[pl.BlockSpec((tb, GLA_W), lambda s, t: (s * nt + t, 0)),
                   pl.BlockSpec((tb, GLA_W), lambda s, t: (s * nt + nt - 1 - t, 0)),
                   st_spec],
        out_shape=[jax.ShapeDtypeStruct((n, GLA_W), F32), jax.ShapeDtypeStruct((n, GLA_W), F32),
                   jax.ShapeDtypeStruct((n_seq, 2, GLA_W, GLA_W), F32)],
        scratch_shapes=[pltpu.VMEM((2, GLA_W, GLA_W), F32)],
        compiler_params=pltpu.CompilerParams(dimension_semantics=("arbitrary", "arbitrary"),
                                             vmem_limit_bytes=VMEM_LIMIT),
        name="gla_scan",
    )(pgla, pgla, pgla, la, pgla, pgla, pgla, la, st0)


def _softmax_pv(s_list, v_list):
    m = s_list[0].max(axis=-1, keepdims=True)
    for s in s_list[1:]:
        m = jnp.maximum(m, s.max(axis=-1, keepdims=True))
    acc = None
    l = None
    for s, v in zip(s_list, v_list):
        p = jnp.exp(s - m)
        pl_ = p.sum(axis=-1, keepdims=True)
        pv = _dot(p, v)
        acc = pv if acc is None else acc + pv
        l = pl_ if l is None else l + pl_
    return acc * (1.0 / l)


def _ctx_attn_kernel(pnat_ref, q_ref, k_ref, v_ref, onat_ref, ogqa_ref):
    nq = pnat_ref[:, 0:NAT_W] * (HEAD_DIM ** -0.5)
    nk = pnat_ref[:, NAT_W:2 * NAT_W].astype(BF16)
    nv = pnat_ref[:, 2 * NAT_W:3 * NAT_W].astype(BF16)
    o = _softmax_pv([_dot_nt(_stack_heads(nq, H_NAT, HEAD_DIM), nk)], [nv])
    onat_ref[...] = _unstack_heads(o, H_NAT, HEAD_DIM).astype(BF16)

    k = k_ref[...].astype(BF16)
    v = v_ref[...].astype(BF16)
    k2 = jnp.concatenate([k, k], axis=1)
    v2 = jnp.concatenate([v, v], axis=1)
    outs = []
    for half in range(2):
        q = q_ref[:, half * 256:(half + 1) * 256]
        o = _softmax_pv([_dot_nt(_stack_heads(q, 4, HEAD_DIM), k2)], [v2])
        outs.append(_unstack_heads(o, 4, HEAD_DIM))
    ogqa_ref[...] = jnp.concatenate(outs, axis=1).astype(BF16)


def _ctx_attention(pnat, q, k, v):
    rows = lambda w_: pl.BlockSpec((SEQ, w_), lambda i: (i, 0))
    return pl.pallas_call(
        _ctx_attn_kernel,
        grid=(BATCH,),
        in_specs=[rows(3 * NAT_W), rows(GQA_QW), rows(GQA_KW), rows(GQA_KW)],
        out_specs=[rows(NAT_W), rows(GQA_QW)],
        out_shape=[jax.ShapeDtypeStruct((N_CTX, NAT_W), BF16),
                   jax.ShapeDtypeStruct((N_CTX, GQA_QW), BF16)],
        compiler_params=pltpu.CompilerParams(dimension_semantics=("arbitrary",),
                                             vmem_limit_bytes=VMEM_LIMIT),
        name="ctx_attention",
    )(pnat, q, k, v)


def _gqa_lat_kernel(q_ref, k_ref, v_ref, o_ref):
    k = k_ref[0]
    v = v_ref[0]
    k2 = jnp.concatenate([k, k], axis=1)
    v2 = jnp.concatenate([v, v], axis=1)
    slot = _lane_group((TQ_GQA, 256), 1, HEAD_DIM)
    outs = []
    for half in range(2):
        q = q_ref[:, half * 256:(half + 1) * 256]
        out = jnp.zeros((TQ_GQA, 256), F32)
        for s in range(4):
            qs = jnp.where(slot == s, q, jnp.zeros_like(q))
            o = _softmax_pv([_dot_nt(qs, k2)], [v2])
            out = jnp.where(slot == s, o, out)
        outs.append(out)
    o_ref[...] = jnp.concatenate(outs, axis=1).astype(BF16)


def _gqa_latent(q, k_all, v_all):
    tq = TQ_GQA
    nq = DEC_SEQ // tq
    tk = k_all.shape[1]
    return pl.pallas_call(
        _gqa_lat_kernel,
        grid=(DEC_BATCH, nq),
        in_specs=[pl.BlockSpec((tq, GQA_QW), lambda b, i: (N_CTX // tq + b * nq + i, 0)),
                  pl.BlockSpec((1, tk, GQA_KW), lambda b, i: (b, 0, 0)),
                  pl.BlockSpec((1, tk, GQA_KW), lambda b, i: (b, 0, 0))],
        out_specs=pl.BlockSpec((tq, GQA_QW), lambda b, i: (b * nq + i, 0)),
        out_shape=jax.ShapeDtypeStruct((N_LAT, GQA_QW), BF16),
        compiler_params=pltpu.CompilerParams(dimension_semantics=("arbitrary", "arbitrary"),
                                             vmem_limit_bytes=VMEM_LIMIT),
        name="gqa_latent",
    )(q, k_all, v_all)


def _nat_lat_kernel(q_ref, k_ref, v_ref, kc_ref, vc_ref, bias_ref, o_ref):
    j = pl.program_id(1)
    kc = kc_ref[0].astype(BF16)
    vc = vc_ref[0].astype(BF16)
    for i in range(NAT_ROWS_PER_STEP):
        r = j * NAT_ROWS_PER_STEP + i
        r0 = jnp.clip(r - NAT_KH // 2, 0, ROWS - NAT_KH)
        win = pl.ds(pl.multiple_of(r0 * GRID_W, GRID_W), NAT_KH * GRID_W)
        q = q_ref[i * GRID_W:(i + 1) * GRID_W, :] * (HEAD_DIM ** -0.5)
        qs = _stack_heads(q, H_NAT, HEAD_DIM)
        s_loc = _dot_nt(qs, k_ref[win, :]) + bias_ref[r - r0]
        s_ctx = _dot_nt(qs, kc)
        o = _softmax_pv([s_loc, s_ctx], [v_ref[win, :], vc])
        o_ref[i * GRID_W:(i + 1) * GRID_W, :] = _unstack_heads(o, H_NAT, HEAD_DIM).astype(BF16)


def _nat_latent(pnat, kc, vc, bias):
    tq = NAT_ROWS_PER_STEP * GRID_W
    nq = DEC_SEQ // tq
    lat_blk = N_CTX // DEC_SEQ
    return pl.pallas_call(
        _nat_lat_kernel,
        grid=(DEC_BATCH, nq),
        in_specs=[pl.BlockSpec((tq, NAT_W), lambda b, j: (N_CTX // tq + b * nq + j, 0)),
                  pl.BlockSpec((DEC_SEQ, NAT_W), lambda b, j: (lat_blk + b, 1)),
                  pl.BlockSpec((DEC_SEQ, NAT_W), lambda b, j: (lat_blk + b, 2)),
                  pl.BlockSpec((1, PAST_LEN, NAT_W), lambda b, j: (b, 0, 0)),
                  pl.BlockSpec((1, PAST_LEN, NAT_W), lambda b, j: (b, 0, 0)),
                  pl.BlockSpec(bias.shape, lambda b, j: (0, 0, 0))],
        out_specs=pl.BlockSpec((tq, NAT_W), lambda b, j: (b * nq + j, 0)),
        out_shape=jax.ShapeDtypeStruct((N_LAT, NAT_W), BF16),
        compiler_params=pltpu.CompilerParams(dimension_semantics=("arbitrary", "arbitrary"),
                                             vmem_limit_bytes=VMEM_LIMIT),
        name="nat_latent",
    )(pnat, pnat, pnat, kc, vc, bias)


def _route_gates(logits_t, bias_ref):
    per = N_EXPERTS // N_GROUPS
    t = logits_t.shape[1]
    neg = -jnp.inf
    pos = lax.broadcasted_iota(jnp.int32, (per, t), 0)
    scores = [_sigmoid(logits_t[g * per:(g + 1) * per, :]) for g in range(N_GROUPS)]
    biased = [scores[g] + bias_ref[g * per:(g + 1) * per, :] for g in range(N_GROUPS)]

    grp = []
    for v in biased:
        m1 = jnp.max(v, axis=0, keepdims=True)
        i1 = jnp.min(jnp.where(v == m1, pos, per), axis=0, keepdims=True)
        m2 = jnp.max(jnp.where(pos == i1, neg, v), axis=0, keepdims=True)
        grp.append(m1 + m2)

    keep = [jnp.zeros((1, t), jnp.bool_) for _ in range(N_GROUPS)]
    for _ in range(TOPK_GROUPS):
        best = functools.reduce(jnp.maximum, grp)
        first = jnp.full((1, t), N_GROUPS, jnp.int32)
        for g in reversed(range(N_GROUPS)):
            first = jnp.where(grp[g] == best, g, first)
        for g in range(N_GROUPS):
            hit = first == g
            keep[g] = keep[g] | hit
            grp[g] = jnp.where(hit, neg, grp[g])

    cand = [jnp.where(keep[g], biased[g], neg) for g in range(N_GROUPS)]
    flat = [pos + g * per for g in range(N_GROUPS)]
    picks, weights = [], []
    for _ in range(TOP_K):
        best = functools.reduce(jnp.maximum, [jnp.max(v, axis=0, keepdims=True) for v in cand])
        first = functools.reduce(jnp.minimum, [
            jnp.min(jnp.where(cand[g] == best, flat[g], N_EXPERTS), axis=0, keepdims=True)
            for g in range(N_GROUPS)])
        score = jnp.zeros((1, t), F32)
        for g in range(N_GROUPS):
            hit = flat[g] == first
            score = score + jnp.sum(jnp.where(hit, scores[g], 0.0), axis=0, keepdims=True)
            cand[g] = jnp.where(hit, neg, cand[g])
        picks.append(first)
        weights.append(score)

    total = functools.reduce(lambda a, b: a + b, weights)
    gates = jnp.concatenate([v / total * ROUTE_SCALE for v in weights], axis=0)

    sel = [functools.reduce(lambda a, b: a | b, [flat[g] == k for k in picks]) for g in range(N_GROUPS)]
    sel = jnp.where(jnp.concatenate(sel, axis=0), 1.0, 0.0).astype(BF16)
    before = (lax.broadcasted_iota(jnp.int32, (t, t), 0) < lax.broadcasted_iota(jnp.int32, (t, t), 1))
    rank = jnp.dot(sel, jnp.where(before, 1.0, 0.0).astype(BF16), preferred_element_type=F32)
    count = jnp.sum(sel.astype(F32), axis=1, keepdims=True)
    pieces = jnp.floor((count + (PIECE - 1)) * (1.0 / PIECE))
    lower = (lax.broadcasted_iota(jnp.int32, (N_EXPERTS, N_EXPERTS), 1)
             < lax.broadcasted_iota(jnp.int32, (N_EXPERTS, N_EXPERTS), 0))
    start = PIECE * jnp.dot(jnp.where(lower, 1.0, 0.0).astype(BF16),
                            jnp.broadcast_to(pieces, (N_EXPERTS, 128)).astype(BF16),
                            preferred_element_type=F32)[:, 0:1]
    row = rank + start
    rows = []
    for k in picks:
        r = jnp.zeros((1, t), F32)
        for g in range(N_GROUPS):
            r = r + jnp.sum(jnp.where(flat[g] == k, row[g * per:(g + 1) * per, :], 0.0), axis=0, keepdims=True)
        rows.append(r)
    return (jnp.concatenate(rows, axis=0).astype(jnp.int32), gates,
            jnp.broadcast_to(pieces, (N_EXPERTS, 128)).astype(jnp.int32))


def _outproj_kernel(x_ref, of_ref, ob_ref, gg_ref, onat_ref, ogqa_ref, w_ref, ng_ref, g1_ref, lng_ref,
                    lnb_ref, sh2_ref, sc2_ref, wr_ref, rb_ref, x1_ref, xm_ref, row_ref, gate_ref, pieces_ref):
    ci = _cond_row(pl.program_id(0), TM_PROJ)
    og = of_ref[...] + ob_ref[...]
    halves = [og[:, j * 128:(j + 1) * 128] for j in range(GLA_W // 128)]
    ms = jnp.concatenate([_head_mean_sq(h) for h in halves], axis=1)
    ogla = og * lax.rsqrt(ms + EPS) * ng_ref[...] * _silu(gg_ref[...])
    mix = (_dot(ogla, w_ref[0:GLA_W, :]) + _dot(onat_ref[...], w_ref[GLA_W:GLA_W + NAT_W, :])
           + _dot(ogqa_ref[...], w_ref[GLA_W + NAT_W:, :]))
    x1 = _ln(ALPHA * x_ref[...] + g1_ref[pl.ds(ci, 1), :] * mix) * lng_ref[...] + lnb_ref[...]
    x1_ref[...] = x1
    xm = (_ln(x1) * (1.0 + sc2_ref[pl.ds(ci, 1), :]) + sh2_ref[pl.ds(ci, 1), :]).astype(BF16)
    xm_ref[...] = xm
    row_ref[...], gate_ref[...], pieces_ref[...] = _route_gates(_dot_nt(wr_ref[...], xm), rb_ref)


def _out_proj(x, of, ob, pgla, onat, ogqa, w, ng, g1, lng, lnb, sh2, sc2, wr, rb):
    tm = TM_PROJ
    full = lambda a: pl.BlockSpec(a.shape, lambda i: (0,) * a.ndim)
    rows = lambda w_: pl.BlockSpec((tm, w_), lambda i: (i, 0))
    return pl.pallas_call(
        _outproj_kernel,
        grid=(N_TOK // tm,),
        in_specs=[rows(D_MODEL), rows(GLA_W), rows(GLA_W), pl.BlockSpec((tm, GLA_W), lambda i: (i, 3)),
                  rows(NAT_W), rows(GQA_QW), full(w), full(ng), full(g1), full(lng), full(lnb),
                  full(sh2), full(sc2), full(wr), full(rb)],
        out_specs=[rows(D_MODEL), rows(D_MODEL), pl.BlockSpec((TOP_K, tm), lambda i: (0, i)),
                   pl.BlockSpec((TOP_K, tm), lambda i: (0, i)),
                   pl.BlockSpec((N_EXPERTS, 128), lambda i: (0, i))],
        out_shape=[jax.ShapeDtypeStruct((N_TOK, D_MODEL), F32),
                   jax.ShapeDtypeStruct((N_TOK, D_MODEL), BF16),
                   jax.ShapeDtypeStruct((TOP_K, N_TOK), jnp.int32),
                   jax.ShapeDtypeStruct((TOP_K, N_TOK), F32),
                   jax.ShapeDtypeStruct((N_EXPERTS, MOE_NT * 128), jnp.int32)],
        compiler_params=pltpu.CompilerParams(dimension_semantics=("arbitrary",),
                                             vmem_limit_bytes=VMEM_LIMIT),
        name="out_proj",
    )(x, of, ob, pgla, onat, ogqa, w, ng, g1, lng, lnb, sh2, sc2, wr, rb)


def _plan_specs(n_prefetch, **kw):
    return pltpu.PrefetchScalarGridSpec(num_scalar_prefetch=n_prefetch, **kw)


def _for_each_copy(t, copies, fn):
    bsrc_ref, bdst_ref, ssrc_ref, sdst_ref, nbig_ref, nsmall_ref = copies

    def big(i, c):
        j = t * MAX_BIG + i
        fn(pl.multiple_of(bsrc_ref[j], PIECE), pl.multiple_of(bdst_ref[j], PIECE), COPY_ROWS)
        return c

    def small(i, c):
        j = t * N_EXPERTS + i
        fn(pl.multiple_of(ssrc_ref[j], PIECE), pl.multiple_of(sdst_ref[j], PIECE), PIECE)
        return c

    lax.fori_loop(0, nbig_ref[t], big, 0)
    lax.fori_loop(0, nsmall_ref[t], small, 0)


def _wait_copies(n_big, n_small, make):
    for n, rows in ((n_big, COPY_ROWS), (n_small, PIECE)):
        def one(i, c, rows=rows):
            make(rows).wait()
            return c
        lax.fori_loop(0, n, one, 0)


def _dispatch_kernel(bsrc_ref, bdst_ref, ssrc_ref, sdst_ref, nbig_ref, nsmall_ref, tot_ref, xm_ref, ldt_ref, xs_ref,
                     stage_ref, zero_ref, sem_big, sem_small, tail_sem):
    copies = (bsrc_ref, bdst_ref, ssrc_ref, sdst_ref, nbig_ref, nsmall_ref)
    t = pl.program_id(0)
    nt = pl.num_programs(0)
    slot = lax.rem(t, 2)

    def run_copy(local_row, xs_row, rows, s):
        sem = sem_big if rows == COPY_ROWS else sem_small
        return pltpu.make_async_copy(stage_ref.at[s, pl.ds(local_row, rows)], xs_ref.at[pl.ds(xs_row, rows)],
                                     sem.at[s])

    def wait_tile(tt, s):
        _wait_copies(nbig_ref[tt], nsmall_ref[tt], lambda rows: run_copy(0, 0, rows, s))

    @pl.when(t >= 2)
    def _():
        wait_tile(t - 2, slot)

    p = lax.broadcasted_iota(jnp.int32, (MOE_CAP, MOE_TILE), 0)
    hit = p == ldt_ref[0:1, :]
    for k in range(1, TOP_K):
        hit = hit | (p == ldt_ref[k:k + 1, :])
    stage_ref[slot] = jnp.dot(jnp.where(hit, 1.0, 0.0).astype(BF16), xm_ref[...],
                              preferred_element_type=F32).astype(BF16)
    _for_each_copy(t, copies, lambda a, b, rows: run_copy(a, b, rows, slot).start())

    @pl.when(t == nt - 1)
    def _():
        zero_ref[...] = jnp.zeros_like(zero_ref)
        total = tot_ref[0]
        first_free_tile = (total + ROW_TILE - 1) // ROW_TILE
        n_piece = (first_free_tile * ROW_TILE - total) // PIECE
        n_tile = XS_ROWS // ROW_TILE - first_free_tile

        def zero_piece(i):
            return pltpu.make_async_copy(zero_ref.at[pl.ds(0, PIECE)],
                                         xs_ref.at[pl.ds(pl.multiple_of(total + i * PIECE, PIECE), PIECE)],
                                         tail_sem.at[0])

        def zero_tile(i):
            row = pl.multiple_of((first_free_tile + i) * ROW_TILE, ROW_TILE)
            return pltpu.make_async_copy(zero_ref, xs_ref.at[pl.ds(row, ROW_TILE)], tail_sem.at[1])

        def run(n, make, op):
            def body(i, c):
                op(make(i))
                return c
            lax.fori_loop(0, n, body, 0)

        run(n_piece, zero_piece, lambda cp: cp.start())
        run(n_tile, zero_tile, lambda cp: cp.start())
        run(n_piece, zero_piece, lambda cp: cp.wait())
        run(n_tile, zero_tile, lambda cp: cp.wait())
        wait_tile(t - 1, 1 - slot)
        wait_tile(t, slot)


def _moe_dispatch(plan, xm, row_t):
    return pl.pallas_call(
        _dispatch_kernel,
        grid_spec=_plan_specs(
            7, grid=(MOE_NT,),
            in_specs=[pl.BlockSpec((MOE_TILE, D_MODEL), lambda t, *_: (t, 0)),
                      pl.BlockSpec((TOP_K, MOE_TILE), lambda t, *_: (0, t))],
            out_specs=pl.BlockSpec(memory_space=pl.ANY),
            scratch_shapes=[pltpu.VMEM((2, MOE_CAP, D_MODEL), BF16), pltpu.VMEM((ROW_TILE, D_MODEL), BF16),
                            pltpu.SemaphoreType.DMA((2,)), pltpu.SemaphoreType.DMA((2,)),
                            pltpu.SemaphoreType.DMA((2,))]),
        out_shape=jax.ShapeDtypeStruct((XS_ROWS, D_MODEL), BF16),
        compiler_params=pltpu.CompilerParams(dimension_semantics=("arbitrary",),
                                             vmem_limit_bytes=VMEM_LIMIT),
        name="moe_dispatch",
    )(*plan["copies"], plan["total"], xm, row_t)


def _experts_kernel(tile_ref, exp_ref, lo_ref, hi_ref, first_ref, fresh_ref, xs_ref, wgu_ref, wd_ref, y_ref,
                    wgu_bf_ref, wd_bf_ref):
    w = pl.program_id(0)
    lo, hi, first = lo_ref[w], hi_ref[w], first_ref[w]

    @pl.when(fresh_ref[w] == 1)
    def _():
        wgu_bf_ref[...] = wgu_ref[0].astype(BF16)
        wd_bf_ref[...] = wd_ref[0].astype(BF16)

    def rows_of_this_expert():
        ab = jnp.dot(xs_ref[...], wgu_bf_ref[...], preferred_element_type=F32)
        h = (_silu(ab[:, :D_EXPERT]) * ab[:, D_EXPERT:]).astype(BF16)
        out = jnp.dot(h, wd_bf_ref[...], preferred_element_type=F32).astype(BF16)
        row = lax.broadcasted_iota(jnp.int32, (ROW_TILE, 1), 0)
        return (row >= lo) & (row < hi), out

    @pl.when((hi > lo) & (first == 1))
    def _():
        mine, out = rows_of_this_expert()
        y_ref[...] = jnp.where(mine, out, jnp.zeros_like(out))

    @pl.when((hi > lo) & (first == 0))
    def _():
        mine, out = rows_of_this_expert()
        y_ref[...] = jnp.where(mine, out, y_ref[...])

    @pl.when((hi <= lo) & (first == 1))
    def _():
        y_ref[...] = jnp.zeros_like(y_ref)


def _moe_experts(layer, plan, xs, wgu, wd):
    return pl.pallas_call(
        _experts_kernel,
        grid_spec=_plan_specs(
            6, grid=(N_ITEMS,),
            in_specs=[pl.BlockSpec((ROW_TILE, D_MODEL), lambda w, tile, *_: (tile[w], 0)),
                      pl.BlockSpec((None, 1, D_MODEL, 2 * D_EXPERT), lambda w, tile, ex, *_: (layer, ex[w], 0, 0)),
                      pl.BlockSpec((None, 1, D_EXPERT, D_MODEL), lambda w, tile, ex, *_: (layer, ex[w], 0, 0))],
            out_specs=pl.BlockSpec((ROW_TILE, D_MODEL), lambda w, tile, *_: (tile[w], 0)),
            scratch_shapes=[pltpu.VMEM((D_MODEL, 2 * D_EXPERT), BF16), pltpu.VMEM((D_EXPERT, D_MODEL), BF16)]),
        out_shape=jax.ShapeDtypeStruct((XS_ROWS, D_MODEL), BF16),
        compiler_params=pltpu.CompilerParams(dimension_semantics=("arbitrary",),
                                             vmem_limit_bytes=VMEM_LIMIT),
        name="moe_experts",
    )(plan["item_tile"], plan["item_expert"], plan["item_lo"], plan["item_hi"], plan["item_first"],
      plan["item_fresh"], xs, wgu, wd)


def _combine_kernel(bsrc_ref, bdst_ref, ssrc_ref, sdst_ref, nbig_ref, nsmall_ref, y_ref, ldt_ref, gate_ref, xm_ref,
                    x1_ref, g2_ref, wsgu_ref, wsd_ref, lng_ref, lnb_ref, out_ref, ybuf_ref, sem_big, sem_small):
    copies = (bsrc_ref, bdst_ref, ssrc_ref, sdst_ref, nbig_ref, nsmall_ref)
    t = pl.program_id(0)
    nt = pl.num_programs(0)
    slot = lax.rem(t, 2)
    ci = _cond_row(t, MOE_TILE)

    def run_copy(local_row, y_row, rows, s):
        sem = sem_big if rows == COPY_ROWS else sem_small
        return pltpu.make_async_copy(y_ref.at[pl.ds(y_row, rows)], ybuf_ref.at[s, pl.ds(local_row, rows)], sem.at[s])

    def fetch(tt, s):
        _for_each_copy(tt, copies, lambda a, b, rows: run_copy(a, b, rows, s).start())

    @pl.when(t == 0)
    def _():
        ybuf_ref[...] = jnp.zeros_like(ybuf_ref)
        fetch(0, 0)

    @pl.when(t + 1 < nt)
    def _():
        fetch(t + 1, 1 - slot)

    _wait_copies(nbig_ref[t], nsmall_ref[t], lambda rows: run_copy(0, 0, rows, slot))

    p = lax.broadcasted_iota(jnp.int32, (MOE_CAP, MOE_TILE), 0)
    wmat = jnp.zeros((MOE_CAP, MOE_TILE), F32)
    for k in range(TOP_K):
        wmat = jnp.where(p == ldt_ref[k:k + 1, :], gate_ref[k:k + 1, :], wmat)
    w_hi = wmat.astype(BF16)
    w_lo = (wmat - w_hi.astype(F32)).astype(BF16)
    y = ybuf_ref[slot]
    routed = _dot_tn(w_hi, y) + _dot_tn(w_lo, y)

    sab = jnp.dot(xm_ref[...], wsgu_ref[...], preferred_element_type=F32)
    shared = _dot(_silu(sab[:, :D_SHARED]) * sab[:, D_SHARED:], wsd_ref[...])
    z = ALPHA * x1_ref[...] + g2_ref[pl.ds(ci, 1), :] * (routed + shared)
    out_ref[...] = _ln(z) * lng_ref[...] + lnb_ref[...]


def _moe_combine(plan, y, row_t, gate_t, xm, x1, g2, wsgu, wsd, lng, lnb):
    full = lambda a: pl.BlockSpec(a.shape, lambda t, *_: (0,) * a.ndim)
    rows = lambda w_: pl.BlockSpec((MOE_TILE, w_), lambda t, *_: (t, 0))
    picks = pl.BlockSpec((TOP_K, MOE_TILE), lambda t, *_: (0, t))
    return pl.pallas_call(
        _combine_kernel,
        grid_spec=_plan_specs(
            6, grid=(MOE_NT,),
            in_specs=[pl.BlockSpec(memory_space=pl.ANY), picks, picks, rows(D_MODEL), rows(D_MODEL),
                      full(g2), full(wsgu), full(wsd), full(lng), full(lnb)],
            out_specs=rows(D_MODEL),
            scratch_shapes=[pltpu.VMEM((2, MOE_CAP, D_MODEL), BF16), pltpu.SemaphoreType.DMA((2,)),
                            pltpu.SemaphoreType.DMA((2,))]),
        out_shape=jax.ShapeDtypeStruct((N_TOK, D_MODEL), F32),
        compiler_params=pltpu.CompilerParams(dimension_semantics=("arbitrary",),
                                             vmem_limit_bytes=VMEM_LIMIT),
        name="moe_combine",
    )(*plan["copies"], y, row_t, gate_t, xm, x1, g2, wsgu, wsd, lng, lnb)


def _moe_plan(npc):
    i32 = jnp.int32
    run = npc * PIECE
    lo = jnp.cumsum(run, axis=1) - run
    tot_e = jnp.sum(run, axis=0)
    ends = jnp.cumsum(tot_e)
    off = ends - tot_e
    hs = off[None, :] + jnp.cumsum(run, axis=0) - run

    per_copy = COPY_ROWS // PIECE
    n_big = npc // per_copy
    odd = npc - n_big * per_copy

    def expand(count, width):
        end = jnp.cumsum(count, axis=1)
        i = jnp.arange(width, dtype=i32)[None, :, None]
        mine = (i >= (end - count)[:, None, :]) & (i < end[:, None, :])
        pick = lambda a: jnp.sum(jnp.where(mine, a[:, None, :], 0), axis=2)
        return pick, i[:, :, 0] - pick(end - count)

    pick_b, k_b = expand(n_big, MAX_BIG)
    pick_s, _ = expand(odd, N_EXPERTS)
    copies = [pick_b(lo) + k_b * COPY_ROWS, pick_b(hs) + k_b * COPY_ROWS,
              pick_s(lo + n_big * COPY_ROWS), pick_s(hs + n_big * COPY_ROWS)]
    copies = [c.reshape(-1).astype(i32) for c in copies] + [jnp.sum(n_big, axis=1).astype(i32),
                                                            jnp.sum(odd, axis=1).astype(i32)]

    first_tile = off // ROW_TILE
    n_tile = jnp.where(tot_e > 0, (ends - 1) // ROW_TILE - first_tile + 1, 0)
    item_end = jnp.cumsum(n_tile)
    item_start = item_end - n_tile
    w = jnp.arange(N_ITEMS, dtype=i32)
    valid = w < item_end[-1]
    wc = jnp.minimum(w, item_end[-1] - 1)
    ex = jnp.sum((wc[:, None] >= item_end[None, :]).astype(i32), axis=1)
    tile = first_tile[ex] + wc - item_start[ex]
    r_lo = jnp.where(valid, jnp.clip(off[ex] - tile * ROW_TILE, 0, ROW_TILE), 0)
    r_hi = jnp.where(valid, jnp.clip(ends[ex] - tile * ROW_TILE, 0, ROW_TILE), 0)
    n_tiles = XS_ROWS // ROW_TILE
    spare = (ends[-1] + ROW_TILE - 1) // ROW_TILE + w - item_end[-1]
    tile = jnp.where(valid, tile, jnp.minimum(spare, n_tiles - 1))
    valid = valid | (spare < n_tiles)
    prev = jnp.concatenate([jnp.full((1,), -1, i32), tile[:-1]])
    prev_ex = jnp.concatenate([jnp.full((1,), -1, i32), ex[:-1]])
    return dict(copies=copies, total=ends[-1:].astype(i32),
                item_tile=tile.astype(i32), item_expert=ex.astype(i32), item_lo=r_lo.astype(i32),
                item_hi=r_hi.astype(i32), item_first=(valid & (tile != prev)).astype(i32),
                item_fresh=(ex != prev_ex).astype(i32))


def _rope_tables():
    t = jnp.arange(DEC_SEQ)
    n_freq = HEAD_DIM // 4
    freqs = ROPE_THETA ** (-jnp.arange(n_freq, dtype=F32) / n_freq)
    ang = jnp.concatenate([(t // GRID_W).astype(F32)[:, None] * freqs,
                           (t % GRID_W).astype(F32)[:, None] * freqs], -1)
    cos, sin = jnp.cos(ang), jnp.sin(ang)
    cos_h = jnp.concatenate([cos, cos], -1)
    sin_h = jnp.concatenate([-sin, sin], -1)
    lat = lambda a: jnp.tile(a, (DEC_BATCH, 128 // HEAD_DIM))
    cos_t = jnp.concatenate([jnp.ones((N_CTX, 128), F32), lat(cos_h)], 0)
    sin_t = jnp.concatenate([jnp.zeros((N_CTX, 128), F32), lat(sin_h)], 0)
    return cos_t, sin_t


def _nat_bias_table(rpb):
    n_dr, n_dc = 2 * NAT_KH - 1, 2 * NAT_KW - 1
    cidx = np.arange(GRID_W)
    dc_idx = np.clip(cidx[None, :] - cidx[:, None] + NAT_KW - 1, 0, n_dc - 1)
    col_start = np.clip(cidx - NAT_KW // 2, 0, GRID_W - NAT_KW)
    col_in = (cidx[None, :] >= col_start[:, None]) & (cidx[None, :] < col_start[:, None] + NAT_KW)
    onehot = (dc_idx.reshape(1, -1) == np.arange(n_dc)[:, None]).astype(np.float32)
    t = jnp.dot(rpb.reshape(H_NAT * n_dr, n_dc), onehot, precision=lax.Precision.HIGHEST)
    t = jnp.where(col_in[None, None], t.reshape(H_NAT, n_dr, GRID_W, GRID_W), -jnp.inf)
    bias = jnp.stack([t[:, NAT_KH - 1 - p:2 * NAT_KH - 1 - p] for p in range(NAT_KH)], axis=0)
    return bias.transpose(0, 1, 3, 2, 4).reshape(NAT_KH, H_NAT * GRID_W, NAT_KH * GRID_W)


def _prep_w_in(w_in):
    cuts = np.cumsum((0,) + IN_SPLITS)
    seg = [w_in[:, cuts[i]:cuts[i + 1]] for i in range(len(IN_SPLITS))]
    gq, gk, gv, gg, ga, nq, nk, nv, aq, ak, av = seg
    aq = aq.reshape(D_MODEL, H_GQA, HEAD_DIM)[:, np.array(GQA_SLOT_HEADS), :].reshape(D_MODEL, GQA_QW)
    ga = jnp.pad(ga, ((0, 0), (0, 128 - 2 * GLA_LOWRANK)))
    return jnp.concatenate([gq, gk, gv, gg, nq, nk, nv, aq, ak, av, ga], axis=1).astype(BF16)


def _prep_w_a(w_a2, b_a):
    wa = jnp.zeros((128, 2 * GLA_W), F32)
    wa = wa.at[0:GLA_LOWRANK, 0:GLA_W].set(w_a2[0])
    wa = wa.at[GLA_LOWRANK:2 * GLA_LOWRANK, GLA_W:].set(w_a2[1])
    return wa.astype(BF16), b_a.reshape(1, 2 * GLA_W)


def _prep_w_out(w_out):
    gqa = w_out[GLA_W + NAT_W:].reshape(H_GQA, HEAD_DIM, D_MODEL)[np.array(GQA_SLOT_HEADS)]
    return jnp.concatenate([w_out[:GLA_W + NAT_W], gqa.reshape(GQA_QW, D_MODEL)], 0).astype(BF16)


def _state_to_blockdiag(s):
    out = jnp.zeros(s.shape[:-3] + (GLA_W, GLA_W), F32)
    for h in range(H_GLA):
        out = out.at[..., h * GLA_DV:(h + 1) * GLA_DV, h * GLA_DK:(h + 1) * GLA_DK].set(
            jnp.swapaxes(s[..., h, :, :], -1, -2))
    return out


def _blockdiag_to_state(st):
    blocks = [st[..., h * GLA_DV:(h + 1) * GLA_DV, h * GLA_DK:(h + 1) * GLA_DK] for h in range(H_GLA)]
    return jnp.swapaxes(jnp.stack(blocks, axis=-3), -1, -2)


def kernel(x_prompt, x_sample, state_gla, cache_nat_k, cache_nat_v, cache_gqa_k, cache_gqa_v, c, c_ctx, w_mod, b_mod, w_in, gla_w_a2, gla_b_a, gla_norm_g, nat_rpb, gqa_q_norm_g, gqa_k_norm_g, w_out, ln1_g, ln1_b, w_router, router_bias, w_expert_gu, w_expert_down, w_shared_gu, w_shared_down, ln2_g, ln2_b):
    x = jnp.concatenate([x_prompt.reshape(N_CTX, D_MODEL), x_sample.reshape(N_LAT, D_MODEL)], axis=0)
    cond = jnp.concatenate([c_ctx[None, :], c, jnp.zeros((N_COND - 1 - DEC_BATCH, D_MODEL), F32)], axis=0)
    mods = _modulation(cond, w_mod, b_mod)
    cos_t, sin_t = _rope_tables()
    row = lambda a: a.reshape(1, -1)

    st_gla, st_nk, st_nv, st_gk, st_gv = [], [], [], [], []
    for l in range(DEPTH):
        sh1, sc1, g1, sh2, sc2, g2 = [mods[l, :, j * D_MODEL:(j + 1) * D_MODEL] for j in range(6)]
        wa, ba = _prep_w_a(gla_w_a2[l], gla_b_a[l])
        qg = row(jnp.tile(gqa_q_norm_g[l], 128 // HEAD_DIM))
        kg = row(jnp.tile(gqa_k_norm_g[l], 128 // HEAD_DIM))
        pgla, la, pnat, q, k, v = _in_proj(x, sh1, sc1, _prep_w_in(w_in[l]), wa, ba, qg, kg, cos_t, sin_t)

        zero_st = jnp.zeros((BATCH, 2, GLA_W, GLA_W), F32)
        of_c, ob_c, st_c = _gla(pgla, la, zero_st, 0, BATCH, SEQ)
        of_l, ob_l, _ = _gla(pgla, la, _state_to_blockdiag(state_gla[:, l]), N_CTX, DEC_BATCH, DEC_SEQ)

        onat_c, ogqa_c = _ctx_attention(pnat, q, k, v)
        k_all = jnp.concatenate([k[N_CTX:].reshape(DEC_BATCH, DEC_SEQ, GQA_KW),
                                 cache_gqa_k[:, l].reshape(DEC_BATCH, PAST_LEN, GQA_KW)], axis=1).astype(BF16)
        v_all = jnp.concatenate([v[N_CTX:].reshape(DEC_BATCH, DEC_SEQ, GQA_KW),
                                 cache_gqa_v[:, l].reshape(DEC_BATCH, PAST_LEN, GQA_KW)], axis=1).astype(BF16)
        ogqa_l = _gqa_latent(q, k_all, v_all)
        onat_l = _nat_latent(pnat, cache_nat_k[:, l].reshape(DEC_BATCH, PAST_LEN, NAT_W),
                             cache_nat_v[:, l].reshape(DEC_BATCH, PAST_LEN, NAT_W),
                             _nat_bias_table(nat_rpb[l]))

        x1, xm, row_t, gate_t, pieces = _out_proj(
            x, jnp.concatenate([of_c, of_l], 0), jnp.concatenate([ob_c, ob_l], 0), pgla,
            jnp.concatenate([onat_c, onat_l], 0), jnp.concatenate([ogqa_c, ogqa_l], 0),
            _prep_w_out(w_out[l]), row(jnp.tile(gla_norm_g[l], H_GLA)), g1, row(ln1_g[l]), row(ln1_b[l]),
            sh2, sc2, w_router[l].T.astype(BF16), router_bias[l].reshape(N_EXPERTS, 1))
        plan = _moe_plan(pieces[:, ::128].T)
        y = _moe_experts(l, plan, _moe_dispatch(plan, xm, row_t), w_expert_gu, w_expert_down)
        x = _moe_combine(plan, y, row_t, gate_t, xm, x1, g2, w_shared_gu[l].astype(BF16),
                         w_shared_down[l].astype(BF16), row(ln2_g[l]), row(ln2_b[l]))

        st_gla.append(_blockdiag_to_state(st_c))
        st_nk.append(pnat[:N_CTX, NAT_W:2 * NAT_W].reshape(BATCH, SEQ, H_NAT, HEAD_DIM))
        st_nv.append(pnat[:N_CTX, 2 * NAT_W:].reshape(BATCH, SEQ, H_NAT, HEAD_DIM))
        st_gk.append(k[:N_CTX].reshape(BATCH, SEQ, KV_GQA, HEAD_DIM))
        st_gv.append(v[:N_CTX].reshape(BATCH, SEQ, KV_GQA, HEAD_DIM))

    y_prompt = x[:N_CTX].reshape(BATCH, SEQ, D_MODEL)
    y_sample = x[N_CTX:].reshape(DEC_BATCH, DEC_SEQ, D_MODEL)
    return (y_prompt, y_sample, jnp.stack(st_gla, axis=1), jnp.stack(st_nk, axis=1), jnp.stack(st_nv, axis=1),
            jnp.stack(st_gk, axis=1), jnp.stack(st_gv, axis=1))
```

```python
import functools

import numpy as np
import jax
import jax.numpy as jnp
from jax import lax
from jax.experimental import pallas as pl
from jax.experimental.pallas import tpu as pltpu

D_MODEL = 1024
BATCH = 16
SEQ = 256
DEPTH = 2
DEC_BATCH = 2
DEC_SEQ = 4096
PAST_LEN = 512
GRID_W = 64
HEAD_DIM = 64
H_GLA = 4
GLA_DK = 64
GLA_DV = 64
GLA_LOWRANK = 16
GLA_TAU = 16.0
H_NAT = 4
NAT_KH = 8
NAT_KW = 16
H_GQA = 8
KV_GQA = 2
ROPE_THETA = 10000.0
N_EXPERTS = 64
TOP_K = 8
N_GROUPS = 8
TOPK_GROUPS = 4
D_EXPERT = 256
D_SHARED = 256
ROUTE_SCALE = 2.5
EPS = 1e-6
ALPHA = (2 * DEPTH) ** 0.25
IN_SPLITS = (H_GLA * GLA_DK, H_GLA * GLA_DK, H_GLA * GLA_DV, H_GLA * GLA_DV, 2 * GLA_LOWRANK,
             H_NAT * HEAD_DIM, H_NAT * HEAD_DIM, H_NAT * HEAD_DIM,
             H_GQA * HEAD_DIM, KV_GQA * HEAD_DIM, KV_GQA * HEAD_DIM)

F32 = jnp.float32
BF16 = jnp.bfloat16

N_CTX = BATCH * SEQ
N_LAT = DEC_BATCH * DEC_SEQ
N_TOK = N_CTX + N_LAT
N_COND = 8
GLA_W = H_GLA * GLA_DK
NAT_W = H_NAT * HEAD_DIM
GQA_QW = H_GQA * HEAD_DIM
GQA_KW = KV_GQA * HEAD_DIM
W_IN_COLS = 4 * GLA_W + 3 * NAT_W + GQA_QW + 2 * GQA_KW + 128
GA_COL = W_IN_COLS - 128
GQA_SLOT_HEADS = (0, 4, 1, 5, 2, 6, 3, 7)
GLA_CHUNK = 32
GLA_TB = 256
ROWS = DEC_SEQ // GRID_W
NAT_ROWS_PER_STEP = 8
VMEM_LIMIT = 56 * 1024 * 1024

TM_PROJ = 256
MOE_TILE = 256
MOE_NT = N_TOK // MOE_TILE
PIECE = 16
COPY_ROWS = 2 * PIECE
MOE_CAP = MOE_TILE * TOP_K + N_EXPERTS * PIECE
MAX_BIG = MOE_CAP // COPY_ROWS
MOE_CHUNK = 512
ROW_TILE = 512
XS_ROWS = N_TOK * TOP_K + MOE_NT * N_EXPERTS * (PIECE - 1)
assert XS_ROWS % ROW_TILE == 0
N_ITEMS = XS_ROWS // ROW_TILE + N_EXPERTS
TQ_GQA = 256


def _dot(a, b):
    return jnp.dot(a.astype(BF16), b.astype(BF16), preferred_element_type=F32)


def _dot_nt(a, b):
    return lax.dot_general(a.astype(BF16), b.astype(BF16), (((1,), (1,)), ((), ())),
                           preferred_element_type=F32)


def _dot_tn(a, b):
    return lax.dot_general(a.astype(BF16), b.astype(BF16), (((0,), (0,)), ((), ())),
                           preferred_element_type=F32)


def _dot_split(a, b_bf16):
    hi = a.astype(BF16)
    lo = (a - hi.astype(F32)).astype(BF16)
    return (jnp.dot(hi, b_bf16, preferred_element_type=F32)
            + jnp.dot(lo, b_bf16, preferred_element_type=F32))


def _sigmoid(x):
    return 1.0 / (1.0 + jnp.exp(-x))


def _silu(x):
    return x * _sigmoid(x)


def _ln(x):
    xc = x - jnp.mean(x, axis=-1, keepdims=True)
    return xc * lax.rsqrt(jnp.mean(xc * xc, axis=-1, keepdims=True) + EPS)


def _lane_group(shape, axis, width):
    return lax.shift_right_logical(lax.broadcasted_iota(jnp.int32, shape, axis), int(np.log2(width)))


def _head_mean_sq(x):
    w = x.shape[-1]
    bmat = jnp.where(_lane_group((w, w), 0, HEAD_DIM) == _lane_group((w, w), 1, HEAD_DIM),
                     1.0 / HEAD_DIM, 0.0).astype(BF16)
    return _dot_split(x * x, bmat)


def _swap_half_heads(x):
    w = x.shape[-1]
    lane = lax.broadcasted_iota(jnp.int32, x.shape, x.ndim - 1)
    first = (lane & (HEAD_DIM - 1)) < HEAD_DIM // 2
    return jnp.where(first, pltpu.roll(x, w - HEAD_DIM // 2, x.ndim - 1),
                     pltpu.roll(x, HEAD_DIM // 2, x.ndim - 1))


def _cond_row(tile, tm):
    r = tile * tm
    return jnp.where(r < N_CTX, 0, 1 + (r - N_CTX) // DEC_SEQ)


def _stack_heads(x, n, width):
    head = _lane_group(x.shape, 1, width)
    return jnp.concatenate([jnp.where(head == h, x, jnp.zeros_like(x)) for h in range(n)], axis=0)


def _unstack_heads(o, n, width):
    m = o.shape[0] // n
    head = _lane_group((m, o.shape[1]), 1, width)
    out = jnp.zeros((m, o.shape[1]), o.dtype)
    for h in range(n):
        out = jnp.where(head == h, o[h * m:(h + 1) * m], out)
    return out


def _mod_kernel(c_ref, w_ref, b_ref, o_ref):
    o_ref[0] = _dot(_silu(c_ref[...]), w_ref[0]) + b_ref[0]


def _modulation(cond, w_mod, b_mod):
    tn = 1536
    return pl.pallas_call(
        _mod_kernel,
        grid=(DEPTH, 6 * D_MODEL // tn),
        in_specs=[pl.BlockSpec((N_COND, D_MODEL), lambda l, j: (0, 0)),
                  pl.BlockSpec((1, D_MODEL, tn), lambda l, j: (l, 0, j)),
                  pl.BlockSpec((1, 1, tn), lambda l, j: (l, 0, j))],
        out_specs=pl.BlockSpec((1, N_COND, tn), lambda l, j: (l, 0, j)),
        out_shape=jax.ShapeDtypeStruct((DEPTH, N_COND, 6 * D_MODEL), F32),
        compiler_params=pltpu.CompilerParams(dimension_semantics=("arbitrary", "arbitrary"),
                                             vmem_limit_bytes=VMEM_LIMIT),
        name="modulation",
    )(cond, w_mod, b_mod.reshape(DEPTH, 1, 6 * D_MODEL))


def _inproj_kernel(x_ref, sh_ref, sc_ref, w_ref, wa_ref, ba_ref, qg_ref, kg_ref, cos_ref, sin_ref,
                   pgla_ref, la_ref, pnat_ref, q_ref, k_ref, v_ref):
    ci = _cond_row(pl.program_id(0), TM_PROJ)
    xn = _ln(x_ref[...])
    xm = xn * (1.0 + sc_ref[pl.ds(ci, 1), :]) + sh_ref[pl.ds(ci, 1), :]
    acc = _dot(xm, w_ref[...])
    pgla_ref[...] = acc[:, 0:4 * GLA_W]
    pnat_ref[...] = acc[:, 4 * GLA_W:4 * GLA_W + 3 * NAT_W]
    z = _dot(acc[:, GA_COL:GA_COL + 128], wa_ref[...]) + ba_ref[...]
    la_ref[...] = (jnp.minimum(z, 0.0) - jnp.log(1.0 + jnp.exp(-jnp.abs(z)))) * (1.0 / GLA_TAU)

    c0 = 4 * GLA_W + 3 * NAT_W
    cos = cos_ref[...]
    sin = sin_ref[...]

    def norm_rope(a, g):
        an = a * lax.rsqrt(_head_mean_sq(a) + EPS) * g
        return an * cos + _swap_half_heads(an) * sin

    qs = [norm_rope(acc[:, c0 + j * 128:c0 + (j + 1) * 128], qg_ref[...]) for j in range(GQA_QW // 128)]
    q_ref[...] = (jnp.concatenate(qs, axis=1) * (HEAD_DIM ** -0.5)).astype(BF16)
    k_ref[...] = norm_rope(acc[:, c0 + GQA_QW:c0 + GQA_QW + GQA_KW], kg_ref[...])
    v_ref[...] = acc[:, c0 + GQA_QW + GQA_KW:c0 + GQA_QW + 2 * GQA_KW]


def _in_proj(x, sh, sc, w, wa, ba, qg, kg, cos_t, sin_t):
    tm = TM_PROJ
    full = lambda a: pl.BlockSpec(a.shape, lambda i: (0,) * a.ndim)
    rows = lambda w_: pl.BlockSpec((tm, w_), lambda i: (i, 0))
    return pl.pallas_call(
        _inproj_kernel,
        grid=(N_TOK // tm,),
        in_specs=[rows(D_MODEL), full(sh), full(sc), full(w), full(wa), full(ba), full(qg), full(kg),
                  rows(128), rows(128)],
        out_specs=[rows(4 * GLA_W), rows(2 * GLA_W), rows(3 * NAT_W), rows(GQA_QW), rows(GQA_KW),
                   rows(GQA_KW)],
        out_shape=[jax.ShapeDtypeStruct((N_TOK, 4 * GLA_W), F32),
                   jax.ShapeDtypeStruct((N_TOK, 2 * GLA_W), F32),
                   jax.ShapeDtypeStruct((N_TOK, 3 * NAT_W), F32),
                   jax.ShapeDtypeStruct((N_TOK, GQA_QW), BF16),
                   jax.ShapeDtypeStruct((N_TOK, GQA_KW), F32),
                   jax.ShapeDtypeStruct((N_TOK, GQA_KW), F32)],
        compiler_params=pltpu.CompilerParams(dimension_semantics=("arbitrary",),
                                             vmem_limit_bytes=VMEM_LIMIT),
        name="in_proj",
    )(x, sh, sc, w, wa, ba, qg, kg, cos_t, sin_t)


def _gla_kernel(qf_ref, kf_ref, vf_ref, laf_ref, qb_ref, kb_ref, vb_ref, lab_ref, st0_ref,
                of_ref, ob_ref, stout_ref, st_ref):
    t = pl.program_id(1)
    last_t = pl.num_programs(1) - 1
    C = GLA_CHUNK
    nchunk = GLA_TB // C

    @pl.when(t == 0)
    def _():
        st_ref[...] = st0_ref[0]

    tb = GLA_TB
    shift = int(np.log2(C))
    r = lax.broadcasted_iota(jnp.int32, (tb, tb), 0)
    c = lax.broadcasted_iota(jnp.int32, (tb, tb), 1)
    same = lax.shift_right_logical(r, shift) == lax.shift_right_logical(c, shift)
    ra = lax.broadcasted_iota(jnp.int32, (tb, H_GLA * tb), 0)
    ca = lax.broadcasted_iota(jnp.int32, (tb, H_GLA * tb), 1) & (tb - 1)
    same_a = lax.shift_right_logical(ra, shift) == lax.shift_right_logical(ca, shift)
    blk = _lane_group((H_GLA * tb, GLA_W), 0, tb) == _lane_group((H_GLA * tb, GLA_W), 1, GLA_DK)
    diag = _lane_group((GLA_W, GLA_W), 0, GLA_DV) == _lane_group((GLA_W, GLA_W), 1, GLA_DK)
    one = lambda m: jnp.where(m, 1.0, 0.0).astype(BF16)
    chunk_ones = one(same)
    ref_pick = one(c == (r & ~(C - 1)) + C // 2)

    def direction(q_ref, k_ref, v_ref, la_ref, o_ref, d, causal):
        q = q_ref[...] * (GLA_DK ** -0.5)
        k = k_ref[...]
        v = v_ref[...]
        la = la_ref[...]
        la_hi = la.astype(BF16)
        la_lo = (la - la_hi.astype(F32)).astype(BF16)
        csum = lambda m: (jnp.dot(m, la_hi, preferred_element_type=F32)
                          + jnp.dot(m, la_lo, preferred_element_type=F32))
        b = csum(one(same & ((r >= c) if causal else (r <= c))))
        btot = csum(chunk_ones)
        bref = jnp.dot(ref_pick, b.astype(BF16), preferred_element_type=F32)
        q_att = q * jnp.exp(b - bref)
        k_att = k * jnp.exp(bref - b)
        kblk = jnp.where(blk, jnp.concatenate([k_att] * H_GLA, axis=0), 0.0)
        att = _dot_nt(q_att, kblk)
        att = jnp.where(same_a & ((ra >= ca) if causal else (ra <= ca)), att, 0.0)
        vblk = jnp.where(blk, jnp.concatenate([v] * H_GLA, axis=0), 0.0)
        o_intra = _dot(att, vblk)

        q_dec = (q * jnp.exp(b)).astype(BF16)
        k_dec = (k * jnp.exp(btot - b)).astype(BF16)
        g = jnp.exp(btot)
        vb = v.astype(BF16)
        st = st_ref[d]
        o_inter = [None] * nchunk
        for ci in (range(nchunk) if causal else reversed(range(nchunk))):
            rows = slice(ci * C, (ci + 1) * C)
            o_inter[ci] = _dot_nt(q_dec[rows], st)
            u = _dot_tn(vb[rows], k_dec[rows])
            st = st * g[ci * C:ci * C + 1, :] + jnp.where(diag, u, 0.0)
        st_ref[d] = st
        o_ref[...] = o_intra + jnp.concatenate(o_inter, axis=0)

    direction(qf_ref, kf_ref, vf_ref, laf_ref, of_ref, 0, True)
    direction(qb_ref, kb_ref, vb_ref, lab_ref, ob_ref, 1, False)

    @pl.when(t == last_t)
    def _():
        stout_ref[0] = st_ref[...]


def _gla(pgla, la, st0, row0, n_seq, seq_len):
    tb = GLA_TB
    nt = seq_len // tb
    b0 = row0 // tb
    fwd = lambda col: pl.BlockSpec((tb, GLA_W), lambda s, t: (b0 + s * nt + t, col))
    bwd = lambda col: pl.BlockSpec((tb, GLA_W), lambda s, t: (b0 + s * nt + nt - 1 - t, col))
    st_spec = pl.BlockSpec((1, 2, GLA_W, GLA_W), lambda s, t: (s, 0, 0, 0))
    n = n_seq * seq_len
    return pl.pallas_call(
        _gla_kernel,
        grid=(n_seq, nt),
        in_specs=[fwd(0), fwd(1), fwd(2), fwd(0), bwd(0), bwd(1), bwd(2), bwd(1), st_spec],
        out_specs=[pl.BlockSpec((tb, GLA_W), lambda s, t: (s * nt + t, 0)),
                   pl.BlockSpec((tb, GLA_W), lambda s, t: (s * nt + nt - 1 - t, 0)),
                   st_spec],
        out_shape=[jax.ShapeDtypeStruct((n, GLA_W), F32), jax.ShapeDtypeStruct((n, GLA_W), F32),
                   jax.ShapeDtypeStruct((n_seq, 2, GLA_W, GLA_W), F32)],
        scratch_shapes=[pltpu.VMEM((2, GLA_W, GLA_W), F32)],
        compiler_params=pltpu.CompilerParams(dimension_semantics=("arbitrary", "arbitrary"),
                                             vmem_limit_bytes=VMEM_LIMIT),
        name="gla_scan",
    )(pgla, pgla, pgla, la, pgla, pgla, pgla, la, st0)


def _softmax_pv(s_list, v_list):
    m = s_list[0].max(axis=-1, keepdims=True)
    for s in s_list[1:]:
        m = jnp.maximum(m, s.max(axis=-1, keepdims=True))
    acc = None
    l = None
    for s, v in zip(s_list, v_list):
        p = jnp.exp(s - m)
        pl_ = p.sum(axis=-1, keepdims=True)
        pv = _dot(p, v)
        acc = pv if acc is None else acc + pv
        l = pl_ if l is None else l + pl_
    return acc * (1.0 / l)


def _ctx_attn_kernel(pnat_ref, q_ref, k_ref, v_ref, onat_ref, ogqa_ref):
    nq = pnat_ref[:, 0:NAT_W] * (HEAD_DIM ** -0.5)
    nk = pnat_ref[:, NAT_W:2 * NAT_W].astype(BF16)
    nv = pnat_ref[:, 2 * NAT_W:3 * NAT_W].astype(BF16)
    o = _softmax_pv([_dot_nt(_stack_heads(nq, H_NAT, HEAD_DIM), nk)], [nv])
    onat_ref[...] = _unstack_heads(o, H_NAT, HEAD_DIM).astype(BF16)

    k = k_ref[...].astype(BF16)
    v = v_ref[...].astype(BF16)
    k2 = jnp.concatenate([k, k], axis=1)
    v2 = jnp.concatenate([v, v], axis=1)
    outs = []
    for half in range(2):
        q = q_ref[:, half * 256:(half + 1) * 256]
        o = _softmax_pv([_dot_nt(_stack_heads(q, 4, HEAD_DIM), k2)], [v2])
        outs.append(_unstack_heads(o, 4, HEAD_DIM))
    ogqa_ref[...] = jnp.concatenate(outs, axis=1).astype(BF16)


def _ctx_attention(pnat, q, k, v):
    rows = lambda w_: pl.BlockSpec((SEQ, w_), lambda i: (i, 0))
    return pl.pallas_call(
        _ctx_attn_kernel,
        grid=(BATCH,),
        in_specs=[rows(3 * NAT_W), rows(GQA_QW), rows(GQA_KW), rows(GQA_KW)],
        out_specs=[rows(NAT_W), rows(GQA_QW)],
        out_shape=[jax.ShapeDtypeStruct((N_CTX, NAT_W), BF16),
                   jax.ShapeDtypeStruct((N_CTX, GQA_QW), BF16)],
        compiler_params=pltpu.CompilerParams(dimension_semantics=("arbitrary",),
                                             vmem_limit_bytes=VMEM_LIMIT),
        name="ctx_attention",
    )(pnat, q, k, v)


def _gqa_lat_kernel(q_ref, k_ref, v_ref, o_ref):
    k = k_ref[0]
    v = v_ref[0]
    k2 = jnp.concatenate([k, k], axis=1)
    v2 = jnp.concatenate([v, v], axis=1)
    slot = _lane_group((TQ_GQA, 256), 1, HEAD_DIM)
    outs = []
    for half in range(2):
        q = q_ref[:, half * 256:(half + 1) * 256]
        out = jnp.zeros((TQ_GQA, 256), F32)
        for s in range(4):
            qs = jnp.where(slot == s, q, jnp.zeros_like(q))
            o = _softmax_pv([_dot_nt(qs, k2)], [v2])
            out = jnp.where(slot == s, o, out)
        outs.append(out)
    o_ref[...] = jnp.concatenate(outs, axis=1).astype(BF16)


def _gqa_latent(q, k_all, v_all):
    tq = TQ_GQA
    nq = DEC_SEQ // tq
    tk = k_all.shape[1]
    return pl.pallas_call(
        _gqa_lat_kernel,
        grid=(DEC_BATCH, nq),
        in_specs=[pl.BlockSpec((tq, GQA_QW), lambda b, i: (N_CTX // tq + b * nq + i, 0)),
                  pl.BlockSpec((1, tk, GQA_KW), lambda b, i: (b, 0, 0)),
                  pl.BlockSpec((1, tk, GQA_KW), lambda b, i: (b, 0, 0))],
        out_specs=pl.BlockSpec((tq, GQA_QW), lambda b, i: (b * nq + i, 0)),
        out_shape=jax.ShapeDtypeStruct((N_LAT, GQA_QW), BF16),
        compiler_params=pltpu.CompilerParams(dimension_semantics=("arbitrary", "arbitrary"),
                                             vmem_limit_bytes=VMEM_LIMIT),
        name="gqa_latent",
    )(q, k_all, v_all)


def _nat_lat_kernel(q_ref, k_ref, v_ref, kc_ref, vc_ref, bias_ref, o_ref):
    j = pl.program_id(1)
    kc = kc_ref[0].astype(BF16)
    vc = vc_ref[0].astype(BF16)
    for i in range(NAT_ROWS_PER_STEP):
        r = j * NAT_ROWS_PER_STEP + i
        r0 = jnp.clip(r - NAT_KH // 2, 0, ROWS - NAT_KH)
        win = pl.ds(pl.multiple_of(r0 * GRID_W, GRID_W), NAT_KH * GRID_W)
        q = q_ref[i * GRID_W:(i + 1) * GRID_W, :] * (HEAD_DIM ** -0.5)
        qs = _stack_heads(q, H_NAT, HEAD_DIM)
        s_loc = _dot_nt(qs, k_ref[win, :]) + bias_ref[r - r0]
        s_ctx = _dot_nt(qs, kc)
        o = _softmax_pv([s_loc, s_ctx], [v_ref[win, :], vc])
        o_ref[i * GRID_W:(i + 1) * GRID_W, :] = _unstack_heads(o, H_NAT, HEAD_DIM).astype(BF16)


def _nat_latent(pnat, kc, vc, bias):
    tq = NAT_ROWS_PER_STEP * GRID_W
    nq = DEC_SEQ // tq
    lat_blk = N_CTX // DEC_SEQ
    return pl.pallas_call(
        _nat_lat_kernel,
        grid=(DEC_BATCH, nq),
        in_specs=[pl.BlockSpec((tq, NAT_W), lambda b, j: (N_CTX // tq + b * nq + j, 0)),
                  pl.BlockSpec((DEC_SEQ, NAT_W), lambda b, j: (lat_blk + b, 1)),
                  pl.BlockSpec((DEC_SEQ, NAT_W), lambda b, j: (lat_blk + b, 2)),
                  pl.BlockSpec((1, PAST_LEN, NAT_W), lambda b, j: (b, 0, 0)),
                  pl.BlockSpec((1, PAST_LEN, NAT_W), lambda b, j: (b, 0, 0)),
                  pl.BlockSpec(bias.shape, lambda b, j: (0, 0, 0))],
        out_specs=pl.BlockSpec((tq, NAT_W), lambda b, j: (b * nq + j, 0)),
        out_shape=jax.ShapeDtypeStruct((N_LAT, NAT_W), BF16),
        compiler_params=pltpu.CompilerParams(dimension_semantics=("arbitrary", "arbitrary"),
                                             vmem_limit_bytes=VMEM_LIMIT),
        name="nat_latent",
    )(pnat, pnat, pnat, kc, vc, bias)


def _route_gates(logits_t, bias_ref):
    per = N_EXPERTS // N_GROUPS
    t = logits_t.shape[1]
    neg = -jnp.inf
    pos = lax.broadcasted_iota(jnp.int32, (per, t), 0)
    scores = [_sigmoid(logits_t[g * per:(g + 1) * per, :]) for g in range(N_GROUPS)]
    biased = [scores[g] + bias_ref[g * per:(g + 1) * per, :] for g in range(N_GROUPS)]

    grp = []
    for v in biased:
        m1 = jnp.max(v, axis=0, keepdims=True)
        i1 = jnp.min(jnp.where(v == m1, pos, per), axis=0, keepdims=True)
        m2 = jnp.max(jnp.where(pos == i1, neg, v), axis=0, keepdims=True)
        grp.append(m1 + m2)

    keep = [jnp.zeros((1, t), jnp.bool_) for _ in range(N_GROUPS)]
    for _ in range(TOPK_GROUPS):
        best = functools.reduce(jnp.maximum, grp)
        first = jnp.full((1, t), N_GROUPS, jnp.int32)
        for g in reversed(range(N_GROUPS)):
            first = jnp.where(grp[g] == best, g, first)
        for g in range(N_GROUPS):
            hit = first == g
            keep[g] = keep[g] | hit
            grp[g] = jnp.where(hit, neg, grp[g])

    cand = [jnp.where(keep[g], biased[g], neg) for g in range(N_GROUPS)]
    flat = [pos + g * per for g in range(N_GROUPS)]
    picks, weights = [], []
    for _ in range(TOP_K):
        best = functools.reduce(jnp.maximum, [jnp.max(v, axis=0, keepdims=True) for v in cand])
        first = functools.reduce(jnp.minimum, [
            jnp.min(jnp.where(cand[g] == best, flat[g], N_EXPERTS), axis=0, keepdims=True)
            for g in range(N_GROUPS)])
        score = jnp.zeros((1, t), F32)
        for g in range(N_GROUPS):
            hit = flat[g] == first
            score = score + jnp.sum(jnp.where(hit, scores[g], 0.0), axis=0, keepdims=True)
            cand[g] = jnp.where(hit, neg, cand[g])
        picks.append(first)
        weights.append(score)

    total = functools.reduce(lambda a, b: a + b, weights)
    gates = jnp.concatenate([v / total * ROUTE_SCALE for v in weights], axis=0)

    sel = [functools.reduce(lambda a, b: a | b, [flat[g] == k for k in picks]) for g in range(N_GROUPS)]
    sel = jnp.where(jnp.concatenate(sel, axis=0), 1.0, 0.0).astype(BF16)
    before = (lax.broadcasted_iota(jnp.int32, (t, t), 0) < lax.broadcasted_iota(jnp.int32, (t, t), 1))
    rank = jnp.dot(sel, jnp.where(before, 1.0, 0.0).astype(BF16), preferred_element_type=F32)
    count = jnp.sum(sel.astype(F32), axis=1, keepdims=True)
    pieces = jnp.floor((count + (PIECE - 1)) * (1.0 / PIECE))
    lower = (lax.broadcasted_iota(jnp.int32, (N_EXPERTS, N_EXPERTS), 1)
             < lax.broadcasted_iota(jnp.int32, (N_EXPERTS, N_EXPERTS), 0))
    start = PIECE * jnp.dot(jnp.where(lower, 1.0, 0.0).astype(BF16),
                            jnp.broadcast_to(pieces, (N_EXPERTS, 128)).astype(BF16),
                            preferred_element_type=F32)[:, 0:1]
    row = rank + start
    rows = []
    for k in picks:
        r = jnp.zeros((1, t), F32)
        for g in range(N_GROUPS):
            r = r + jnp.sum(jnp.where(flat[g] == k, row[g * per:(g + 1) * per, :], 0.0), axis=0, keepdims=True)
        rows.append(r)
    return (jnp.concatenate(rows, axis=0).astype(jnp.int32), gates,
            jnp.broadcast_to(pieces, (N_EXPERTS, 128)).astype(jnp.int32))


def _outproj_kernel(x_ref, ofc_ref, ofl_ref, obc_ref, obl_ref, gg_ref, onatc_ref, onatl_ref, ogqac_ref, ogqal_ref,
                    w_ref, ng_ref, g1_ref, lng_ref, lnb_ref, sh2_ref, sc2_ref, wr_ref, rb_ref, x1_ref, xm_ref,
                    row_ref, gate_ref, pieces_ref):
    ci = _cond_row(pl.program_id(0), TM_PROJ)
    is_ctx = pl.program_id(0) < N_CTX // TM_PROJ
    either = lambda c_ref, l_ref: jnp.where(is_ctx, c_ref[...], l_ref[...])
    og = either(ofc_ref, ofl_ref) + either(obc_ref, obl_ref)
    halves = [og[:, j * 128:(j + 1) * 128] for j in range(GLA_W // 128)]
    ms = jnp.concatenate([_head_mean_sq(h) for h in halves], axis=1)
    ogla = og * lax.rsqrt(ms + EPS) * ng_ref[...] * _silu(gg_ref[...])
    mix = (_dot(ogla, w_ref[0:GLA_W, :]) + _dot(either(onatc_ref, onatl_ref), w_ref[GLA_W:GLA_W + NAT_W, :])
           + _dot(either(ogqac_ref, ogqal_ref), w_ref[GLA_W + NAT_W:, :]))
    x1 = _ln(ALPHA * x_ref[...] + g1_ref[pl.ds(ci, 1), :] * mix) * lng_ref[...] + lnb_ref[...]
    x1_ref[...] = x1
    xm = (_ln(x1) * (1.0 + sc2_ref[pl.ds(ci, 1), :]) + sh2_ref[pl.ds(ci, 1), :]).astype(BF16)
    xm_ref[...] = xm
    row_ref[...], gate_ref[...], pieces_ref[...] = _route_gates(_dot_nt(wr_ref[...], xm), rb_ref)


def _out_proj(x, of, ob, pgla, onat, ogqa, w, ng, g1, lng, lnb, sh2, sc2, wr, rb):
    tm = TM_PROJ
    n_ctx = N_CTX // tm
    full = lambda a: pl.BlockSpec(a.shape, lambda i: (0,) * a.ndim)
    rows = lambda w_: pl.BlockSpec((tm, w_), lambda i: (i, 0))
    ctx = lambda w_: pl.BlockSpec((tm, w_), lambda i: (jnp.minimum(i, n_ctx - 1), 0))
    lat = lambda w_: pl.BlockSpec((tm, w_), lambda i: (jnp.maximum(i - n_ctx, 0), 0))
    return pl.pallas_call(
        _outproj_kernel,
        grid=(N_TOK // tm,),
        in_specs=[rows(D_MODEL), ctx(GLA_W), lat(GLA_W), ctx(GLA_W), lat(GLA_W),
                  pl.BlockSpec((tm, GLA_W), lambda i: (i, 3)),
                  ctx(NAT_W), lat(NAT_W), ctx(GQA_QW), lat(GQA_QW), full(w), full(ng), full(g1), full(lng),
                  full(lnb), full(sh2), full(sc2), full(wr), full(rb)],
        out_specs=[rows(D_MODEL), rows(D_MODEL), pl.BlockSpec((TOP_K, tm), lambda i: (0, i)),
                   pl.BlockSpec((TOP_K, tm), lambda i: (0, i)),
                   pl.BlockSpec((N_EXPERTS, 128), lambda i: (0, i))],
        out_shape=[jax.ShapeDtypeStruct((N_TOK, D_MODEL), F32),
                   jax.ShapeDtypeStruct((N_TOK, D_MODEL), BF16),
                   jax.ShapeDtypeStruct((TOP_K, N_TOK), jnp.int32),
                   jax.ShapeDtypeStruct((TOP_K, N_TOK), F32),
                   jax.ShapeDtypeStruct((N_EXPERTS, MOE_NT * 128), jnp.int32)],
        compiler_params=pltpu.CompilerParams(dimension_semantics=("arbitrary",),
                                             vmem_limit_bytes=VMEM_LIMIT),
        name="out_proj",
    )(x, *of, *ob, pgla, *onat, *ogqa, w, ng, g1, lng, lnb, sh2, sc2, wr, rb)


def _plan_specs(n_prefetch, **kw):
    return pltpu.PrefetchScalarGridSpec(num_scalar_prefetch=n_prefetch, **kw)


def _for_each_copy(t, copies, fn):
    bsrc_ref, bdst_ref, ssrc_ref, sdst_ref, nbig_ref, nsmall_ref = copies

    def big(i, c):
        j = t * MAX_BIG + i
        fn(pl.multiple_of(bsrc_ref[j], PIECE), pl.multiple_of(bdst_ref[j], PIECE), COPY_ROWS)
        return c

    def small(i, c):
        j = t * N_EXPERTS + i
        fn(pl.multiple_of(ssrc_ref[j], PIECE), pl.multiple_of(sdst_ref[j], PIECE), PIECE)
        return c

    lax.fori_loop(0, nbig_ref[t], big, 0)
    lax.fori_loop(0, nsmall_ref[t], small, 0)


def _wait_copies(n_big, n_small, make):
    for n, rows in ((n_big, COPY_ROWS), (n_small, PIECE)):
        def one(i, c, rows=rows):
            make(rows).wait()
            return c
        lax.fori_loop(0, n, one, 0)


def _dispatch_kernel(bsrc_ref, bdst_ref, ssrc_ref, sdst_ref, nbig_ref, nsmall_ref, tot_ref, xm_ref, ldt_ref, xs_ref,
                     stage_ref, zero_ref, sem_big, sem_small, tail_sem):
    copies = (bsrc_ref, bdst_ref, ssrc_ref, sdst_ref, nbig_ref, nsmall_ref)
    t = pl.program_id(0)
    nt = pl.num_programs(0)
    slot = lax.rem(t, 2)

    def run_copy(local_row, xs_row, rows, s):
        sem = sem_big if rows == COPY_ROWS else sem_small
        return pltpu.make_async_copy(stage_ref.at[s, pl.ds(local_row, rows)], xs_ref.at[pl.ds(xs_row, rows)],
                                     sem.at[s])

    def wait_tile(tt, s):
        _wait_copies(nbig_ref[tt], nsmall_ref[tt], lambda rows: run_copy(0, 0, rows, s))

    @pl.when(t >= 2)
    def _():
        wait_tile(t - 2, slot)

    xm = xm_ref[...]
    for c in range(MOE_CAP // MOE_CHUNK):
        p = lax.broadcasted_iota(jnp.int32, (MOE_CHUNK, MOE_TILE), 0) + c * MOE_CHUNK
        hit = p == ldt_ref[0:1, :]
        for k in range(1, TOP_K):
            hit = hit | (p == ldt_ref[k:k + 1, :])
        stage_ref[slot, c * MOE_CHUNK:(c + 1) * MOE_CHUNK, :] = jnp.dot(
            jnp.where(hit, 1.0, 0.0).astype(BF16), xm, preferred_element_type=F32).astype(BF16)
    _for_each_copy(t, copies, lambda a, b, rows: run_copy(a, b, rows, slot).start())

    @pl.when(t == nt - 1)
    def _():
        zero_ref[...] = jnp.zeros_like(zero_ref)
        total = tot_ref[0]
        first_free_tile = (total + ROW_TILE - 1) // ROW_TILE
        n_piece = (first_free_tile * ROW_TILE - total) // PIECE
        n_tile = XS_ROWS // ROW_TILE - first_free_tile

        def zero_piece(i):
            return pltpu.make_async_copy(zero_ref.at[pl.ds(0, PIECE)],
                                         xs_ref.at[pl.ds(pl.multiple_of(total + i * PIECE, PIECE), PIECE)],
                                         tail_sem.at[0])

        def zero_tile(i):
            row = pl.multiple_of((first_free_tile + i) * ROW_TILE, ROW_TILE)
            return pltpu.make_async_copy(zero_ref, xs_ref.at[pl.ds(row, ROW_TILE)], tail_sem.at[1])

        def run(n, make, op):
            def body(i, c):
                op(make(i))
                return c
            lax.fori_loop(0, n, body, 0)

        run(n_piece, zero_piece, lambda cp: cp.start())
        run(n_tile, zero_tile, lambda cp: cp.start())
        run(n_piece, zero_piece, lambda cp: cp.wait())
        run(n_tile, zero_tile, lambda cp: cp.wait())
        wait_tile(t - 1, 1 - slot)
        wait_tile(t, slot)


def _moe_dispatch(plan, xm, row_t):
    return pl.pallas_call(
        _dispatch_kernel,
        grid_spec=_plan_specs(
            7, grid=(MOE_NT,),
            in_specs=[pl.BlockSpec((MOE_TILE, D_MODEL), lambda t, *_: (t, 0)),
                      pl.BlockSpec((TOP_K, MOE_TILE), lambda t, *_: (0, t))],
            out_specs=pl.BlockSpec(memory_space=pl.ANY),
            scratch_shapes=[pltpu.VMEM((2, MOE_CAP, D_MODEL), BF16), pltpu.VMEM((ROW_TILE, D_MODEL), BF16),
                            pltpu.SemaphoreType.DMA((2,)), pltpu.SemaphoreType.DMA((2,)),
                            pltpu.SemaphoreType.DMA((2,))]),
        out_shape=jax.ShapeDtypeStruct((XS_ROWS, D_MODEL), BF16),
        compiler_params=pltpu.CompilerParams(dimension_semantics=("arbitrary",),
                                             vmem_limit_bytes=VMEM_LIMIT),
        name="moe_dispatch",
    )(*plan["copies"], plan["total"], xm, row_t)


def _experts_kernel(tile_ref, exp_ref, lo_ref, hi_ref, first_ref, fresh_ref, xs_ref, wgu_ref, wd_ref, y_ref,
                    wgu_bf_ref, wd_bf_ref):
    w = pl.program_id(0)
    lo, hi, first = lo_ref[w], hi_ref[w], first_ref[w]

    @pl.when(fresh_ref[w] == 1)
    def _():
        wgu_bf_ref[...] = wgu_ref[0].astype(BF16)
        wd_bf_ref[...] = wd_ref[0].astype(BF16)

    def rows_of_this_expert():
        ab = jnp.dot(xs_ref[...], wgu_bf_ref[...], preferred_element_type=F32)
        h = (_silu(ab[:, :D_EXPERT]) * ab[:, D_EXPERT:]).astype(BF16)
        out = jnp.dot(h, wd_bf_ref[...], preferred_element_type=F32).astype(BF16)
        row = lax.broadcasted_iota(jnp.int32, (ROW_TILE, 1), 0)
        return (row >= lo) & (row < hi), out

    @pl.when((hi > lo) & (first == 1))
    def _():
        mine, out = rows_of_this_expert()
        y_ref[...] = jnp.where(mine, out, jnp.zeros_like(out))

    @pl.when((hi > lo) & (first == 0))
    def _():
        mine, out = rows_of_this_expert()
        y_ref[...] = jnp.where(mine, out, y_ref[...])

    @pl.when((hi <= lo) & (first == 1))
    def _():
        y_ref[...] = jnp.zeros_like(y_ref)


def _moe_experts(layer, plan, xs, wgu, wd):
    return pl.pallas_call(
        _experts_kernel,
        grid_spec=_plan_specs(
            6, grid=(N_ITEMS,),
            in_specs=[pl.BlockSpec((ROW_TILE, D_MODEL), lambda w, tile, *_: (tile[w], 0)),
                      pl.BlockSpec((None, 1, D_MODEL, 2 * D_EXPERT), lambda w, tile, ex, *_: (layer, ex[w], 0, 0)),
                      pl.BlockSpec((None, 1, D_EXPERT, D_MODEL), lambda w, tile, ex, *_: (layer, ex[w], 0, 0))],
            out_specs=pl.BlockSpec((ROW_TILE, D_MODEL), lambda w, tile, *_: (tile[w], 0)),
            scratch_shapes=[pltpu.VMEM((D_MODEL, 2 * D_EXPERT), BF16), pltpu.VMEM((D_EXPERT, D_MODEL), BF16)]),
        out_shape=jax.ShapeDtypeStruct((XS_ROWS, D_MODEL), BF16),
        compiler_params=pltpu.CompilerParams(dimension_semantics=("arbitrary",),
                                             vmem_limit_bytes=VMEM_LIMIT),
        name="moe_experts",
    )(plan["item_tile"], plan["item_expert"], plan["item_lo"], plan["item_hi"], plan["item_first"],
      plan["item_fresh"], xs, wgu, wd)


def _combine_kernel(bsrc_ref, bdst_ref, ssrc_ref, sdst_ref, nbig_ref, nsmall_ref, y_ref, ldt_ref, gate_ref, xm_ref,
                    x1_ref, g2_ref, wsgu_ref, wsd_ref, lng_ref, lnb_ref, out_ref, ybuf_ref, sem_big, sem_small):
    copies = (bsrc_ref, bdst_ref, ssrc_ref, sdst_ref, nbig_ref, nsmall_ref)
    t = pl.program_id(0)
    nt = pl.num_programs(0)
    slot = lax.rem(t, 2)
    ci = _cond_row(t, MOE_TILE)

    def run_copy(local_row, y_row, rows, s):
        sem = sem_big if rows == COPY_ROWS else sem_small
        return pltpu.make_async_copy(y_ref.at[pl.ds(y_row, rows)], ybuf_ref.at[s, pl.ds(local_row, rows)], sem.at[s])

    def fetch(tt, s):
        _for_each_copy(tt, copies, lambda a, b, rows: run_copy(a, b, rows, s).start())

    @pl.when(t == 0)
    def _():
        ybuf_ref[...] = jnp.zeros_like(ybuf_ref)
        fetch(0, 0)

    @pl.when(t + 1 < nt)
    def _():
        fetch(t + 1, 1 - slot)

    _wait_copies(nbig_ref[t], nsmall_ref[t], lambda rows: run_copy(0, 0, rows, slot))

    routed = jnp.zeros((MOE_TILE, D_MODEL), F32)
    for c in range(MOE_CAP // MOE_CHUNK):
        p = lax.broadcasted_iota(jnp.int32, (MOE_CHUNK, MOE_TILE), 0) + c * MOE_CHUNK
        wmat = jnp.zeros((MOE_CHUNK, MOE_TILE), F32)
        for k in range(TOP_K):
            wmat = jnp.where(p == ldt_ref[k:k + 1, :], gate_ref[k:k + 1, :], wmat)
        w_hi = wmat.astype(BF16)
        w_lo = (wmat - w_hi.astype(F32)).astype(BF16)
        y = ybuf_ref[slot, c * MOE_CHUNK:(c + 1) * MOE_CHUNK, :]
        routed = routed + _dot_tn(w_hi, y) + _dot_tn(w_lo, y)

    sab = jnp.dot(xm_ref[...], wsgu_ref[...], preferred_element_type=F32)
    shared = _dot(_silu(sab[:, :D_SHARED]) * sab[:, D_SHARED:], wsd_ref[...])
    z = ALPHA * x1_ref[...] + g2_ref[pl.ds(ci, 1), :] * (routed + shared)
    out_ref[...] = _ln(z) * lng_ref[...] + lnb_ref[...]


def _moe_combine(plan, y, row_t, gate_t, xm, x1, g2, wsgu, wsd, lng, lnb):
    full = lambda a: pl.BlockSpec(a.shape, lambda t, *_: (0,) * a.ndim)
    rows = lambda w_: pl.BlockSpec((MOE_TILE, w_), lambda t, *_: (t, 0))
    picks = pl.BlockSpec((TOP_K, MOE_TILE), lambda t, *_: (0, t))
    return pl.pallas_call(
        _combine_kernel,
        grid_spec=_plan_specs(
            6, grid=(MOE_NT,),
            in_specs=[pl.BlockSpec(memory_space=pl.ANY), picks, picks, rows(D_MODEL), rows(D_MODEL),
                      full(g2), full(wsgu), full(wsd), full(lng), full(lnb)],
            out_specs=rows(D_MODEL),
            scratch_shapes=[pltpu.VMEM((2, MOE_CAP, D_MODEL), BF16), pltpu.SemaphoreType.DMA((2,)),
                            pltpu.SemaphoreType.DMA((2,))]),
        out_shape=jax.ShapeDtypeStruct((N_TOK, D_MODEL), F32),
        compiler_params=pltpu.CompilerParams(dimension_semantics=("arbitrary",),
                                             vmem_limit_bytes=VMEM_LIMIT),
        name="moe_combine",
    )(*plan["copies"], y, row_t, gate_t, xm, x1, g2, wsgu, wsd, lng, lnb)


def _moe_plan(npc):
    i32 = jnp.int32
    run = npc * PIECE
    lo = jnp.cumsum(run, axis=1) - run
    tot_e = jnp.sum(run, axis=0)
    ends = jnp.cumsum(tot_e)
    off = ends - tot_e
    hs = off[None, :] + jnp.cumsum(run, axis=0) - run

    per_copy = COPY_ROWS // PIECE
    n_big = npc // per_copy
    odd = npc - n_big * per_copy

    def expand(count, width):
        end = jnp.cumsum(count, axis=1)
        i = jnp.arange(width, dtype=i32)[None, :, None]
        mine = (i >= (end - count)[:, None, :]) & (i < end[:, None, :])
        pick = lambda a: jnp.sum(jnp.where(mine, a[:, None, :], 0), axis=2)
        return pick, i[:, :, 0] - pick(end - count)

    pick_b, k_b = expand(n_big, MAX_BIG)
    pick_s, _ = expand(odd, N_EXPERTS)
    copies = [pick_b(lo) + k_b * COPY_ROWS, pick_b(hs) + k_b * COPY_ROWS,
              pick_s(lo + n_big * COPY_ROWS), pick_s(hs + n_big * COPY_ROWS)]
    copies = [c.reshape(-1).astype(i32) for c in copies] + [jnp.sum(n_big, axis=1).astype(i32),
                                                            jnp.sum(odd, axis=1).astype(i32)]

    first_tile = off // ROW_TILE
    n_tile = jnp.where(tot_e > 0, (ends - 1) // ROW_TILE - first_tile + 1, 0)
    item_end = jnp.cumsum(n_tile)
    item_start = item_end - n_tile
    w = jnp.arange(N_ITEMS, dtype=i32)
    valid = w < item_end[-1]
    wc = jnp.minimum(w, item_end[-1] - 1)
    mine = (wc[:, None] >= item_start[None, :]) & (wc[:, None] < item_end[None, :])
    of_item = lambda a: jnp.sum(jnp.where(mine, a[None, :], 0), axis=1)
    ex = of_item(jnp.arange(N_EXPERTS, dtype=i32))
    tile = of_item(first_tile) + wc - of_item(item_start)
    r_lo = jnp.where(valid, jnp.clip(of_item(off) - tile * ROW_TILE, 0, ROW_TILE), 0)
    r_hi = jnp.where(valid, jnp.clip(of_item(ends) - tile * ROW_TILE, 0, ROW_TILE), 0)
    n_tiles = XS_ROWS // ROW_TILE
    spare = (ends[-1] + ROW_TILE - 1) // ROW_TILE + w - item_end[-1]
    tile = jnp.where(valid, tile, jnp.minimum(spare, n_tiles - 1))
    valid = valid | (spare < n_tiles)
    prev = jnp.concatenate([jnp.full((1,), -1, i32), tile[:-1]])
    prev_ex = jnp.concatenate([jnp.full((1,), -1, i32), ex[:-1]])
    return dict(copies=copies, total=ends[-1:].astype(i32),
                item_tile=tile.astype(i32), item_expert=ex.astype(i32), item_lo=r_lo.astype(i32),
                item_hi=r_hi.astype(i32), item_first=(valid & (tile != prev)).astype(i32),
                item_fresh=(ex != prev_ex).astype(i32))


def _rope_tables():
    t = jnp.arange(DEC_SEQ)
    n_freq = HEAD_DIM // 4
    freqs = ROPE_THETA ** (-jnp.arange(n_freq, dtype=F32) / n_freq)
    ang = jnp.concatenate([(t // GRID_W).astype(F32)[:, None] * freqs,
                           (t % GRID_W).astype(F32)[:, None] * freqs], -1)
    cos, sin = jnp.cos(ang), jnp.sin(ang)
    cos_h = jnp.concatenate([cos, cos], -1)
    sin_h = jnp.concatenate([-sin, sin], -1)
    lat = lambda a: jnp.tile(a, (DEC_BATCH, 128 // HEAD_DIM))
    cos_t = jnp.concatenate([jnp.ones((N_CTX, 128), F32), lat(cos_h)], 0)
    sin_t = jnp.concatenate([jnp.zeros((N_CTX, 128), F32), lat(sin_h)], 0)
    return cos_t, sin_t


def _nat_bias_table(rpb):
    n_dr, n_dc = 2 * NAT_KH - 1, 2 * NAT_KW - 1
    cidx = np.arange(GRID_W)
    dc_idx = np.clip(cidx[None, :] - cidx[:, None] + NAT_KW - 1, 0, n_dc - 1)
    col_start = np.clip(cidx - NAT_KW // 2, 0, GRID_W - NAT_KW)
    col_in = (cidx[None, :] >= col_start[:, None]) & (cidx[None, :] < col_start[:, None] + NAT_KW)
    onehot = (dc_idx.reshape(1, -1) == np.arange(n_dc)[:, None]).astype(np.float32)
    t = jnp.dot(rpb.reshape(H_NAT * n_dr, n_dc), onehot, precision=lax.Precision.HIGHEST)
    t = jnp.where(col_in[None, None], t.reshape(H_NAT, n_dr, GRID_W, GRID_W), -jnp.inf)
    bias = jnp.stack([t[:, NAT_KH - 1 - p:2 * NAT_KH - 1 - p] for p in range(NAT_KH)], axis=0)
    return bias.transpose(0, 1, 3, 2, 4).reshape(NAT_KH, H_NAT * GRID_W, NAT_KH * GRID_W)


def _prep_w_in(w_in):
    cuts = np.cumsum((0,) + IN_SPLITS)
    seg = [w_in[:, cuts[i]:cuts[i + 1]] for i in range(len(IN_SPLITS))]
    gq, gk, gv, gg, ga, nq, nk, nv, aq, ak, av = seg
    aq = aq.reshape(D_MODEL, H_GQA, HEAD_DIM)[:, np.array(GQA_SLOT_HEADS), :].reshape(D_MODEL, GQA_QW)
    ga = jnp.pad(ga, ((0, 0), (0, 128 - 2 * GLA_LOWRANK)))
    return jnp.concatenate([gq, gk, gv, gg, nq, nk, nv, aq, ak, av, ga], axis=1).astype(BF16)


def _prep_w_a(w_a2, b_a):
    wa = jnp.zeros((128, 2 * GLA_W), F32)
    wa = wa.at[0:GLA_LOWRANK, 0:GLA_W].set(w_a2[0])
    wa = wa.at[GLA_LOWRANK:2 * GLA_LOWRANK, GLA_W:].set(w_a2[1])
    return wa.astype(BF16), b_a.reshape(1, 2 * GLA_W)


def _prep_w_out(w_out):
    gqa = w_out[GLA_W + NAT_W:].reshape(H_GQA, HEAD_DIM, D_MODEL)[np.array(GQA_SLOT_HEADS)]
    return jnp.concatenate([w_out[:GLA_W + NAT_W], gqa.reshape(GQA_QW, D_MODEL)], 0).astype(BF16)


def _state_to_blockdiag(s):
    out = jnp.zeros(s.shape[:-3] + (GLA_W, GLA_W), F32)
    for h in range(H_GLA):
        out = out.at[..., h * GLA_DV:(h + 1) * GLA_DV, h * GLA_DK:(h + 1) * GLA_DK].set(
            jnp.swapaxes(s[..., h, :, :], -1, -2))
    return out


def _blockdiag_to_state(st):
    blocks = [st[..., h * GLA_DV:(h + 1) * GLA_DV, h * GLA_DK:(h + 1) * GLA_DK] for h in range(H_GLA)]
    return jnp.swapaxes(jnp.stack(blocks, axis=-3), -1, -2)


def kernel(x_prompt, x_sample, state_gla, cache_nat_k, cache_nat_v, cache_gqa_k, cache_gqa_v, c, c_ctx, w_mod, b_mod, w_in, gla_w_a2, gla_b_a, gla_norm_g, nat_rpb, gqa_q_norm_g, gqa_k_norm_g, w_out, ln1_g, ln1_b, w_router, router_bias, w_expert_gu, w_expert_down, w_shared_gu, w_shared_down, ln2_g, ln2_b):
    x = jnp.concatenate([x_prompt.reshape(N_CTX, D_MODEL), x_sample.reshape(N_LAT, D_MODEL)], axis=0)
    cond = jnp.concatenate([c_ctx[None, :], c, jnp.zeros((N_COND - 1 - DEC_BATCH, D_MODEL), F32)], axis=0)
    mods = _modulation(cond, w_mod, b_mod)
    cos_t, sin_t = _rope_tables()
    row = lambda a: a.reshape(1, -1)

    st_gla, st_nk, st_nv, st_gk, st_gv = [], [], [], [], []
    for l in range(DEPTH):
        sh1, sc1, g1, sh2, sc2, g2 = [mods[l, :, j * D_MODEL:(j + 1) * D_MODEL] for j in range(6)]
        wa, ba = _prep_w_a(gla_w_a2[l], gla_b_a[l])
        qg = row(jnp.tile(gqa_q_norm_g[l], 128 // HEAD_DIM))
        kg = row(jnp.tile(gqa_k_norm_g[l], 128 // HEAD_DIM))
        pgla, la, pnat, q, k, v = _in_proj(x, sh1, sc1, _prep_w_in(w_in[l]), wa, ba, qg, kg, cos_t, sin_t)

        zero_st = jnp.zeros((BATCH, 2, GLA_W, GLA_W), F32)
        of_c, ob_c, st_c = _gla(pgla, la, zero_st, 0, BATCH, SEQ)
        of_l, ob_l, _ = _gla(pgla, la, _state_to_blockdiag(state_gla[:, l]), N_CTX, DEC_BATCH, DEC_SEQ)

        onat_c, ogqa_c = _ctx_attention(pnat, q, k, v)
        k_all = jnp.concatenate([k[N_CTX:].reshape(DEC_BATCH, DEC_SEQ, GQA_KW),
                                 cache_gqa_k[:, l].reshape(DEC_BATCH, PAST_LEN, GQA_KW)], axis=1).astype(BF16)
        v_all = jnp.concatenate([v[N_CTX:].reshape(DEC_BATCH, DEC_SEQ, GQA_KW),
                                 cache_gqa_v[:, l].reshape(DEC_BATCH, PAST_LEN, GQA_KW)], axis=1).astype(BF16)
        ogqa_l = _gqa_latent(q, k_all, v_all)
        onat_l = _nat_latent(pnat, cache_nat_k[:, l].reshape(DEC_BATCH, PAST_LEN, NAT_W),
                             cache_nat_v[:, l].reshape(DEC_BATCH, PAST_LEN, NAT_W),
                             _nat_bias_table(nat_rpb[l]))

        x1, xm, row_t, gate_t, pieces = _out_proj(
            x, (of_c, of_l), (ob_c, ob_l), pgla, (onat_c, onat_l), (ogqa_c, ogqa_l),
            _prep_w_out(w_out[l]), row(jnp.tile(gla_norm_g[l], H_GLA)), g1, row(ln1_g[l]), row(ln1_b[l]),
            sh2, sc2, w_router[l].T.astype(BF16), router_bias[l].reshape(N_EXPERTS, 1))
        plan = _moe_plan(pieces[:, ::128].T)
        y = _moe_experts(l, plan, _moe_dispatch(plan, xm, row_t), w_expert_gu, w_expert_down)
        x = _moe_combine(plan, y, row_t, gate_t, xm, x1, g2, w_shared_gu[l].astype(BF16),
                         w_shared_down[l].astype(BF16), row(ln2_g[l]), row(ln2_b[l]))

        st_gla.append(_blockdiag_to_state(st_c))
        st_nk.append(pnat[:N_CTX, NAT_W:2 * NAT_W].reshape(BATCH, SEQ, H_NAT, HEAD_DIM))
        st_nv.append(pnat[:N_CTX, 2 * NAT_W:].reshape(BATCH, SEQ, H_NAT, HEAD_DIM))
        st_gk.append(k[:N_CTX].reshape(BATCH, SEQ, KV_GQA, HEAD_DIM))
        st_gv.append(v[:N_CTX].reshape(BATCH, SEQ, KV_GQA, HEAD_DIM))

    y_prompt = x[:N_CTX].reshape(BATCH, SEQ, D_MODEL)
    y_sample = x[N_CTX:].reshape(DEC_BATCH, DEC_SEQ, D_MODEL)
    return (y_prompt, y_sample, jnp.stack(st_gla, axis=1), jnp.stack(st_nk, axis=1), jnp.stack(st_nv, axis=1),
            jnp.stack(st_gk, axis=1), jnp.stack(st_gv, axis=1))
```

```python
import functools

import numpy as np
import jax
import jax.numpy as jnp
from jax import lax
from jax.experimental import pallas as pl
from jax.experimental.pallas import tpu as pltpu

D_MODEL = 1024
BATCH = 16
SEQ = 256
DEPTH = 2
DEC_BATCH = 2
DEC_SEQ = 4096
PAST_LEN = 512
GRID_W = 64
HEAD_DIM = 64
H_GLA = 4
GLA_DK = 64
GLA_DV = 64
GLA_LOWRANK = 16
GLA_TAU = 16.0
H_NAT = 4
NAT_KH = 8
NAT_KW = 16
H_GQA = 8
KV_GQA = 2
ROPE_THETA = 10000.0
N_EXPERTS = 64
TOP_K = 8
N_GROUPS = 8
TOPK_GROUPS = 4
D_EXPERT = 256
D_SHARED = 256
ROUTE_SCALE = 2.5
EPS = 1e-6
ALPHA = (2 * DEPTH) ** 0.25
IN_SPLITS = (H_GLA * GLA_DK, H_GLA * GLA_DK, H_GLA * GLA_DV, H_GLA * GLA_DV, 2 * GLA_LOWRANK,
             H_NAT * HEAD_DIM, H_NAT * HEAD_DIM, H_NAT * HEAD_DIM,
             H_GQA * HEAD_DIM, KV_GQA * HEAD_DIM, KV_GQA * HEAD_DIM)

F32 = jnp.float32
BF16 = jnp.bfloat16

N_CTX = BATCH * SEQ
N_LAT = DEC_BATCH * DEC_SEQ
N_TOK = N_CTX + N_LAT
N_COND = 8
GLA_W = H_GLA * GLA_DK
NAT_W = H_NAT * HEAD_DIM
GQA_QW = H_GQA * HEAD_DIM
GQA_KW = KV_GQA * HEAD_DIM
W_IN_COLS = 4 * GLA_W + 3 * NAT_W + GQA_QW + 2 * GQA_KW + 128
GA_COL = W_IN_COLS - 128
GQA_SLOT_HEADS = (0, 4, 1, 5, 2, 6, 3, 7)
GLA_CHUNK = 32
GLA_TB = 256
ROWS = DEC_SEQ // GRID_W
NAT_ROWS_PER_STEP = 8
VMEM_LIMIT = 56 * 1024 * 1024

TM_PROJ = 256
MOE_TILE = 256
MOE_NT = N_TOK // MOE_TILE
PIECE = 16
COPY_ROWS = 2 * PIECE
MOE_CAP = MOE_TILE * TOP_K + N_EXPERTS * PIECE
MAX_BIG = MOE_CAP // COPY_ROWS
MOE_CHUNK = 512
ROW_TILE = 512
XS_ROWS = N_TOK * TOP_K + MOE_NT * N_EXPERTS * (PIECE - 1) + N_EXPERTS * (ROW_TILE - PIECE)
assert XS_ROWS % ROW_TILE == 0
MAX_ZERO = N_EXPERTS * (ROW_TILE // PIECE - 1)
TQ_GQA = 256


def _dot(a, b):
    return jnp.dot(a.astype(BF16), b.astype(BF16), preferred_element_type=F32)


def _dot_nt(a, b):
    return lax.dot_general(a.astype(BF16), b.astype(BF16), (((1,), (1,)), ((), ())),
                           preferred_element_type=F32)


def _dot_tn(a, b):
    return lax.dot_general(a.astype(BF16), b.astype(BF16), (((0,), (0,)), ((), ())),
                           preferred_element_type=F32)


def _dot_split(a, b_bf16):
    hi = a.astype(BF16)
    lo = (a - hi.astype(F32)).astype(BF16)
    return (jnp.dot(hi, b_bf16, preferred_element_type=F32)
            + jnp.dot(lo, b_bf16, preferred_element_type=F32))


def _sigmoid(x):
    return 1.0 / (1.0 + jnp.exp(-x))


def _silu(x):
    return x * _sigmoid(x)


def _ln(x):
    xc = x - jnp.mean(x, axis=-1, keepdims=True)
    return xc * lax.rsqrt(jnp.mean(xc * xc, axis=-1, keepdims=True) + EPS)


def _lane_group(shape, axis, width):
    return lax.shift_right_logical(lax.broadcasted_iota(jnp.int32, shape, axis), int(np.log2(width)))


def _head_mean_sq(x):
    w = x.shape[-1]
    bmat = jnp.where(_lane_group((w, w), 0, HEAD_DIM) == _lane_group((w, w), 1, HEAD_DIM),
                     1.0 / HEAD_DIM, 0.0).astype(BF16)
    return _dot_split(x * x, bmat)


def _swap_half_heads(x):
    w = x.shape[-1]
    lane = lax.broadcasted_iota(jnp.int32, x.shape, x.ndim - 1)
    first = (lane & (HEAD_DIM - 1)) < HEAD_DIM // 2
    return jnp.where(first, pltpu.roll(x, w - HEAD_DIM // 2, x.ndim - 1),
                     pltpu.roll(x, HEAD_DIM // 2, x.ndim - 1))


def _cond_row(tile, tm):
    r = tile * tm
    return jnp.where(r < N_CTX, 0, 1 + (r - N_CTX) // DEC_SEQ)


def _stack_heads(x, n, width):
    head = _lane_group(x.shape, 1, width)
    return jnp.concatenate([jnp.where(head == h, x, jnp.zeros_like(x)) for h in range(n)], axis=0)


def _unstack_heads(o, n, width):
    m = o.shape[0] // n
    head = _lane_group((m, o.shape[1]), 1, width)
    out = jnp.zeros((m, o.shape[1]), o.dtype)
    for h in range(n):
        out = jnp.where(head == h, o[h * m:(h + 1) * m], out)
    return out


def _mod_kernel(c_ref, w_ref, b_ref, o_ref):
    o_ref[0] = _dot(_silu(c_ref[...]), w_ref[0]) + b_ref[0]


def _modulation(cond, w_mod, b_mod):
    tn = 1536
    return pl.pallas_call(
        _mod_kernel,
        grid=(DEPTH, 6 * D_MODEL // tn),
        in_specs=[pl.BlockSpec((N_COND, D_MODEL), lambda l, j: (0, 0)),
                  pl.BlockSpec((1, D_MODEL, tn), lambda l, j: (l, 0, j)),
                  pl.BlockSpec((1, 1, tn), lambda l, j: (l, 0, j))],
        out_specs=pl.BlockSpec((1, N_COND, tn), lambda l, j: (l, 0, j)),
        out_shape=jax.ShapeDtypeStruct((DEPTH, N_COND, 6 * D_MODEL), F32),
        compiler_params=pltpu.CompilerParams(dimension_semantics=("arbitrary", "arbitrary"),
                                             vmem_limit_bytes=VMEM_LIMIT),
        name="modulation",
    )(cond, w_mod, b_mod.reshape(DEPTH, 1, 6 * D_MODEL))


def _inproj_kernel(x_ref, sh_ref, sc_ref, w_ref, wa_ref, ba_ref, qg_ref, kg_ref, cos_ref, sin_ref,
                   pgla_ref, la_ref, pnat_ref, q_ref, k_ref, v_ref):
    ci = _cond_row(pl.program_id(0), TM_PROJ)
    xn = _ln(x_ref[...])
    xm = xn * (1.0 + sc_ref[pl.ds(ci, 1), :]) + sh_ref[pl.ds(ci, 1), :]
    acc = _dot(xm, w_ref[...])
    pgla_ref[...] = acc[:, 0:4 * GLA_W]
    pnat_ref[...] = acc[:, 4 * GLA_W:4 * GLA_W + 3 * NAT_W]
    z = _dot(acc[:, GA_COL:GA_COL + 128], wa_ref[...]) + ba_ref[...]
    la_ref[...] = (jnp.minimum(z, 0.0) - jnp.log(1.0 + jnp.exp(-jnp.abs(z)))) * (1.0 / GLA_TAU)

    c0 = 4 * GLA_W + 3 * NAT_W
    cos = cos_ref[...]
    sin = sin_ref[...]

    def norm_rope(a, g):
        an = a * lax.rsqrt(_head_mean_sq(a) + EPS) * g
        return an * cos + _swap_half_heads(an) * sin

    qs = [norm_rope(acc[:, c0 + j * 128:c0 + (j + 1) * 128], qg_ref[...]) for j in range(GQA_QW // 128)]
    q_ref[...] = (jnp.concatenate(qs, axis=1) * (HEAD_DIM ** -0.5)).astype(BF16)
    k_ref[...] = norm_rope(acc[:, c0 + GQA_QW:c0 + GQA_QW + GQA_KW], kg_ref[...])
    v_ref[...] = acc[:, c0 + GQA_QW + GQA_KW:c0 + GQA_QW + 2 * GQA_KW]


def _in_proj(x, sh, sc, w, wa, ba, qg, kg, cos_t, sin_t):
    tm = TM_PROJ
    full = lambda a: pl.BlockSpec(a.shape, lambda i: (0,) * a.ndim)
    rows = lambda w_: pl.BlockSpec((tm, w_), lambda i: (i, 0))
    return pl.pallas_call(
        _inproj_kernel,
        grid=(N_TOK // tm,),
        in_specs=[rows(D_MODEL), full(sh), full(sc), full(w), full(wa), full(ba), full(qg), full(kg),
                  rows(128), rows(128)],
        out_specs=[rows(4 * GLA_W), rows(2 * GLA_W), rows(3 * NAT_W), rows(GQA_QW), rows(GQA_KW),
                   rows(GQA_KW)],
        out_shape=[jax.ShapeDtypeStruct((N_TOK, 4 * GLA_W), F32),
                   jax.ShapeDtypeStruct((N_TOK, 2 * GLA_W), F32),
                   jax.ShapeDtypeStruct((N_TOK, 3 * NAT_W), F32),
                   jax.ShapeDtypeStruct((N_TOK, GQA_QW), BF16),
                   jax.ShapeDtypeStruct((N_TOK, GQA_KW), F32),
                   jax.ShapeDtypeStruct((N_TOK, GQA_KW), F32)],
        compiler_params=pltpu.CompilerParams(dimension_semantics=("arbitrary",),
                                             vmem_limit_bytes=VMEM_LIMIT),
        name="in_proj",
    )(x, sh, sc, w, wa, ba, qg, kg, cos_t, sin_t)


def _gla_kernel(qf_ref, kf_ref, vf_ref, laf_ref, qb_ref, kb_ref, vb_ref, lab_ref, st0_ref,
                of_ref, ob_ref, stout_ref, st_ref):
    t = pl.program_id(1)
    last_t = pl.num_programs(1) - 1
    C = GLA_CHUNK
    nchunk = GLA_TB // C

    @pl.when(t == 0)
    def _():
        st_ref[...] = st0_ref[0]

    tb = GLA_TB
    shift = int(np.log2(C))
    r = lax.broadcasted_iota(jnp.int32, (tb, tb), 0)
    c = lax.broadcasted_iota(jnp.int32, (tb, tb), 1)
    same = lax.shift_right_logical(r, shift) == lax.shift_right_logical(c, shift)
    ra = lax.broadcasted_iota(jnp.int32, (tb, H_GLA * tb), 0)
    ca = lax.broadcasted_iota(jnp.int32, (tb, H_GLA * tb), 1) & (tb - 1)
    same_a = lax.shift_right_logical(ra, shift) == lax.shift_right_logical(ca, shift)
    blk = _lane_group((H_GLA * tb, GLA_W), 0, tb) == _lane_group((H_GLA * tb, GLA_W), 1, GLA_DK)
    diag = _lane_group((GLA_W, GLA_W), 0, GLA_DV) == _lane_group((GLA_W, GLA_W), 1, GLA_DK)
    one = lambda m: jnp.where(m, 1.0, 0.0).astype(BF16)
    chunk_ones = one(same)
    ref_pick = one(c == (r & ~(C - 1)) + C // 2)

    def direction(q_ref, k_ref, v_ref, la_ref, o_ref, d, causal):
        q = q_ref[...] * (GLA_DK ** -0.5)
        k = k_ref[...]
        v = v_ref[...]
        la = la_ref[...]
        la_hi = la.astype(BF16)
        la_lo = (la - la_hi.astype(F32)).astype(BF16)
        csum = lambda m: (jnp.dot(m, la_hi, preferred_element_type=F32)
                          + jnp.dot(m, la_lo, preferred_element_type=F32))
        b = csum(one(same & ((r >= c) if causal else (r <= c))))
        btot = csum(chunk_ones)
        bref = jnp.dot(ref_pick, b.astype(BF16), preferred_element_type=F32)
        q_att = q * jnp.exp(b - bref)
        k_att = k * jnp.exp(bref - b)
        kblk = jnp.where(blk, jnp.concatenate([k_att] * H_GLA, axis=0), 0.0)
        att = _dot_nt(q_att, kblk)
        att = jnp.where(same_a & ((ra >= ca) if causal else (ra <= ca)), att, 0.0)
        vblk = jnp.where(blk, jnp.concatenate([v] * H_GLA, axis=0), 0.0)
        o_intra = _dot(att, vblk)

        q_dec = (q * jnp.exp(b)).astype(BF16)
        k_dec = (k * jnp.exp(btot - b)).astype(BF16)
        g = jnp.exp(btot)
        vb = v.astype(BF16)
        st = st_ref[d]
        o_inter = [None] * nchunk
        for ci in (range(nchunk) if causal else reversed(range(nchunk))):
            rows = slice(ci * C, (ci + 1) * C)
            o_inter[ci] = _dot_nt(q_dec[rows], st)
            u = _dot_tn(vb[rows], k_dec[rows])
            st = st * g[ci * C:ci * C + 1, :] + jnp.where(diag, u, 0.0)
        st_ref[d] = st
        o_ref[...] = o_intra + jnp.concatenate(o_inter, axis=0)

    direction(qf_ref, kf_ref, vf_ref, laf_ref, of_ref, 0, True)
    direction(qb_ref, kb_ref, vb_ref, lab_ref, ob_ref, 1, False)

    @pl.when(t == last_t)
    def _():
        stout_ref[0] = st_ref[...]


def _gla(pgla, la, st0, row0, n_seq, seq_len):
    tb = GLA_TB
    nt = seq_len // tb
    b0 = row0 // tb
    fwd = lambda col: pl.BlockSpec((tb, GLA_W), lambda s, t: (b0 + s * nt + t, col))
    bwd = lambda col: pl.BlockSpec((tb, GLA_W), lambda s, t: (b0 + s * nt + nt - 1 - t, col))
    st_spec = pl.BlockSpec((1, 2, GLA_W, GLA_W), lambda s, t: (s, 0, 0, 0))
    n = n_seq * seq_len
    return pl.pallas_call(
        _gla_kernel,
        grid=(n_seq, nt),
        in_specs=[fwd(0), fwd(1), fwd(2), fwd(0), bwd(0), bwd(1), bwd(2), bwd(1), st_spec],
        out_specs=[pl.BlockSpec((tb, GLA_W), lambda s, t: (s * nt + t, 0)),
                   pl.BlockSpec((tb, GLA_W), lambda s, t: (s * nt + nt - 1 - t, 0)),
                   st_spec],
        out_shape=[jax.ShapeDtypeStruct((n, GLA_W), F32), jax.ShapeDtypeStruct((n, GLA_W), F32),
                   jax.ShapeDtypeStruct((n_seq, 2, GLA_W, GLA_W), F32)],
        scratch_shapes=[pltpu.VMEM((2, GLA_W, GLA_W), F32)],
        compiler_params=pltpu.CompilerParams(dimension_semantics=("arbitrary", "arbitrary"),
                                             vmem_limit_bytes=VMEM_LIMIT),
        name="gla_scan",
    )(pgla, pgla, pgla, la, pgla, pgla, pgla, la, st0)


def _softmax_pv(s_list, v_list):
    m = s_list[0].max(axis=-1, keepdims=True)
    for s in s_list[1:]:
        m = jnp.maximum(m, s.max(axis=-1, keepdims=True))
    acc = None
    l = None
    for s, v in zip(s_list, v_list):
        p = jnp.exp(s - m)
        pl_ = p.sum(axis=-1, keepdims=True)
        pv = _dot(p, v)
        acc = pv if acc is None else acc + pv
        l = pl_ if l is None else l + pl_
    return acc * (1.0 / l)


def _ctx_attn_kernel(pnat_ref, q_ref, k_ref, v_ref, onat_ref, ogqa_ref):
    nq = pnat_ref[:, 0:NAT_W] * (HEAD_DIM ** -0.5)
    nk = pnat_ref[:, NAT_W:2 * NAT_W].astype(BF16)
    nv = pnat_ref[:, 2 * NAT_W:3 * NAT_W].astype(BF16)
    o = _softmax_pv([_dot_nt(_stack_heads(nq, H_NAT, HEAD_DIM), nk)], [nv])
    onat_ref[...] = _unstack_heads(o, H_NAT, HEAD_DIM).astype(BF16)

    k = k_ref[...].astype(BF16)
    v = v_ref[...].astype(BF16)
    k2 = jnp.concatenate([k, k], axis=1)
    v2 = jnp.concatenate([v, v], axis=1)
    outs = []
    for half in range(2):
        q = q_ref[:, half * 256:(half + 1) * 256]
        o = _softmax_pv([_dot_nt(_stack_heads(q, 4, HEAD_DIM), k2)], [v2])
        outs.append(_unstack_heads(o, 4, HEAD_DIM))
    ogqa_ref[...] = jnp.concatenate(outs, axis=1).astype(BF16)


def _ctx_attention(pnat, q, k, v):
    rows = lambda w_: pl.BlockSpec((SEQ, w_), lambda i: (i, 0))
    return pl.pallas_call(
        _ctx_attn_kernel,
        grid=(BATCH,),
        in_specs=[rows(3 * NAT_W), rows(GQA_QW), rows(GQA_KW), rows(GQA_KW)],
        out_specs=[rows(NAT_W), rows(GQA_QW)],
        out_shape=[jax.ShapeDtypeStruct((N_CTX, NAT_W), BF16),
                   jax.ShapeDtypeStruct((N_CTX, GQA_QW), BF16)],
        compiler_params=pltpu.CompilerParams(dimension_semantics=("arbitrary",),
                                             vmem_limit_bytes=VMEM_LIMIT),
        name="ctx_attention",
    )(pnat, q, k, v)


def _gqa_lat_kernel(q_ref, k_ref, v_ref, o_ref):
    k = k_ref[0]
    v = v_ref[0]
    k2 = jnp.concatenate([k, k], axis=1)
    v2 = jnp.concatenate([v, v], axis=1)
    slot = _lane_group((TQ_GQA, 256), 1, HEAD_DIM)
    outs = []
    for half in range(2):
        q = q_ref[:, half * 256:(half + 1) * 256]
        out = jnp.zeros((TQ_GQA, 256), F32)
        for s in range(4):
            qs = jnp.where(slot == s, q, jnp.zeros_like(q))
            o = _softmax_pv([_dot_nt(qs, k2)], [v2])
            out = jnp.where(slot == s, o, out)
        outs.append(out)
    o_ref[...] = jnp.concatenate(outs, axis=1).astype(BF16)


def _gqa_latent(q, k_all, v_all):
    tq = TQ_GQA
    nq = DEC_SEQ // tq
    tk = k_all.shape[1]
    return pl.pallas_call(
        _gqa_lat_kernel,
        grid=(DEC_BATCH, nq),
        in_specs=[pl.BlockSpec((tq, GQA_QW), lambda b, i: (N_CTX // tq + b * nq + i, 0)),
                  pl.BlockSpec((1, tk, GQA_KW), lambda b, i: (b, 0, 0)),
                  pl.BlockSpec((1, tk, GQA_KW), lambda b, i: (b, 0, 0))],
        out_specs=pl.BlockSpec((tq, GQA_QW), lambda b, i: (b * nq + i, 0)),
        out_shape=jax.ShapeDtypeStruct((N_LAT, GQA_QW), BF16),
        compiler_params=pltpu.CompilerParams(dimension_semantics=("arbitrary", "arbitrary"),
                                             vmem_limit_bytes=VMEM_LIMIT),
        name="gqa_latent",
    )(q, k_all, v_all)


def _nat_lat_kernel(q_ref, k_ref, v_ref, kc_ref, vc_ref, bias_ref, o_ref):
    j = pl.program_id(1)
    kc = kc_ref[0].astype(BF16)
    vc = vc_ref[0].astype(BF16)
    for i in range(NAT_ROWS_PER_STEP):
        r = j * NAT_ROWS_PER_STEP + i
        r0 = jnp.clip(r - NAT_KH // 2, 0, ROWS - NAT_KH)
        win = pl.ds(pl.multiple_of(r0 * GRID_W, GRID_W), NAT_KH * GRID_W)
        q = q_ref[i * GRID_W:(i + 1) * GRID_W, :] * (HEAD_DIM ** -0.5)
        qs = _stack_heads(q, H_NAT, HEAD_DIM)
        s_loc = _dot_nt(qs, k_ref[win, :]) + bias_ref[r - r0]
        s_ctx = _dot_nt(qs, kc)
        o = _softmax_pv([s_loc, s_ctx], [v_ref[win, :], vc])
        o_ref[i * GRID_W:(i + 1) * GRID_W, :] = _unstack_heads(o, H_NAT, HEAD_DIM).astype(BF16)


def _nat_latent(pnat, kc, vc, bias):
    tq = NAT_ROWS_PER_STEP * GRID_W
    nq = DEC_SEQ // tq
    lat_blk = N_CTX // DEC_SEQ
    return pl.pallas_call(
        _nat_lat_kernel,
        grid=(DEC_BATCH, nq),
        in_specs=[pl.BlockSpec((tq, NAT_W), lambda b, j: (N_CTX // tq + b * nq + j, 0)),
                  pl.BlockSpec((DEC_SEQ, NAT_W), lambda b, j: (lat_blk + b, 1)),
                  pl.BlockSpec((DEC_SEQ, NAT_W), lambda b, j: (lat_blk + b, 2)),
                  pl.BlockSpec((1, PAST_LEN, NAT_W), lambda b, j: (b, 0, 0)),
                  pl.BlockSpec((1, PAST_LEN, NAT_W), lambda b, j: (b, 0, 0)),
                  pl.BlockSpec(bias.shape, lambda b, j: (0, 0, 0))],
        out_specs=pl.BlockSpec((tq, NAT_W), lambda b, j: (b * nq + j, 0)),
        out_shape=jax.ShapeDtypeStruct((N_LAT, NAT_W), BF16),
        compiler_params=pltpu.CompilerParams(dimension_semantics=("arbitrary", "arbitrary"),
                                             vmem_limit_bytes=VMEM_LIMIT),
        name="nat_latent",
    )(pnat, pnat, pnat, kc, vc, bias)


def _route_gates(logits_t, bias_ref):
    per = N_EXPERTS // N_GROUPS
    t = logits_t.shape[1]
    neg = -jnp.inf
    pos = lax.broadcasted_iota(jnp.int32, (per, t), 0)
    scores = [_sigmoid(logits_t[g * per:(g + 1) * per, :]) for g in range(N_GROUPS)]
    biased = [scores[g] + bias_ref[g * per:(g + 1) * per, :] for g in range(N_GROUPS)]

    grp = []
    for v in biased:
        m1 = jnp.max(v, axis=0, keepdims=True)
        i1 = jnp.min(jnp.where(v == m1, pos, per), axis=0, keepdims=True)
        m2 = jnp.max(jnp.where(pos == i1, neg, v), axis=0, keepdims=True)
        grp.append(m1 + m2)

    keep = [jnp.zeros((1, t), jnp.bool_) for _ in range(N_GROUPS)]
    for _ in range(TOPK_GROUPS):
        best = functools.reduce(jnp.maximum, grp)
        first = jnp.full((1, t), N_GROUPS, jnp.int32)
        for g in reversed(range(N_GROUPS)):
            first = jnp.where(grp[g] == best, g, first)
        for g in range(N_GROUPS):
            hit = first == g
            keep[g] = keep[g] | hit
            grp[g] = jnp.where(hit, neg, grp[g])

    cand = [jnp.where(keep[g], biased[g], neg) for g in range(N_GROUPS)]
    flat = [pos + g * per for g in range(N_GROUPS)]
    picks, weights = [], []
    for _ in range(TOP_K):
        best = functools.reduce(jnp.maximum, [jnp.max(v, axis=0, keepdims=True) for v in cand])
        first = functools.reduce(jnp.minimum, [
            jnp.min(jnp.where(cand[g] == best, flat[g], N_EXPERTS), axis=0, keepdims=True)
            for g in range(N_GROUPS)])
        score = jnp.zeros((1, t), F32)
        for g in range(N_GROUPS):
            hit = flat[g] == first
            score = score + jnp.sum(jnp.where(hit, scores[g], 0.0), axis=0, keepdims=True)
            cand[g] = jnp.where(hit, neg, cand[g])
        picks.append(first)
        weights.append(score)

    total = functools.reduce(lambda a, b: a + b, weights)
    gates = jnp.concatenate([v / total * ROUTE_SCALE for v in weights], axis=0)

    sel = [functools.reduce(lambda a, b: a | b, [flat[g] == k for k in picks]) for g in range(N_GROUPS)]
    sel = jnp.where(jnp.concatenate(sel, axis=0), 1.0, 0.0).astype(BF16)
    before = (lax.broadcasted_iota(jnp.int32, (t, t), 0) < lax.broadcasted_iota(jnp.int32, (t, t), 1))
    rank = jnp.dot(sel, jnp.where(before, 1.0, 0.0).astype(BF16), preferred_element_type=F32)
    count = jnp.sum(sel.astype(F32), axis=1, keepdims=True)
    pieces = jnp.floor((count + (PIECE - 1)) * (1.0 / PIECE))
    lower = (lax.broadcasted_iota(jnp.int32, (N_EXPERTS, N_EXPERTS), 1)
             < lax.broadcasted_iota(jnp.int32, (N_EXPERTS, N_EXPERTS), 0))
    start = PIECE * jnp.dot(jnp.where(lower, 1.0, 0.0).astype(BF16),
                            jnp.broadcast_to(pieces, (N_EXPERTS, 128)).astype(BF16),
                            preferred_element_type=F32)[:, 0:1]
    row = rank + start
    rows = []
    for k in picks:
        r = jnp.zeros((1, t), F32)
        for g in range(N_GROUPS):
            r = r + jnp.sum(jnp.where(flat[g] == k, row[g * per:(g + 1) * per, :], 0.0), axis=0, keepdims=True)
        rows.append(r)
    return (jnp.concatenate(rows, axis=0).astype(jnp.int32), gates,
            jnp.broadcast_to(pieces, (N_EXPERTS, 128)).astype(jnp.int32))


def _outproj_kernel(x_ref, ofc_ref, ofl_ref, obc_ref, obl_ref, gg_ref, onatc_ref, onatl_ref, ogqac_ref, ogqal_ref,
                    w_ref, ng_ref, g1_ref, lng_ref, lnb_ref, sh2_ref, sc2_ref, wr_ref, rb_ref, x1_ref, xm_ref,
                    row_ref, gate_ref, pieces_ref):
    ci = _cond_row(pl.program_id(0), TM_PROJ)
    is_ctx = pl.program_id(0) < N_CTX // TM_PROJ
    either = lambda c_ref, l_ref: jnp.where(is_ctx, c_ref[...], l_ref[...])
    og = either(ofc_ref, ofl_ref) + either(obc_ref, obl_ref)
    halves = [og[:, j * 128:(j + 1) * 128] for j in range(GLA_W // 128)]
    ms = jnp.concatenate([_head_mean_sq(h) for h in halves], axis=1)
    ogla = og * lax.rsqrt(ms + EPS) * ng_ref[...] * _silu(gg_ref[...])
    mix = (_dot(ogla, w_ref[0:GLA_W, :]) + _dot(either(onatc_ref, onatl_ref), w_ref[GLA_W:GLA_W + NAT_W, :])
           + _dot(either(ogqac_ref, ogqal_ref), w_ref[GLA_W + NAT_W:, :]))
    x1 = _ln(ALPHA * x_ref[...] + g1_ref[pl.ds(ci, 1), :] * mix) * lng_ref[...] + lnb_ref[...]
    x1_ref[...] = x1
    xm = (_ln(x1) * (1.0 + sc2_ref[pl.ds(ci, 1), :]) + sh2_ref[pl.ds(ci, 1), :]).astype(BF16)
    xm_ref[...] = xm
    row_ref[...], gate_ref[...], pieces_ref[...] = _route_gates(_dot_nt(wr_ref[...], xm), rb_ref)


def _out_proj(x, of, ob, pgla, onat, ogqa, w, ng, g1, lng, lnb, sh2, sc2, wr, rb):
    tm = TM_PROJ
    n_ctx = N_CTX // tm
    full = lambda a: pl.BlockSpec(a.shape, lambda i: (0,) * a.ndim)
    rows = lambda w_: pl.BlockSpec((tm, w_), lambda i: (i, 0))
    ctx = lambda w_: pl.BlockSpec((tm, w_), lambda i: (jnp.minimum(i, n_ctx - 1), 0))
    lat = lambda w_: pl.BlockSpec((tm, w_), lambda i: (jnp.maximum(i - n_ctx, 0), 0))
    return pl.pallas_call(
        _outproj_kernel,
        grid=(N_TOK // tm,),
        in_specs=[rows(D_MODEL), ctx(GLA_W), lat(GLA_W), ctx(GLA_W), lat(GLA_W),
                  pl.BlockSpec((tm, GLA_W), lambda i: (i, 3)),
                  ctx(NAT_W), lat(NAT_W), ctx(GQA_QW), lat(GQA_QW), full(w), full(ng), full(g1), full(lng),
                  full(lnb), full(sh2), full(sc2), full(wr), full(rb)],
        out_specs=[rows(D_MODEL), rows(D_MODEL), pl.BlockSpec((TOP_K, tm), lambda i: (0, i)),
                   pl.BlockSpec((TOP_K, tm), lambda i: (0, i)),
                   pl.BlockSpec((N_EXPERTS, 128), lambda i: (0, i))],
        out_shape=[jax.ShapeDtypeStruct((N_TOK, D_MODEL), F32),
                   jax.ShapeDtypeStruct((N_TOK, D_MODEL), BF16),
                   jax.ShapeDtypeStruct((TOP_K, N_TOK), jnp.int32),
                   jax.ShapeDtypeStruct((TOP_K, N_TOK), F32),
                   jax.ShapeDtypeStruct((N_EXPERTS, MOE_NT * 128), jnp.int32)],
        compiler_params=pltpu.CompilerParams(dimension_semantics=("arbitrary",),
                                             vmem_limit_bytes=VMEM_LIMIT),
        name="out_proj",
    )(x, *of, *ob, pgla, *onat, *ogqa, w, ng, g1, lng, lnb, sh2, sc2, wr, rb)


def _plan_specs(n_prefetch, **kw):
    return pltpu.PrefetchScalarGridSpec(num_scalar_prefetch=n_prefetch, **kw)


def _for_each_copy(t, copies, fn):
    bsrc_ref, bdst_ref, ssrc_ref, sdst_ref, nbig_ref, nsmall_ref = copies

    def big(i, c):
        j = t * MAX_BIG + i
        fn(pl.multiple_of(bsrc_ref[j], PIECE), pl.multiple_of(bdst_ref[j], PIECE), COPY_ROWS)
        return c

    def small(i, c):
        j = t * N_EXPERTS + i
        fn(pl.multiple_of(ssrc_ref[j], PIECE), pl.multiple_of(sdst_ref[j], PIECE), PIECE)
        return c

    lax.fori_loop(0, nbig_ref[t], big, 0)
    lax.fori_loop(0, nsmall_ref[t], small, 0)


def _wait_copies(n_big, n_small, make):
    for n, rows in ((n_big, COPY_ROWS), (n_small, PIECE)):
        def one(i, c, rows=rows):
            make(rows).wait()
            return c
        lax.fori_loop(0, n, one, 0)


def _dispatch_kernel(bsrc_ref, bdst_ref, ssrc_ref, sdst_ref, nbig_ref, nsmall_ref, zdst_ref, tot_ref, xm_ref, ldt_ref,
                     xs_ref, stage_ref, zero_ref, sem_big, sem_small, tail_sem):
    copies = (bsrc_ref, bdst_ref, ssrc_ref, sdst_ref, nbig_ref, nsmall_ref)
    t = pl.program_id(0)
    nt = pl.num_programs(0)
    slot = lax.rem(t, 2)

    def run_copy(local_row, xs_row, rows, s):
        sem = sem_big if rows == COPY_ROWS else sem_small
        return pltpu.make_async_copy(stage_ref.at[s, pl.ds(local_row, rows)], xs_ref.at[pl.ds(xs_row, rows)],
                                     sem.at[s])

    def wait_tile(tt, s):
        _wait_copies(nbig_ref[tt], nsmall_ref[tt], lambda rows: run_copy(0, 0, rows, s))

    @pl.when(t >= 2)
    def _():
        wait_tile(t - 2, slot)

    xm = xm_ref[...]
    ld = ldt_ref[...].astype(jnp.int16)
    for c in range(MOE_CAP // MOE_CHUNK):
        p = (lax.broadcasted_iota(jnp.int32, (MOE_CHUNK, MOE_TILE), 0) + c * MOE_CHUNK).astype(jnp.int16)
        hit = p == ld[0:1, :]
        for k in range(1, TOP_K):
            hit = hit | (p == ld[k:k + 1, :])
        stage_ref[slot, c * MOE_CHUNK:(c + 1) * MOE_CHUNK, :] = jnp.dot(
            jnp.where(hit, jnp.ones((), BF16), jnp.zeros((), BF16)), xm, preferred_element_type=F32).astype(BF16)
    _for_each_copy(t, copies, lambda a, b, rows: run_copy(a, b, rows, slot).start())

    @pl.when(t == nt - 1)
    def _():
        zero_ref[...] = jnp.zeros_like(zero_ref)
        n_piece = tot_ref[1]
        first_free_tile = tot_ref[0] // ROW_TILE
        n_tile = XS_ROWS // ROW_TILE - first_free_tile

        def zero_piece(i):
            return pltpu.make_async_copy(zero_ref.at[pl.ds(0, PIECE)],
                                         xs_ref.at[pl.ds(pl.multiple_of(zdst_ref[i], PIECE), PIECE)],
                                         tail_sem.at[0])

        def zero_tile(i):
            row = pl.multiple_of((first_free_tile + i) * ROW_TILE, ROW_TILE)
            return pltpu.make_async_copy(zero_ref, xs_ref.at[pl.ds(row, ROW_TILE)], tail_sem.at[1])

        def run(n, make, op):
            def body(i, c):
                op(make(i))
                return c
            lax.fori_loop(0, n, body, 0)

        run(n_piece, zero_piece, lambda cp: cp.start())
        run(n_tile, zero_tile, lambda cp: cp.start())
        run(n_piece, zero_piece, lambda cp: cp.wait())
        run(n_tile, zero_tile, lambda cp: cp.wait())
        wait_tile(t - 1, 1 - slot)
        wait_tile(t, slot)


def _moe_dispatch(plan, xm, row_t):
    return pl.pallas_call(
        _dispatch_kernel,
        grid_spec=_plan_specs(
            8, grid=(MOE_NT,),
            in_specs=[pl.BlockSpec((MOE_TILE, D_MODEL), lambda t, *_: (t, 0)),
                      pl.BlockSpec((TOP_K, MOE_TILE), lambda t, *_: (0, t))],
            out_specs=pl.BlockSpec(memory_space=pl.ANY),
            scratch_shapes=[pltpu.VMEM((2, MOE_CAP, D_MODEL), BF16), pltpu.VMEM((ROW_TILE, D_MODEL), BF16),
                            pltpu.SemaphoreType.DMA((2,)), pltpu.SemaphoreType.DMA((2,)),
                            pltpu.SemaphoreType.DMA((2,))]),
        out_shape=jax.ShapeDtypeStruct((XS_ROWS, D_MODEL), BF16),
        compiler_params=pltpu.CompilerParams(dimension_semantics=("arbitrary",),
                                             vmem_limit_bytes=VMEM_LIMIT),
        name="moe_dispatch",
    )(*plan["copies"], plan["zero_dst"], plan["total"], xm, row_t)


def _experts_kernel(exp_ref, fresh_ref, used_ref, xs_ref, wgu_ref, wd_ref, y_ref, wgu_bf_ref, wd_bf_ref):
    w = pl.program_id(0)

    @pl.when(fresh_ref[w] == 1)
    def _():
        wgu_bf_ref[...] = wgu_ref[0].astype(BF16)
        wd_bf_ref[...] = wd_ref[0].astype(BF16)

    @pl.when(used_ref[w] == 1)
    def _():
        ab = jnp.dot(xs_ref[...], wgu_bf_ref[...], preferred_element_type=F32)
        h = (_silu(ab[:, :D_EXPERT]) * ab[:, D_EXPERT:]).astype(BF16)
        y_ref[...] = jnp.dot(h, wd_bf_ref[...], preferred_element_type=F32).astype(BF16)

    @pl.when(used_ref[w] == 0)
    def _():
        y_ref[...] = jnp.zeros_like(y_ref)


def _moe_experts(layer, plan, xs, wgu, wd):
    return pl.pallas_call(
        _experts_kernel,
        grid_spec=_plan_specs(
            3, grid=(XS_ROWS // ROW_TILE,),
            in_specs=[pl.BlockSpec((ROW_TILE, D_MODEL), lambda w, *_: (w, 0)),
                      pl.BlockSpec((None, 1, D_MODEL, 2 * D_EXPERT), lambda w, ex, *_: (layer, ex[w], 0, 0)),
                      pl.BlockSpec((None, 1, D_EXPERT, D_MODEL), lambda w, ex, *_: (layer, ex[w], 0, 0))],
            out_specs=pl.BlockSpec((ROW_TILE, D_MODEL), lambda w, *_: (w, 0)),
            scratch_shapes=[pltpu.VMEM((D_MODEL, 2 * D_EXPERT), BF16), pltpu.VMEM((D_EXPERT, D_MODEL), BF16)]),
        out_shape=jax.ShapeDtypeStruct((XS_ROWS, D_MODEL), BF16),
        compiler_params=pltpu.CompilerParams(dimension_semantics=("arbitrary",),
                                             vmem_limit_bytes=VMEM_LIMIT),
        name="moe_experts",
    )(plan["tile_expert"], plan["tile_fresh"], plan["tile_used"], xs, wgu, wd)


def _combine_kernel(bsrc_ref, bdst_ref, ssrc_ref, sdst_ref, nbig_ref, nsmall_ref, y_ref, ldt_ref, gate_ref, xm_ref,
                    x1_ref, g2_ref, wsgu_ref, wsd_ref, lng_ref, lnb_ref, out_ref, ybuf_ref, sem_big, sem_small):
    copies = (bsrc_ref, bdst_ref, ssrc_ref, sdst_ref, nbig_ref, nsmall_ref)
    t = pl.program_id(0)
    nt = pl.num_programs(0)
    slot = lax.rem(t, 2)
    ci = _cond_row(t, MOE_TILE)

    def run_copy(local_row, y_row, rows, s):
        sem = sem_big if rows == COPY_ROWS else sem_small
        return pltpu.make_async_copy(y_ref.at[pl.ds(y_row, rows)], ybuf_ref.at[s, pl.ds(local_row, rows)], sem.at[s])

    def fetch(tt, s):
        _for_each_copy(tt, copies, lambda a, b, rows: run_copy(a, b, rows, s).start())

    @pl.when(t == 0)
    def _():
        ybuf_ref[...] = jnp.zeros_like(ybuf_ref)
        fetch(0, 0)

    @pl.when(t + 1 < nt)
    def _():
        fetch(t + 1, 1 - slot)

    _wait_copies(nbig_ref[t], nsmall_ref[t], lambda rows: run_copy(0, 0, rows, slot))

    routed = jnp.zeros((MOE_TILE, D_MODEL), F32)
    ld = ldt_ref[...].astype(jnp.int16)
    gate = gate_ref[...].astype(BF16)
    for c in range(MOE_CAP // MOE_CHUNK):
        p = (lax.broadcasted_iota(jnp.int32, (MOE_CHUNK, MOE_TILE), 0) + c * MOE_CHUNK).astype(jnp.int16)
        wmat = jnp.zeros((MOE_CHUNK, MOE_TILE), BF16)
        for k in range(TOP_K):
            wmat = jnp.where(p == ld[k:k + 1, :], gate[k:k + 1, :], wmat)
        routed = routed + _dot_tn(wmat, ybuf_ref[slot, c * MOE_CHUNK:(c + 1) * MOE_CHUNK, :])

    sab = jnp.dot(xm_ref[...], wsgu_ref[...], preferred_element_type=F32)
    shared = _dot(_silu(sab[:, :D_SHARED]) * sab[:, D_SHARED:], wsd_ref[...])
    z = ALPHA * x1_ref[...] + g2_ref[pl.ds(ci, 1), :] * (routed + shared)
    out_ref[...] = _ln(z) * lng_ref[...] + lnb_ref[...]


def _moe_combine(plan, y, row_t, gate_t, xm, x1, g2, wsgu, wsd, lng, lnb):
    full = lambda a: pl.BlockSpec(a.shape, lambda t, *_: (0,) * a.ndim)
    rows = lambda w_: pl.BlockSpec((MOE_TILE, w_), lambda t, *_: (t, 0))
    picks = pl.BlockSpec((TOP_K, MOE_TILE), lambda t, *_: (0, t))
    return pl.pallas_call(
        _combine_kernel,
        grid_spec=_plan_specs(
            6, grid=(MOE_NT,),
            in_specs=[pl.BlockSpec(memory_space=pl.ANY), picks, picks, rows(D_MODEL), rows(D_MODEL),
                      full(g2), full(wsgu), full(wsd), full(lng), full(lnb)],
            out_specs=rows(D_MODEL),
            scratch_shapes=[pltpu.VMEM((2, MOE_CAP, D_MODEL), BF16), pltpu.SemaphoreType.DMA((2,)),
                            pltpu.SemaphoreType.DMA((2,))]),
        out_shape=jax.ShapeDtypeStruct((N_TOK, D_MODEL), F32),
        compiler_params=pltpu.CompilerParams(dimension_semantics=("arbitrary",),
                                             vmem_limit_bytes=VMEM_LIMIT),
        name="moe_combine",
    )(*plan["copies"], y, row_t, gate_t, xm, x1, g2, wsgu, wsd, lng, lnb)


def _moe_plan(npc):
    i32 = jnp.int32
    run = npc * PIECE
    lo = jnp.cumsum(run, axis=1) - run
    tot_e = jnp.sum(run, axis=0)
    region = (tot_e + ROW_TILE - 1) // ROW_TILE * ROW_TILE
    region_end = jnp.cumsum(region)
    off = region_end - region
    hs = off[None, :] + jnp.cumsum(run, axis=0) - run

    per_copy = COPY_ROWS // PIECE
    n_big = npc // per_copy
    odd = npc - n_big * per_copy

    def expand(count, width):
        end = jnp.cumsum(count, axis=1)
        i = jnp.arange(width, dtype=i32)[None, :, None]
        mine = (i >= (end - count)[:, None, :]) & (i < end[:, None, :])
        pick = lambda a: jnp.sum(jnp.where(mine, a[:, None, :], 0), axis=2)
        return pick, i[:, :, 0] - pick(end - count)

    pick_b, k_b = expand(n_big, MAX_BIG)
    pick_s, _ = expand(odd, N_EXPERTS)
    copies = [pick_b(lo) + k_b * COPY_ROWS, pick_b(hs) + k_b * COPY_ROWS,
              pick_s(lo + n_big * COPY_ROWS), pick_s(hs + n_big * COPY_ROWS)]
    copies = [c.reshape(-1).astype(i32) for c in copies] + [jnp.sum(n_big, axis=1).astype(i32),
                                                            jnp.sum(odd, axis=1).astype(i32)]

    n_zero = ((region - tot_e) // PIECE)[None, :]
    pick_z, k_z = expand(n_zero, MAX_ZERO)
    zero_dst = (pick_z((off + tot_e)[None, :]) + k_z * PIECE).reshape(-1).astype(i32)

    start = jnp.arange(XS_ROWS // ROW_TILE, dtype=i32) * ROW_TILE
    ex = jnp.minimum(jnp.sum((start[:, None] >= region_end[None, :]).astype(i32), axis=1), N_EXPERTS - 1)
    prev_ex = jnp.concatenate([jnp.full((1,), -1, i32), ex[:-1]])
    return dict(copies=copies, zero_dst=zero_dst,
                total=jnp.stack([region_end[-1], jnp.sum(n_zero)]).astype(i32),
                tile_expert=ex.astype(i32), tile_fresh=(ex != prev_ex).astype(i32),
                tile_used=(start < region_end[-1]).astype(i32))


def _rope_tables():
    t = jnp.arange(DEC_SEQ)
    n_freq = HEAD_DIM // 4
    freqs = ROPE_THETA ** (-jnp.arange(n_freq, dtype=F32) / n_freq)
    ang = jnp.concatenate([(t // GRID_W).astype(F32)[:, None] * freqs,
                           (t % GRID_W).astype(F32)[:, None] * freqs], -1)
    cos, sin = jnp.cos(ang), jnp.sin(ang)
    cos_h = jnp.concatenate([cos, cos], -1)
    sin_h = jnp.concatenate([-sin, sin], -1)
    lat = lambda a: jnp.tile(a, (DEC_BATCH, 128 // HEAD_DIM))
    cos_t = jnp.concatenate([jnp.ones((N_CTX, 128), F32), lat(cos_h)], 0)
    sin_t = jnp.concatenate([jnp.zeros((N_CTX, 128), F32), lat(sin_h)], 0)
    return cos_t, sin_t


def _nat_bias_table(rpb):
    n_dr, n_dc = 2 * NAT_KH - 1, 2 * NAT_KW - 1
    cidx = np.arange(GRID_W)
    dc_idx = np.clip(cidx[None, :] - cidx[:, None] + NAT_KW - 1, 0, n_dc - 1)
    col_start = np.clip(cidx - NAT_KW // 2, 0, GRID_W - NAT_KW)
    col_in = (cidx[None, :] >= col_start[:, None]) & (cidx[None, :] < col_start[:, None] + NAT_KW)
    onehot = (dc_idx.reshape(1, -1) == np.arange(n_dc)[:, None]).astype(np.float32)
    t = jnp.dot(rpb.reshape(H_NAT * n_dr, n_dc), onehot, precision=lax.Precision.HIGHEST)
    t = jnp.where(col_in[None, None], t.reshape(H_NAT, n_dr, GRID_W, GRID_W), -jnp.inf)
    bias = jnp.stack([t[:, NAT_KH - 1 - p:2 * NAT_KH - 1 - p] for p in range(NAT_KH)], axis=0)
    return bias.transpose(0, 1, 3, 2, 4).reshape(NAT_KH, H_NAT * GRID_W, NAT_KH * GRID_W)


def _prep_w_in(w_in):
    cuts = np.cumsum((0,) + IN_SPLITS)
    seg = [w_in[:, cuts[i]:cuts[i + 1]] for i in range(len(IN_SPLITS))]
    gq, gk, gv, gg, ga, nq, nk, nv, aq, ak, av = seg
    aq = aq.reshape(D_MODEL, H_GQA, HEAD_DIM)[:, np.array(GQA_SLOT_HEADS), :].reshape(D_MODEL, GQA_QW)
    ga = jnp.pad(ga, ((0, 0), (0, 128 - 2 * GLA_LOWRANK)))
    return jnp.concatenate([gq, gk, gv, gg, nq, nk, nv, aq, ak, av, ga], axis=1).astype(BF16)


def _prep_w_a(w_a2, b_a):
    wa = jnp.zeros((128, 2 * GLA_W), F32)
    wa = wa.at[0:GLA_LOWRANK, 0:GLA_W].set(w_a2[0])
    wa = wa.at[GLA_LOWRANK:2 * GLA_LOWRANK, GLA_W:].set(w_a2[1])
    return wa.astype(BF16), b_a.reshape(1, 2 * GLA_W)


def _prep_w_out(w_out):
    gqa = w_out[GLA_W + NAT_W:].reshape(H_GQA, HEAD_DIM, D_MODEL)[np.array(GQA_SLOT_HEADS)]
    return jnp.concatenate([w_out[:GLA_W + NAT_W], gqa.reshape(GQA_QW, D_MODEL)], 0).astype(BF16)


def _state_to_blockdiag(s):
    out = jnp.zeros(s.shape[:-3] + (GLA_W, GLA_W), F32)
    for h in range(H_GLA):
        out = out.at[..., h * GLA_DV:(h + 1) * GLA_DV, h * GLA_DK:(h + 1) * GLA_DK].set(
            jnp.swapaxes(s[..., h, :, :], -1, -2))
    return out


def _blockdiag_to_state(st):
    blocks = [st[..., h * GLA_DV:(h + 1) * GLA_DV, h * GLA_DK:(h + 1) * GLA_DK] for h in range(H_GLA)]
    return jnp.swapaxes(jnp.stack(blocks, axis=-3), -1, -2)


def kernel(x_prompt, x_sample, state_gla, cache_nat_k, cache_nat_v, cache_gqa_k, cache_gqa_v, c, c_ctx, w_mod, b_mod, w_in, gla_w_a2, gla_b_a, gla_norm_g, nat_rpb, gqa_q_norm_g, gqa_k_norm_g, w_out, ln1_g, ln1_b, w_router, router_bias, w_expert_gu, w_expert_down, w_shared_gu, w_shared_down, ln2_g, ln2_b):
    x = jnp.concatenate([x_prompt.reshape(N_CTX, D_MODEL), x_sample.reshape(N_LAT, D_MODEL)], axis=0)
    cond = jnp.concatenate([c_ctx[None, :], c, jnp.zeros((N_COND - 1 - DEC_BATCH, D_MODEL), F32)], axis=0)
    mods = _modulation(cond, w_mod, b_mod)
    cos_t, sin_t = _rope_tables()
    row = lambda a: a.reshape(1, -1)

    st_gla, st_nk, st_nv, st_gk, st_gv = [], [], [], [], []
    for l in range(DEPTH):
        sh1, sc1, g1, sh2, sc2, g2 = [mods[l, :, j * D_MODEL:(j + 1) * D_MODEL] for j in range(6)]
        wa, ba = _prep_w_a(gla_w_a2[l], gla_b_a[l])
        qg = row(jnp.tile(gqa_q_norm_g[l], 128 // HEAD_DIM))
        kg = row(jnp.tile(gqa_k_norm_g[l], 128 // HEAD_DIM))
        pgla, la, pnat, q, k, v = _in_proj(x, sh1, sc1, _prep_w_in(w_in[l]), wa, ba, qg, kg, cos_t, sin_t)

        zero_st = jnp.zeros((BATCH, 2, GLA_W, GLA_W), F32)
        of_c, ob_c, st_c = _gla(pgla, la, zero_st, 0, BATCH, SEQ)
        of_l, ob_l, _ = _gla(pgla, la, _state_to_blockdiag(state_gla[:, l]), N_CTX, DEC_BATCH, DEC_SEQ)

        onat_c, ogqa_c = _ctx_attention(pnat, q, k, v)
        k_all = jnp.concatenate([k[N_CTX:].reshape(DEC_BATCH, DEC_SEQ, GQA_KW),
                                 cache_gqa_k[:, l].reshape(DEC_BATCH, PAST_LEN, GQA_KW)], axis=1).astype(BF16)
        v_all = jnp.concatenate([v[N_CTX:].reshape(DEC_BATCH, DEC_SEQ, GQA_KW),
                                 cache_gqa_v[:, l].reshape(DEC_BATCH, PAST_LEN, GQA_KW)], axis=1).astype(BF16)
        ogqa_l = _gqa_latent(q, k_all, v_all)
        onat_l = _nat_latent(pnat, cache_nat_k[:, l].reshape(DEC_BATCH, PAST_LEN, NAT_W),
                             cache_nat_v[:, l].reshape(DEC_BATCH, PAST_LEN, NAT_W),
                             _nat_bias_table(nat_rpb[l]))

        x1, xm, row_t, gate_t, pieces = _out_proj(
            x, (of_c, of_l), (ob_c, ob_l), pgla, (onat_c, onat_l), (ogqa_c, ogqa_l),
            _prep_w_out(w_out[l]), row(jnp.tile(gla_norm_g[l], H_GLA)), g1, row(ln1_g[l]), row(ln1_b[l]),
            sh2, sc2, w_router[l].T.astype(BF16), router_bias[l].reshape(N_EXPERTS, 1))
        plan = _moe_plan(pieces[:, ::128].T)
        y = _moe_experts(l, plan, _moe_dispatch(plan, xm, row_t), w_expert_gu, w_expert_down)
        x = _moe_combine(plan, y, row_t, gate_t, xm, x1, g2, w_shared_gu[l].astype(BF16),
                         w_shared_down[l].astype(BF16), row(ln2_g[l]), row(ln2_b[l]))

        st_gla.append(_blockdiag_to_state(st_c))
        st_nk.append(pnat[:N_CTX, NAT_W:2 * NAT_W].reshape(BATCH, SEQ, H_NAT, HEAD_DIM))
        st_nv.append(pnat[:N_CTX, 2 * NAT_W:].reshape(BATCH, SEQ, H_NAT, HEAD_DIM))
        st_gk.append(k[:N_CTX].reshape(BATCH, SEQ, KV_GQA, HEAD_DIM))
        st_gv.append(v[:N_CTX].reshape(BATCH, SEQ, KV_GQA, HEAD_DIM))

    y_prompt = x[:N_CTX].reshape(BATCH, SEQ, D_MODEL)
    y_sample = x[N_CTX:].reshape(DEC_BATCH, DEC_SEQ, D_MODEL)
    return (y_prompt, y_sample, jnp.stack(st_gla, axis=1), jnp.stack(st_nk, axis=1), jnp.stack(st_nv, axis=1),
            jnp.stack(st_gk, axis=1), jnp.stack(st_gv, axis=1))
```

```python
import functools

import numpy as np
import jax
import jax.numpy as jnp
from jax import lax
from jax.experimental import pallas as pl
from jax.experimental.pallas import tpu as pltpu

D_MODEL = 1024
BATCH = 16
SEQ = 256
DEPTH = 2
DEC_BATCH = 2
DEC_SEQ = 4096
PAST_LEN = 512
GRID_W = 64
HEAD_DIM = 64
H_GLA = 4
GLA_DK = 64
GLA_DV = 64
GLA_LOWRANK = 16
GLA_TAU = 16.0
H_NAT = 4
NAT_KH = 8
NAT_KW = 16
H_GQA = 8
KV_GQA = 2
ROPE_THETA = 10000.0
N_EXPERTS = 64
TOP_K = 8
N_GROUPS = 8
TOPK_GROUPS = 4
D_EXPERT = 256
D_SHARED = 256
ROUTE_SCALE = 2.5
EPS = 1e-6
ALPHA = (2 * DEPTH) ** 0.25
IN_SPLITS = (H_GLA * GLA_DK, H_GLA * GLA_DK, H_GLA * GLA_DV, H_GLA * GLA_DV, 2 * GLA_LOWRANK,
             H_NAT * HEAD_DIM, H_NAT * HEAD_DIM, H_NAT * HEAD_DIM,
             H_GQA * HEAD_DIM, KV_GQA * HEAD_DIM, KV_GQA * HEAD_DIM)

F32 = jnp.float32
BF16 = jnp.bfloat16

N_CTX = BATCH * SEQ
N_LAT = DEC_BATCH * DEC_SEQ
N_TOK = N_CTX + N_LAT
N_COND = 8
GLA_W = H_GLA * GLA_DK
NAT_W = H_NAT * HEAD_DIM
GQA_QW = H_GQA * HEAD_DIM
GQA_KW = KV_GQA * HEAD_DIM
W_IN_COLS = 4 * GLA_W + 3 * NAT_W + GQA_QW + 2 * GQA_KW + 128
GA_COL = W_IN_COLS - 128
GQA_SLOT_HEADS = (0, 4, 1, 5, 2, 6, 3, 7)
GLA_CHUNK = 32
GLA_TB = 256
ROWS = DEC_SEQ // GRID_W
NAT_ROWS_PER_STEP = 8
VMEM_LIMIT = 56 * 1024 * 1024

TM_PROJ = 256
MOE_TILE = 256
MOE_NT = N_TOK // MOE_TILE
PIECE = 16
COPY_ROWS = 2 * PIECE
MOE_CAP = MOE_TILE * TOP_K + N_EXPERTS * PIECE
MAX_BIG = MOE_CAP // COPY_ROWS
MOE_CHUNK = 512
ROW_TILE = 512
XS_ROWS = N_TOK * TOP_K + MOE_NT * N_EXPERTS * (PIECE - 1) + N_EXPERTS * (ROW_TILE - PIECE)
assert XS_ROWS % ROW_TILE == 0
MAX_ZERO = N_EXPERTS * (ROW_TILE // PIECE - 1)
TQ_GQA = 256


def _dot(a, b):
    return jnp.dot(a.astype(BF16), b.astype(BF16), preferred_element_type=F32)


def _dot_nt(a, b):
    return lax.dot_general(a.astype(BF16), b.astype(BF16), (((1,), (1,)), ((), ())),
                           preferred_element_type=F32)


def _dot_tn(a, b):
    return lax.dot_general(a.astype(BF16), b.astype(BF16), (((0,), (0,)), ((), ())),
                           preferred_element_type=F32)


def _dot_split(a, b_bf16):
    hi = a.astype(BF16)
    lo = (a - hi.astype(F32)).astype(BF16)
    return (jnp.dot(hi, b_bf16, preferred_element_type=F32)
            + jnp.dot(lo, b_bf16, preferred_element_type=F32))


def _sigmoid(x):
    return 1.0 / (1.0 + jnp.exp(-x))


def _silu(x):
    return x * _sigmoid(x)


def _ln(x):
    xc = x - jnp.mean(x, axis=-1, keepdims=True)
    return xc * lax.rsqrt(jnp.mean(xc * xc, axis=-1, keepdims=True) + EPS)


def _lane_group(shape, axis, width):
    return lax.shift_right_logical(lax.broadcasted_iota(jnp.int32, shape, axis), int(np.log2(width)))


def _head_mean_sq(x):
    w = x.shape[-1]
    bmat = jnp.where(_lane_group((w, w), 0, HEAD_DIM) == _lane_group((w, w), 1, HEAD_DIM),
                     1.0 / HEAD_DIM, 0.0).astype(BF16)
    return _dot_split(x * x, bmat)


def _swap_half_heads(x):
    w = x.shape[-1]
    lane = lax.broadcasted_iota(jnp.int32, x.shape, x.ndim - 1)
    first = (lane & (HEAD_DIM - 1)) < HEAD_DIM // 2
    return jnp.where(first, pltpu.roll(x, w - HEAD_DIM // 2, x.ndim - 1),
                     pltpu.roll(x, HEAD_DIM // 2, x.ndim - 1))


def _cond_row(tile, tm):
    r = tile * tm
    return jnp.where(r < N_CTX, 0, 1 + (r - N_CTX) // DEC_SEQ)


def _stack_heads(x, n, width):
    head = _lane_group(x.shape, 1, width)
    return jnp.concatenate([jnp.where(head == h, x, jnp.zeros_like(x)) for h in range(n)], axis=0)


def _unstack_heads(o, n, width):
    m = o.shape[0] // n
    head = _lane_group((m, o.shape[1]), 1, width)
    out = jnp.zeros((m, o.shape[1]), o.dtype)
    for h in range(n):
        out = jnp.where(head == h, o[h * m:(h + 1) * m], out)
    return out


def _mod_kernel(c_ref, w_ref, b_ref, o_ref):
    o_ref[0] = _dot(_silu(c_ref[...]), w_ref[0]) + b_ref[0]


def _modulation(cond, w_mod, b_mod):
    tn = 1536
    return pl.pallas_call(
        _mod_kernel,
        grid=(DEPTH, 6 * D_MODEL // tn),
        in_specs=[pl.BlockSpec((N_COND, D_MODEL), lambda l, j: (0, 0)),
                  pl.BlockSpec((1, D_MODEL, tn), lambda l, j: (l, 0, j)),
                  pl.BlockSpec((1, 1, tn), lambda l, j: (l, 0, j))],
        out_specs=pl.BlockSpec((1, N_COND, tn), lambda l, j: (l, 0, j)),
        out_shape=jax.ShapeDtypeStruct((DEPTH, N_COND, 6 * D_MODEL), F32),
        compiler_params=pltpu.CompilerParams(dimension_semantics=("arbitrary", "arbitrary"),
                                             vmem_limit_bytes=VMEM_LIMIT),
        name="modulation",
    )(cond, w_mod, b_mod.reshape(DEPTH, 1, 6 * D_MODEL))


def _inproj_kernel(x_ref, sh_ref, sc_ref, w_ref, wa_ref, ba_ref, qg_ref, kg_ref, cos_ref, sin_ref,
                   pgla_ref, la_ref, pnat_ref, q_ref, k_ref, v_ref):
    ci = _cond_row(pl.program_id(0), TM_PROJ)
    xn = _ln(x_ref[...])
    xm = xn * (1.0 + sc_ref[pl.ds(ci, 1), :]) + sh_ref[pl.ds(ci, 1), :]
    acc = _dot(xm, w_ref[...])
    pgla_ref[...] = acc[:, 0:4 * GLA_W]
    pnat_ref[...] = acc[:, 4 * GLA_W:4 * GLA_W + 3 * NAT_W].astype(BF16)
    z = _dot(acc[:, GA_COL:GA_COL + 128], wa_ref[...]) + ba_ref[...]
    la_ref[...] = (jnp.minimum(z, 0.0) - jnp.log(1.0 + jnp.exp(-jnp.abs(z)))) * (1.0 / GLA_TAU)

    c0 = 4 * GLA_W + 3 * NAT_W
    cos = cos_ref[...]
    sin = sin_ref[...]

    def norm_rope(a, g):
        an = a * lax.rsqrt(_head_mean_sq(a) + EPS) * g
        return an * cos + _swap_half_heads(an) * sin

    qs = [norm_rope(acc[:, c0 + j * 128:c0 + (j + 1) * 128], qg_ref[...]) for j in range(GQA_QW // 128)]
    q_ref[...] = (jnp.concatenate(qs, axis=1) * (HEAD_DIM ** -0.5)).astype(BF16)
    k_ref[...] = norm_rope(acc[:, c0 + GQA_QW:c0 + GQA_QW + GQA_KW], kg_ref[...])
    v_ref[...] = acc[:, c0 + GQA_QW + GQA_KW:c0 + GQA_QW + 2 * GQA_KW]


def _in_proj(x, sh, sc, w, wa, ba, qg, kg, cos_t, sin_t):
    tm = TM_PROJ
    full = lambda a: pl.BlockSpec(a.shape, lambda i: (0,) * a.ndim)
    rows = lambda w_: pl.BlockSpec((tm, w_), lambda i: (i, 0))
    return pl.pallas_call(
        _inproj_kernel,
        grid=(N_TOK // tm,),
        in_specs=[rows(D_MODEL), full(sh), full(sc), full(w), full(wa), full(ba), full(qg), full(kg),
                  rows(128), rows(128)],
        out_specs=[rows(4 * GLA_W), rows(2 * GLA_W), rows(3 * NAT_W), rows(GQA_QW), rows(GQA_KW),
                   rows(GQA_KW)],
        out_shape=[jax.ShapeDtypeStruct((N_TOK, 4 * GLA_W), F32),
                   jax.ShapeDtypeStruct((N_TOK, 2 * GLA_W), F32),
                   jax.ShapeDtypeStruct((N_TOK, 3 * NAT_W), BF16),
                   jax.ShapeDtypeStruct((N_TOK, GQA_QW), BF16),
                   jax.ShapeDtypeStruct((N_TOK, GQA_KW), F32),
                   jax.ShapeDtypeStruct((N_TOK, GQA_KW), F32)],
        compiler_params=pltpu.CompilerParams(dimension_semantics=("arbitrary",),
                                             vmem_limit_bytes=VMEM_LIMIT),
        name="in_proj",
    )(x, sh, sc, w, wa, ba, qg, kg, cos_t, sin_t)


def _gla_kernel(qf_ref, kf_ref, vf_ref, laf_ref, qb_ref, kb_ref, vb_ref, lab_ref, st0_ref,
                of_ref, ob_ref, stout_ref, st_ref):
    t = pl.program_id(1)
    last_t = pl.num_programs(1) - 1
    C = GLA_CHUNK
    nchunk = GLA_TB // C

    @pl.when(t == 0)
    def _():
        st_ref[...] = st0_ref[0]

    tb = GLA_TB
    shift = int(np.log2(C))
    r = lax.broadcasted_iota(jnp.int32, (tb, tb), 0)
    c = lax.broadcasted_iota(jnp.int32, (tb, tb), 1)
    same = lax.shift_right_logical(r, shift) == lax.shift_right_logical(c, shift)
    ra = lax.broadcasted_iota(jnp.int32, (tb, H_GLA * tb), 0)
    ca = lax.broadcasted_iota(jnp.int32, (tb, H_GLA * tb), 1) & (tb - 1)
    same_a = lax.shift_right_logical(ra, shift) == lax.shift_right_logical(ca, shift)
    blk = _lane_group((H_GLA * tb, GLA_W), 0, tb) == _lane_group((H_GLA * tb, GLA_W), 1, GLA_DK)
    diag = _lane_group((GLA_W, GLA_W), 0, GLA_DV) == _lane_group((GLA_W, GLA_W), 1, GLA_DK)
    one = lambda m: jnp.where(m, 1.0, 0.0).astype(BF16)
    chunk_ones = one(same)
    ref_pick = one(c == (r & ~(C - 1)) + C // 2)

    def direction(q_ref, k_ref, v_ref, la_ref, o_ref, d, causal):
        q = q_ref[...] * (GLA_DK ** -0.5)
        k = k_ref[...]
        v = v_ref[...]
        la = la_ref[...]
        la_hi = la.astype(BF16)
        la_lo = (la - la_hi.astype(F32)).astype(BF16)
        csum = lambda m: (jnp.dot(m, la_hi, preferred_element_type=F32)
                          + jnp.dot(m, la_lo, preferred_element_type=F32))
        b = csum(one(same & ((r >= c) if causal else (r <= c))))
        btot = csum(chunk_ones)
        bref = jnp.dot(ref_pick, b.astype(BF16), preferred_element_type=F32)
        q_att = q * jnp.exp(b - bref)
        k_att = k * jnp.exp(bref - b)
        kblk = jnp.where(blk, jnp.concatenate([k_att] * H_GLA, axis=0), 0.0)
        att = _dot_nt(q_att, kblk)
        att = jnp.where(same_a & ((ra >= ca) if causal else (ra <= ca)), att, 0.0)
        vblk = jnp.where(blk, jnp.concatenate([v] * H_GLA, axis=0), 0.0)
        o_intra = _dot(att, vblk)

        q_dec = (q * jnp.exp(b)).astype(BF16)
        k_dec = (k * jnp.exp(btot - b)).astype(BF16)
        g = jnp.exp(btot)
        vb = v.astype(BF16)
        st = st_ref[d]
        o_inter = [None] * nchunk
        for ci in (range(nchunk) if causal else reversed(range(nchunk))):
            rows = slice(ci * C, (ci + 1) * C)
            o_inter[ci] = _dot_nt(q_dec[rows], st)
            u = _dot_tn(vb[rows], k_dec[rows])
            st = st * g[ci * C:ci * C + 1, :] + jnp.where(diag, u, 0.0)
        st_ref[d] = st
        o_ref[...] = o_intra + jnp.concatenate(o_inter, axis=0)

    direction(qf_ref, kf_ref, vf_ref, laf_ref, of_ref, 0, True)
    direction(qb_ref, kb_ref, vb_ref, lab_ref, ob_ref, 1, False)

    @pl.when(t == last_t)
    def _():
        stout_ref[0] = st_ref[...]


def _gla(pgla, la, st0, row0, n_seq, seq_len):
    tb = GLA_TB
    nt = seq_len // tb
    b0 = row0 // tb
    fwd = lambda col: pl.BlockSpec((tb, GLA_W), lambda s, t: (b0 + s * nt + t, col))
    bwd = lambda col: pl.BlockSpec((tb, GLA_W), lambda s, t: (b0 + s * nt + nt - 1 - t, col))
    st_spec = pl.BlockSpec((1, 2, GLA_W, GLA_W), lambda s, t: (s, 0, 0, 0))
    n = n_seq * seq_len
    return pl.pallas_call(
        _gla_kernel,
        grid=(n_seq, nt),
        in_specs=[fwd(0), fwd(1), fwd(2), fwd(0), bwd(0), bwd(1), bwd(2), bwd(1), st_spec],
        out_specs=[pl.BlockSpec((tb, GLA_W), lambda s, t: (s * nt + t, 0)),
                   pl.BlockSpec((tb, GLA_W), lambda s, t: (s * nt + nt - 1 - t, 0)),
                   st_spec],
        out_shape=[jax.ShapeDtypeStruct((n, GLA_W), F32), jax.ShapeDtypeStruct((n, GLA_W), F32),
                   jax.ShapeDtypeStruct((n_seq, 2, GLA_W, GLA_W), F32)],
        scratch_shapes=[pltpu.VMEM((2, GLA_W, GLA_W), F32)],
        compiler_params=pltpu.CompilerParams(dimension_semantics=("arbitrary", "arbitrary"),
                                             vmem_limit_bytes=VMEM_LIMIT),
        name="gla_scan",
    )(pgla, pgla, pgla, la, pgla, pgla, pgla, la, st0)


def _softmax_pv(s_list, v_list):
    m = s_list[0].max(axis=-1, keepdims=True)
    for s in s_list[1:]:
        m = jnp.maximum(m, s.max(axis=-1, keepdims=True))
    acc = None
    l = None
    for s, v in zip(s_list, v_list):
        p = jnp.exp(s - m)
        pl_ = p.sum(axis=-1, keepdims=True)
        pv = _dot(p, v)
        acc = pv if acc is None else acc + pv
        l = pl_ if l is None else l + pl_
    return acc * (1.0 / l)


def _ctx_attn_kernel(pnat_ref, q_ref, k_ref, v_ref, onat_ref, ogqa_ref):
    nq = pnat_ref[:, 0:NAT_W] * (HEAD_DIM ** -0.5)
    nk = pnat_ref[:, NAT_W:2 * NAT_W].astype(BF16)
    nv = pnat_ref[:, 2 * NAT_W:3 * NAT_W].astype(BF16)
    o = _softmax_pv([_dot_nt(_stack_heads(nq, H_NAT, HEAD_DIM), nk)], [nv])
    onat_ref[...] = _unstack_heads(o, H_NAT, HEAD_DIM).astype(BF16)

    k = k_ref[...].astype(BF16)
    v = v_ref[...].astype(BF16)
    k2 = jnp.concatenate([k, k], axis=1)
    v2 = jnp.concatenate([v, v], axis=1)
    outs = []
    for half in range(2):
        q = q_ref[:, half * 256:(half + 1) * 256]
        o = _softmax_pv([_dot_nt(_stack_heads(q, 4, HEAD_DIM), k2)], [v2])
        outs.append(_unstack_heads(o, 4, HEAD_DIM))
    ogqa_ref[...] = jnp.concatenate(outs, axis=1).astype(BF16)


def _ctx_attention(pnat, q, k, v):
    rows = lambda w_: pl.BlockSpec((SEQ, w_), lambda i: (i, 0))
    return pl.pallas_call(
        _ctx_attn_kernel,
        grid=(BATCH,),
        in_specs=[rows(3 * NAT_W), rows(GQA_QW), rows(GQA_KW), rows(GQA_KW)],
        out_specs=[rows(NAT_W), rows(GQA_QW)],
        out_shape=[jax.ShapeDtypeStruct((N_CTX, NAT_W), BF16),
                   jax.ShapeDtypeStruct((N_CTX, GQA_QW), BF16)],
        compiler_params=pltpu.CompilerParams(dimension_semantics=("arbitrary",),
                                             vmem_limit_bytes=VMEM_LIMIT),
        name="ctx_attention",
    )(pnat, q, k, v)


def _gqa_lat_kernel(q_ref, k_ref, v_ref, o_ref):
    k = k_ref[0]
    v = v_ref[0]
    k2 = jnp.concatenate([k, k], axis=1)
    v2 = jnp.concatenate([v, v], axis=1)
    slot = _lane_group((TQ_GQA, 256), 1, HEAD_DIM)
    outs = []
    for half in range(2):
        q = q_ref[:, half * 256:(half + 1) * 256]
        out = jnp.zeros((TQ_GQA, 256), F32)
        for s in range(4):
            qs = jnp.where(slot == s, q, jnp.zeros_like(q))
            o = _softmax_pv([_dot_nt(qs, k2)], [v2])
            out = jnp.where(slot == s, o, out)
        outs.append(out)
    o_ref[...] = jnp.concatenate(outs, axis=1).astype(BF16)


def _gqa_latent(q, k_all, v_all):
    tq = TQ_GQA
    nq = DEC_SEQ // tq
    tk = k_all.shape[1]
    return pl.pallas_call(
        _gqa_lat_kernel,
        grid=(DEC_BATCH, nq),
        in_specs=[pl.BlockSpec((tq, GQA_QW), lambda b, i: (N_CTX // tq + b * nq + i, 0)),
                  pl.BlockSpec((1, tk, GQA_KW), lambda b, i: (b, 0, 0)),
                  pl.BlockSpec((1, tk, GQA_KW), lambda b, i: (b, 0, 0))],
        out_specs=pl.BlockSpec((tq, GQA_QW), lambda b, i: (b * nq + i, 0)),
        out_shape=jax.ShapeDtypeStruct((N_LAT, GQA_QW), BF16),
        compiler_params=pltpu.CompilerParams(dimension_semantics=("arbitrary", "arbitrary"),
                                             vmem_limit_bytes=VMEM_LIMIT),
        name="gqa_latent",
    )(q, k_all, v_all)


def _nat_lat_kernel(q_ref, k_ref, v_ref, kc_ref, vc_ref, bias_ref, o_ref):
    j = pl.program_id(1)
    kc = kc_ref[0].astype(BF16)
    vc = vc_ref[0].astype(BF16)
    for i in range(NAT_ROWS_PER_STEP):
        r = j * NAT_ROWS_PER_STEP + i
        r0 = jnp.clip(r - NAT_KH // 2, 0, ROWS - NAT_KH)
        win = pl.ds(pl.multiple_of(r0 * GRID_W, GRID_W), NAT_KH * GRID_W)
        q = q_ref[i * GRID_W:(i + 1) * GRID_W, :] * (HEAD_DIM ** -0.5)
        qs = _stack_heads(q, H_NAT, HEAD_DIM)
        s_loc = _dot_nt(qs, k_ref[win, :]) + bias_ref[r - r0]
        s_ctx = _dot_nt(qs, kc)
        o = _softmax_pv([s_loc, s_ctx], [v_ref[win, :], vc])
        o_ref[i * GRID_W:(i + 1) * GRID_W, :] = _unstack_heads(o, H_NAT, HEAD_DIM).astype(BF16)


def _nat_latent(pnat, kc, vc, bias):
    tq = NAT_ROWS_PER_STEP * GRID_W
    nq = DEC_SEQ // tq
    lat_blk = N_CTX // DEC_SEQ
    return pl.pallas_call(
        _nat_lat_kernel,
        grid=(DEC_BATCH, nq),
        in_specs=[pl.BlockSpec((tq, NAT_W), lambda b, j: (N_CTX // tq + b * nq + j, 0)),
                  pl.BlockSpec((DEC_SEQ, NAT_W), lambda b, j: (lat_blk + b, 1)),
                  pl.BlockSpec((DEC_SEQ, NAT_W), lambda b, j: (lat_blk + b, 2)),
                  pl.BlockSpec((1, PAST_LEN, NAT_W), lambda b, j: (b, 0, 0)),
                  pl.BlockSpec((1, PAST_LEN, NAT_W), lambda b, j: (b, 0, 0)),
                  pl.BlockSpec(bias.shape, lambda b, j: (0, 0, 0))],
        out_specs=pl.BlockSpec((tq, NAT_W), lambda b, j: (b * nq + j, 0)),
        out_shape=jax.ShapeDtypeStruct((N_LAT, NAT_W), BF16),
        compiler_params=pltpu.CompilerParams(dimension_semantics=("arbitrary", "arbitrary"),
                                             vmem_limit_bytes=VMEM_LIMIT),
        name="nat_latent",
    )(pnat, pnat, pnat, kc, vc, bias)


def _route_gates(logits_t, bias_ref):
    per = N_EXPERTS // N_GROUPS
    t = logits_t.shape[1]
    neg = -jnp.inf
    pos = lax.broadcasted_iota(jnp.int32, (per, t), 0)
    scores = [_sigmoid(logits_t[g * per:(g + 1) * per, :]) for g in range(N_GROUPS)]
    biased = [scores[g] + bias_ref[g * per:(g + 1) * per, :] for g in range(N_GROUPS)]

    grp = []
    for v in biased:
        m1 = jnp.max(v, axis=0, keepdims=True)
        i1 = jnp.min(jnp.where(v == m1, pos, per), axis=0, keepdims=True)
        m2 = jnp.max(jnp.where(pos == i1, neg, v), axis=0, keepdims=True)
        grp.append(m1 + m2)

    keep = [jnp.zeros((1, t), jnp.bool_) for _ in range(N_GROUPS)]
    for _ in range(TOPK_GROUPS):
        best = functools.reduce(jnp.maximum, grp)
        first = jnp.full((1, t), N_GROUPS, jnp.int32)
        for g in reversed(range(N_GROUPS)):
            first = jnp.where(grp[g] == best, g, first)
        for g in range(N_GROUPS):
            hit = first == g
            keep[g] = keep[g] | hit
            grp[g] = jnp.where(hit, neg, grp[g])

    cand = [jnp.where(keep[g], biased[g], neg) for g in range(N_GROUPS)]
    flat = [pos + g * per for g in range(N_GROUPS)]
    picks, weights = [], []
    for _ in range(TOP_K):
        best = functools.reduce(jnp.maximum, [jnp.max(v, axis=0, keepdims=True) for v in cand])
        first = functools.reduce(jnp.minimum, [
            jnp.min(jnp.where(cand[g] == best, flat[g], N_EXPERTS), axis=0, keepdims=True)
            for g in range(N_GROUPS)])
        score = jnp.zeros((1, t), F32)
        for g in range(N_GROUPS):
            hit = flat[g] == first
            score = score + jnp.sum(jnp.where(hit, scores[g], 0.0), axis=0, keepdims=True)
            cand[g] = jnp.where(hit, neg, cand[g])
        picks.append(first)
        weights.append(score)

    total = functools.reduce(lambda a, b: a + b, weights)
    gates = jnp.concatenate([v / total * ROUTE_SCALE for v in weights], axis=0)

    sel = [functools.reduce(lambda a, b: a | b, [flat[g] == k for k in picks]) for g in range(N_GROUPS)]
    sel = jnp.where(jnp.concatenate(sel, axis=0), 1.0, 0.0).astype(BF16)
    before = (lax.broadcasted_iota(jnp.int32, (t, t), 0) < lax.broadcasted_iota(jnp.int32, (t, t), 1))
    rank = jnp.dot(sel, jnp.where(before, 1.0, 0.0).astype(BF16), preferred_element_type=F32)
    count = jnp.sum(sel.astype(F32), axis=1, keepdims=True)
    pieces = jnp.floor((count + (PIECE - 1)) * (1.0 / PIECE))
    lower = (lax.broadcasted_iota(jnp.int32, (N_EXPERTS, N_EXPERTS), 1)
             < lax.broadcasted_iota(jnp.int32, (N_EXPERTS, N_EXPERTS), 0))
    start = PIECE * jnp.dot(jnp.where(lower, 1.0, 0.0).astype(BF16),
                            jnp.broadcast_to(pieces, (N_EXPERTS, 128)).astype(BF16),
                            preferred_element_type=F32)[:, 0:1]
    row = rank + start
    rows = []
    for k in picks:
        r = jnp.zeros((1, t), F32)
        for g in range(N_GROUPS):
            r = r + jnp.sum(jnp.where(flat[g] == k, row[g * per:(g + 1) * per, :], 0.0), axis=0, keepdims=True)
        rows.append(r)
    return (jnp.concatenate(rows, axis=0).astype(jnp.int32), gates,
            jnp.broadcast_to(pieces, (N_EXPERTS, 128)).astype(jnp.int32))


def _outproj_kernel(x_ref, ofc_ref, ofl_ref, obc_ref, obl_ref, gg_ref, onatc_ref, onatl_ref, ogqac_ref, ogqal_ref,
                    w_ref, ng_ref, g1_ref, lng_ref, lnb_ref, sh2_ref, sc2_ref, wr_ref, rb_ref, x1_ref, xm_ref,
                    row_ref, gate_ref, pieces_ref):
    ci = _cond_row(pl.program_id(0), TM_PROJ)
    is_ctx = pl.program_id(0) < N_CTX // TM_PROJ
    either = lambda c_ref, l_ref: jnp.where(is_ctx, c_ref[...], l_ref[...])
    og = either(ofc_ref, ofl_ref) + either(obc_ref, obl_ref)
    halves = [og[:, j * 128:(j + 1) * 128] for j in range(GLA_W // 128)]
    ms = jnp.concatenate([_head_mean_sq(h) for h in halves], axis=1)
    ogla = og * lax.rsqrt(ms + EPS) * ng_ref[...] * _silu(gg_ref[...])
    mix = (_dot(ogla, w_ref[0:GLA_W, :]) + _dot(either(onatc_ref, onatl_ref), w_ref[GLA_W:GLA_W + NAT_W, :])
           + _dot(either(ogqac_ref, ogqal_ref), w_ref[GLA_W + NAT_W:, :]))
    x1 = _ln(ALPHA * x_ref[...] + g1_ref[pl.ds(ci, 1), :] * mix) * lng_ref[...] + lnb_ref[...]
    x1_ref[...] = x1
    xm = (_ln(x1) * (1.0 + sc2_ref[pl.ds(ci, 1), :]) + sh2_ref[pl.ds(ci, 1), :]).astype(BF16)
    xm_ref[...] = xm
    row_ref[...], gate_ref[...], pieces_ref[...] = _route_gates(_dot_nt(wr_ref[...], xm), rb_ref)


def _out_proj(x, of, ob, pgla, onat, ogqa, w, ng, g1, lng, lnb, sh2, sc2, wr, rb):
    tm = TM_PROJ
    n_ctx = N_CTX // tm
    full = lambda a: pl.BlockSpec(a.shape, lambda i: (0,) * a.ndim)
    rows = lambda w_: pl.BlockSpec((tm, w_), lambda i: (i, 0))
    ctx = lambda w_: pl.BlockSpec((tm, w_), lambda i: (jnp.minimum(i, n_ctx - 1), 0))
    lat = lambda w_: pl.BlockSpec((tm, w_), lambda i: (jnp.maximum(i - n_ctx, 0), 0))
    return pl.pallas_call(
        _outproj_kernel,
        grid=(N_TOK // tm,),
        in_specs=[rows(D_MODEL), ctx(GLA_W), lat(GLA_W), ctx(GLA_W), lat(GLA_W),
                  pl.BlockSpec((tm, GLA_W), lambda i: (i, 3)),
                  ctx(NAT_W), lat(NAT_W), ctx(GQA_QW), lat(GQA_QW), full(w), full(ng), full(g1), full(lng),
                  full(lnb), full(sh2), full(sc2), full(wr), full(rb)],
        out_specs=[rows(D_MODEL), rows(D_MODEL), pl.BlockSpec((TOP_K, tm), lambda i: (0, i)),
                   pl.BlockSpec((TOP_K, tm), lambda i: (0, i)),
                   pl.BlockSpec((N_EXPERTS, 128), lambda i: (0, i))],
        out_shape=[jax.ShapeDtypeStruct((N_TOK, D_MODEL), F32),
                   jax.ShapeDtypeStruct((N_TOK, D_MODEL), BF16),
                   jax.ShapeDtypeStruct((TOP_K, N_TOK), jnp.int32),
                   jax.ShapeDtypeStruct((TOP_K, N_TOK), F32),
                   jax.ShapeDtypeStruct((N_EXPERTS, MOE_NT * 128), jnp.int32)],
        compiler_params=pltpu.CompilerParams(dimension_semantics=("arbitrary",),
                                             vmem_limit_bytes=VMEM_LIMIT),
        name="out_proj",
    )(x, *of, *ob, pgla, *onat, *ogqa, w, ng, g1, lng, lnb, sh2, sc2, wr, rb)


def _plan_specs(n_prefetch, **kw):
    return pltpu.PrefetchScalarGridSpec(num_scalar_prefetch=n_prefetch, **kw)


def _for_each_copy(t, copies, fn):
    bsrc_ref, bdst_ref, ssrc_ref, sdst_ref, nbig_ref, nsmall_ref = copies

    def big(i, c):
        j = t * MAX_BIG + i
        fn(pl.multiple_of(bsrc_ref[j], PIECE), pl.multiple_of(bdst_ref[j], PIECE), COPY_ROWS)
        return c

    def small(i, c):
        j = t * N_EXPERTS + i
        fn(pl.multiple_of(ssrc_ref[j], PIECE), pl.multiple_of(sdst_ref[j], PIECE), PIECE)
        return c

    lax.fori_loop(0, nbig_ref[t], big, 0)
    lax.fori_loop(0, nsmall_ref[t], small, 0)


def _wait_copies(n_big, n_small, make):
    for n, rows in ((n_big, COPY_ROWS), (n_small, PIECE)):
        def one(i, c, rows=rows):
            make(rows).wait()
            return c
        lax.fori_loop(0, n, one, 0)


def _dispatch_kernel(bsrc_ref, bdst_ref, ssrc_ref, sdst_ref, nbig_ref, nsmall_ref, zdst_ref, tot_ref, xm_ref, ldt_ref,
                     xs_ref, stage_ref, zero_ref, sem_big, sem_small, tail_sem):
    copies = (bsrc_ref, bdst_ref, ssrc_ref, sdst_ref, nbig_ref, nsmall_ref)
    t = pl.program_id(0)
    nt = pl.num_programs(0)
    slot = lax.rem(t, 2)

    def run_copy(local_row, xs_row, rows, s):
        sem = sem_big if rows == COPY_ROWS else sem_small
        return pltpu.make_async_copy(stage_ref.at[s, pl.ds(local_row, rows)], xs_ref.at[pl.ds(xs_row, rows)],
                                     sem.at[s])

    def wait_tile(tt, s):
        _wait_copies(nbig_ref[tt], nsmall_ref[tt], lambda rows: run_copy(0, 0, rows, s))

    @pl.when(t >= 2)
    def _():
        wait_tile(t - 2, slot)

    xm = xm_ref[...]
    ld = ldt_ref[...].astype(jnp.int16)
    for c in range(MOE_CAP // MOE_CHUNK):
        p = (lax.broadcasted_iota(jnp.int32, (MOE_CHUNK, MOE_TILE), 0) + c * MOE_CHUNK).astype(jnp.int16)
        hit = p == ld[0:1, :]
        for k in range(1, TOP_K):
            hit = hit | (p == ld[k:k + 1, :])
        stage_ref[slot, c * MOE_CHUNK:(c + 1) * MOE_CHUNK, :] = jnp.dot(
            jnp.where(hit, jnp.ones((), BF16), jnp.zeros((), BF16)), xm, preferred_element_type=F32).astype(BF16)
    _for_each_copy(t, copies, lambda a, b, rows: run_copy(a, b, rows, slot).start())

    @pl.when(t == nt - 1)
    def _():
        zero_ref[...] = jnp.zeros_like(zero_ref)
        n_piece = tot_ref[1]
        first_free_tile = tot_ref[0] // ROW_TILE
        n_tile = XS_ROWS // ROW_TILE - first_free_tile

        def zero_piece(i):
            return pltpu.make_async_copy(zero_ref.at[pl.ds(0, PIECE)],
                                         xs_ref.at[pl.ds(pl.multiple_of(zdst_ref[i], PIECE), PIECE)],
                                         tail_sem.at[0])

        def zero_tile(i):
            row = pl.multiple_of((first_free_tile + i) * ROW_TILE, ROW_TILE)
            return pltpu.make_async_copy(zero_ref, xs_ref.at[pl.ds(row, ROW_TILE)], tail_sem.at[1])

        def run(n, make, op):
            def body(i, c):
                op(make(i))
                return c
            lax.fori_loop(0, n, body, 0)

        run(n_piece, zero_piece, lambda cp: cp.start())
        run(n_tile, zero_tile, lambda cp: cp.start())
        run(n_piece, zero_piece, lambda cp: cp.wait())
        run(n_tile, zero_tile, lambda cp: cp.wait())
        wait_tile(t - 1, 1 - slot)
        wait_tile(t, slot)


def _moe_dispatch(plan, xm, row_t):
    return pl.pallas_call(
        _dispatch_kernel,
        grid_spec=_plan_specs(
            8, grid=(MOE_NT,),
            in_specs=[pl.BlockSpec((MOE_TILE, D_MODEL), lambda t, *_: (t, 0)),
                      pl.BlockSpec((TOP_K, MOE_TILE), lambda t, *_: (0, t))],
            out_specs=pl.BlockSpec(memory_space=pl.ANY),
            scratch_shapes=[pltpu.VMEM((2, MOE_CAP, D_MODEL), BF16), pltpu.VMEM((ROW_TILE, D_MODEL), BF16),
                            pltpu.SemaphoreType.DMA((2,)), pltpu.SemaphoreType.DMA((2,)),
                            pltpu.SemaphoreType.DMA((2,))]),
        out_shape=jax.ShapeDtypeStruct((XS_ROWS, D_MODEL), BF16),
        compiler_params=pltpu.CompilerParams(dimension_semantics=("arbitrary",),
                                             vmem_limit_bytes=VMEM_LIMIT),
        name="moe_dispatch",
    )(*plan["copies"], plan["zero_dst"], plan["total"], xm, row_t)


def _experts_kernel(exp_ref, fresh_ref, used_ref, src_ref, xs_ref, wgu_ref, wd_ref, y_ref, wgu_bf_ref, wd_bf_ref):
    w = pl.program_id(0)

    @pl.when(fresh_ref[w] == 1)
    def _():
        wgu_bf_ref[...] = wgu_ref[0].astype(BF16)
        wd_bf_ref[...] = wd_ref[0].astype(BF16)

    @pl.when(used_ref[w] == 1)
    def _():
        ab = jnp.dot(xs_ref[...], wgu_bf_ref[...], preferred_element_type=F32)
        h = (_silu(ab[:, :D_EXPERT]) * ab[:, D_EXPERT:]).astype(BF16)
        y_ref[...] = jnp.dot(h, wd_bf_ref[...], preferred_element_type=F32).astype(BF16)

    @pl.when(used_ref[w] == 0)
    def _():
        y_ref[...] = jnp.zeros_like(y_ref)


def _moe_experts(layer, plan, xs, wgu, wd):
    return pl.pallas_call(
        _experts_kernel,
        grid_spec=_plan_specs(
            4, grid=(XS_ROWS // ROW_TILE,),
            in_specs=[pl.BlockSpec((ROW_TILE, D_MODEL), lambda w, ex, fresh, used, src: (src[w], 0)),
                      pl.BlockSpec((None, 1, D_MODEL, 2 * D_EXPERT), lambda w, ex, *_: (layer, ex[w], 0, 0)),
                      pl.BlockSpec((None, 1, D_EXPERT, D_MODEL), lambda w, ex, *_: (layer, ex[w], 0, 0))],
            out_specs=pl.BlockSpec((ROW_TILE, D_MODEL), lambda w, *_: (w, 0)),
            scratch_shapes=[pltpu.VMEM((D_MODEL, 2 * D_EXPERT), BF16), pltpu.VMEM((D_EXPERT, D_MODEL), BF16)]),
        out_shape=jax.ShapeDtypeStruct((XS_ROWS, D_MODEL), BF16),
        compiler_params=pltpu.CompilerParams(dimension_semantics=("arbitrary",),
                                             vmem_limit_bytes=VMEM_LIMIT),
        name="moe_experts",
    )(plan["tile_expert"], plan["tile_fresh"], plan["tile_used"], plan["tile_src"], xs, wgu, wd)


def _combine_kernel(bsrc_ref, bdst_ref, ssrc_ref, sdst_ref, nbig_ref, nsmall_ref, y_ref, ldt_ref, gate_ref, xm_ref,
                    x1_ref, g2_ref, wsgu_ref, wsd_ref, lng_ref, lnb_ref, out_ref, ybuf_ref, sem_big, sem_small):
    copies = (bsrc_ref, bdst_ref, ssrc_ref, sdst_ref, nbig_ref, nsmall_ref)
    t = pl.program_id(0)
    nt = pl.num_programs(0)
    slot = lax.rem(t, 2)
    ci = _cond_row(t, MOE_TILE)

    def run_copy(local_row, y_row, rows, s):
        sem = sem_big if rows == COPY_ROWS else sem_small
        return pltpu.make_async_copy(y_ref.at[pl.ds(y_row, rows)], ybuf_ref.at[s, pl.ds(local_row, rows)], sem.at[s])

    def fetch(tt, s):
        _for_each_copy(tt, copies, lambda a, b, rows: run_copy(a, b, rows, s).start())

    @pl.when(t == 0)
    def _():
        ybuf_ref[...] = jnp.zeros_like(ybuf_ref)
        fetch(0, 0)

    @pl.when(t + 1 < nt)
    def _():
        fetch(t + 1, 1 - slot)

    _wait_copies(nbig_ref[t], nsmall_ref[t], lambda rows: run_copy(0, 0, rows, slot))

    routed = jnp.zeros((MOE_TILE, D_MODEL), F32)
    ld = ldt_ref[...].astype(jnp.int16)
    gate = gate_ref[...].astype(BF16)
    for c in range(MOE_CAP // MOE_CHUNK):
        p = (lax.broadcasted_iota(jnp.int32, (MOE_CHUNK, MOE_TILE), 0) + c * MOE_CHUNK).astype(jnp.int16)
        wmat = jnp.zeros((MOE_CHUNK, MOE_TILE), BF16)
        for k in range(TOP_K):
            wmat = jnp.where(p == ld[k:k + 1, :], gate[k:k + 1, :], wmat)
        routed = routed + _dot_tn(wmat, ybuf_ref[slot, c * MOE_CHUNK:(c + 1) * MOE_CHUNK, :])

    sab = jnp.dot(xm_ref[...], wsgu_ref[...], preferred_element_type=F32)
    shared = _dot(_silu(sab[:, :D_SHARED]) * sab[:, D_SHARED:], wsd_ref[...])
    z = ALPHA * x1_ref[...] + g2_ref[pl.ds(ci, 1), :] * (routed + shared)
    out_ref[...] = _ln(z) * lng_ref[...] + lnb_ref[...]


def _moe_combine(plan, y, row_t, gate_t, xm, x1, g2, wsgu, wsd, lng, lnb):
    full = lambda a: pl.BlockSpec(a.shape, lambda t, *_: (0,) * a.ndim)
    rows = lambda w_: pl.BlockSpec((MOE_TILE, w_), lambda t, *_: (t, 0))
    picks = pl.BlockSpec((TOP_K, MOE_TILE), lambda t, *_: (0, t))
    return pl.pallas_call(
        _combine_kernel,
        grid_spec=_plan_specs(
            6, grid=(MOE_NT,),
            in_specs=[pl.BlockSpec(memory_space=pl.ANY), picks, picks, rows(D_MODEL), rows(D_MODEL),
                      full(g2), full(wsgu), full(wsd), full(lng), full(lnb)],
            out_specs=rows(D_MODEL),
            scratch_shapes=[pltpu.VMEM((2, MOE_CAP, D_MODEL), BF16), pltpu.SemaphoreType.DMA((2,)),
                            pltpu.SemaphoreType.DMA((2,))]),
        out_shape=jax.ShapeDtypeStruct((N_TOK, D_MODEL), F32),
        compiler_params=pltpu.CompilerParams(dimension_semantics=("arbitrary",),
                                             vmem_limit_bytes=VMEM_LIMIT),
        name="moe_combine",
    )(*plan["copies"], y, row_t, gate_t, xm, x1, g2, wsgu, wsd, lng, lnb)


def _moe_plan(npc):
    i32 = jnp.int32
    run = npc * PIECE
    lo = jnp.cumsum(run, axis=1) - run
    tot_e = jnp.sum(run, axis=0)
    region = (tot_e + ROW_TILE - 1) // ROW_TILE * ROW_TILE
    region_end = jnp.cumsum(region)
    off = region_end - region
    hs = off[None, :] + jnp.cumsum(run, axis=0) - run

    per_copy = COPY_ROWS // PIECE
    n_big = npc // per_copy
    odd = npc - n_big * per_copy

    def expand(count, width):
        end = jnp.cumsum(count, axis=1)
        i = jnp.arange(width, dtype=i32)[None, :, None]
        mine = (i >= (end - count)[:, None, :]) & (i < end[:, None, :])
        pick = lambda a: jnp.sum(jnp.where(mine, a[:, None, :], 0), axis=2)
        return pick, i[:, :, 0] - pick(end - count)

    pick_b, k_b = expand(n_big, MAX_BIG)
    pick_s, _ = expand(odd, N_EXPERTS)
    copies = [pick_b(lo) + k_b * COPY_ROWS, pick_b(hs) + k_b * COPY_ROWS,
              pick_s(lo + n_big * COPY_ROWS), pick_s(hs + n_big * COPY_ROWS)]
    copies = [c.reshape(-1).astype(i32) for c in copies] + [jnp.sum(n_big, axis=1).astype(i32),
                                                            jnp.sum(odd, axis=1).astype(i32)]

    n_zero = ((region - tot_e) // PIECE)[None, :]
    pick_z, k_z = expand(n_zero, MAX_ZERO)
    zero_dst = (pick_z((off + tot_e)[None, :]) + k_z * PIECE).reshape(-1).astype(i32)

    start = jnp.arange(XS_ROWS // ROW_TILE, dtype=i32) * ROW_TILE
    ex = jnp.minimum(jnp.sum((start[:, None] >= region_end[None, :]).astype(i32), axis=1), N_EXPERTS - 1)
    prev_ex = jnp.concatenate([jnp.full((1,), -1, i32), ex[:-1]])
    return dict(copies=copies, zero_dst=zero_dst,
                total=jnp.stack([region_end[-1], jnp.sum(n_zero)]).astype(i32),
                tile_expert=ex.astype(i32), tile_fresh=(ex != prev_ex).astype(i32),
                tile_used=(start < region_end[-1]).astype(i32),
                tile_src=(jnp.minimum(start, region_end[-1] - ROW_TILE) // ROW_TILE).astype(i32))


def _rope_tables():
    t = jnp.arange(DEC_SEQ)
    n_freq = HEAD_DIM // 4
    freqs = ROPE_THETA ** (-jnp.arange(n_freq, dtype=F32) / n_freq)
    ang = jnp.concatenate([(t // GRID_W).astype(F32)[:, None] * freqs,
                           (t % GRID_W).astype(F32)[:, None] * freqs], -1)
    cos, sin = jnp.cos(ang), jnp.sin(ang)
    cos_h = jnp.concatenate([cos, cos], -1)
    sin_h = jnp.concatenate([-sin, sin], -1)
    lat = lambda a: jnp.tile(a, (DEC_BATCH, 128 // HEAD_DIM))
    cos_t = jnp.concatenate([jnp.ones((N_CTX, 128), F32), lat(cos_h)], 0)
    sin_t = jnp.concatenate([jnp.zeros((N_CTX, 128), F32), lat(sin_h)], 0)
    return cos_t, sin_t


def _nat_bias_table(rpb):
    n_dr, n_dc = 2 * NAT_KH - 1, 2 * NAT_KW - 1
    cidx = np.arange(GRID_W)
    dc_idx = np.clip(cidx[None, :] - cidx[:, None] + NAT_KW - 1, 0, n_dc - 1)
    col_start = np.clip(cidx - NAT_KW // 2, 0, GRID_W - NAT_KW)
    col_in = (cidx[None, :] >= col_start[:, None]) & (cidx[None, :] < col_start[:, None] + NAT_KW)
    onehot = (dc_idx.reshape(1, -1) == np.arange(n_dc)[:, None]).astype(np.float32)
    t = jnp.dot(rpb.reshape(H_NAT * n_dr, n_dc), onehot, precision=lax.Precision.HIGHEST)
    t = jnp.where(col_in[None, None], t.reshape(H_NAT, n_dr, GRID_W, GRID_W), -jnp.inf)
    bias = jnp.stack([t[:, NAT_KH - 1 - p:2 * NAT_KH - 1 - p] for p in range(NAT_KH)], axis=0)
    return bias.transpose(0, 1, 3, 2, 4).reshape(NAT_KH, H_NAT * GRID_W, NAT_KH * GRID_W)


def _prep_w_in(w_in):
    cuts = np.cumsum((0,) + IN_SPLITS)
    seg = [w_in[:, cuts[i]:cuts[i + 1]] for i in range(len(IN_SPLITS))]
    gq, gk, gv, gg, ga, nq, nk, nv, aq, ak, av = seg
    aq = aq.reshape(D_MODEL, H_GQA, HEAD_DIM)[:, np.array(GQA_SLOT_HEADS), :].reshape(D_MODEL, GQA_QW)
    ga = jnp.pad(ga, ((0, 0), (0, 128 - 2 * GLA_LOWRANK)))
    return jnp.concatenate([gq, gk, gv, gg, nq, nk, nv, aq, ak, av, ga], axis=1).astype(BF16)


def _prep_w_a(w_a2, b_a):
    wa = jnp.zeros((128, 2 * GLA_W), F32)
    wa = wa.at[0:GLA_LOWRANK, 0:GLA_W].set(w_a2[0])
    wa = wa.at[GLA_LOWRANK:2 * GLA_LOWRANK, GLA_W:].set(w_a2[1])
    return wa.astype(BF16), b_a.reshape(1, 2 * GLA_W)


def _prep_w_out(w_out):
    gqa = w_out[GLA_W + NAT_W:].reshape(H_GQA, HEAD_DIM, D_MODEL)[np.array(GQA_SLOT_HEADS)]
    return jnp.concatenate([w_out[:GLA_W + NAT_W], gqa.reshape(GQA_QW, D_MODEL)], 0).astype(BF16)


def _state_to_blockdiag(s):
    out = jnp.zeros(s.shape[:-3] + (GLA_W, GLA_W), F32)
    for h in range(H_GLA):
        out = out.at[..., h * GLA_DV:(h + 1) * GLA_DV, h * GLA_DK:(h + 1) * GLA_DK].set(
            jnp.swapaxes(s[..., h, :, :], -1, -2))
    return out


def _blockdiag_to_state(st):
    blocks = [st[..., h * GLA_DV:(h + 1) * GLA_DV, h * GLA_DK:(h + 1) * GLA_DK] for h in range(H_GLA)]
    return jnp.swapaxes(jnp.stack(blocks, axis=-3), -1, -2)


def kernel(x_prompt, x_sample, state_gla, cache_nat_k, cache_nat_v, cache_gqa_k, cache_gqa_v, c, c_ctx, w_mod, b_mod, w_in, gla_w_a2, gla_b_a, gla_norm_g, nat_rpb, gqa_q_norm_g, gqa_k_norm_g, w_out, ln1_g, ln1_b, w_router, router_bias, w_expert_gu, w_expert_down, w_shared_gu, w_shared_down, ln2_g, ln2_b):
    x = jnp.concatenate([x_prompt.reshape(N_CTX, D_MODEL), x_sample.reshape(N_LAT, D_MODEL)], axis=0)
    cond = jnp.concatenate([c_ctx[None, :], c, jnp.zeros((N_COND - 1 - DEC_BATCH, D_MODEL), F32)], axis=0)
    mods = _modulation(cond, w_mod, b_mod)
    cos_t, sin_t = _rope_tables()
    row = lambda a: a.reshape(1, -1)

    st_gla, st_nk, st_nv, st_gk, st_gv = [], [], [], [], []
    for l in range(DEPTH):
        sh1, sc1, g1, sh2, sc2, g2 = [mods[l, :, j * D_MODEL:(j + 1) * D_MODEL] for j in range(6)]
        wa, ba = _prep_w_a(gla_w_a2[l], gla_b_a[l])
        qg = row(jnp.tile(gqa_q_norm_g[l], 128 // HEAD_DIM))
        kg = row(jnp.tile(gqa_k_norm_g[l], 128 // HEAD_DIM))
        pgla, la, pnat, q, k, v = _in_proj(x, sh1, sc1, _prep_w_in(w_in[l]), wa, ba, qg, kg, cos_t, sin_t)

        zero_st = jnp.zeros((BATCH, 2, GLA_W, GLA_W), F32)
        of_c, ob_c, st_c = _gla(pgla, la, zero_st, 0, BATCH, SEQ)
        of_l, ob_l, _ = _gla(pgla, la, _state_to_blockdiag(state_gla[:, l]), N_CTX, DEC_BATCH, DEC_SEQ)

        onat_c, ogqa_c = _ctx_attention(pnat, q, k, v)
        k_all = jnp.concatenate([k[N_CTX:].reshape(DEC_BATCH, DEC_SEQ, GQA_KW),
                                 cache_gqa_k[:, l].reshape(DEC_BATCH, PAST_LEN, GQA_KW)], axis=1).astype(BF16)
        v_all = jnp.concatenate([v[N_CTX:].reshape(DEC_BATCH, DEC_SEQ, GQA_KW),
                                 cache_gqa_v[:, l].reshape(DEC_BATCH, PAST_LEN, GQA_KW)], axis=1).astype(BF16)
        ogqa_l = _gqa_latent(q, k_all, v_all)
        onat_l = _nat_latent(pnat, cache_nat_k[:, l].reshape(DEC_BATCH, PAST_LEN, NAT_W),
                             cache_nat_v[:, l].reshape(DEC_BATCH, PAST_LEN, NAT_W),
                             _nat_bias_table(nat_rpb[l]))

        x1, xm, row_t, gate_t, pieces = _out_proj(
            x, (of_c, of_l), (ob_c, ob_l), pgla, (onat_c, onat_l), (ogqa_c, ogqa_l),
            _prep_w_out(w_out[l]), row(jnp.tile(gla_norm_g[l], H_GLA)), g1, row(ln1_g[l]), row(ln1_b[l]),
            sh2, sc2, w_router[l].T.astype(BF16), router_bias[l].reshape(N_EXPERTS, 1))
        plan = _moe_plan(pieces[:, ::128].T)
        y = _moe_experts(l, plan, _moe_dispatch(plan, xm, row_t), w_expert_gu, w_expert_down)
        x = _moe_combine(plan, y, row_t, gate_t, xm, x1, g2, w_shared_gu[l].astype(BF16),
                         w_shared_down[l].astype(BF16), row(ln2_g[l]), row(ln2_b[l]))

        st_gla.append(_blockdiag_to_state(st_c))
        st_nk.append(pnat[:N_CTX, NAT_W:2 * NAT_W].astype(F32).reshape(BATCH, SEQ, H_NAT, HEAD_DIM))
        st_nv.append(pnat[:N_CTX, 2 * NAT_W:].astype(F32).reshape(BATCH, SEQ, H_NAT, HEAD_DIM))
        st_gk.append(k[:N_CTX].reshape(BATCH, SEQ, KV_GQA, HEAD_DIM))
        st_gv.append(v[:N_CTX].reshape(BATCH, SEQ, KV_GQA, HEAD_DIM))

    y_prompt = x[:N_CTX].reshape(BATCH, SEQ, D_MODEL)
    y_sample = x[N_CTX:].reshape(DEC_BATCH, DEC_SEQ, D_MODEL)
    return (y_prompt, y_sample, jnp.stack(st_gla, axis=1), jnp.stack(st_nk, axis=1), jnp.stack(st_nv, axis=1),
            jnp.stack(st_gk, axis=1), jnp.stack(st_gv, axis=1))
```

```python
import functools

import numpy as np
import jax
import jax.numpy as jnp
from jax import lax
from jax.experimental import pallas as pl
from jax.experimental.pallas import tpu as pltpu

D_MODEL = 1024
BATCH = 16
SEQ = 256
DEPTH = 2
DEC_BATCH = 2
DEC_SEQ = 4096
PAST_LEN = 512
GRID_W = 64
HEAD_DIM = 64
H_GLA = 4
GLA_DK = 64
GLA_DV = 64
GLA_LOWRANK = 16
GLA_TAU = 16.0
H_NAT = 4
NAT_KH = 8
NAT_KW = 16
H_GQA = 8
KV_GQA = 2
ROPE_THETA = 10000.0
N_EXPERTS = 64
TOP_K = 8
N_GROUPS = 8
TOPK_GROUPS = 4
D_EXPERT = 256
D_SHARED = 256
ROUTE_SCALE = 2.5
EPS = 1e-6
ALPHA = (2 * DEPTH) ** 0.25
IN_SPLITS = (H_GLA * GLA_DK, H_GLA * GLA_DK, H_GLA * GLA_DV, H_GLA * GLA_DV, 2 * GLA_LOWRANK,
             H_NAT * HEAD_DIM, H_NAT * HEAD_DIM, H_NAT * HEAD_DIM,
             H_GQA * HEAD_DIM, KV_GQA * HEAD_DIM, KV_GQA * HEAD_DIM)

F32 = jnp.float32
BF16 = jnp.bfloat16

N_CTX = BATCH * SEQ
N_LAT = DEC_BATCH * DEC_SEQ
N_TOK = N_CTX + N_LAT
N_COND = 8
GLA_W = H_GLA * GLA_DK
NAT_W = H_NAT * HEAD_DIM
GQA_QW = H_GQA * HEAD_DIM
GQA_KW = KV_GQA * HEAD_DIM
W_IN_COLS = 4 * GLA_W + 3 * NAT_W + GQA_QW + 2 * GQA_KW + 128
GA_COL = W_IN_COLS - 128
GQA_SLOT_HEADS = (0, 4, 1, 5, 2, 6, 3, 7)
GLA_CHUNK = 32
GLA_TB = 256
ROWS = DEC_SEQ // GRID_W
NAT_ROWS_PER_STEP = 8
VMEM_LIMIT = 56 * 1024 * 1024

TM_PROJ = 256
MOE_TILE = 256
MOE_NT = N_TOK // MOE_TILE
PIECE = 16
COPY_ROWS = 2 * PIECE
MOE_CAP = MOE_TILE * TOP_K + N_EXPERTS * PIECE
MAX_BIG = MOE_CAP // COPY_ROWS
MOE_CHUNK = 512
LOOP_UNROLL = 4
ROW_TILE = 512
XS_ROWS = N_TOK * TOP_K + MOE_NT * N_EXPERTS * (PIECE - 1) + N_EXPERTS * (ROW_TILE - PIECE)
assert XS_ROWS % ROW_TILE == 0
MAX_ZERO = N_EXPERTS * (ROW_TILE // PIECE - 1)
TQ_GQA = 256


def _dot(a, b):
    return jnp.dot(a.astype(BF16), b.astype(BF16), preferred_element_type=F32)


def _dot_nt(a, b):
    return lax.dot_general(a.astype(BF16), b.astype(BF16), (((1,), (1,)), ((), ())),
                           preferred_element_type=F32)


def _dot_tn(a, b):
    return lax.dot_general(a.astype(BF16), b.astype(BF16), (((0,), (0,)), ((), ())),
                           preferred_element_type=F32)


def _dot_split(a, b_bf16):
    hi = a.astype(BF16)
    lo = (a - hi.astype(F32)).astype(BF16)
    return (jnp.dot(hi, b_bf16, preferred_element_type=F32)
            + jnp.dot(lo, b_bf16, preferred_element_type=F32))


def _sigmoid(x):
    return 1.0 / (1.0 + jnp.exp(-x))


def _silu(x):
    return x * _sigmoid(x)


def _ln(x):
    xc = x - jnp.mean(x, axis=-1, keepdims=True)
    return xc * lax.rsqrt(jnp.mean(xc * xc, axis=-1, keepdims=True) + EPS)


def _lane_group(shape, axis, width):
    return lax.shift_right_logical(lax.broadcasted_iota(jnp.int32, shape, axis), int(np.log2(width)))


def _head_mean_sq(x):
    w = x.shape[-1]
    bmat = jnp.where(_lane_group((w, w), 0, HEAD_DIM) == _lane_group((w, w), 1, HEAD_DIM),
                     1.0 / HEAD_DIM, 0.0).astype(BF16)
    return _dot_split(x * x, bmat)


def _swap_half_heads(x):
    w = x.shape[-1]
    lane = lax.broadcasted_iota(jnp.int32, x.shape, x.ndim - 1)
    first = (lane & (HEAD_DIM - 1)) < HEAD_DIM // 2
    return jnp.where(first, pltpu.roll(x, w - HEAD_DIM // 2, x.ndim - 1),
                     pltpu.roll(x, HEAD_DIM // 2, x.ndim - 1))


def _cond_row(tile, tm):
    r = tile * tm
    return jnp.where(r < N_CTX, 0, 1 + (r - N_CTX) // DEC_SEQ)


def _stack_heads(x, n, width):
    head = _lane_group(x.shape, 1, width)
    return jnp.concatenate([jnp.where(head == h, x, jnp.zeros_like(x)) for h in range(n)], axis=0)


def _unstack_heads(o, n, width):
    m = o.shape[0] // n
    head = _lane_group((m, o.shape[1]), 1, width)
    out = jnp.zeros((m, o.shape[1]), o.dtype)
    for h in range(n):
        out = jnp.where(head == h, o[h * m:(h + 1) * m], out)
    return out


def _mod_kernel(c_ref, w_ref, b_ref, o_ref):
    o_ref[0] = _dot(_silu(c_ref[...]), w_ref[0]) + b_ref[0]


def _modulation(cond, w_mod, b_mod):
    tn = 1536
    return pl.pallas_call(
        _mod_kernel,
        grid=(DEPTH, 6 * D_MODEL // tn),
        in_specs=[pl.BlockSpec((N_COND, D_MODEL), lambda l, j: (0, 0)),
                  pl.BlockSpec((1, D_MODEL, tn), lambda l, j: (l, 0, j)),
                  pl.BlockSpec((1, 1, tn), lambda l, j: (l, 0, j))],
        out_specs=pl.BlockSpec((1, N_COND, tn), lambda l, j: (l, 0, j)),
        out_shape=jax.ShapeDtypeStruct((DEPTH, N_COND, 6 * D_MODEL), F32),
        compiler_params=pltpu.CompilerParams(dimension_semantics=("arbitrary", "arbitrary"),
                                             vmem_limit_bytes=VMEM_LIMIT),
        name="modulation",
    )(cond, w_mod, b_mod.reshape(DEPTH, 1, 6 * D_MODEL))


def _inproj_kernel(x_ref, sh_ref, sc_ref, w_ref, wa_ref, ba_ref, qg_ref, kg_ref, cos_ref, sin_ref,
                   pgla_ref, la_ref, pnat_ref, q_ref, k_ref, v_ref):
    ci = _cond_row(pl.program_id(0), TM_PROJ)
    xn = _ln(x_ref[...])
    xm = xn * (1.0 + sc_ref[pl.ds(ci, 1), :]) + sh_ref[pl.ds(ci, 1), :]
    acc = _dot(xm, w_ref[...])
    pgla_ref[...] = acc[:, 0:4 * GLA_W]
    pnat_ref[...] = acc[:, 4 * GLA_W:4 * GLA_W + 3 * NAT_W].astype(BF16)
    z = _dot(acc[:, GA_COL:GA_COL + 128], wa_ref[...]) + ba_ref[...]
    la_ref[...] = (jnp.minimum(z, 0.0) - jnp.log(1.0 + jnp.exp(-jnp.abs(z)))) * (1.0 / GLA_TAU)

    c0 = 4 * GLA_W + 3 * NAT_W
    cos = cos_ref[...]
    sin = sin_ref[...]

    def norm_rope(a, g):
        an = a * lax.rsqrt(_head_mean_sq(a) + EPS) * g
        return an * cos + _swap_half_heads(an) * sin

    qs = [norm_rope(acc[:, c0 + j * 128:c0 + (j + 1) * 128], qg_ref[...]) for j in range(GQA_QW // 128)]
    q_ref[...] = (jnp.concatenate(qs, axis=1) * (HEAD_DIM ** -0.5)).astype(BF16)
    k_ref[...] = norm_rope(acc[:, c0 + GQA_QW:c0 + GQA_QW + GQA_KW], kg_ref[...])
    v_ref[...] = acc[:, c0 + GQA_QW + GQA_KW:c0 + GQA_QW + 2 * GQA_KW]


def _in_proj(x, sh, sc, w, wa, ba, qg, kg, cos_t, sin_t):
    tm = TM_PROJ
    full = lambda a: pl.BlockSpec(a.shape, lambda i: (0,) * a.ndim)
    rows = lambda w_: pl.BlockSpec((tm, w_), lambda i: (i, 0))
    return pl.pallas_call(
        _inproj_kernel,
        grid=(N_TOK // tm,),
        in_specs=[rows(D_MODEL), full(sh), full(sc), full(w), full(wa), full(ba), full(qg), full(kg),
                  rows(128), rows(128)],
        out_specs=[rows(4 * GLA_W), rows(2 * GLA_W), rows(3 * NAT_W), rows(GQA_QW), rows(GQA_KW),
                   rows(GQA_KW)],
        out_shape=[jax.ShapeDtypeStruct((N_TOK, 4 * GLA_W), F32),
                   jax.ShapeDtypeStruct((N_TOK, 2 * GLA_W), F32),
                   jax.ShapeDtypeStruct((N_TOK, 3 * NAT_W), BF16),
                   jax.ShapeDtypeStruct((N_TOK, GQA_QW), BF16),
                   jax.ShapeDtypeStruct((N_TOK, GQA_KW), F32),
                   jax.ShapeDtypeStruct((N_TOK, GQA_KW), F32)],
        compiler_params=pltpu.CompilerParams(dimension_semantics=("arbitrary",),
                                             vmem_limit_bytes=VMEM_LIMIT),
        name="in_proj",
    )(x, sh, sc, w, wa, ba, qg, kg, cos_t, sin_t)


def _gla_kernel(qf_ref, kf_ref, vf_ref, laf_ref, qb_ref, kb_ref, vb_ref, lab_ref, st0_ref,
                of_ref, ob_ref, stout_ref, st_ref):
    t = pl.program_id(1)
    last_t = pl.num_programs(1) - 1
    C = GLA_CHUNK
    nchunk = GLA_TB // C

    @pl.when(t == 0)
    def _():
        st_ref[...] = st0_ref[0]

    tb = GLA_TB
    shift = int(np.log2(C))
    r = lax.broadcasted_iota(jnp.int32, (tb, tb), 0)
    c = lax.broadcasted_iota(jnp.int32, (tb, tb), 1)
    same = lax.shift_right_logical(r, shift) == lax.shift_right_logical(c, shift)
    ra = lax.broadcasted_iota(jnp.int32, (tb, H_GLA * tb), 0)
    ca = lax.broadcasted_iota(jnp.int32, (tb, H_GLA * tb), 1) & (tb - 1)
    same_a = lax.shift_right_logical(ra, shift) == lax.shift_right_logical(ca, shift)
    blk = _lane_group((H_GLA * tb, GLA_W), 0, tb) == _lane_group((H_GLA * tb, GLA_W), 1, GLA_DK)
    diag = _lane_group((GLA_W, GLA_W), 0, GLA_DV) == _lane_group((GLA_W, GLA_W), 1, GLA_DK)
    one = lambda m: jnp.where(m, 1.0, 0.0).astype(BF16)
    chunk_ones = one(same)
    ref_pick = one(c == (r & ~(C - 1)) + C // 2)

    def direction(q_ref, k_ref, v_ref, la_ref, o_ref, d, causal):
        q = q_ref[...] * (GLA_DK ** -0.5)
        k = k_ref[...]
        v = v_ref[...]
        la = la_ref[...]
        la_hi = la.astype(BF16)
        la_lo = (la - la_hi.astype(F32)).astype(BF16)
        csum = lambda m: (jnp.dot(m, la_hi, preferred_element_type=F32)
                          + jnp.dot(m, la_lo, preferred_element_type=F32))
        b = csum(one(same & ((r >= c) if causal else (r <= c))))
        btot = csum(chunk_ones)
        bref = jnp.dot(ref_pick, b.astype(BF16), preferred_element_type=F32)
        q_att = q * jnp.exp(b - bref)
        k_att = k * jnp.exp(bref - b)
        kblk = jnp.where(blk, jnp.concatenate([k_att] * H_GLA, axis=0), 0.0)
        att = _dot_nt(q_att, kblk)
        att = jnp.where(same_a & ((ra >= ca) if causal else (ra <= ca)), att, 0.0)
        vblk = jnp.where(blk, jnp.concatenate([v] * H_GLA, axis=0), 0.0)
        o_intra = _dot(att, vblk)

        q_dec = (q * jnp.exp(b)).astype(BF16)
        k_dec = (k * jnp.exp(btot - b)).astype(BF16)
        g = jnp.exp(btot)
        vb = v.astype(BF16)
        st = st_ref[d]
        o_inter = [None] * nchunk
        for ci in (range(nchunk) if causal else reversed(range(nchunk))):
            rows = slice(ci * C, (ci + 1) * C)
            o_inter[ci] = _dot_nt(q_dec[rows], st)
            u = _dot_tn(vb[rows], k_dec[rows])
            st = st * g[ci * C:ci * C + 1, :] + jnp.where(diag, u, 0.0)
        st_ref[d] = st
        o_ref[...] = o_intra + jnp.concatenate(o_inter, axis=0)

    direction(qf_ref, kf_ref, vf_ref, laf_ref, of_ref, 0, True)
    direction(qb_ref, kb_ref, vb_ref, lab_ref, ob_ref, 1, False)

    @pl.when(t == last_t)
    def _():
        stout_ref[0] = st_ref[...]


def _gla(pgla, la, st0, row0, n_seq, seq_len):
    tb = GLA_TB
    nt = seq_len // tb
    b0 = row0 // tb
    fwd = lambda col: pl.BlockSpec((tb, GLA_W), lambda s, t: (b0 + s * nt + t, col))
    bwd = lambda col: pl.BlockSpec((tb, GLA_W), lambda s, t: (b0 + s * nt + nt - 1 - t, col))
    st_spec = pl.BlockSpec((1, 2, GLA_W, GLA_W), lambda s, t: (s, 0, 0, 0))
    n = n_seq * seq_len
    return pl.pallas_call(
        _gla_kernel,
        grid=(n_seq, nt),
        in_specs=[fwd(0), fwd(1), fwd(2), fwd(0), bwd(0), bwd(1), bwd(2), bwd(1), st_spec],
        out_specs=[pl.BlockSpec((tb, GLA_W), lambda s, t: (s * nt + t, 0)),
                   pl.BlockSpec((tb, GLA_W), lambda s, t: (s * nt + nt - 1 - t, 0)),
                   st_spec],
        out_shape=[jax.ShapeDtypeStruct((n, GLA_W), F32), jax.ShapeDtypeStruct((n, GLA_W), F32),
                   jax.ShapeDtypeStruct((n_seq, 2, GLA_W, GLA_W), F32)],
        scratch_shapes=[pltpu.VMEM((2, GLA_W, GLA_W), F32)],
        compiler_params=pltpu.CompilerParams(dimension_semantics=("arbitrary", "arbitrary"),
                                             vmem_limit_bytes=VMEM_LIMIT),
        name="gla_scan",
    )(pgla, pgla, pgla, la, pgla, pgla, pgla, la, st0)


def _softmax_pv(s_list, v_list):
    m = s_list[0].max(axis=-1, keepdims=True)
    for s in s_list[1:]:
        m = jnp.maximum(m, s.max(axis=-1, keepdims=True))
    acc = None
    l = None
    for s, v in zip(s_list, v_list):
        p = jnp.exp(s - m)
        pl_ = p.sum(axis=-1, keepdims=True)
        pv = _dot(p, v)
        acc = pv if acc is None else acc + pv
        l = pl_ if l is None else l + pl_
    return acc * (1.0 / l)


def _ctx_attn_kernel(pnat_ref, q_ref, k_ref, v_ref, onat_ref, ogqa_ref):
    nq = pnat_ref[:, 0:NAT_W] * (HEAD_DIM ** -0.5)
    nk = pnat_ref[:, NAT_W:2 * NAT_W].astype(BF16)
    nv = pnat_ref[:, 2 * NAT_W:3 * NAT_W].astype(BF16)
    o = _softmax_pv([_dot_nt(_stack_heads(nq, H_NAT, HEAD_DIM), nk)], [nv])
    onat_ref[...] = _unstack_heads(o, H_NAT, HEAD_DIM).astype(BF16)

    k = k_ref[...].astype(BF16)
    v = v_ref[...].astype(BF16)
    k2 = jnp.concatenate([k, k], axis=1)
    v2 = jnp.concatenate([v, v], axis=1)
    outs = []
    for half in range(2):
        q = q_ref[:, half * 256:(half + 1) * 256]
        o = _softmax_pv([_dot_nt(_stack_heads(q, 4, HEAD_DIM), k2)], [v2])
        outs.append(_unstack_heads(o, 4, HEAD_DIM))
    ogqa_ref[...] = jnp.concatenate(outs, axis=1).astype(BF16)


def _ctx_attention(pnat, q, k, v):
    rows = lambda w_: pl.BlockSpec((SEQ, w_), lambda i: (i, 0))
    return pl.pallas_call(
        _ctx_attn_kernel,
        grid=(BATCH,),
        in_specs=[rows(3 * NAT_W), rows(GQA_QW), rows(GQA_KW), rows(GQA_KW)],
        out_specs=[rows(NAT_W), rows(GQA_QW)],
        out_shape=[jax.ShapeDtypeStruct((N_CTX, NAT_W), BF16),
                   jax.ShapeDtypeStruct((N_CTX, GQA_QW), BF16)],
        compiler_params=pltpu.CompilerParams(dimension_semantics=("arbitrary",),
                                             vmem_limit_bytes=VMEM_LIMIT),
        name="ctx_attention",
    )(pnat, q, k, v)


def _gqa_lat_kernel(q_ref, k_ref, v_ref, o_ref):
    k = k_ref[0]
    v = v_ref[0]
    k2 = jnp.concatenate([k, k], axis=1)
    v2 = jnp.concatenate([v, v], axis=1)
    slot = _lane_group((TQ_GQA, 256), 1, HEAD_DIM)
    outs = []
    for half in range(2):
        q = q_ref[:, half * 256:(half + 1) * 256]
        out = jnp.zeros((TQ_GQA, 256), F32)
        for s in range(4):
            qs = jnp.where(slot == s, q, jnp.zeros_like(q))
            o = _softmax_pv([_dot_nt(qs, k2)], [v2])
            out = jnp.where(slot == s, o, out)
        outs.append(out)
    o_ref[...] = jnp.concatenate(outs, axis=1).astype(BF16)


def _gqa_latent(q, k_all, v_all):
    tq = TQ_GQA
    nq = DEC_SEQ // tq
    tk = k_all.shape[1]
    return pl.pallas_call(
        _gqa_lat_kernel,
        grid=(DEC_BATCH, nq),
        in_specs=[pl.BlockSpec((tq, GQA_QW), lambda b, i: (N_CTX // tq + b * nq + i, 0)),
                  pl.BlockSpec((1, tk, GQA_KW), lambda b, i: (b, 0, 0)),
                  pl.BlockSpec((1, tk, GQA_KW), lambda b, i: (b, 0, 0))],
        out_specs=pl.BlockSpec((tq, GQA_QW), lambda b, i: (b * nq + i, 0)),
        out_shape=jax.ShapeDtypeStruct((N_LAT, GQA_QW), BF16),
        compiler_params=pltpu.CompilerParams(dimension_semantics=("arbitrary", "arbitrary"),
                                             vmem_limit_bytes=VMEM_LIMIT),
        name="gqa_latent",
    )(q, k_all, v_all)


def _nat_lat_kernel(q_ref, k_ref, v_ref, kc_ref, vc_ref, bias_ref, o_ref):
    j = pl.program_id(1)
    kc = kc_ref[0].astype(BF16)
    vc = vc_ref[0].astype(BF16)
    for i in range(NAT_ROWS_PER_STEP):
        r = j * NAT_ROWS_PER_STEP + i
        r0 = jnp.clip(r - NAT_KH // 2, 0, ROWS - NAT_KH)
        win = pl.ds(pl.multiple_of(r0 * GRID_W, GRID_W), NAT_KH * GRID_W)
        q = q_ref[i * GRID_W:(i + 1) * GRID_W, :] * (HEAD_DIM ** -0.5)
        qs = _stack_heads(q, H_NAT, HEAD_DIM)
        s_loc = _dot_nt(qs, k_ref[win, :]) + bias_ref[r - r0]
        s_ctx = _dot_nt(qs, kc)
        o = _softmax_pv([s_loc, s_ctx], [v_ref[win, :], vc])
        o_ref[i * GRID_W:(i + 1) * GRID_W, :] = _unstack_heads(o, H_NAT, HEAD_DIM).astype(BF16)


def _nat_latent(pnat, kc, vc, bias):
    tq = NAT_ROWS_PER_STEP * GRID_W
    nq = DEC_SEQ // tq
    lat_blk = N_CTX // DEC_SEQ
    return pl.pallas_call(
        _nat_lat_kernel,
        grid=(DEC_BATCH, nq),
        in_specs=[pl.BlockSpec((tq, NAT_W), lambda b, j: (N_CTX // tq + b * nq + j, 0)),
                  pl.BlockSpec((DEC_SEQ, NAT_W), lambda b, j: (lat_blk + b, 1)),
                  pl.BlockSpec((DEC_SEQ, NAT_W), lambda b, j: (lat_blk + b, 2)),
                  pl.BlockSpec((1, PAST_LEN, NAT_W), lambda b, j: (b, 0, 0)),
                  pl.BlockSpec((1, PAST_LEN, NAT_W), lambda b, j: (b, 0, 0)),
                  pl.BlockSpec(bias.shape, lambda b, j: (0, 0, 0))],
        out_specs=pl.BlockSpec((tq, NAT_W), lambda b, j: (b * nq + j, 0)),
        out_shape=jax.ShapeDtypeStruct((N_LAT, NAT_W), BF16),
        compiler_params=pltpu.CompilerParams(dimension_semantics=("arbitrary", "arbitrary"),
                                             vmem_limit_bytes=VMEM_LIMIT),
        name="nat_latent",
    )(pnat, pnat, pnat, kc, vc, bias)


def _route_gates(logits_t, bias_ref):
    per = N_EXPERTS // N_GROUPS
    t = logits_t.shape[1]
    neg = -jnp.inf
    pos = lax.broadcasted_iota(jnp.int32, (per, t), 0)
    scores = [_sigmoid(logits_t[g * per:(g + 1) * per, :]) for g in range(N_GROUPS)]
    biased = [scores[g] + bias_ref[g * per:(g + 1) * per, :] for g in range(N_GROUPS)]

    grp = []
    for v in biased:
        m1 = jnp.max(v, axis=0, keepdims=True)
        i1 = jnp.min(jnp.where(v == m1, pos, per), axis=0, keepdims=True)
        m2 = jnp.max(jnp.where(pos == i1, neg, v), axis=0, keepdims=True)
        grp.append(m1 + m2)

    keep = [jnp.zeros((1, t), jnp.bool_) for _ in range(N_GROUPS)]
    for _ in range(TOPK_GROUPS):
        best = functools.reduce(jnp.maximum, grp)
        first = jnp.full((1, t), N_GROUPS, jnp.int32)
        for g in reversed(range(N_GROUPS)):
            first = jnp.where(grp[g] == best, g, first)
        for g in range(N_GROUPS):
            hit = first == g
            keep[g] = keep[g] | hit
            grp[g] = jnp.where(hit, neg, grp[g])

    cand = [jnp.where(keep[g], biased[g], neg) for g in range(N_GROUPS)]
    flat = [pos + g * per for g in range(N_GROUPS)]
    picks, weights = [], []
    for _ in range(TOP_K):
        best = functools.reduce(jnp.maximum, [jnp.max(v, axis=0, keepdims=True) for v in cand])
        first = functools.reduce(jnp.minimum, [
            jnp.min(jnp.where(cand[g] == best, flat[g], N_EXPERTS), axis=0, keepdims=True)
            for g in range(N_GROUPS)])
        score = jnp.zeros((1, t), F32)
        for g in range(N_GROUPS):
            hit = flat[g] == first
            score = score + jnp.sum(jnp.where(hit, scores[g], 0.0), axis=0, keepdims=True)
            cand[g] = jnp.where(hit, neg, cand[g])
        picks.append(first)
        weights.append(score)

    total = functools.reduce(lambda a, b: a + b, weights)
    gates = jnp.concatenate([v / total * ROUTE_SCALE for v in weights], axis=0)

    sel = [functools.reduce(lambda a, b: a | b, [flat[g] == k for k in picks]) for g in range(N_GROUPS)]
    sel = jnp.where(jnp.concatenate(sel, axis=0), 1.0, 0.0).astype(BF16)
    before = (lax.broadcasted_iota(jnp.int32, (t, t), 0) < lax.broadcasted_iota(jnp.int32, (t, t), 1))
    rank = jnp.dot(sel, jnp.where(before, 1.0, 0.0).astype(BF16), preferred_element_type=F32)
    count = jnp.sum(sel.astype(F32), axis=1, keepdims=True)
    pieces = jnp.floor((count + (PIECE - 1)) * (1.0 / PIECE))
    lower = (lax.broadcasted_iota(jnp.int32, (N_EXPERTS, N_EXPERTS), 1)
             < lax.broadcasted_iota(jnp.int32, (N_EXPERTS, N_EXPERTS), 0))
    start = PIECE * jnp.dot(jnp.where(lower, 1.0, 0.0).astype(BF16),
                            jnp.broadcast_to(pieces, (N_EXPERTS, 128)).astype(BF16),
                            preferred_element_type=F32)[:, 0:1]
    row = rank + start
    rows = []
    for k in picks:
        r = jnp.zeros((1, t), F32)
        for g in range(N_GROUPS):
            r = r + jnp.sum(jnp.where(flat[g] == k, row[g * per:(g + 1) * per, :], 0.0), axis=0, keepdims=True)
        rows.append(r)
    return (jnp.concatenate(rows, axis=0).astype(jnp.int32), gates,
            jnp.broadcast_to(pieces, (N_EXPERTS, 128)).astype(jnp.int32))


def _outproj_kernel(x_ref, ofc_ref, ofl_ref, obc_ref, obl_ref, gg_ref, onatc_ref, onatl_ref, ogqac_ref, ogqal_ref,
                    w_ref, ng_ref, g1_ref, lng_ref, lnb_ref, sh2_ref, sc2_ref, wr_ref, rb_ref, x1_ref, xm_ref,
                    row_ref, gate_ref, pieces_ref):
    ci = _cond_row(pl.program_id(0), TM_PROJ)
    is_ctx = pl.program_id(0) < N_CTX // TM_PROJ
    either = lambda c_ref, l_ref: jnp.where(is_ctx, c_ref[...], l_ref[...])
    og = either(ofc_ref, ofl_ref) + either(obc_ref, obl_ref)
    halves = [og[:, j * 128:(j + 1) * 128] for j in range(GLA_W // 128)]
    ms = jnp.concatenate([_head_mean_sq(h) for h in halves], axis=1)
    ogla = og * lax.rsqrt(ms + EPS) * ng_ref[...] * _silu(gg_ref[...])
    mix = (_dot(ogla, w_ref[0:GLA_W, :]) + _dot(either(onatc_ref, onatl_ref), w_ref[GLA_W:GLA_W + NAT_W, :])
           + _dot(either(ogqac_ref, ogqal_ref), w_ref[GLA_W + NAT_W:, :]))
    x1 = _ln(ALPHA * x_ref[...] + g1_ref[pl.ds(ci, 1), :] * mix) * lng_ref[...] + lnb_ref[...]
    x1_ref[...] = x1
    xm = (_ln(x1) * (1.0 + sc2_ref[pl.ds(ci, 1), :]) + sh2_ref[pl.ds(ci, 1), :]).astype(BF16)
    xm_ref[...] = xm
    row_ref[...], gate_ref[...], pieces_ref[...] = _route_gates(_dot_nt(wr_ref[...], xm), rb_ref)


def _out_proj(x, of, ob, pgla, onat, ogqa, w, ng, g1, lng, lnb, sh2, sc2, wr, rb):
    tm = TM_PROJ
    n_ctx = N_CTX // tm
    full = lambda a: pl.BlockSpec(a.shape, lambda i: (0,) * a.ndim)
    rows = lambda w_: pl.BlockSpec((tm, w_), lambda i: (i, 0))
    ctx = lambda w_: pl.BlockSpec((tm, w_), lambda i: (jnp.minimum(i, n_ctx - 1), 0))
    lat = lambda w_: pl.BlockSpec((tm, w_), lambda i: (jnp.maximum(i - n_ctx, 0), 0))
    return pl.pallas_call(
        _outproj_kernel,
        grid=(N_TOK // tm,),
        in_specs=[rows(D_MODEL), ctx(GLA_W), lat(GLA_W), ctx(GLA_W), lat(GLA_W),
                  pl.BlockSpec((tm, GLA_W), lambda i: (i, 3)),
                  ctx(NAT_W), lat(NAT_W), ctx(GQA_QW), lat(GQA_QW), full(w), full(ng), full(g1), full(lng),
                  full(lnb), full(sh2), full(sc2), full(wr), full(rb)],
        out_specs=[rows(D_MODEL), rows(D_MODEL), pl.BlockSpec((TOP_K, tm), lambda i: (0, i)),
                   pl.BlockSpec((TOP_K, tm), lambda i: (0, i)),
                   pl.BlockSpec((N_EXPERTS, 128), lambda i: (0, i))],
        out_shape=[jax.ShapeDtypeStruct((N_TOK, D_MODEL), F32),
                   jax.ShapeDtypeStruct((N_TOK, D_MODEL), BF16),
                   jax.ShapeDtypeStruct((TOP_K, N_TOK), jnp.int32),
                   jax.ShapeDtypeStruct((TOP_K, N_TOK), F32),
                   jax.ShapeDtypeStruct((N_EXPERTS, MOE_NT * 128), jnp.int32)],
        compiler_params=pltpu.CompilerParams(dimension_semantics=("arbitrary",),
                                             vmem_limit_bytes=VMEM_LIMIT),
        name="out_proj",
    )(x, *of, *ob, pgla, *onat, *ogqa, w, ng, g1, lng, lnb, sh2, sc2, wr, rb)


def _plan_specs(n_prefetch, **kw):
    return pltpu.PrefetchScalarGridSpec(num_scalar_prefetch=n_prefetch, **kw)


def _for_each_copy(t, copies, fn):
    bsrc_ref, bdst_ref, ssrc_ref, sdst_ref, nbig_ref, nsmall_ref = copies

    def big(i):
        j = t * MAX_BIG + i
        fn(pl.multiple_of(bsrc_ref[j], PIECE), pl.multiple_of(bdst_ref[j], PIECE), COPY_ROWS)

    def small(i):
        j = t * N_EXPERTS + i
        fn(pl.multiple_of(ssrc_ref[j], PIECE), pl.multiple_of(sdst_ref[j], PIECE), PIECE)

    _unrolled_loop(nbig_ref[t], big)
    _unrolled_loop(nsmall_ref[t], small)


def _unrolled_loop(n, body):
    def group(g, c):
        for u in range(LOOP_UNROLL):
            body(g * LOOP_UNROLL + u)
        return c

    def single(i, c):
        body(i)
        return c

    n_group = n // LOOP_UNROLL
    lax.fori_loop(0, n_group, group, 0)
    lax.fori_loop(n_group * LOOP_UNROLL, n, single, 0)


def _wait_copies(n_big, n_small, make):
    _unrolled_loop(n_big, lambda i: make(COPY_ROWS).wait())
    _unrolled_loop(n_small, lambda i: make(PIECE).wait())


def _dispatch_kernel(bsrc_ref, bdst_ref, ssrc_ref, sdst_ref, nbig_ref, nsmall_ref, zdst_ref, tot_ref, xm_ref, ldt_ref,
                     xs_ref, stage_ref, zero_ref, sem_big, sem_small, tail_sem):
    copies = (bsrc_ref, bdst_ref, ssrc_ref, sdst_ref, nbig_ref, nsmall_ref)
    t = pl.program_id(0)
    nt = pl.num_programs(0)
    slot = lax.rem(t, 2)

    def run_copy(local_row, xs_row, rows, s):
        sem = sem_big if rows == COPY_ROWS else sem_small
        return pltpu.make_async_copy(stage_ref.at[s, pl.ds(local_row, rows)], xs_ref.at[pl.ds(xs_row, rows)],
                                     sem.at[s])

    def wait_tile(tt, s):
        _wait_copies(nbig_ref[tt], nsmall_ref[tt], lambda rows: run_copy(0, 0, rows, s))

    @pl.when(t >= 2)
    def _():
        wait_tile(t - 2, slot)

    xm = xm_ref[...]
    ld = ldt_ref[...].astype(jnp.int16)
    for c in range(MOE_CAP // MOE_CHUNK):
        p = (lax.broadcasted_iota(jnp.int32, (MOE_CHUNK, MOE_TILE), 0) + c * MOE_CHUNK).astype(jnp.int16)
        hit = p == ld[0:1, :]
        for k in range(1, TOP_K):
            hit = hit | (p == ld[k:k + 1, :])
        stage_ref[slot, c * MOE_CHUNK:(c + 1) * MOE_CHUNK, :] = jnp.dot(
            jnp.where(hit, jnp.ones((), BF16), jnp.zeros((), BF16)), xm, preferred_element_type=F32).astype(BF16)
    _for_each_copy(t, copies, lambda a, b, rows: run_copy(a, b, rows, slot).start())

    @pl.when(t == nt - 1)
    def _():
        zero_ref[...] = jnp.zeros_like(zero_ref)
        n_piece = tot_ref[1]
        first_free_tile = tot_ref[0] // ROW_TILE
        n_tile = XS_ROWS // ROW_TILE - first_free_tile

        def zero_piece(i):
            return pltpu.make_async_copy(zero_ref.at[pl.ds(0, PIECE)],
                                         xs_ref.at[pl.ds(pl.multiple_of(zdst_ref[i], PIECE), PIECE)],
                                         tail_sem.at[0])

        def zero_tile(i):
            row = pl.multiple_of((first_free_tile + i) * ROW_TILE, ROW_TILE)
            return pltpu.make_async_copy(zero_ref, xs_ref.at[pl.ds(row, ROW_TILE)], tail_sem.at[1])

        def run(n, make, op):
            _unrolled_loop(n, lambda i: op(make(i)))

        run(n_piece, zero_piece, lambda cp: cp.start())
        run(n_tile, zero_tile, lambda cp: cp.start())
        run(n_piece, zero_piece, lambda cp: cp.wait())
        run(n_tile, zero_tile, lambda cp: cp.wait())
        wait_tile(t - 1, 1 - slot)
        wait_tile(t, slot)


def _moe_dispatch(plan, xm, row_t):
    return pl.pallas_call(
        _dispatch_kernel,
        grid_spec=_plan_specs(
            8, grid=(MOE_NT,),
            in_specs=[pl.BlockSpec((MOE_TILE, D_MODEL), lambda t, *_: (t, 0)),
                      pl.BlockSpec((TOP_K, MOE_TILE), lambda t, *_: (0, t))],
            out_specs=pl.BlockSpec(memory_space=pl.ANY),
            scratch_shapes=[pltpu.VMEM((2, MOE_CAP, D_MODEL), BF16), pltpu.VMEM((ROW_TILE, D_MODEL), BF16),
                            pltpu.SemaphoreType.DMA((2,)), pltpu.SemaphoreType.DMA((2,)),
                            pltpu.SemaphoreType.DMA((2,))]),
        out_shape=jax.ShapeDtypeStruct((XS_ROWS, D_MODEL), BF16),
        compiler_params=pltpu.CompilerParams(dimension_semantics=("arbitrary",),
                                             vmem_limit_bytes=VMEM_LIMIT),
        name="moe_dispatch",
    )(*plan["copies"], plan["zero_dst"], plan["total"], xm, row_t)


def _experts_kernel(exp_ref, fresh_ref, used_ref, src_ref, xs_ref, wgu_ref, wd_ref, y_ref, wgu_bf_ref, wd_bf_ref):
    w = pl.program_id(0)

    @pl.when(fresh_ref[w] == 1)
    def _():
        wgu_bf_ref[...] = wgu_ref[0].astype(BF16)
        wd_bf_ref[...] = wd_ref[0].astype(BF16)

    @pl.when(used_ref[w] == 1)
    def _():
        ab = jnp.dot(xs_ref[...], wgu_bf_ref[...], preferred_element_type=F32)
        h = (_silu(ab[:, :D_EXPERT]) * ab[:, D_EXPERT:]).astype(BF16)
        y_ref[...] = jnp.dot(h, wd_bf_ref[...], preferred_element_type=F32).astype(BF16)

    @pl.when(used_ref[w] == 0)
    def _():
        y_ref[...] = jnp.zeros_like(y_ref)


def _moe_experts(layer, plan, xs, wgu, wd):
    return pl.pallas_call(
        _experts_kernel,
        grid_spec=_plan_specs(
            4, grid=(XS_ROWS // ROW_TILE,),
            in_specs=[pl.BlockSpec((ROW_TILE, D_MODEL), lambda w, ex, fresh, used, src: (src[w], 0)),
                      pl.BlockSpec((None, 1, D_MODEL, 2 * D_EXPERT), lambda w, ex, *_: (layer, ex[w], 0, 0)),
                      pl.BlockSpec((None, 1, D_EXPERT, D_MODEL), lambda w, ex, *_: (layer, ex[w], 0, 0))],
            out_specs=pl.BlockSpec((ROW_TILE, D_MODEL), lambda w, *_: (w, 0)),
            scratch_shapes=[pltpu.VMEM((D_MODEL, 2 * D_EXPERT), BF16), pltpu.VMEM((D_EXPERT, D_MODEL), BF16)]),
        out_shape=jax.ShapeDtypeStruct((XS_ROWS, D_MODEL), BF16),
        compiler_params=pltpu.CompilerParams(dimension_semantics=("arbitrary",),
                                             vmem_limit_bytes=VMEM_LIMIT),
        name="moe_experts",
    )(plan["tile_expert"], plan["tile_fresh"], plan["tile_used"], plan["tile_src"], xs, wgu, wd)


def _combine_kernel(bsrc_ref, bdst_ref, ssrc_ref, sdst_ref, nbig_ref, nsmall_ref, y_ref, ldt_ref, gate_ref, xm_ref,
                    x1_ref, g2_ref, wsgu_ref, wsd_ref, lng_ref, lnb_ref, out_ref, ybuf_ref, sem_big, sem_small):
    copies = (bsrc_ref, bdst_ref, ssrc_ref, sdst_ref, nbig_ref, nsmall_ref)
    t = pl.program_id(0)
    nt = pl.num_programs(0)
    slot = lax.rem(t, 2)
    ci = _cond_row(t, MOE_TILE)

    def run_copy(local_row, y_row, rows, s):
        sem = sem_big if rows == COPY_ROWS else sem_small
        return pltpu.make_async_copy(y_ref.at[pl.ds(y_row, rows)], ybuf_ref.at[s, pl.ds(local_row, rows)], sem.at[s])

    def fetch(tt, s):
        _for_each_copy(tt, copies, lambda a, b, rows: run_copy(a, b, rows, s).start())

    @pl.when(t == 0)
    def _():
        ybuf_ref[...] = jnp.zeros_like(ybuf_ref)
        fetch(0, 0)

    @pl.when(t + 1 < nt)
    def _():
        fetch(t + 1, 1 - slot)

    _wait_copies(nbig_ref[t], nsmall_ref[t], lambda rows: run_copy(0, 0, rows, slot))

    routed = jnp.zeros((MOE_TILE, D_MODEL), F32)
    ld = ldt_ref[...].astype(jnp.int16)
    gate = gate_ref[...].astype(BF16)
    for c in range(MOE_CAP // MOE_CHUNK):
        p = (lax.broadcasted_iota(jnp.int32, (MOE_CHUNK, MOE_TILE), 0) + c * MOE_CHUNK).astype(jnp.int16)
        wmat = jnp.zeros((MOE_CHUNK, MOE_TILE), BF16)
        for k in range(TOP_K):
            wmat = jnp.where(p == ld[k:k + 1, :], gate[k:k + 1, :], wmat)
        routed = routed + _dot_tn(wmat, ybuf_ref[slot, c * MOE_CHUNK:(c + 1) * MOE_CHUNK, :])

    sab = jnp.dot(xm_ref[...], wsgu_ref[...], preferred_element_type=F32)
    shared = _dot(_silu(sab[:, :D_SHARED]) * sab[:, D_SHARED:], wsd_ref[...])
    z = ALPHA * x1_ref[...] + g2_ref[pl.ds(ci, 1), :] * (routed + shared)
    out_ref[...] = _ln(z) * lng_ref[...] + lnb_ref[...]


def _moe_combine(plan, y, row_t, gate_t, xm, x1, g2, wsgu, wsd, lng, lnb):
    full = lambda a: pl.BlockSpec(a.shape, lambda t, *_: (0,) * a.ndim)
    rows = lambda w_: pl.BlockSpec((MOE_TILE, w_), lambda t, *_: (t, 0))
    picks = pl.BlockSpec((TOP_K, MOE_TILE), lambda t, *_: (0, t))
    return pl.pallas_call(
        _combine_kernel,
        grid_spec=_plan_specs(
            6, grid=(MOE_NT,),
            in_specs=[pl.BlockSpec(memory_space=pl.ANY), picks, picks, rows(D_MODEL), rows(D_MODEL),
                      full(g2), full(wsgu), full(wsd), full(lng), full(lnb)],
            out_specs=rows(D_MODEL),
            scratch_shapes=[pltpu.VMEM((2, MOE_CAP, D_MODEL), BF16), pltpu.SemaphoreType.DMA((2,)),
                            pltpu.SemaphoreType.DMA((2,))]),
        out_shape=jax.ShapeDtypeStruct((N_TOK, D_MODEL), F32),
        compiler_params=pltpu.CompilerParams(dimension_semantics=("arbitrary",),
                                             vmem_limit_bytes=VMEM_LIMIT),
        name="moe_combine",
    )(*plan["copies"], y, row_t, gate_t, xm, x1, g2, wsgu, wsd, lng, lnb)


def _moe_plan(npc):
    i32 = jnp.int32
    run = npc * PIECE
    lo = jnp.cumsum(run, axis=1) - run
    tot_e = jnp.sum(run, axis=0)
    region = (tot_e + ROW_TILE - 1) // ROW_TILE * ROW_TILE
    region_end = jnp.cumsum(region)
    off = region_end - region
    hs = off[None, :] + jnp.cumsum(run, axis=0) - run

    per_copy = COPY_ROWS // PIECE
    n_big = npc // per_copy
    odd = npc - n_big * per_copy

    def expand(count, width):
        end = jnp.cumsum(count, axis=1)
        i = jnp.arange(width, dtype=i32)[None, :, None]
        mine = (i >= (end - count)[:, None, :]) & (i < end[:, None, :])
        pick = lambda a: jnp.sum(jnp.where(mine, a[:, None, :], 0), axis=2)
        return pick, i[:, :, 0] - pick(end - count)

    pick_b, k_b = expand(n_big, MAX_BIG)
    pick_s, _ = expand(odd, N_EXPERTS)
    copies = [pick_b(lo) + k_b * COPY_ROWS, pick_b(hs) + k_b * COPY_ROWS,
              pick_s(lo + n_big * COPY_ROWS), pick_s(hs + n_big * COPY_ROWS)]
    copies = [c.reshape(-1).astype(i32) for c in copies] + [jnp.sum(n_big, axis=1).astype(i32),
                                                            jnp.sum(odd, axis=1).astype(i32)]

    n_zero = ((region - tot_e) // PIECE)[None, :]
    pick_z, k_z = expand(n_zero, MAX_ZERO)
    zero_dst = (pick_z((off + tot_e)[None, :]) + k_z * PIECE).reshape(-1).astype(i32)

    start = jnp.arange(XS_ROWS // ROW_TILE, dtype=i32) * ROW_TILE
    ex = jnp.minimum(jnp.sum((start[:, None] >= region_end[None, :]).astype(i32), axis=1), N_EXPERTS - 1)
    prev_ex = jnp.concatenate([jnp.full((1,), -1, i32), ex[:-1]])
    return dict(copies=copies, zero_dst=zero_dst,
                total=jnp.stack([region_end[-1], jnp.sum(n_zero)]).astype(i32),
                tile_expert=ex.astype(i32), tile_fresh=(ex != prev_ex).astype(i32),
                tile_used=(start < region_end[-1]).astype(i32),
                tile_src=(jnp.minimum(start, region_end[-1] - ROW_TILE) // ROW_TILE).astype(i32))


def _rope_tables():
    t = jnp.arange(DEC_SEQ)
    n_freq = HEAD_DIM // 4
    freqs = ROPE_THETA ** (-jnp.arange(n_freq, dtype=F32) / n_freq)
    ang = jnp.concatenate([(t // GRID_W).astype(F32)[:, None] * freqs,
                           (t % GRID_W).astype(F32)[:, None] * freqs], -1)
    cos, sin = jnp.cos(ang), jnp.sin(ang)
    cos_h = jnp.concatenate([cos, cos], -1)
    sin_h = jnp.concatenate([-sin, sin], -1)
    lat = lambda a: jnp.tile(a, (DEC_BATCH, 128 // HEAD_DIM))
    cos_t = jnp.concatenate([jnp.ones((N_CTX, 128), F32), lat(cos_h)], 0)
    sin_t = jnp.concatenate([jnp.zeros((N_CTX, 128), F32), lat(sin_h)], 0)
    return cos_t, sin_t


def _nat_bias_table(rpb):
    n_dr, n_dc = 2 * NAT_KH - 1, 2 * NAT_KW - 1
    cidx = np.arange(GRID_W)
    dc_idx = np.clip(cidx[None, :] - cidx[:, None] + NAT_KW - 1, 0, n_dc - 1)
    col_start = np.clip(cidx - NAT_KW // 2, 0, GRID_W - NAT_KW)
    col_in = (cidx[None, :] >= col_start[:, None]) & (cidx[None, :] < col_start[:, None] + NAT_KW)
    onehot = (dc_idx.reshape(1, -1) == np.arange(n_dc)[:, None]).astype(np.float32)
    t = jnp.dot(rpb.reshape(H_NAT * n_dr, n_dc), onehot, precision=lax.Precision.HIGHEST)
    t = jnp.where(col_in[None, None], t.reshape(H_NAT, n_dr, GRID_W, GRID_W), -jnp.inf)
    bias = jnp.stack([t[:, NAT_KH - 1 - p:2 * NAT_KH - 1 - p] for p in range(NAT_KH)], axis=0)
    return bias.transpose(0, 1, 3, 2, 4).reshape(NAT_KH, H_NAT * GRID_W, NAT_KH * GRID_W)


def _prep_w_in(w_in):
    cuts = np.cumsum((0,) + IN_SPLITS)
    seg = [w_in[:, cuts[i]:cuts[i + 1]] for i in range(len(IN_SPLITS))]
    gq, gk, gv, gg, ga, nq, nk, nv, aq, ak, av = seg
    aq = aq.reshape(D_MODEL, H_GQA, HEAD_DIM)[:, np.array(GQA_SLOT_HEADS), :].reshape(D_MODEL, GQA_QW)
    ga = jnp.pad(ga, ((0, 0), (0, 128 - 2 * GLA_LOWRANK)))
    return jnp.concatenate([gq, gk, gv, gg, nq, nk, nv, aq, ak, av, ga], axis=1).astype(BF16)


def _prep_w_a(w_a2, b_a):
    wa = jnp.zeros((128, 2 * GLA_W), F32)
    wa = wa.at[0:GLA_LOWRANK, 0:GLA_W].set(w_a2[0])
    wa = wa.at[GLA_LOWRANK:2 * GLA_LOWRANK, GLA_W:].set(w_a2[1])
    return wa.astype(BF16), b_a.reshape(1, 2 * GLA_W)


def _prep_w_out(w_out):
    gqa = w_out[GLA_W + NAT_W:].reshape(H_GQA, HEAD_DIM, D_MODEL)[np.array(GQA_SLOT_HEADS)]
    return jnp.concatenate([w_out[:GLA_W + NAT_W], gqa.reshape(GQA_QW, D_MODEL)], 0).astype(BF16)


def _state_to_blockdiag(s):
    out = jnp.zeros(s.shape[:-3] + (GLA_W, GLA_W), F32)
    for h in range(H_GLA):
        out = out.at[..., h * GLA_DV:(h + 1) * GLA_DV, h * GLA_DK:(h + 1) * GLA_DK].set(
            jnp.swapaxes(s[..., h, :, :], -1, -2))
    return out


def _blockdiag_to_state(st):
    blocks = [st[..., h * GLA_DV:(h + 1) * GLA_DV, h * GLA_DK:(h + 1) * GLA_DK] for h in range(H_GLA)]
    return jnp.swapaxes(jnp.stack(blocks, axis=-3), -1, -2)


def kernel(x_prompt, x_sample, state_gla, cache_nat_k, cache_nat_v, cache_gqa_k, cache_gqa_v, c, c_ctx, w_mod, b_mod, w_in, gla_w_a2, gla_b_a, gla_norm_g, nat_rpb, gqa_q_norm_g, gqa_k_norm_g, w_out, ln1_g, ln1_b, w_router, router_bias, w_expert_gu, w_expert_down, w_shared_gu, w_shared_down, ln2_g, ln2_b):
    x = jnp.concatenate([x_prompt.reshape(N_CTX, D_MODEL), x_sample.reshape(N_LAT, D_MODEL)], axis=0)
    cond = jnp.concatenate([c_ctx[None, :], c, jnp.zeros((N_COND - 1 - DEC_BATCH, D_MODEL), F32)], axis=0)
    mods = _modulation(cond, w_mod, b_mod)
    cos_t, sin_t = _rope_tables()
    row = lambda a: a.reshape(1, -1)

    st_gla, st_nk, st_nv, st_gk, st_gv = [], [], [], [], []
    for l in range(DEPTH):
        sh1, sc1, g1, sh2, sc2, g2 = [mods[l, :, j * D_MODEL:(j + 1) * D_MODEL] for j in range(6)]
        wa, ba = _prep_w_a(gla_w_a2[l], gla_b_a[l])
        qg = row(jnp.tile(gqa_q_norm_g[l], 128 // HEAD_DIM))
        kg = row(jnp.tile(gqa_k_norm_g[l], 128 // HEAD_DIM))
        pgla, la, pnat, q, k, v = _in_proj(x, sh1, sc1, _prep_w_in(w_in[l]), wa, ba, qg, kg, cos_t, sin_t)

        zero_st = jnp.zeros((BATCH, 2, GLA_W, GLA_W), F32)
        of_c, ob_c, st_c = _gla(pgla, la, zero_st, 0, BATCH, SEQ)
        of_l, ob_l, _ = _gla(pgla, la, _state_to_blockdiag(state_gla[:, l]), N_CTX, DEC_BATCH, DEC_SEQ)

        onat_c, ogqa_c = _ctx_attention(pnat, q, k, v)
        k_all = jnp.concatenate([k[N_CTX:].reshape(DEC_BATCH, DEC_SEQ, GQA_KW),
                                 cache_gqa_k[:, l].reshape(DEC_BATCH, PAST_LEN, GQA_KW)], axis=1).astype(BF16)
        v_all = jnp.concatenate([v[N_CTX:].reshape(DEC_BATCH, DEC_SEQ, GQA_KW),
                                 cache_gqa_v[:, l].reshape(DEC_BATCH, PAST_LEN, GQA_KW)], axis=1).astype(BF16)
        ogqa_l = _gqa_latent(q, k_all, v_all)
        onat_l = _nat_latent(pnat, cache_nat_k[:, l].reshape(DEC_BATCH, PAST_LEN, NAT_W),
                             cache_nat_v[:, l].reshape(DEC_BATCH, PAST_LEN, NAT_W),
                             _nat_bias_table(nat_rpb[l]))

        x1, xm, row_t, gate_t, pieces = _out_proj(
            x, (of_c, of_l), (ob_c, ob_l), pgla, (onat_c, onat_l), (ogqa_c, ogqa_l),
            _prep_w_out(w_out[l]), row(jnp.tile(gla_norm_g[l], H_GLA)), g1, row(ln1_g[l]), row(ln1_b[l]),
            sh2, sc2, w_router[l].T.astype(BF16), router_bias[l].reshape(N_EXPERTS, 1))
        plan = _moe_plan(pieces[:, ::128].T)
        y = _moe_experts(l, plan, _moe_dispatch(plan, xm, row_t), w_expert_gu, w_expert_down)
        x = _moe_combine(plan, y, row_t, gate_t, xm, x1, g2, w_shared_gu[l].astype(BF16),
                         w_shared_down[l].astype(BF16), row(ln2_g[l]), row(ln2_b[l]))

        st_gla.append(_blockdiag_to_state(st_c))
        st_nk.append(pnat[:N_CTX, NAT_W:2 * NAT_W].astype(F32).reshape(BATCH, SEQ, H_NAT, HEAD_DIM))
        st_nv.append(pnat[:N_CTX, 2 * NAT_W:].astype(F32).reshape(BATCH, SEQ, H_NAT, HEAD_DIM))
        st_gk.append(k[:N_CTX].reshape(BATCH, SEQ, KV_GQA, HEAD_DIM))
        st_gv.append(v[:N_CTX].reshape(BATCH, SEQ, KV_GQA, HEAD_DIM))

    y_prompt = x[:N_CTX].reshape(BATCH, SEQ, D_MODEL)
    y_sample = x[N_CTX:].reshape(DEC_BATCH, DEC_SEQ, D_MODEL)
    return (y_prompt, y_sample, jnp.stack(st_gla, axis=1), jnp.stack(st_nk, axis=1), jnp.stack(st_nv, axis=1),
            jnp.stack(st_gk, axis=1), jnp.stack(st_gv, axis=1))
```

```python
import functools

import numpy as np
import jax
import jax.numpy as jnp
from jax import lax
from jax.experimental import pallas as pl
from jax.experimental.pallas import tpu as pltpu

D_MODEL = 1024
BATCH = 16
SEQ = 256
DEPTH = 2
DEC_BATCH = 2
DEC_SEQ = 4096
PAST_LEN = 512
GRID_W = 64
HEAD_DIM = 64
H_GLA = 4
GLA_DK = 64
GLA_DV = 64
GLA_LOWRANK = 16
GLA_TAU = 16.0
H_NAT = 4
NAT_KH = 8
NAT_KW = 16
H_GQA = 8
KV_GQA = 2
ROPE_THETA = 10000.0
N_EXPERTS = 64
TOP_K = 8
N_GROUPS = 8
TOPK_GROUPS = 4
D_EXPERT = 256
D_SHARED = 256
ROUTE_SCALE = 2.5
EPS = 1e-6
ALPHA = (2 * DEPTH) ** 0.25
IN_SPLITS = (H_GLA * GLA_DK, H_GLA * GLA_DK, H_GLA * GLA_DV, H_GLA * GLA_DV, 2 * GLA_LOWRANK,
             H_NAT * HEAD_DIM, H_NAT * HEAD_DIM, H_NAT * HEAD_DIM,
             H_GQA * HEAD_DIM, KV_GQA * HEAD_DIM, KV_GQA * HEAD_DIM)

F32 = jnp.float32
BF16 = jnp.bfloat16

N_CTX = BATCH * SEQ
N_LAT = DEC_BATCH * DEC_SEQ
N_TOK = N_CTX + N_LAT
N_COND = 8
GLA_W = H_GLA * GLA_DK
NAT_W = H_NAT * HEAD_DIM
GQA_QW = H_GQA * HEAD_DIM
GQA_KW = KV_GQA * HEAD_DIM
LANES = 128
W_IN_COLS = 4 * GLA_W + 3 * NAT_W + GQA_QW + 2 * GQA_KW + LANES
GA_COL = W_IN_COLS - LANES
GQA_SLOT_HEADS = (0, 4, 1, 5, 2, 6, 3, 7)
GLA_CHUNK = 32
GLA_TB = 256
ROWS = DEC_SEQ // GRID_W
NAT_ROWS_PER_STEP = 8
VMEM_LIMIT = 56 * 1024 * 1024

TM_PROJ = 256
MOE_TILE = 256
MOE_NT = N_TOK // MOE_TILE
PIECE = 16
COPY_ROWS = 2 * PIECE
MOE_CAP = MOE_TILE * TOP_K + N_EXPERTS * PIECE
MAX_BIG = MOE_CAP // COPY_ROWS
MOE_CHUNK = 512
LOOP_UNROLL = 8
ROW_TILE = 512
XS_ROWS = N_TOK * TOP_K + MOE_NT * N_EXPERTS * (PIECE - 1) + N_EXPERTS * (ROW_TILE - PIECE)
assert XS_ROWS % ROW_TILE == 0
MAX_ZERO = N_EXPERTS * (ROW_TILE // PIECE - 1)
TQ_GQA = 256


def _dot(a, b):
    return jnp.dot(a.astype(BF16), b.astype(BF16), preferred_element_type=F32)


def _dot_nt(a, b):
    return lax.dot_general(a.astype(BF16), b.astype(BF16), (((1,), (1,)), ((), ())),
                           preferred_element_type=F32)


def _dot_tn(a, b):
    return lax.dot_general(a.astype(BF16), b.astype(BF16), (((0,), (0,)), ((), ())),
                           preferred_element_type=F32)


def _dot_split(a, b_bf16):
    hi = a.astype(BF16)
    lo = (a - hi.astype(F32)).astype(BF16)
    return (jnp.dot(hi, b_bf16, preferred_element_type=F32)
            + jnp.dot(lo, b_bf16, preferred_element_type=F32))


def _sigmoid(x):
    return 1.0 / (1.0 + jnp.exp(-x))


def _silu(x):
    return x * _sigmoid(x)


def _ln(x):
    xc = x - jnp.mean(x, axis=-1, keepdims=True)
    return xc * lax.rsqrt(jnp.mean(xc * xc, axis=-1, keepdims=True) + EPS)


def _lane_group(shape, axis, width):
    return lax.shift_right_logical(lax.broadcasted_iota(jnp.int32, shape, axis), int(np.log2(width)))


def _head_mean_sq(x):
    w = x.shape[-1]
    bmat = jnp.where(_lane_group((w, w), 0, HEAD_DIM) == _lane_group((w, w), 1, HEAD_DIM),
                     1.0 / HEAD_DIM, 0.0).astype(BF16)
    return _dot_split(x * x, bmat)


def _swap_half_heads(x):
    w = x.shape[-1]
    lane = lax.broadcasted_iota(jnp.int32, x.shape, x.ndim - 1)
    first = (lane & (HEAD_DIM - 1)) < HEAD_DIM // 2
    return jnp.where(first, pltpu.roll(x, w - HEAD_DIM // 2, x.ndim - 1),
                     pltpu.roll(x, HEAD_DIM // 2, x.ndim - 1))


def _cond_row(tile, tm):
    r = tile * tm
    return jnp.where(r < N_CTX, 0, 1 + (r - N_CTX) // DEC_SEQ)


def _stack_heads(x, n, width):
    head = _lane_group(x.shape, 1, width)
    return jnp.concatenate([jnp.where(head == h, x, jnp.zeros_like(x)) for h in range(n)], axis=0)


def _unstack_heads(o, n, width):
    m = o.shape[0] // n
    head = _lane_group((m, o.shape[1]), 1, width)
    out = jnp.zeros((m, o.shape[1]), o.dtype)
    for h in range(n):
        out = jnp.where(head == h, o[h * m:(h + 1) * m], out)
    return out


def _mod_kernel(c_ref, w_ref, b_ref, o_ref):
    o_ref[0] = _dot(_silu(c_ref[...]), w_ref[0]) + b_ref[0]


def _modulation(cond, w_mod, b_mod):
    tn = 1536
    return pl.pallas_call(
        _mod_kernel,
        grid=(DEPTH, 6 * D_MODEL // tn),
        in_specs=[pl.BlockSpec((N_COND, D_MODEL), lambda l, j: (0, 0)),
                  pl.BlockSpec((1, D_MODEL, tn), lambda l, j: (l, 0, j)),
                  pl.BlockSpec((1, 1, tn), lambda l, j: (l, 0, j))],
        out_specs=pl.BlockSpec((1, N_COND, tn), lambda l, j: (l, 0, j)),
        out_shape=jax.ShapeDtypeStruct((DEPTH, N_COND, 6 * D_MODEL), F32),
        compiler_params=pltpu.CompilerParams(dimension_semantics=("arbitrary", "arbitrary"),
                                             vmem_limit_bytes=VMEM_LIMIT),
        name="modulation",
    )(cond, w_mod, b_mod.reshape(DEPTH, 1, 6 * D_MODEL))


def _inproj_kernel(x_ref, sh_ref, sc_ref, w_ref, wa_ref, ba_ref, qg_ref, kg_ref, cos_ref, sin_ref,
                   pgla_ref, la_ref, pnat_ref, q_ref, k_ref, v_ref):
    ci = _cond_row(pl.program_id(0), TM_PROJ)
    xn = _ln(x_ref[...])
    xm = xn * (1.0 + sc_ref[pl.ds(ci, 1), :]) + sh_ref[pl.ds(ci, 1), :]
    acc = _dot(xm, w_ref[...])
    pgla_ref[...] = acc[:, 0:4 * GLA_W]
    pnat_ref[...] = acc[:, 4 * GLA_W:4 * GLA_W + 3 * NAT_W].astype(BF16)
    z = _dot(acc[:, GA_COL:GA_COL + LANES], wa_ref[...]) + ba_ref[...]
    la_ref[...] = (jnp.minimum(z, 0.0) - jnp.log(1.0 + jnp.exp(-jnp.abs(z)))) * (1.0 / GLA_TAU)

    c0 = 4 * GLA_W + 3 * NAT_W
    cos = cos_ref[...]
    sin = sin_ref[...]

    def norm_rope(a, g):
        an = a * lax.rsqrt(_head_mean_sq(a) + EPS) * g
        return an * cos + _swap_half_heads(an) * sin

    qs = [norm_rope(acc[:, c0 + j * LANES:c0 + (j + 1) * LANES], qg_ref[...]) for j in range(GQA_QW // LANES)]
    q_ref[...] = (jnp.concatenate(qs, axis=1) * (HEAD_DIM ** -0.5)).astype(BF16)
    k_ref[...] = norm_rope(acc[:, c0 + GQA_QW:c0 + GQA_QW + GQA_KW], kg_ref[...])
    v_ref[...] = acc[:, c0 + GQA_QW + GQA_KW:c0 + GQA_QW + 2 * GQA_KW]


def _in_proj(x, sh, sc, w, wa, ba, qg, kg, cos_t, sin_t):
    tm = TM_PROJ
    full = lambda a: pl.BlockSpec(a.shape, lambda i: (0,) * a.ndim)
    rows = lambda w_: pl.BlockSpec((tm, w_), lambda i: (i, 0))
    return pl.pallas_call(
        _inproj_kernel,
        grid=(N_TOK // tm,),
        in_specs=[rows(D_MODEL), full(sh), full(sc), full(w), full(wa), full(ba), full(qg), full(kg),
                  rows(LANES), rows(LANES)],
        out_specs=[rows(4 * GLA_W), rows(2 * GLA_W), rows(3 * NAT_W), rows(GQA_QW), rows(GQA_KW),
                   rows(GQA_KW)],
        out_shape=[jax.ShapeDtypeStruct((N_TOK, 4 * GLA_W), F32),
                   jax.ShapeDtypeStruct((N_TOK, 2 * GLA_W), F32),
                   jax.ShapeDtypeStruct((N_TOK, 3 * NAT_W), BF16),
                   jax.ShapeDtypeStruct((N_TOK, GQA_QW), BF16),
                   jax.ShapeDtypeStruct((N_TOK, GQA_KW), F32),
                   jax.ShapeDtypeStruct((N_TOK, GQA_KW), F32)],
        compiler_params=pltpu.CompilerParams(dimension_semantics=("arbitrary",),
                                             vmem_limit_bytes=VMEM_LIMIT),
        name="in_proj",
    )(x, sh, sc, w, wa, ba, qg, kg, cos_t, sin_t)


def _gla_kernel(qf_ref, kf_ref, vf_ref, laf_ref, qb_ref, kb_ref, vb_ref, lab_ref, st0_ref,
                of_ref, ob_ref, stout_ref, st_ref):
    t = pl.program_id(1)
    last_t = pl.num_programs(1) - 1
    C = GLA_CHUNK
    nchunk = GLA_TB // C

    @pl.when(t == 0)
    def _():
        st_ref[...] = st0_ref[0]

    tb = GLA_TB
    shift = int(np.log2(C))
    r = lax.broadcasted_iota(jnp.int32, (tb, tb), 0)
    c = lax.broadcasted_iota(jnp.int32, (tb, tb), 1)
    same = lax.shift_right_logical(r, shift) == lax.shift_right_logical(c, shift)
    ra = lax.broadcasted_iota(jnp.int32, (tb, H_GLA * tb), 0)
    ca = lax.broadcasted_iota(jnp.int32, (tb, H_GLA * tb), 1) & (tb - 1)
    same_a = lax.shift_right_logical(ra, shift) == lax.shift_right_logical(ca, shift)
    blk = _lane_group((H_GLA * tb, GLA_W), 0, tb) == _lane_group((H_GLA * tb, GLA_W), 1, GLA_DK)
    diag = _lane_group((GLA_W, GLA_W), 0, GLA_DV) == _lane_group((GLA_W, GLA_W), 1, GLA_DK)
    one = lambda m: jnp.where(m, 1.0, 0.0).astype(BF16)
    chunk_ones = one(same)
    ref_pick = one(c == (r & ~(C - 1)) + C // 2)

    def direction(q_ref, k_ref, v_ref, la_ref, o_ref, d, causal):
        q = q_ref[...] * (GLA_DK ** -0.5)
        k = k_ref[...]
        v = v_ref[...]
        la = la_ref[...]
        la_hi = la.astype(BF16)
        la_lo = (la - la_hi.astype(F32)).astype(BF16)
        csum = lambda m: (jnp.dot(m, la_hi, preferred_element_type=F32)
                          + jnp.dot(m, la_lo, preferred_element_type=F32))
        b = csum(one(same & ((r >= c) if causal else (r <= c))))
        btot = csum(chunk_ones)
        bref = jnp.dot(ref_pick, b.astype(BF16), preferred_element_type=F32)
        q_att = q * jnp.exp(b - bref)
        k_att = k * jnp.exp(bref - b)
        kblk = jnp.where(blk, jnp.concatenate([k_att] * H_GLA, axis=0), 0.0)
        att = _dot_nt(q_att, kblk)
        att = jnp.where(same_a & ((ra >= ca) if causal else (ra <= ca)), att, 0.0)
        vblk = jnp.where(blk, jnp.concatenate([v] * H_GLA, axis=0), 0.0)
        o_intra = _dot(att, vblk)

        q_dec = (q * jnp.exp(b)).astype(BF16)
        k_dec = (k * jnp.exp(btot - b)).astype(BF16)
        g = jnp.exp(btot)
        vb = v.astype(BF16)
        st = st_ref[d]
        o_inter = [None] * nchunk
        for ci in (range(nchunk) if causal else reversed(range(nchunk))):
            rows = slice(ci * C, (ci + 1) * C)
            o_inter[ci] = _dot_nt(q_dec[rows], st)
            u = _dot_tn(vb[rows], k_dec[rows])
            st = st * g[ci * C:ci * C + 1, :] + jnp.where(diag, u, 0.0)
        st_ref[d] = st
        o_ref[...] = o_intra + jnp.concatenate(o_inter, axis=0)

    direction(qf_ref, kf_ref, vf_ref, laf_ref, of_ref, 0, True)
    direction(qb_ref, kb_ref, vb_ref, lab_ref, ob_ref, 1, False)

    @pl.when(t == last_t)
    def _():
        stout_ref[0] = st_ref[...]


def _gla(pgla, la, st0, row0, n_seq, seq_len):
    tb = GLA_TB
    nt = seq_len // tb
    b0 = row0 // tb
    fwd = lambda col: pl.BlockSpec((tb, GLA_W), lambda s, t: (b0 + s * nt + t, col))
    bwd = lambda col: pl.BlockSpec((tb, GLA_W), lambda s, t: (b0 + s * nt + nt - 1 - t, col))
    st_spec = pl.BlockSpec((1, 2, GLA_W, GLA_W), lambda s, t: (s, 0, 0, 0))
    n = n_seq * seq_len
    return pl.pallas_call(
        _gla_kernel,
        grid=(n_seq, nt),
        in_specs=[fwd(0), fwd(1), fwd(2), fwd(0), bwd(0), bwd(1), bwd(2), bwd(1), st_spec],
        out_specs=[pl.BlockSpec((tb, GLA_W), lambda s, t: (s * nt + t, 0)),
                   pl.BlockSpec((tb, GLA_W), lambda s, t: (s * nt + nt - 1 - t, 0)),
                   st_spec],
        out_shape=[jax.ShapeDtypeStruct((n, GLA_W), F32), jax.ShapeDtypeStruct((n, GLA_W), F32),
                   jax.ShapeDtypeStruct((n_seq, 2, GLA_W, GLA_W), F32)],
        scratch_shapes=[pltpu.VMEM((2, GLA_W, GLA_W), F32)],
        compiler_params=pltpu.CompilerParams(dimension_semantics=("arbitrary", "arbitrary"),
                                             vmem_limit_bytes=VMEM_LIMIT),
        name="gla_scan",
    )(pgla, pgla, pgla, la, pgla, pgla, pgla, la, st0)


def _softmax_pv(s_list, v_list):
    m = s_list[0].max(axis=-1, keepdims=True)
    for s in s_list[1:]:
        m = jnp.maximum(m, s.max(axis=-1, keepdims=True))
    acc = None
    l = None
    for s, v in zip(s_list, v_list):
        p = jnp.exp(s - m)
        pl_ = p.sum(axis=-1, keepdims=True)
        pv = _dot(p, v)
        acc = pv if acc is None else acc + pv
        l = pl_ if l is None else l + pl_
    return acc * (1.0 / l)


def _ctx_attn_kernel(pnat_ref, q_ref, k_ref, v_ref, onat_ref, ogqa_ref):
    nq = pnat_ref[:, 0:NAT_W] * (HEAD_DIM ** -0.5)
    nk = pnat_ref[:, NAT_W:2 * NAT_W].astype(BF16)
    nv = pnat_ref[:, 2 * NAT_W:3 * NAT_W].astype(BF16)
    o = _softmax_pv([_dot_nt(_stack_heads(nq, H_NAT, HEAD_DIM), nk)], [nv])
    onat_ref[...] = _unstack_heads(o, H_NAT, HEAD_DIM).astype(BF16)

    k = k_ref[...].astype(BF16)
    v = v_ref[...].astype(BF16)
    k2 = jnp.concatenate([k, k], axis=1)
    v2 = jnp.concatenate([v, v], axis=1)
    outs = []
    for half in range(2):
        q = q_ref[:, half * 256:(half + 1) * 256]
        o = _softmax_pv([_dot_nt(_stack_heads(q, 4, HEAD_DIM), k2)], [v2])
        outs.append(_unstack_heads(o, 4, HEAD_DIM))
    ogqa_ref[...] = jnp.concatenate(outs, axis=1).astype(BF16)


def _ctx_attention(pnat, q, k, v):
    rows = lambda w_: pl.BlockSpec((SEQ, w_), lambda i: (i, 0))
    return pl.pallas_call(
        _ctx_attn_kernel,
        grid=(BATCH,),
        in_specs=[rows(3 * NAT_W), rows(GQA_QW), rows(GQA_KW), rows(GQA_KW)],
        out_specs=[rows(NAT_W), rows(GQA_QW)],
        out_shape=[jax.ShapeDtypeStruct((N_CTX, NAT_W), BF16),
                   jax.ShapeDtypeStruct((N_CTX, GQA_QW), BF16)],
        compiler_params=pltpu.CompilerParams(dimension_semantics=("arbitrary",),
                                             vmem_limit_bytes=VMEM_LIMIT),
        name="ctx_attention",
    )(pnat, q, k, v)


def _gqa_lat_kernel(q_ref, k_ref, v_ref, o_ref):
    k = k_ref[0]
    v = v_ref[0]
    k2 = jnp.concatenate([k, k], axis=1)
    v2 = jnp.concatenate([v, v], axis=1)
    slot = _lane_group((TQ_GQA, 256), 1, HEAD_DIM)
    outs = []
    for half in range(2):
        q = q_ref[:, half * 256:(half + 1) * 256]
        out = jnp.zeros((TQ_GQA, 256), F32)
        for s in range(4):
            qs = jnp.where(slot == s, q, jnp.zeros_like(q))
            o = _softmax_pv([_dot_nt(qs, k2)], [v2])
            out = jnp.where(slot == s, o, out)
        outs.append(out)
    o_ref[...] = jnp.concatenate(outs, axis=1).astype(BF16)


def _gqa_latent(q, k_all, v_all):
    tq = TQ_GQA
    nq = DEC_SEQ // tq
    tk = k_all.shape[1]
    return pl.pallas_call(
        _gqa_lat_kernel,
        grid=(DEC_BATCH, nq),
        in_specs=[pl.BlockSpec((tq, GQA_QW), lambda b, i: (N_CTX // tq + b * nq + i, 0)),
                  pl.BlockSpec((1, tk, GQA_KW), lambda b, i: (b, 0, 0)),
                  pl.BlockSpec((1, tk, GQA_KW), lambda b, i: (b, 0, 0))],
        out_specs=pl.BlockSpec((tq, GQA_QW), lambda b, i: (b * nq + i, 0)),
        out_shape=jax.ShapeDtypeStruct((N_LAT, GQA_QW), BF16),
        compiler_params=pltpu.CompilerParams(dimension_semantics=("arbitrary", "arbitrary"),
                                             vmem_limit_bytes=VMEM_LIMIT),
        name="gqa_latent",
    )(q, k_all, v_all)


def _nat_lat_kernel(q_ref, k_ref, v_ref, kc_ref, vc_ref, bias_ref, o_ref):
    j = pl.program_id(1)
    kc = kc_ref[0].astype(BF16)
    vc = vc_ref[0].astype(BF16)
    for i in range(NAT_ROWS_PER_STEP):
        r = j * NAT_ROWS_PER_STEP + i
        r0 = jnp.clip(r - NAT_KH // 2, 0, ROWS - NAT_KH)
        win = pl.ds(pl.multiple_of(r0 * GRID_W, GRID_W), NAT_KH * GRID_W)
        q = q_ref[i * GRID_W:(i + 1) * GRID_W, :] * (HEAD_DIM ** -0.5)
        qs = _stack_heads(q, H_NAT, HEAD_DIM)
        s_loc = _dot_nt(qs, k_ref[win, :]) + bias_ref[r - r0]
        s_ctx = _dot_nt(qs, kc)
        o = _softmax_pv([s_loc, s_ctx], [v_ref[win, :], vc])
        o_ref[i * GRID_W:(i + 1) * GRID_W, :] = _unstack_heads(o, H_NAT, HEAD_DIM).astype(BF16)


def _nat_latent(pnat, kc, vc, bias):
    tq = NAT_ROWS_PER_STEP * GRID_W
    nq = DEC_SEQ // tq
    lat_blk = N_CTX // DEC_SEQ
    return pl.pallas_call(
        _nat_lat_kernel,
        grid=(DEC_BATCH, nq),
        in_specs=[pl.BlockSpec((tq, NAT_W), lambda b, j: (N_CTX // tq + b * nq + j, 0)),
                  pl.BlockSpec((DEC_SEQ, NAT_W), lambda b, j: (lat_blk + b, 1)),
                  pl.BlockSpec((DEC_SEQ, NAT_W), lambda b, j: (lat_blk + b, 2)),
                  pl.BlockSpec((1, PAST_LEN, NAT_W), lambda b, j: (b, 0, 0)),
                  pl.BlockSpec((1, PAST_LEN, NAT_W), lambda b, j: (b, 0, 0)),
                  pl.BlockSpec(bias.shape, lambda b, j: (0, 0, 0))],
        out_specs=pl.BlockSpec((tq, NAT_W), lambda b, j: (b * nq + j, 0)),
        out_shape=jax.ShapeDtypeStruct((N_LAT, NAT_W), BF16),
        compiler_params=pltpu.CompilerParams(dimension_semantics=("arbitrary", "arbitrary"),
                                             vmem_limit_bytes=VMEM_LIMIT),
        name="nat_latent",
    )(pnat, pnat, pnat, kc, vc, bias)


def _route_gates(logits_t, bias_ref):
    per = N_EXPERTS // N_GROUPS
    t = logits_t.shape[1]
    neg = -jnp.inf
    pos = lax.broadcasted_iota(jnp.int32, (per, t), 0)
    scores = [_sigmoid(logits_t[g * per:(g + 1) * per, :]) for g in range(N_GROUPS)]
    biased = [scores[g] + bias_ref[g * per:(g + 1) * per, :] for g in range(N_GROUPS)]

    grp = []
    for v in biased:
        m1 = jnp.max(v, axis=0, keepdims=True)
        i1 = jnp.min(jnp.where(v == m1, pos, per), axis=0, keepdims=True)
        m2 = jnp.max(jnp.where(pos == i1, neg, v), axis=0, keepdims=True)
        grp.append(m1 + m2)

    keep = [jnp.zeros((1, t), jnp.bool_) for _ in range(N_GROUPS)]
    for _ in range(TOPK_GROUPS):
        best = functools.reduce(jnp.maximum, grp)
        first = jnp.full((1, t), N_GROUPS, jnp.int32)
        for g in reversed(range(N_GROUPS)):
            first = jnp.where(grp[g] == best, g, first)
        for g in range(N_GROUPS):
            hit = first == g
            keep[g] = keep[g] | hit
            grp[g] = jnp.where(hit, neg, grp[g])

    cand = [jnp.where(keep[g], biased[g], neg) for g in range(N_GROUPS)]
    flat = [pos + g * per for g in range(N_GROUPS)]
    picks, weights = [], []
    for _ in range(TOP_K):
        best = functools.reduce(jnp.maximum, [jnp.max(v, axis=0, keepdims=True) for v in cand])
        first = functools.reduce(jnp.minimum, [
            jnp.min(jnp.where(cand[g] == best, flat[g], N_EXPERTS), axis=0, keepdims=True)
            for g in range(N_GROUPS)])
        score = jnp.zeros((1, t), F32)
        for g in range(N_GROUPS):
            hit = flat[g] == first
            score = score + jnp.sum(jnp.where(hit, scores[g], 0.0), axis=0, keepdims=True)
            cand[g] = jnp.where(hit, neg, cand[g])
        picks.append(first)
        weights.append(score)

    total = functools.reduce(lambda a, b: a + b, weights)
    gates = jnp.concatenate([v / total * ROUTE_SCALE for v in weights], axis=0)

    sel = [functools.reduce(lambda a, b: a | b, [flat[g] == k for k in picks]) for g in range(N_GROUPS)]
    sel = jnp.where(jnp.concatenate(sel, axis=0), 1.0, 0.0).astype(BF16)
    before = (lax.broadcasted_iota(jnp.int32, (t, t), 0) < lax.broadcasted_iota(jnp.int32, (t, t), 1))
    rank = jnp.dot(sel, jnp.where(before, 1.0, 0.0).astype(BF16), preferred_element_type=F32)
    count = jnp.sum(sel.astype(F32), axis=1, keepdims=True)
    pieces = jnp.floor((count + (PIECE - 1)) * (1.0 / PIECE))
    lower = (lax.broadcasted_iota(jnp.int32, (N_EXPERTS, N_EXPERTS), 1)
             < lax.broadcasted_iota(jnp.int32, (N_EXPERTS, N_EXPERTS), 0))
    start = PIECE * jnp.dot(jnp.where(lower, 1.0, 0.0).astype(BF16),
                            jnp.broadcast_to(pieces, (N_EXPERTS, LANES)).astype(BF16),
                            preferred_element_type=F32)[:, 0:1]
    row = rank + start
    rows = []
    for k in picks:
        r = jnp.zeros((1, t), F32)
        for g in range(N_GROUPS):
            r = r + jnp.sum(jnp.where(flat[g] == k, row[g * per:(g + 1) * per, :], 0.0), axis=0, keepdims=True)
        rows.append(r)
    return (jnp.concatenate(rows, axis=0).astype(jnp.int32), gates,
            jnp.broadcast_to(pieces, (N_EXPERTS, LANES)).astype(jnp.int32))


def _outproj_kernel(x_ref, ofc_ref, ofl_ref, obc_ref, obl_ref, gg_ref, onatc_ref, onatl_ref, ogqac_ref, ogqal_ref,
                    w_ref, ng_ref, g1_ref, lng_ref, lnb_ref, sh2_ref, sc2_ref, wr_ref, rb_ref, x1_ref, xm_ref,
                    row_ref, gate_ref, pieces_ref):
    ci = _cond_row(pl.program_id(0), TM_PROJ)
    is_ctx = pl.program_id(0) < N_CTX // TM_PROJ
    either = lambda c_ref, l_ref: jnp.where(is_ctx, c_ref[...], l_ref[...])
    og = either(ofc_ref, ofl_ref) + either(obc_ref, obl_ref)
    halves = [og[:, j * LANES:(j + 1) * LANES] for j in range(GLA_W // LANES)]
    ms = jnp.concatenate([_head_mean_sq(h) for h in halves], axis=1)
    ogla = og * lax.rsqrt(ms + EPS) * ng_ref[...] * _silu(gg_ref[...])
    mix = (_dot(ogla, w_ref[0:GLA_W, :]) + _dot(either(onatc_ref, onatl_ref), w_ref[GLA_W:GLA_W + NAT_W, :])
           + _dot(either(ogqac_ref, ogqal_ref), w_ref[GLA_W + NAT_W:, :]))
    x1 = _ln(ALPHA * x_ref[...] + g1_ref[pl.ds(ci, 1), :] * mix) * lng_ref[...] + lnb_ref[...]
    x1_ref[...] = x1
    xm = (_ln(x1) * (1.0 + sc2_ref[pl.ds(ci, 1), :]) + sh2_ref[pl.ds(ci, 1), :]).astype(BF16)
    xm_ref[...] = xm
    row_ref[...], gate_ref[...], pieces_ref[...] = _route_gates(_dot_nt(wr_ref[...], xm), rb_ref)


def _out_proj(x, of, ob, pgla, onat, ogqa, w, ng, g1, lng, lnb, sh2, sc2, wr, rb):
    tm = TM_PROJ
    n_ctx = N_CTX // tm
    full = lambda a: pl.BlockSpec(a.shape, lambda i: (0,) * a.ndim)
    rows = lambda w_: pl.BlockSpec((tm, w_), lambda i: (i, 0))
    ctx = lambda w_: pl.BlockSpec((tm, w_), lambda i: (jnp.minimum(i, n_ctx - 1), 0))
    lat = lambda w_: pl.BlockSpec((tm, w_), lambda i: (jnp.maximum(i - n_ctx, 0), 0))
    return pl.pallas_call(
        _outproj_kernel,
        grid=(N_TOK // tm,),
        in_specs=[rows(D_MODEL), ctx(GLA_W), lat(GLA_W), ctx(GLA_W), lat(GLA_W),
                  pl.BlockSpec((tm, GLA_W), lambda i: (i, 3)),
                  ctx(NAT_W), lat(NAT_W), ctx(GQA_QW), lat(GQA_QW), full(w), full(ng), full(g1), full(lng),
                  full(lnb), full(sh2), full(sc2), full(wr), full(rb)],
        out_specs=[rows(D_MODEL), rows(D_MODEL), pl.BlockSpec((TOP_K, tm), lambda i: (0, i)),
                   pl.BlockSpec((TOP_K, tm), lambda i: (0, i)),
                   pl.BlockSpec((N_EXPERTS, LANES), lambda i: (0, i))],
        out_shape=[jax.ShapeDtypeStruct((N_TOK, D_MODEL), F32),
                   jax.ShapeDtypeStruct((N_TOK, D_MODEL), BF16),
                   jax.ShapeDtypeStruct((TOP_K, N_TOK), jnp.int32),
                   jax.ShapeDtypeStruct((TOP_K, N_TOK), F32),
                   jax.ShapeDtypeStruct((N_EXPERTS, MOE_NT * LANES), jnp.int32)],
        compiler_params=pltpu.CompilerParams(dimension_semantics=("arbitrary",),
                                             vmem_limit_bytes=VMEM_LIMIT),
        name="out_proj",
    )(x, *of, *ob, pgla, *onat, *ogqa, w, ng, g1, lng, lnb, sh2, sc2, wr, rb)


def _plan_specs(n_prefetch, **kw):
    return pltpu.PrefetchScalarGridSpec(num_scalar_prefetch=n_prefetch, **kw)


def _for_each_copy(t, copies, fn):
    bsrc_ref, bdst_ref, ssrc_ref, sdst_ref, nbig_ref, nsmall_ref = copies

    def big(i):
        j = t * MAX_BIG + i
        fn(pl.multiple_of(bsrc_ref[j], PIECE), pl.multiple_of(bdst_ref[j], PIECE), COPY_ROWS)

    def small(i):
        j = t * N_EXPERTS + i
        fn(pl.multiple_of(ssrc_ref[j], PIECE), pl.multiple_of(sdst_ref[j], PIECE), PIECE)

    _unrolled_loop(nbig_ref[t], big)
    _unrolled_loop(nsmall_ref[t], small)


def _unrolled_loop(n, body):
    def group(g, c):
        for u in range(LOOP_UNROLL):
            body(g * LOOP_UNROLL + u)
        return c

    def single(i, c):
        body(i)
        return c

    n_group = n // LOOP_UNROLL
    lax.fori_loop(0, n_group, group, 0)
    lax.fori_loop(n_group * LOOP_UNROLL, n, single, 0)


def _wait_copies(n_big, n_small, make):
    _unrolled_loop(n_big, lambda i: make(COPY_ROWS).wait())
    _unrolled_loop(n_small, lambda i: make(PIECE).wait())


def _dispatch_kernel(bsrc_ref, bdst_ref, ssrc_ref, sdst_ref, nbig_ref, nsmall_ref, zdst_ref, tot_ref, xm_ref, ldt_ref,
                     xs_ref, stage_ref, zero_ref, sem_big, sem_small, tail_sem):
    copies = (bsrc_ref, bdst_ref, ssrc_ref, sdst_ref, nbig_ref, nsmall_ref)
    t = pl.program_id(0)
    nt = pl.num_programs(0)
    slot = lax.rem(t, 2)

    def run_copy(local_row, xs_row, rows, s):
        sem = sem_big if rows == COPY_ROWS else sem_small
        return pltpu.make_async_copy(stage_ref.at[s, pl.ds(local_row, rows)], xs_ref.at[pl.ds(xs_row, rows)],
                                     sem.at[s])

    def wait_tile(tt, s):
        _wait_copies(nbig_ref[tt], nsmall_ref[tt], lambda rows: run_copy(0, 0, rows, s))

    @pl.when(t >= 2)
    def _():
        wait_tile(t - 2, slot)

    xm = xm_ref[...]
    ld = ldt_ref[...].astype(jnp.int16)
    for c in range(MOE_CAP // MOE_CHUNK):
        p = (lax.broadcasted_iota(jnp.int32, (MOE_CHUNK, MOE_TILE), 0) + c * MOE_CHUNK).astype(jnp.int16)
        hit = p == ld[0:1, :]
        for k in range(1, TOP_K):
            hit = hit | (p == ld[k:k + 1, :])
        stage_ref[slot, c * MOE_CHUNK:(c + 1) * MOE_CHUNK, :] = jnp.dot(
            jnp.where(hit, jnp.ones((), BF16), jnp.zeros((), BF16)), xm, preferred_element_type=F32).astype(BF16)
    _for_each_copy(t, copies, lambda a, b, rows: run_copy(a, b, rows, slot).start())

    @pl.when(t == nt - 1)
    def _():
        zero_ref[...] = jnp.zeros_like(zero_ref)
        n_piece = tot_ref[1]
        first_free_tile = tot_ref[0] // ROW_TILE
        n_tile = XS_ROWS // ROW_TILE - first_free_tile

        def zero_piece(i):
            return pltpu.make_async_copy(zero_ref.at[pl.ds(0, PIECE)],
                                         xs_ref.at[pl.ds(pl.multiple_of(zdst_ref[i], PIECE), PIECE)],
                                         tail_sem.at[0])

        def zero_tile(i):
            row = pl.multiple_of((first_free_tile + i) * ROW_TILE, ROW_TILE)
            return pltpu.make_async_copy(zero_ref, xs_ref.at[pl.ds(row, ROW_TILE)], tail_sem.at[1])

        def run(n, make, op):
            _unrolled_loop(n, lambda i: op(make(i)))

        run(n_piece, zero_piece, lambda cp: cp.start())
        run(n_tile, zero_tile, lambda cp: cp.start())
        run(n_piece, zero_piece, lambda cp: cp.wait())
        run(n_tile, zero_tile, lambda cp: cp.wait())
        wait_tile(t - 1, 1 - slot)
        wait_tile(t, slot)


def _moe_dispatch(plan, xm, row_t):
    return pl.pallas_call(
        _dispatch_kernel,
        grid_spec=_plan_specs(
            8, grid=(MOE_NT,),
            in_specs=[pl.BlockSpec((MOE_TILE, D_MODEL), lambda t, *_: (t, 0)),
                      pl.BlockSpec((TOP_K, MOE_TILE), lambda t, *_: (0, t))],
            out_specs=pl.BlockSpec(memory_space=pl.ANY),
            scratch_shapes=[pltpu.VMEM((2, MOE_CAP, D_MODEL), BF16), pltpu.VMEM((ROW_TILE, D_MODEL), BF16),
                            pltpu.SemaphoreType.DMA((2,)), pltpu.SemaphoreType.DMA((2,)),
                            pltpu.SemaphoreType.DMA((2,))]),
        out_shape=jax.ShapeDtypeStruct((XS_ROWS, D_MODEL), BF16),
        compiler_params=pltpu.CompilerParams(dimension_semantics=("arbitrary",),
                                             vmem_limit_bytes=VMEM_LIMIT),
        name="moe_dispatch",
    )(*plan["copies"], plan["zero_dst"], plan["total"], xm, row_t)


def _experts_kernel(exp_ref, fresh_ref, used_ref, src_ref, xs_ref, wgu_ref, wd_ref, y_ref, wgu_bf_ref, wd_bf_ref):
    w = pl.program_id(0)

    @pl.when(fresh_ref[w] == 1)
    def _():
        wgu_bf_ref[...] = wgu_ref[0].astype(BF16)
        wd_bf_ref[...] = wd_ref[0].astype(BF16)

    @pl.when(used_ref[w] == 1)
    def _():
        ab = jnp.dot(xs_ref[...], wgu_bf_ref[...], preferred_element_type=F32)
        h = (_silu(ab[:, :D_EXPERT]) * ab[:, D_EXPERT:]).astype(BF16)
        y_ref[...] = jnp.dot(h, wd_bf_ref[...], preferred_element_type=F32).astype(BF16)

    @pl.when(used_ref[w] == 0)
    def _():
        y_ref[...] = jnp.zeros_like(y_ref)


def _moe_experts(layer, plan, xs, wgu, wd):
    return pl.pallas_call(
        _experts_kernel,
        grid_spec=_plan_specs(
            4, grid=(XS_ROWS // ROW_TILE,),
            in_specs=[pl.BlockSpec((ROW_TILE, D_MODEL), lambda w, ex, fresh, used, src: (src[w], 0)),
                      pl.BlockSpec((None, 1, D_MODEL, 2 * D_EXPERT), lambda w, ex, *_: (layer, ex[w], 0, 0)),
                      pl.BlockSpec((None, 1, D_EXPERT, D_MODEL), lambda w, ex, *_: (layer, ex[w], 0, 0))],
            out_specs=pl.BlockSpec((ROW_TILE, D_MODEL), lambda w, *_: (w, 0)),
            scratch_shapes=[pltpu.VMEM((D_MODEL, 2 * D_EXPERT), BF16), pltpu.VMEM((D_EXPERT, D_MODEL), BF16)]),
        out_shape=jax.ShapeDtypeStruct((XS_ROWS, D_MODEL), BF16),
        compiler_params=pltpu.CompilerParams(dimension_semantics=("arbitrary",),
                                             vmem_limit_bytes=VMEM_LIMIT),
        name="moe_experts",
    )(plan["tile_expert"], plan["tile_fresh"], plan["tile_used"], plan["tile_src"], xs, wgu, wd)


def _combine_kernel(bsrc_ref, bdst_ref, ssrc_ref, sdst_ref, nbig_ref, nsmall_ref, y_ref, ldt_ref, gate_ref, xm_ref,
                    x1_ref, g2_ref, wsgu_ref, wsd_ref, lng_ref, lnb_ref, out_ref, ybuf_ref, sem_big, sem_small):
    copies = (bsrc_ref, bdst_ref, ssrc_ref, sdst_ref, nbig_ref, nsmall_ref)
    t = pl.program_id(0)
    nt = pl.num_programs(0)
    slot = lax.rem(t, 2)
    ci = _cond_row(t, MOE_TILE)

    def run_copy(local_row, y_row, rows, s):
        sem = sem_big if rows == COPY_ROWS else sem_small
        return pltpu.make_async_copy(y_ref.at[pl.ds(y_row, rows)], ybuf_ref.at[s, pl.ds(local_row, rows)], sem.at[s])

    def fetch(tt, s):
        _for_each_copy(tt, copies, lambda a, b, rows: run_copy(a, b, rows, s).start())

    @pl.when(t == 0)
    def _():
        ybuf_ref[...] = jnp.zeros_like(ybuf_ref)
        fetch(0, 0)

    @pl.when(t + 1 < nt)
    def _():
        fetch(t + 1, 1 - slot)

    _wait_copies(nbig_ref[t], nsmall_ref[t], lambda rows: run_copy(0, 0, rows, slot))

    routed = jnp.zeros((MOE_TILE, D_MODEL), F32)
    ld = ldt_ref[...].astype(jnp.int16)
    gate = gate_ref[...].astype(BF16)
    for c in range(MOE_CAP // MOE_CHUNK):
        p = (lax.broadcasted_iota(jnp.int32, (MOE_CHUNK, MOE_TILE), 0) + c * MOE_CHUNK).astype(jnp.int16)
        wmat = jnp.zeros((MOE_CHUNK, MOE_TILE), BF16)
        for k in range(TOP_K):
            wmat = jnp.where(p == ld[k:k + 1, :], gate[k:k + 1, :], wmat)
        routed = routed + _dot_tn(wmat, ybuf_ref[slot, c * MOE_CHUNK:(c + 1) * MOE_CHUNK, :])

    sab = jnp.dot(xm_ref[...], wsgu_ref[...], preferred_element_type=F32)
    shared = _dot(_silu(sab[:, :D_SHARED]) * sab[:, D_SHARED:], wsd_ref[...])
    z = ALPHA * x1_ref[...] + g2_ref[pl.ds(ci, 1), :] * (routed + shared)
    out_ref[...] = _ln(z) * lng_ref[...] + lnb_ref[...]


def _moe_combine(plan, y, row_t, gate_t, xm, x1, g2, wsgu, wsd, lng, lnb):
    full = lambda a: pl.BlockSpec(a.shape, lambda t, *_: (0,) * a.ndim)
    rows = lambda w_: pl.BlockSpec((MOE_TILE, w_), lambda t, *_: (t, 0))
    picks = pl.BlockSpec((TOP_K, MOE_TILE), lambda t, *_: (0, t))
    return pl.pallas_call(
        _combine_kernel,
        grid_spec=_plan_specs(
            6, grid=(MOE_NT,),
            in_specs=[pl.BlockSpec(memory_space=pl.ANY), picks, picks, rows(D_MODEL), rows(D_MODEL),
                      full(g2), full(wsgu), full(wsd), full(lng), full(lnb)],
            out_specs=rows(D_MODEL),
            scratch_shapes=[pltpu.VMEM((2, MOE_CAP, D_MODEL), BF16), pltpu.SemaphoreType.DMA((2,)),
                            pltpu.SemaphoreType.DMA((2,))]),
        out_shape=jax.ShapeDtypeStruct((N_TOK, D_MODEL), F32),
        compiler_params=pltpu.CompilerParams(dimension_semantics=("arbitrary",),
                                             vmem_limit_bytes=VMEM_LIMIT),
        name="moe_combine",
    )(*plan["copies"], y, row_t, gate_t, xm, x1, g2, wsgu, wsd, lng, lnb)


def _moe_plan(npc):
    i32 = jnp.int32
    run = npc * PIECE
    lo = jnp.cumsum(run, axis=1) - run
    tot_e = jnp.sum(run, axis=0)
    region = (tot_e + ROW_TILE - 1) // ROW_TILE * ROW_TILE
    region_end = jnp.cumsum(region)
    off = region_end - region
    hs = off[None, :] + jnp.cumsum(run, axis=0) - run

    per_copy = COPY_ROWS // PIECE
    n_big = npc // per_copy
    odd = npc - n_big * per_copy

    def expand(count, width):
        end = jnp.cumsum(count, axis=1)
        i = jnp.arange(width, dtype=i32)[None, :, None]
        mine = (i >= (end - count)[:, None, :]) & (i < end[:, None, :])
        pick = lambda a: jnp.sum(jnp.where(mine, a[:, None, :], 0), axis=2)
        return pick, i[:, :, 0] - pick(end - count)

    pick_b, k_b = expand(n_big, MAX_BIG)
    pick_s, _ = expand(odd, N_EXPERTS)
    copies = [pick_b(lo) + k_b * COPY_ROWS, pick_b(hs) + k_b * COPY_ROWS,
              pick_s(lo + n_big * COPY_ROWS), pick_s(hs + n_big * COPY_ROWS)]
    copies = [c.reshape(-1).astype(i32) for c in copies] + [jnp.sum(n_big, axis=1).astype(i32),
                                                            jnp.sum(odd, axis=1).astype(i32)]

    n_zero = ((region - tot_e) // PIECE)[None, :]
    pick_z, k_z = expand(n_zero, MAX_ZERO)
    zero_dst = (pick_z((off + tot_e)[None, :]) + k_z * PIECE).reshape(-1).astype(i32)

    start = jnp.arange(XS_ROWS // ROW_TILE, dtype=i32) * ROW_TILE
    ex = jnp.minimum(jnp.sum((start[:, None] >= region_end[None, :]).astype(i32), axis=1), N_EXPERTS - 1)
    prev_ex = jnp.concatenate([jnp.full((1,), -1, i32), ex[:-1]])
    return dict(copies=copies, zero_dst=zero_dst,
                total=jnp.stack([region_end[-1], jnp.sum(n_zero)]).astype(i32),
                tile_expert=ex.astype(i32), tile_fresh=(ex != prev_ex).astype(i32),
                tile_used=(start < region_end[-1]).astype(i32),
                tile_src=(jnp.minimum(start, region_end[-1] - ROW_TILE) // ROW_TILE).astype(i32))


def _rope_tables():
    t = jnp.arange(DEC_SEQ)
    n_freq = HEAD_DIM // 4
    freqs = ROPE_THETA ** (-jnp.arange(n_freq, dtype=F32) / n_freq)
    ang = jnp.concatenate([(t // GRID_W).astype(F32)[:, None] * freqs,
                           (t % GRID_W).astype(F32)[:, None] * freqs], -1)
    cos, sin = jnp.cos(ang), jnp.sin(ang)
    cos_h = jnp.concatenate([cos, cos], -1)
    sin_h = jnp.concatenate([-sin, sin], -1)
    lat = lambda a: jnp.tile(a, (DEC_BATCH, LANES // HEAD_DIM))
    cos_t = jnp.concatenate([jnp.ones((N_CTX, LANES), F32), lat(cos_h)], 0)
    sin_t = jnp.concatenate([jnp.zeros((N_CTX, LANES), F32), lat(sin_h)], 0)
    return cos_t, sin_t


def _nat_bias_table(rpb):
    n_dr, n_dc = 2 * NAT_KH - 1, 2 * NAT_KW - 1
    cidx = np.arange(GRID_W)
    dc_idx = np.clip(cidx[None, :] - cidx[:, None] + NAT_KW - 1, 0, n_dc - 1)
    col_start = np.clip(cidx - NAT_KW // 2, 0, GRID_W - NAT_KW)
    col_in = (cidx[None, :] >= col_start[:, None]) & (cidx[None, :] < col_start[:, None] + NAT_KW)
    onehot = (dc_idx.reshape(1, -1) == np.arange(n_dc)[:, None]).astype(np.float32)
    t = jnp.dot(rpb.reshape(H_NAT * n_dr, n_dc), onehot, precision=lax.Precision.HIGHEST)
    t = jnp.where(col_in[None, None], t.reshape(H_NAT, n_dr, GRID_W, GRID_W), -jnp.inf)
    bias = jnp.stack([t[:, NAT_KH - 1 - p:2 * NAT_KH - 1 - p] for p in range(NAT_KH)], axis=0)
    return bias.transpose(0, 1, 3, 2, 4).reshape(NAT_KH, H_NAT * GRID_W, NAT_KH * GRID_W)


def _prep_w_in(w_in):
    cuts = np.cumsum((0,) + IN_SPLITS)
    seg = [w_in[:, cuts[i]:cuts[i + 1]] for i in range(len(IN_SPLITS))]
    gq, gk, gv, gg, ga, nq, nk, nv, aq, ak, av = seg
    aq = aq.reshape(D_MODEL, H_GQA, HEAD_DIM)[:, np.array(GQA_SLOT_HEADS), :].reshape(D_MODEL, GQA_QW)
    ga = jnp.pad(ga, ((0, 0), (0, LANES - 2 * GLA_LOWRANK)))
    return jnp.concatenate([gq, gk, gv, gg, nq, nk, nv, aq, ak, av, ga], axis=1).astype(BF16)


def _prep_w_a(w_a2, b_a):
    wa = jnp.zeros((LANES, 2 * GLA_W), F32)
    wa = wa.at[0:GLA_LOWRANK, 0:GLA_W].set(w_a2[0])
    wa = wa.at[GLA_LOWRANK:2 * GLA_LOWRANK, GLA_W:].set(w_a2[1])
    return wa.astype(BF16), b_a.reshape(1, 2 * GLA_W)


def _prep_w_out(w_out):
    gqa = w_out[GLA_W + NAT_W:].reshape(H_GQA, HEAD_DIM, D_MODEL)[np.array(GQA_SLOT_HEADS)]
    return jnp.concatenate([w_out[:GLA_W + NAT_W], gqa.reshape(GQA_QW, D_MODEL)], 0).astype(BF16)


def _state_to_blockdiag(s):
    out = jnp.zeros(s.shape[:-3] + (GLA_W, GLA_W), F32)
    for h in range(H_GLA):
        out = out.at[..., h * GLA_DV:(h + 1) * GLA_DV, h * GLA_DK:(h + 1) * GLA_DK].set(
            jnp.swapaxes(s[..., h, :, :], -1, -2))
    return out


def _blockdiag_to_state(st):
    blocks = [st[..., h * GLA_DV:(h + 1) * GLA_DV, h * GLA_DK:(h + 1) * GLA_DK] for h in range(H_GLA)]
    return jnp.swapaxes(jnp.stack(blocks, axis=-3), -1, -2)


def kernel(x_prompt, x_sample, state_gla, cache_nat_k, cache_nat_v, cache_gqa_k, cache_gqa_v, c, c_ctx, w_mod, b_mod, w_in, gla_w_a2, gla_b_a, gla_norm_g, nat_rpb, gqa_q_norm_g, gqa_k_norm_g, w_out, ln1_g, ln1_b, w_router, router_bias, w_expert_gu, w_expert_down, w_shared_gu, w_shared_down, ln2_g, ln2_b):
    x = jnp.concatenate([x_prompt.reshape(N_CTX, D_MODEL), x_sample.reshape(N_LAT, D_MODEL)], axis=0)
    cond = jnp.concatenate([c_ctx[None, :], c, jnp.zeros((N_COND - 1 - DEC_BATCH, D_MODEL), F32)], axis=0)
    mods = _modulation(cond, w_mod, b_mod)
    cos_t, sin_t = _rope_tables()
    row = lambda a: a.reshape(1, -1)

    st_gla, st_nk, st_nv, st_gk, st_gv = [], [], [], [], []
    for l in range(DEPTH):
        sh1, sc1, g1, sh2, sc2, g2 = [mods[l, :, j * D_MODEL:(j + 1) * D_MODEL] for j in range(6)]
        wa, ba = _prep_w_a(gla_w_a2[l], gla_b_a[l])
        qg = row(jnp.tile(gqa_q_norm_g[l], LANES // HEAD_DIM))
        kg = row(jnp.tile(gqa_k_norm_g[l], LANES // HEAD_DIM))
        pgla, la, pnat, q, k, v = _in_proj(x, sh1, sc1, _prep_w_in(w_in[l]), wa, ba, qg, kg, cos_t, sin_t)

        zero_st = jnp.zeros((BATCH, 2, GLA_W, GLA_W), F32)
        of_c, ob_c, st_c = _gla(pgla, la, zero_st, 0, BATCH, SEQ)
        of_l, ob_l, _ = _gla(pgla, la, _state_to_blockdiag(state_gla[:, l]), N_CTX, DEC_BATCH, DEC_SEQ)

        onat_c, ogqa_c = _ctx_attention(pnat, q, k, v)
        k_all = jnp.concatenate([k[N_CTX:].reshape(DEC_BATCH, DEC_SEQ, GQA_KW),
                                 cache_gqa_k[:, l].reshape(DEC_BATCH, PAST_LEN, GQA_KW)], axis=1).astype(BF16)
        v_all = jnp.concatenate([v[N_CTX:].reshape(DEC_BATCH, DEC_SEQ, GQA_KW),
                                 cache_gqa_v[:, l].reshape(DEC_BATCH, PAST_LEN, GQA_KW)], axis=1).astype(BF16)
        ogqa_l = _gqa_latent(q, k_all, v_all)
        onat_l = _nat_latent(pnat, cache_nat_k[:, l].reshape(DEC_BATCH, PAST_LEN, NAT_W),
                             cache_nat_v[:, l].reshape(DEC_BATCH, PAST_LEN, NAT_W),
                             _nat_bias_table(nat_rpb[l]))

        x1, xm, row_t, gate_t, pieces = _out_proj(
            x, (of_c, of_l), (ob_c, ob_l), pgla, (onat_c, onat_l), (ogqa_c, ogqa_l),
            _prep_w_out(w_out[l]), row(jnp.tile(gla_norm_g[l], H_GLA)), g1, row(ln1_g[l]), row(ln1_b[l]),
            sh2, sc2, w_router[l].T.astype(BF16), router_bias[l].reshape(N_EXPERTS, 1))
        plan = _moe_plan(pieces[:, ::LANES].T)
        y = _moe_experts(l, plan, _moe_dispatch(plan, xm, row_t), w_expert_gu, w_expert_down)
        x = _moe_combine(plan, y, row_t, gate_t, xm, x1, g2, w_shared_gu[l].astype(BF16),
                         w_shared_down[l].astype(BF16), row(ln2_g[l]), row(ln2_b[l]))

        st_gla.append(_blockdiag_to_state(st_c))
        st_nk.append(pnat[:N_CTX, NAT_W:2 * NAT_W].astype(F32).reshape(BATCH, SEQ, H_NAT, HEAD_DIM))
        st_nv.append(pnat[:N_CTX, 2 * NAT_W:].astype(F32).reshape(BATCH, SEQ, H_NAT, HEAD_DIM))
        st_gk.append(k[:N_CTX].reshape(BATCH, SEQ, KV_GQA, HEAD_DIM))
        st_gv.append(v[:N_CTX].reshape(BATCH, SEQ, KV_GQA, HEAD_DIM))

    y_prompt = x[:N_CTX].reshape(BATCH, SEQ, D_MODEL)
    y_sample = x[N_CTX:].reshape(DEC_BATCH, DEC_SEQ, D_MODEL)
    return (y_prompt, y_sample, jnp.stack(st_gla, axis=1), jnp.stack(st_nk, axis=1), jnp.stack(st_nv, axis=1),
            jnp.stack(st_gk, axis=1), jnp.stack(st_gv, axis=1))
```

```python
import functools

import numpy as np
import jax
import jax.numpy as jnp
from jax import lax
from jax.experimental import pallas as pl
from jax.experimental.pallas import tpu as pltpu

D_MODEL = 1024
BATCH = 16
SEQ = 256
DEPTH = 2
DEC_BATCH = 2
DEC_SEQ = 4096
PAST_LEN = 512
GRID_W = 64
HEAD_DIM = 64
H_GLA = 4
GLA_DK = 64
GLA_DV = 64
GLA_LOWRANK = 16
GLA_TAU = 16.0
H_NAT = 4
NAT_KH = 8
NAT_KW = 16
H_GQA = 8
KV_GQA = 2
ROPE_THETA = 10000.0
N_EXPERTS = 64
TOP_K = 8
N_GROUPS = 8
TOPK_GROUPS = 4
D_EXPERT = 256
D_SHARED = 256
ROUTE_SCALE = 2.5
EPS = 1e-6
ALPHA = (2 * DEPTH) ** 0.25
IN_SPLITS = (H_GLA * GLA_DK, H_GLA * GLA_DK, H_GLA * GLA_DV, H_GLA * GLA_DV, 2 * GLA_LOWRANK,
             H_NAT * HEAD_DIM, H_NAT * HEAD_DIM, H_NAT * HEAD_DIM,
             H_GQA * HEAD_DIM, KV_GQA * HEAD_DIM, KV_GQA * HEAD_DIM)

F32 = jnp.float32
BF16 = jnp.bfloat16

N_CTX = BATCH * SEQ
N_LAT = DEC_BATCH * DEC_SEQ
N_TOK = N_CTX + N_LAT
N_COND = 8
GLA_W = H_GLA * GLA_DK
NAT_W = H_NAT * HEAD_DIM
GQA_QW = H_GQA * HEAD_DIM
GQA_KW = KV_GQA * HEAD_DIM
LANES = 128
W_IN_COLS = 4 * GLA_W + 3 * NAT_W + GQA_QW + 2 * GQA_KW + LANES
GA_COL = W_IN_COLS - LANES
GQA_SLOT_HEADS = (0, 4, 1, 5, 2, 6, 3, 7)
GLA_CHUNK = 32
GLA_TB = 256
ROWS = DEC_SEQ // GRID_W
NAT_ROWS_PER_STEP = 8
VMEM_LIMIT = 56 * 1024 * 1024

TM_PROJ = 256
MOE_TILE = 256
MOE_NT = N_TOK // MOE_TILE
PIECE = 16
COPY_ROWS = 2 * PIECE
MOE_CAP = MOE_TILE * TOP_K + N_EXPERTS * PIECE
MAX_BIG = MOE_CAP // COPY_ROWS
MOE_CHUNK = 512
LOOP_UNROLL = 8
ROW_TILE = 512
XS_ROWS = N_TOK * TOP_K + MOE_NT * N_EXPERTS * (PIECE - 1) + N_EXPERTS * (ROW_TILE - PIECE)
assert XS_ROWS % ROW_TILE == 0
MAX_ZERO = N_EXPERTS * (ROW_TILE // PIECE - 1)
TQ_GQA = 256


def _dot(a, b):
    return jnp.dot(a.astype(BF16), b.astype(BF16), preferred_element_type=F32)


def _dot_nt(a, b):
    return lax.dot_general(a.astype(BF16), b.astype(BF16), (((1,), (1,)), ((), ())),
                           preferred_element_type=F32)


def _dot_tn(a, b):
    return lax.dot_general(a.astype(BF16), b.astype(BF16), (((0,), (0,)), ((), ())),
                           preferred_element_type=F32)


def _dot_split(a, b_bf16):
    hi = a.astype(BF16)
    lo = (a - hi.astype(F32)).astype(BF16)
    return (jnp.dot(hi, b_bf16, preferred_element_type=F32)
            + jnp.dot(lo, b_bf16, preferred_element_type=F32))


def _sigmoid(x):
    return 1.0 / (1.0 + jnp.exp(-x))


def _silu(x):
    return x * _sigmoid(x)


def _ln(x):
    xc = x - jnp.mean(x, axis=-1, keepdims=True)
    return xc * lax.rsqrt(jnp.mean(xc * xc, axis=-1, keepdims=True) + EPS)


def _lane_group(shape, axis, width):
    return lax.shift_right_logical(lax.broadcasted_iota(jnp.int32, shape, axis), int(np.log2(width)))


def _head_mean_sq(x):
    w = x.shape[-1]
    bmat = jnp.where(_lane_group((w, w), 0, HEAD_DIM) == _lane_group((w, w), 1, HEAD_DIM),
                     1.0 / HEAD_DIM, 0.0).astype(BF16)
    return _dot_split(x * x, bmat)


def _swap_half_heads(x):
    w = x.shape[-1]
    lane = lax.broadcasted_iota(jnp.int32, x.shape, x.ndim - 1)
    first = (lane & (HEAD_DIM - 1)) < HEAD_DIM // 2
    return jnp.where(first, pltpu.roll(x, w - HEAD_DIM // 2, x.ndim - 1),
                     pltpu.roll(x, HEAD_DIM // 2, x.ndim - 1))


def _cond_row(tile, tm):
    r = tile * tm
    return jnp.where(r < N_CTX, 0, 1 + (r - N_CTX) // DEC_SEQ)


def _stack_heads(x, n, width):
    head = _lane_group(x.shape, 1, width)
    return jnp.concatenate([jnp.where(head == h, x, jnp.zeros_like(x)) for h in range(n)], axis=0)


def _unstack_heads(o, n, width):
    m = o.shape[0] // n
    head = _lane_group((m, o.shape[1]), 1, width)
    out = jnp.zeros((m, o.shape[1]), o.dtype)
    for h in range(n):
        out = jnp.where(head == h, o[h * m:(h + 1) * m], out)
    return out


def _mod_kernel(c_ref, w_ref, b_ref, o_ref):
    o_ref[0] = _dot(_silu(c_ref[...]), w_ref[0]) + b_ref[0]


def _modulation(cond, w_mod, b_mod):
    tn = 1536
    return pl.pallas_call(
        _mod_kernel,
        grid=(DEPTH, 6 * D_MODEL // tn),
        in_specs=[pl.BlockSpec((N_COND, D_MODEL), lambda l, j: (0, 0)),
                  pl.BlockSpec((1, D_MODEL, tn), lambda l, j: (l, 0, j)),
                  pl.BlockSpec((1, 1, tn), lambda l, j: (l, 0, j))],
        out_specs=pl.BlockSpec((1, N_COND, tn), lambda l, j: (l, 0, j)),
        out_shape=jax.ShapeDtypeStruct((DEPTH, N_COND, 6 * D_MODEL), F32),
        compiler_params=pltpu.CompilerParams(dimension_semantics=("arbitrary", "arbitrary"),
                                             vmem_limit_bytes=VMEM_LIMIT),
        name="modulation",
    )(cond, w_mod, b_mod.reshape(DEPTH, 1, 6 * D_MODEL))


def _inproj_kernel(x_ref, sh_ref, sc_ref, w_ref, wa_ref, ba_ref, qg_ref, kg_ref, cos_ref, sin_ref,
                   pgla_ref, la_ref, pnat_ref, q_ref, k_ref, v_ref):
    ci = _cond_row(pl.program_id(0), TM_PROJ)
    xn = _ln(x_ref[...])
    xm = xn * (1.0 + sc_ref[pl.ds(ci, 1), :]) + sh_ref[pl.ds(ci, 1), :]
    acc = _dot(xm, w_ref[...])
    pgla_ref[...] = acc[:, 0:4 * GLA_W]
    pnat_ref[...] = acc[:, 4 * GLA_W:4 * GLA_W + 3 * NAT_W].astype(BF16)
    z = _dot(acc[:, GA_COL:GA_COL + LANES], wa_ref[...]) + ba_ref[...]
    la_ref[...] = (jnp.minimum(z, 0.0) - jnp.log(1.0 + jnp.exp(-jnp.abs(z)))) * (1.0 / GLA_TAU)

    c0 = 4 * GLA_W + 3 * NAT_W
    cos = cos_ref[...]
    sin = sin_ref[...]

    def norm_rope(a, g):
        an = a * lax.rsqrt(_head_mean_sq(a) + EPS) * g
        return an * cos + _swap_half_heads(an) * sin

    qs = [norm_rope(acc[:, c0 + j * LANES:c0 + (j + 1) * LANES], qg_ref[...]) for j in range(GQA_QW // LANES)]
    q_ref[...] = (jnp.concatenate(qs, axis=1) * (HEAD_DIM ** -0.5)).astype(BF16)
    k_ref[...] = norm_rope(acc[:, c0 + GQA_QW:c0 + GQA_QW + GQA_KW], kg_ref[...])
    v_ref[...] = acc[:, c0 + GQA_QW + GQA_KW:c0 + GQA_QW + 2 * GQA_KW]


def _in_proj(x, sh, sc, w, wa, ba, qg, kg, cos_t, sin_t):
    tm = TM_PROJ
    full = lambda a: pl.BlockSpec(a.shape, lambda i: (0,) * a.ndim)
    rows = lambda w_: pl.BlockSpec((tm, w_), lambda i: (i, 0))
    return pl.pallas_call(
        _inproj_kernel,
        grid=(N_TOK // tm,),
        in_specs=[rows(D_MODEL), full(sh), full(sc), full(w), full(wa), full(ba), full(qg), full(kg),
                  rows(LANES), rows(LANES)],
        out_specs=[rows(4 * GLA_W), rows(2 * GLA_W), rows(3 * NAT_W), rows(GQA_QW), rows(GQA_KW),
                   rows(GQA_KW)],
        out_shape=[jax.ShapeDtypeStruct((N_TOK, 4 * GLA_W), F32),
                   jax.ShapeDtypeStruct((N_TOK, 2 * GLA_W), F32),
                   jax.ShapeDtypeStruct((N_TOK, 3 * NAT_W), BF16),
                   jax.ShapeDtypeStruct((N_TOK, GQA_QW), BF16),
                   jax.ShapeDtypeStruct((N_TOK, GQA_KW), F32),
                   jax.ShapeDtypeStruct((N_TOK, GQA_KW), F32)],
        compiler_params=pltpu.CompilerParams(dimension_semantics=("arbitrary",),
                                             vmem_limit_bytes=VMEM_LIMIT),
        name="in_proj",
    )(x, sh, sc, w, wa, ba, qg, kg, cos_t, sin_t)


def _gla_kernel(qf_ref, kf_ref, vf_ref, laf_ref, qb_ref, kb_ref, vb_ref, lab_ref, st0_ref,
                of_ref, ob_ref, stout_ref, st_ref):
    t = pl.program_id(1)
    last_t = pl.num_programs(1) - 1
    C = GLA_CHUNK
    nchunk = GLA_TB // C

    @pl.when(t == 0)
    def _():
        st_ref[...] = st0_ref[0]

    tb = GLA_TB
    shift = int(np.log2(C))
    r = lax.broadcasted_iota(jnp.int32, (tb, tb), 0)
    c = lax.broadcasted_iota(jnp.int32, (tb, tb), 1)
    same = lax.shift_right_logical(r, shift) == lax.shift_right_logical(c, shift)
    ra = lax.broadcasted_iota(jnp.int32, (tb, H_GLA * tb), 0)
    ca = lax.broadcasted_iota(jnp.int32, (tb, H_GLA * tb), 1) & (tb - 1)
    same_a = lax.shift_right_logical(ra, shift) == lax.shift_right_logical(ca, shift)
    blk = _lane_group((H_GLA * tb, GLA_W), 0, tb) == _lane_group((H_GLA * tb, GLA_W), 1, GLA_DK)
    diag = _lane_group((GLA_W, GLA_W), 0, GLA_DV) == _lane_group((GLA_W, GLA_W), 1, GLA_DK)
    one = lambda m: jnp.where(m, 1.0, 0.0).astype(BF16)
    chunk_ones = one(same)
    ref_pick = one(c == (r & ~(C - 1)) + C // 2)

    def direction(q_ref, k_ref, v_ref, la_ref, o_ref, d, causal):
        q = q_ref[...] * (GLA_DK ** -0.5)
        k = k_ref[...]
        v = v_ref[...]
        la = la_ref[...]
        la_hi = la.astype(BF16)
        la_lo = (la - la_hi.astype(F32)).astype(BF16)
        csum = lambda m: (jnp.dot(m, la_hi, preferred_element_type=F32)
                          + jnp.dot(m, la_lo, preferred_element_type=F32))
        b = csum(one(same & ((r >= c) if causal else (r <= c))))
        btot = csum(chunk_ones)
        bref = jnp.dot(ref_pick, b.astype(BF16), preferred_element_type=F32)
        q_att = q * jnp.exp(b - bref)
        k_att = k * jnp.exp(bref - b)
        kblk = jnp.where(blk, jnp.concatenate([k_att] * H_GLA, axis=0), 0.0)
        att = _dot_nt(q_att, kblk)
        att = jnp.where(same_a & ((ra >= ca) if causal else (ra <= ca)), att, 0.0)
        vblk = jnp.where(blk, jnp.concatenate([v] * H_GLA, axis=0), 0.0)
        o_intra = _dot(att, vblk)

        q_dec = (q * jnp.exp(b)).astype(BF16)
        k_dec = (k * jnp.exp(btot - b)).astype(BF16)
        g = jnp.exp(btot)
        vb = v.astype(BF16)
        st = st_ref[d]
        o_inter = [None] * nchunk
        for ci in (range(nchunk) if causal else reversed(range(nchunk))):
            rows = slice(ci * C, (ci + 1) * C)
            o_inter[ci] = _dot_nt(q_dec[rows], st)
            u = _dot_tn(vb[rows], k_dec[rows])
            st = st * g[ci * C:ci * C + 1, :] + jnp.where(diag, u, 0.0)
        st_ref[d] = st
        o_ref[...] = o_intra + jnp.concatenate(o_inter, axis=0)

    direction(qf_ref, kf_ref, vf_ref, laf_ref, of_ref, 0, True)
    direction(qb_ref, kb_ref, vb_ref, lab_ref, ob_ref, 1, False)

    @pl.when(t == last_t)
    def _():
        stout_ref[0] = st_ref[...]


def _gla(pgla, la, st0, row0, n_seq, seq_len):
    tb = GLA_TB
    nt = seq_len // tb
    b0 = row0 // tb
    fwd = lambda col: pl.BlockSpec((tb, GLA_W), lambda s, t: (b0 + s * nt + t, col))
    bwd = lambda col: pl.BlockSpec((tb, GLA_W), lambda s, t: (b0 + s * nt + nt - 1 - t, col))
    st_spec = pl.BlockSpec((1, 2, GLA_W, GLA_W), lambda s, t: (s, 0, 0, 0))
    n = n_seq * seq_len
    return pl.pallas_call(
        _gla_kernel,
        grid=(n_seq, nt),
        in_specs=[fwd(0), fwd(1), fwd(2), fwd(0), bwd(0), bwd(1), bwd(2), bwd(1), st_spec],
        out_specs=[pl.BlockSpec((tb, GLA_W), lambda s, t: (s * nt + t, 0)),
                   pl.BlockSpec((tb, GLA_W), lambda s, t: (s * nt + nt - 1 - t, 0)),
                   st_spec],
        out_shape=[jax.ShapeDtypeStruct((n, GLA_W), F32), jax.ShapeDtypeStruct((n, GLA_W), F32),
                   jax.ShapeDtypeStruct((n_seq, 2, GLA_W, GLA_W), F32)],
        scratch_shapes=[pltpu.VMEM((2, GLA_W, GLA_W), F32)],
        compiler_params=pltpu.CompilerParams(dimension_semantics=("arbitrary", "arbitrary"),
                                             vmem_limit_bytes=VMEM_LIMIT),
        name="gla_scan",
    )(pgla, pgla, pgla, la, pgla, pgla, pgla, la, st0)


def _softmax_pv(s_list, v_list):
    m = s_list[0].max(axis=-1, keepdims=True)
    for s in s_list[1:]:
        m = jnp.maximum(m, s.max(axis=-1, keepdims=True))
    acc = None
    l = None
    for s, v in zip(s_list, v_list):
        p = jnp.exp(s - m)
        pl_ = p.sum(axis=-1, keepdims=True)
        pv = _dot(p, v)
        acc = pv if acc is None else acc + pv
        l = pl_ if l is None else l + pl_
    return acc * (1.0 / l)


def _ctx_attn_kernel(pnat_ref, q_ref, k_ref, v_ref, onat_ref, ogqa_ref):
    nq = pnat_ref[:, 0:NAT_W] * (HEAD_DIM ** -0.5)
    nk = pnat_ref[:, NAT_W:2 * NAT_W].astype(BF16)
    nv = pnat_ref[:, 2 * NAT_W:3 * NAT_W].astype(BF16)
    o = _softmax_pv([_dot_nt(_stack_heads(nq, H_NAT, HEAD_DIM), nk)], [nv])
    onat_ref[...] = _unstack_heads(o, H_NAT, HEAD_DIM).astype(BF16)

    k = k_ref[...].astype(BF16)
    v = v_ref[...].astype(BF16)
    k2 = jnp.concatenate([k, k], axis=1)
    v2 = jnp.concatenate([v, v], axis=1)
    outs = []
    for half in range(2):
        q = q_ref[:, half * 256:(half + 1) * 256]
        o = _softmax_pv([_dot_nt(_stack_heads(q, 4, HEAD_DIM), k2)], [v2])
        outs.append(_unstack_heads(o, 4, HEAD_DIM))
    ogqa_ref[...] = jnp.concatenate(outs, axis=1).astype(BF16)


def _ctx_attention(pnat, q, k, v):
    rows = lambda w_: pl.BlockSpec((SEQ, w_), lambda i: (i, 0))
    return pl.pallas_call(
        _ctx_attn_kernel,
        grid=(BATCH,),
        in_specs=[rows(3 * NAT_W), rows(GQA_QW), rows(GQA_KW), rows(GQA_KW)],
        out_specs=[rows(NAT_W), rows(GQA_QW)],
        out_shape=[jax.ShapeDtypeStruct((N_CTX, NAT_W), BF16),
                   jax.ShapeDtypeStruct((N_CTX, GQA_QW), BF16)],
        compiler_params=pltpu.CompilerParams(dimension_semantics=("arbitrary",),
                                             vmem_limit_bytes=VMEM_LIMIT),
        name="ctx_attention",
    )(pnat, q, k, v)


def _gqa_lat_kernel(q_ref, k_ref, v_ref, o_ref):
    k = k_ref[0]
    v = v_ref[0]
    k2 = jnp.concatenate([k, k], axis=1)
    v2 = jnp.concatenate([v, v], axis=1)
    slot = _lane_group((TQ_GQA, 256), 1, HEAD_DIM)
    outs = []
    for half in range(2):
        q = q_ref[:, half * 256:(half + 1) * 256]
        out = jnp.zeros((TQ_GQA, 256), F32)
        for s in range(4):
            qs = jnp.where(slot == s, q, jnp.zeros_like(q))
            o = _softmax_pv([_dot_nt(qs, k2)], [v2])
            out = jnp.where(slot == s, o, out)
        outs.append(out)
    o_ref[...] = jnp.concatenate(outs, axis=1).astype(BF16)


def _gqa_latent(q, k_all, v_all):
    tq = TQ_GQA
    nq = DEC_SEQ // tq
    tk = k_all.shape[1]
    return pl.pallas_call(
        _gqa_lat_kernel,
        grid=(DEC_BATCH, nq),
        in_specs=[pl.BlockSpec((tq, GQA_QW), lambda b, i: (N_CTX // tq + b * nq + i, 0)),
                  pl.BlockSpec((1, tk, GQA_KW), lambda b, i: (b, 0, 0)),
                  pl.BlockSpec((1, tk, GQA_KW), lambda b, i: (b, 0, 0))],
        out_specs=pl.BlockSpec((tq, GQA_QW), lambda b, i: (b * nq + i, 0)),
        out_shape=jax.ShapeDtypeStruct((N_LAT, GQA_QW), BF16),
        compiler_params=pltpu.CompilerParams(dimension_semantics=("arbitrary", "arbitrary"),
                                             vmem_limit_bytes=VMEM_LIMIT),
        name="gqa_latent",
    )(q, k_all, v_all)


def _nat_lat_kernel(q_ref, k_ref, v_ref, kc_ref, vc_ref, bias_ref, o_ref):
    j = pl.program_id(1)
    kc = kc_ref[0].astype(BF16)
    vc = vc_ref[0].astype(BF16)
    for i in range(NAT_ROWS_PER_STEP):
        r = j * NAT_ROWS_PER_STEP + i
        r0 = jnp.clip(r - NAT_KH // 2, 0, ROWS - NAT_KH)
        win = pl.ds(pl.multiple_of(r0 * GRID_W, GRID_W), NAT_KH * GRID_W)
        q = q_ref[i * GRID_W:(i + 1) * GRID_W, :] * (HEAD_DIM ** -0.5)
        qs = _stack_heads(q, H_NAT, HEAD_DIM)
        s_loc = _dot_nt(qs, k_ref[win, :]) + bias_ref[r - r0]
        s_ctx = _dot_nt(qs, kc)
        o = _softmax_pv([s_loc, s_ctx], [v_ref[win, :], vc])
        o_ref[i * GRID_W:(i + 1) * GRID_W, :] = _unstack_heads(o, H_NAT, HEAD_DIM).astype(BF16)


def _nat_latent(pnat, kc, vc, bias):
    tq = NAT_ROWS_PER_STEP * GRID_W
    nq = DEC_SEQ // tq
    lat_blk = N_CTX // DEC_SEQ
    return pl.pallas_call(
        _nat_lat_kernel,
        grid=(DEC_BATCH, nq),
        in_specs=[pl.BlockSpec((tq, NAT_W), lambda b, j: (N_CTX // tq + b * nq + j, 0)),
                  pl.BlockSpec((DEC_SEQ, NAT_W), lambda b, j: (lat_blk + b, 1)),
                  pl.BlockSpec((DEC_SEQ, NAT_W), lambda b, j: (lat_blk + b, 2)),
                  pl.BlockSpec((1, PAST_LEN, NAT_W), lambda b, j: (b, 0, 0)),
                  pl.BlockSpec((1, PAST_LEN, NAT_W), lambda b, j: (b, 0, 0)),
                  pl.BlockSpec(bias.shape, lambda b, j: (0, 0, 0))],
        out_specs=pl.BlockSpec((tq, NAT_W), lambda b, j: (b * nq + j, 0)),
        out_shape=jax.ShapeDtypeStruct((N_LAT, NAT_W), BF16),
        compiler_params=pltpu.CompilerParams(dimension_semantics=("arbitrary", "arbitrary"),
                                             vmem_limit_bytes=VMEM_LIMIT),
        name="nat_latent",
    )(pnat, pnat, pnat, kc, vc, bias)


def _route_gates(logits_t, bias_ref):
    per = N_EXPERTS // N_GROUPS
    t = logits_t.shape[1]
    neg = -jnp.inf
    pos = lax.broadcasted_iota(jnp.int32, (per, t), 0)
    scores = [_sigmoid(logits_t[g * per:(g + 1) * per, :]) for g in range(N_GROUPS)]
    biased = [scores[g] + bias_ref[g * per:(g + 1) * per, :] for g in range(N_GROUPS)]

    grp = []
    for v in biased:
        m1 = jnp.max(v, axis=0, keepdims=True)
        i1 = jnp.min(jnp.where(v == m1, pos, per), axis=0, keepdims=True)
        m2 = jnp.max(jnp.where(pos == i1, neg, v), axis=0, keepdims=True)
        grp.append(m1 + m2)

    keep = [jnp.zeros((1, t), jnp.bool_) for _ in range(N_GROUPS)]
    for _ in range(TOPK_GROUPS):
        best = functools.reduce(jnp.maximum, grp)
        first = jnp.full((1, t), N_GROUPS, jnp.int32)
        for g in reversed(range(N_GROUPS)):
            first = jnp.where(grp[g] == best, g, first)
        for g in range(N_GROUPS):
            hit = first == g
            keep[g] = keep[g] | hit
            grp[g] = jnp.where(hit, neg, grp[g])

    cand = [jnp.where(keep[g], biased[g], neg) for g in range(N_GROUPS)]
    flat = [pos + g * per for g in range(N_GROUPS)]
    picks, weights = [], []
    for _ in range(TOP_K):
        best = functools.reduce(jnp.maximum, [jnp.max(v, axis=0, keepdims=True) for v in cand])
        first = functools.reduce(jnp.minimum, [
            jnp.min(jnp.where(cand[g] == best, flat[g], N_EXPERTS), axis=0, keepdims=True)
            for g in range(N_GROUPS)])
        score = jnp.zeros((1, t), F32)
        for g in range(N_GROUPS):
            hit = flat[g] == first
            score = score + jnp.sum(jnp.where(hit, scores[g], 0.0), axis=0, keepdims=True)
            cand[g] = jnp.where(hit, neg, cand[g])
        picks.append(first)
        weights.append(score)

    total = functools.reduce(lambda a, b: a + b, weights)
    gates = jnp.concatenate([v / total * ROUTE_SCALE for v in weights], axis=0)

    sel = [functools.reduce(lambda a, b: a | b, [flat[g] == k for k in picks]) for g in range(N_GROUPS)]
    sel = jnp.where(jnp.concatenate(sel, axis=0), 1.0, 0.0).astype(BF16)
    before = (lax.broadcasted_iota(jnp.int32, (t, t), 0) < lax.broadcasted_iota(jnp.int32, (t, t), 1))
    rank = jnp.dot(sel, jnp.where(before, 1.0, 0.0).astype(BF16), preferred_element_type=F32)
    count = jnp.sum(sel.astype(F32), axis=1, keepdims=True)
    pieces = jnp.floor((count + (PIECE - 1)) * (1.0 / PIECE))
    lower = (lax.broadcasted_iota(jnp.int32, (N_EXPERTS, N_EXPERTS), 1)
             < lax.broadcasted_iota(jnp.int32, (N_EXPERTS, N_EXPERTS), 0))
    start = PIECE * jnp.dot(jnp.where(lower, 1.0, 0.0).astype(BF16),
                            jnp.broadcast_to(pieces, (N_EXPERTS, LANES)).astype(BF16),
                            preferred_element_type=F32)[:, 0:1]
    row = rank + start
    rows = []
    for k in picks:
        r = jnp.zeros((1, t), F32)
        for g in range(N_GROUPS):
            r = r + jnp.sum(jnp.where(flat[g] == k, row[g * per:(g + 1) * per, :], 0.0), axis=0, keepdims=True)
        rows.append(r)
    return (jnp.concatenate(rows, axis=0).astype(jnp.int32), gates,
            jnp.broadcast_to(pieces, (N_EXPERTS, LANES)).astype(jnp.int32))


def _outproj_kernel(x_ref, ofc_ref, ofl_ref, obc_ref, obl_ref, gg_ref, onatc_ref, onatl_ref, ogqac_ref, ogqal_ref,
                    w_ref, ng_ref, g1_ref, lng_ref, lnb_ref, sh2_ref, sc2_ref, wr_ref, rb_ref, x1_ref, xm_ref,
                    row_ref, gate_ref, pieces_ref):
    ci = _cond_row(pl.program_id(0), TM_PROJ)
    is_ctx = pl.program_id(0) < N_CTX // TM_PROJ
    either = lambda c_ref, l_ref: jnp.where(is_ctx, c_ref[...], l_ref[...])
    og = either(ofc_ref, ofl_ref) + either(obc_ref, obl_ref)
    halves = [og[:, j * LANES:(j + 1) * LANES] for j in range(GLA_W // LANES)]
    ms = jnp.concatenate([_head_mean_sq(h) for h in halves], axis=1)
    ogla = og * lax.rsqrt(ms + EPS) * ng_ref[...] * _silu(gg_ref[...])
    mix = (_dot(ogla, w_ref[0:GLA_W, :]) + _dot(either(onatc_ref, onatl_ref), w_ref[GLA_W:GLA_W + NAT_W, :])
           + _dot(either(ogqac_ref, ogqal_ref), w_ref[GLA_W + NAT_W:, :]))
    x1 = _ln(ALPHA * x_ref[...] + g1_ref[pl.ds(ci, 1), :] * mix) * lng_ref[...] + lnb_ref[...]
    x1_ref[...] = x1
    xm = (_ln(x1) * (1.0 + sc2_ref[pl.ds(ci, 1), :]) + sh2_ref[pl.ds(ci, 1), :]).astype(BF16)
    xm_ref[...] = xm
    row_ref[...], gate_ref[...], pieces_ref[...] = _route_gates(_dot_nt(wr_ref[...], xm), rb_ref)


def _out_proj(x, of, ob, pgla, onat, ogqa, w, ng, g1, lng, lnb, sh2, sc2, wr, rb):
    tm = TM_PROJ
    n_ctx = N_CTX // tm
    full = lambda a: pl.BlockSpec(a.shape, lambda i: (0,) * a.ndim)
    rows = lambda w_: pl.BlockSpec((tm, w_), lambda i: (i, 0))
    ctx = lambda w_: pl.BlockSpec((tm, w_), lambda i: (jnp.minimum(i, n_ctx - 1), 0))
    lat = lambda w_: pl.BlockSpec((tm, w_), lambda i: (jnp.maximum(i - n_ctx, 0), 0))
    return pl.pallas_call(
        _outproj_kernel,
        grid=(N_TOK // tm,),
        in_specs=[rows(D_MODEL), ctx(GLA_W), lat(GLA_W), ctx(GLA_W), lat(GLA_W),
                  pl.BlockSpec((tm, GLA_W), lambda i: (i, 3)),
                  ctx(NAT_W), lat(NAT_W), ctx(GQA_QW), lat(GQA_QW), full(w), full(ng), full(g1), full(lng),
                  full(lnb), full(sh2), full(sc2), full(wr), full(rb)],
        out_specs=[rows(D_MODEL), rows(D_MODEL), pl.BlockSpec((TOP_K, tm), lambda i: (0, i)),
                   pl.BlockSpec((TOP_K, tm), lambda i: (0, i)),
                   pl.BlockSpec((N_EXPERTS, LANES), lambda i: (0, i))],
        out_shape=[jax.ShapeDtypeStruct((N_TOK, D_MODEL), F32),
                   jax.ShapeDtypeStruct((N_TOK, D_MODEL), BF16),
                   jax.ShapeDtypeStruct((TOP_K, N_TOK), jnp.int32),
                   jax.ShapeDtypeStruct((TOP_K, N_TOK), F32),
                   jax.ShapeDtypeStruct((N_EXPERTS, MOE_NT * LANES), jnp.int32)],
        compiler_params=pltpu.CompilerParams(dimension_semantics=("arbitrary",),
                                             vmem_limit_bytes=VMEM_LIMIT),
        name="out_proj",
    )(x, *of, *ob, pgla, *onat, *ogqa, w, ng, g1, lng, lnb, sh2, sc2, wr, rb)


def _plan_specs(n_prefetch, **kw):
    return pltpu.PrefetchScalarGridSpec(num_scalar_prefetch=n_prefetch, **kw)


def _for_each_copy(t, copies, fn):
    bsrc_ref, bdst_ref, ssrc_ref, sdst_ref, nbig_ref, nsmall_ref = copies

    def big(i):
        j = t * MAX_BIG + i
        fn(pl.multiple_of(bsrc_ref[j], PIECE), pl.multiple_of(bdst_ref[j], PIECE), COPY_ROWS)

    def small(i):
        j = t * N_EXPERTS + i
        fn(pl.multiple_of(ssrc_ref[j], PIECE), pl.multiple_of(sdst_ref[j], PIECE), PIECE)

    _unrolled_loop(nbig_ref[t], big)
    _unrolled_loop(nsmall_ref[t], small)


def _unrolled_loop(n, body):
    def group(g, c):
        for u in range(LOOP_UNROLL):
            body(g * LOOP_UNROLL + u)
        return c

    def single(i, c):
        body(i)
        return c

    n_group = n // LOOP_UNROLL
    lax.fori_loop(0, n_group, group, 0)
    lax.fori_loop(n_group * LOOP_UNROLL, n, single, 0)


def _wait_copies(n_big, n_small, make):
    _unrolled_loop(n_big, lambda i: make(COPY_ROWS).wait())
    _unrolled_loop(n_small, lambda i: make(PIECE).wait())


def _dispatch_kernel(bsrc_ref, bdst_ref, ssrc_ref, sdst_ref, nbig_ref, nsmall_ref, zdst_ref, tot_ref, xm_ref, ldt_ref,
                     xs_ref, stage_ref, zero_ref, sem_big, sem_small, tail_sem):
    copies = (bsrc_ref, bdst_ref, ssrc_ref, sdst_ref, nbig_ref, nsmall_ref)
    t = pl.program_id(0)
    nt = pl.num_programs(0)
    slot = lax.rem(t, 2)

    def run_copy(local_row, xs_row, rows, s):
        sem = sem_big if rows == COPY_ROWS else sem_small
        return pltpu.make_async_copy(stage_ref.at[s, pl.ds(local_row, rows)], xs_ref.at[pl.ds(xs_row, rows)],
                                     sem.at[s])

    def wait_tile(tt, s):
        _wait_copies(nbig_ref[tt], nsmall_ref[tt], lambda rows: run_copy(0, 0, rows, s))

    n_piece = tot_ref[1]
    first_free_tile = tot_ref[0] // ROW_TILE
    n_tile = XS_ROWS // ROW_TILE - first_free_tile

    def zero_piece(i):
        return pltpu.make_async_copy(zero_ref.at[pl.ds(0, PIECE)],
                                     xs_ref.at[pl.ds(pl.multiple_of(zdst_ref[i], PIECE), PIECE)], tail_sem.at[0])

    def zero_tile(i):
        row = pl.multiple_of((first_free_tile + i) * ROW_TILE, ROW_TILE)
        return pltpu.make_async_copy(zero_ref, xs_ref.at[pl.ds(row, ROW_TILE)], tail_sem.at[1])

    def run(n, make, op):
        _unrolled_loop(n, lambda i: op(make(i)))

    @pl.when(t == 0)
    def _():
        zero_ref[...] = jnp.zeros_like(zero_ref)
        run(n_piece, zero_piece, lambda cp: cp.start())
        run(n_tile, zero_tile, lambda cp: cp.start())

    @pl.when(t >= 2)
    def _():
        wait_tile(t - 2, slot)

    xm = xm_ref[...]
    ld = ldt_ref[...].astype(jnp.int16)
    for c in range(MOE_CAP // MOE_CHUNK):
        p = (lax.broadcasted_iota(jnp.int32, (MOE_CHUNK, MOE_TILE), 0) + c * MOE_CHUNK).astype(jnp.int16)
        hit = p == ld[0:1, :]
        for k in range(1, TOP_K):
            hit = hit | (p == ld[k:k + 1, :])
        stage_ref[slot, c * MOE_CHUNK:(c + 1) * MOE_CHUNK, :] = jnp.dot(
            jnp.where(hit, jnp.ones((), BF16), jnp.zeros((), BF16)), xm, preferred_element_type=F32).astype(BF16)
    _for_each_copy(t, copies, lambda a, b, rows: run_copy(a, b, rows, slot).start())

    @pl.when(t == nt - 1)
    def _():
        run(n_piece, zero_piece, lambda cp: cp.wait())
        run(n_tile, zero_tile, lambda cp: cp.wait())
        wait_tile(t - 1, 1 - slot)
        wait_tile(t, slot)


def _moe_dispatch(plan, xm, row_t):
    return pl.pallas_call(
        _dispatch_kernel,
        grid_spec=_plan_specs(
            8, grid=(MOE_NT,),
            in_specs=[pl.BlockSpec((MOE_TILE, D_MODEL), lambda t, *_: (t, 0)),
                      pl.BlockSpec((TOP_K, MOE_TILE), lambda t, *_: (0, t))],
            out_specs=pl.BlockSpec(memory_space=pl.ANY),
            scratch_shapes=[pltpu.VMEM((2, MOE_CAP, D_MODEL), BF16), pltpu.VMEM((ROW_TILE, D_MODEL), BF16),
                            pltpu.SemaphoreType.DMA((2,)), pltpu.SemaphoreType.DMA((2,)),
                            pltpu.SemaphoreType.DMA((2,))]),
        out_shape=jax.ShapeDtypeStruct((XS_ROWS, D_MODEL), BF16),
        compiler_params=pltpu.CompilerParams(dimension_semantics=("arbitrary",),
                                             vmem_limit_bytes=VMEM_LIMIT),
        name="moe_dispatch",
    )(*plan["copies"], plan["zero_dst"], plan["total"], xm, row_t)


def _experts_kernel(exp_ref, fresh_ref, used_ref, src_ref, xs_ref, wgu_ref, wd_ref, y_ref, wgu_bf_ref, wd_bf_ref):
    w = pl.program_id(0)

    @pl.when(fresh_ref[w] == 1)
    def _():
        wgu_bf_ref[...] = wgu_ref[0].astype(BF16)
        wd_bf_ref[...] = wd_ref[0].astype(BF16)

    @pl.when(used_ref[w] == 1)
    def _():
        ab = jnp.dot(xs_ref[...], wgu_bf_ref[...], preferred_element_type=F32)
        h = (_silu(ab[:, :D_EXPERT]) * ab[:, D_EXPERT:]).astype(BF16)
        y_ref[...] = jnp.dot(h, wd_bf_ref[...], preferred_element_type=F32).astype(BF16)

    @pl.when(used_ref[w] == 0)
    def _():
        y_ref[...] = jnp.zeros_like(y_ref)


def _moe_experts(layer, plan, xs, wgu, wd):
    return pl.pallas_call(
        _experts_kernel,
        grid_spec=_plan_specs(
            4, grid=(XS_ROWS // ROW_TILE,),
            in_specs=[pl.BlockSpec((ROW_TILE, D_MODEL), lambda w, ex, fresh, used, src: (src[w], 0)),
                      pl.BlockSpec((None, 1, D_MODEL, 2 * D_EXPERT), lambda w, ex, *_: (layer, ex[w], 0, 0)),
                      pl.BlockSpec((None, 1, D_EXPERT, D_MODEL), lambda w, ex, *_: (layer, ex[w], 0, 0))],
            out_specs=pl.BlockSpec((ROW_TILE, D_MODEL), lambda w, *_: (w, 0)),
            scratch_shapes=[pltpu.VMEM((D_MODEL, 2 * D_EXPERT), BF16), pltpu.VMEM((D_EXPERT, D_MODEL), BF16)]),
        out_shape=jax.ShapeDtypeStruct((XS_ROWS, D_MODEL), BF16),
        compiler_params=pltpu.CompilerParams(dimension_semantics=("arbitrary",),
                                             vmem_limit_bytes=VMEM_LIMIT),
        name="moe_experts",
    )(plan["tile_expert"], plan["tile_fresh"], plan["tile_used"], plan["tile_src"], xs, wgu, wd)


def _combine_kernel(bsrc_ref, bdst_ref, ssrc_ref, sdst_ref, nbig_ref, nsmall_ref, y_ref, ldt_ref, gate_ref, xm_ref,
                    x1_ref, g2_ref, wsgu_ref, wsd_ref, lng_ref, lnb_ref, out_ref, ybuf_ref, sem_big, sem_small):
    copies = (bsrc_ref, bdst_ref, ssrc_ref, sdst_ref, nbig_ref, nsmall_ref)
    t = pl.program_id(0)
    nt = pl.num_programs(0)
    slot = lax.rem(t, 2)
    ci = _cond_row(t, MOE_TILE)

    def run_copy(local_row, y_row, rows, s):
        sem = sem_big if rows == COPY_ROWS else sem_small
        return pltpu.make_async_copy(y_ref.at[pl.ds(y_row, rows)], ybuf_ref.at[s, pl.ds(local_row, rows)], sem.at[s])

    def fetch(tt, s):
        _for_each_copy(tt, copies, lambda a, b, rows: run_copy(a, b, rows, s).start())

    @pl.when(t == 0)
    def _():
        ybuf_ref[...] = jnp.zeros_like(ybuf_ref)
        fetch(0, 0)

    @pl.when(t + 1 < nt)
    def _():
        fetch(t + 1, 1 - slot)

    _wait_copies(nbig_ref[t], nsmall_ref[t], lambda rows: run_copy(0, 0, rows, slot))

    routed = jnp.zeros((MOE_TILE, D_MODEL), F32)
    ld = ldt_ref[...].astype(jnp.int16)
    gate = gate_ref[...].astype(BF16)
    for c in range(MOE_CAP // MOE_CHUNK):
        p = (lax.broadcasted_iota(jnp.int32, (MOE_CHUNK, MOE_TILE), 0) + c * MOE_CHUNK).astype(jnp.int16)
        wmat = jnp.zeros((MOE_CHUNK, MOE_TILE), BF16)
        for k in range(TOP_K):
            wmat = jnp.where(p == ld[k:k + 1, :], gate[k:k + 1, :], wmat)
        routed = routed + _dot_tn(wmat, ybuf_ref[slot, c * MOE_CHUNK:(c + 1) * MOE_CHUNK, :])

    sab = jnp.dot(xm_ref[...], wsgu_ref[...], preferred_element_type=F32)
    shared = _dot(_silu(sab[:, :D_SHARED]) * sab[:, D_SHARED:], wsd_ref[...])
    z = ALPHA * x1_ref[...] + g2_ref[pl.ds(ci, 1), :] * (routed + shared)
    out_ref[...] = _ln(z) * lng_ref[...] + lnb_ref[...]


def _moe_combine(plan, y, row_t, gate_t, xm, x1, g2, wsgu, wsd, lng, lnb):
    full = lambda a: pl.BlockSpec(a.shape, lambda t, *_: (0,) * a.ndim)
    rows = lambda w_: pl.BlockSpec((MOE_TILE, w_), lambda t, *_: (t, 0))
    picks = pl.BlockSpec((TOP_K, MOE_TILE), lambda t, *_: (0, t))
    return pl.pallas_call(
        _combine_kernel,
        grid_spec=_plan_specs(
            6, grid=(MOE_NT,),
            in_specs=[pl.BlockSpec(memory_space=pl.ANY), picks, picks, rows(D_MODEL), rows(D_MODEL),
                      full(g2), full(wsgu), full(wsd), full(lng), full(lnb)],
            out_specs=rows(D_MODEL),
            scratch_shapes=[pltpu.VMEM((2, MOE_CAP, D_MODEL), BF16), pltpu.SemaphoreType.DMA((2,)),
                            pltpu.SemaphoreType.DMA((2,))]),
        out_shape=jax.ShapeDtypeStruct((N_TOK, D_MODEL), F32),
        compiler_params=pltpu.CompilerParams(dimension_semantics=("arbitrary",),
                                             vmem_limit_bytes=VMEM_LIMIT),
        name="moe_combine",
    )(*plan["copies"], y, row_t, gate_t, xm, x1, g2, wsgu, wsd, lng, lnb)


def _moe_plan(npc):
    i32 = jnp.int32
    run = npc * PIECE
    lo = jnp.cumsum(run, axis=1) - run
    tot_e = jnp.sum(run, axis=0)
    region = (tot_e + ROW_TILE - 1) // ROW_TILE * ROW_TILE
    region_end = jnp.cumsum(region)
    off = region_end - region
    hs = off[None, :] + jnp.cumsum(run, axis=0) - run

    per_copy = COPY_ROWS // PIECE
    n_big = npc // per_copy
    odd = npc - n_big * per_copy

    def expand(count, width):
        end = jnp.cumsum(count, axis=1)
        i = jnp.arange(width, dtype=i32)[None, :, None]
        mine = (i >= (end - count)[:, None, :]) & (i < end[:, None, :])
        pick = lambda a: jnp.sum(jnp.where(mine, a[:, None, :], 0), axis=2)
        return pick, i[:, :, 0] - pick(end - count)

    pick_b, k_b = expand(n_big, MAX_BIG)
    pick_s, _ = expand(odd, N_EXPERTS)
    copies = [pick_b(lo) + k_b * COPY_ROWS, pick_b(hs) + k_b * COPY_ROWS,
              pick_s(lo + n_big * COPY_ROWS), pick_s(hs + n_big * COPY_ROWS)]
    copies = [c.reshape(-1).astype(i32) for c in copies] + [jnp.sum(n_big, axis=1).astype(i32),
                                                            jnp.sum(odd, axis=1).astype(i32)]

    n_zero = ((region - tot_e) // PIECE)[None, :]
    pick_z, k_z = expand(n_zero, MAX_ZERO)
    zero_dst = (pick_z((off + tot_e)[None, :]) + k_z * PIECE).reshape(-1).astype(i32)

    start = jnp.arange(XS_ROWS // ROW_TILE, dtype=i32) * ROW_TILE
    ex = jnp.minimum(jnp.sum((start[:, None] >= region_end[None, :]).astype(i32), axis=1), N_EXPERTS - 1)
    prev_ex = jnp.concatenate([jnp.full((1,), -1, i32), ex[:-1]])
    return dict(copies=copies, zero_dst=zero_dst,
                total=jnp.stack([region_end[-1], jnp.sum(n_zero)]).astype(i32),
                tile_expert=ex.astype(i32), tile_fresh=(ex != prev_ex).astype(i32),
                tile_used=(start < region_end[-1]).astype(i32),
                tile_src=(jnp.minimum(start, region_end[-1] - ROW_TILE) // ROW_TILE).astype(i32))


def _rope_tables():
    t = jnp.arange(DEC_SEQ)
    n_freq = HEAD_DIM // 4
    freqs = ROPE_THETA ** (-jnp.arange(n_freq, dtype=F32) / n_freq)
    ang = jnp.concatenate([(t // GRID_W).astype(F32)[:, None] * freqs,
                           (t % GRID_W).astype(F32)[:, None] * freqs], -1)
    cos, sin = jnp.cos(ang), jnp.sin(ang)
    cos_h = jnp.concatenate([cos, cos], -1)
    sin_h = jnp.concatenate([-sin, sin], -1)
    lat = lambda a: jnp.tile(a, (DEC_BATCH, LANES // HEAD_DIM))
    cos_t = jnp.concatenate([jnp.ones((N_CTX, LANES), F32), lat(cos_h)], 0)
    sin_t = jnp.concatenate([jnp.zeros((N_CTX, LANES), F32), lat(sin_h)], 0)
    return cos_t, sin_t


def _nat_bias_table(rpb):
    n_dr, n_dc = 2 * NAT_KH - 1, 2 * NAT_KW - 1
    cidx = np.arange(GRID_W)
    dc_idx = np.clip(cidx[None, :] - cidx[:, None] + NAT_KW - 1, 0, n_dc - 1)
    col_start = np.clip(cidx - NAT_KW // 2, 0, GRID_W - NAT_KW)
    col_in = (cidx[None, :] >= col_start[:, None]) & (cidx[None, :] < col_start[:, None] + NAT_KW)
    onehot = (dc_idx.reshape(1, -1) == np.arange(n_dc)[:, None]).astype(np.float32)
    t = jnp.dot(rpb.reshape(H_NAT * n_dr, n_dc), onehot, precision=lax.Precision.HIGHEST)
    t = jnp.where(col_in[None, None], t.reshape(H_NAT, n_dr, GRID_W, GRID_W), -jnp.inf)
    bias = jnp.stack([t[:, NAT_KH - 1 - p:2 * NAT_KH - 1 - p] for p in range(NAT_KH)], axis=0)
    return bias.transpose(0, 1, 3, 2, 4).reshape(NAT_KH, H_NAT * GRID_W, NAT_KH * GRID_W)


def _prep_w_in(w_in):
    cuts = np.cumsum((0,) + IN_SPLITS)
    seg = [w_in[:, cuts[i]:cuts[i + 1]] for i in range(len(IN_SPLITS))]
    gq, gk, gv, gg, ga, nq, nk, nv, aq, ak, av = seg
    aq = aq.reshape(D_MODEL, H_GQA, HEAD_DIM)[:, np.array(GQA_SLOT_HEADS), :].reshape(D_MODEL, GQA_QW)
    ga = jnp.pad(ga, ((0, 0), (0, LANES - 2 * GLA_LOWRANK)))
    return jnp.concatenate([gq, gk, gv, gg, nq, nk, nv, aq, ak, av, ga], axis=1).astype(BF16)


def _prep_w_a(w_a2, b_a):
    wa = jnp.zeros((LANES, 2 * GLA_W), F32)
    wa = wa.at[0:GLA_LOWRANK, 0:GLA_W].set(w_a2[0])
    wa = wa.at[GLA_LOWRANK:2 * GLA_LOWRANK, GLA_W:].set(w_a2[1])
    return wa.astype(BF16), b_a.reshape(1, 2 * GLA_W)


def _prep_w_out(w_out):
    gqa = w_out[GLA_W + NAT_W:].reshape(H_GQA, HEAD_DIM, D_MODEL)[np.array(GQA_SLOT_HEADS)]
    return jnp.concatenate([w_out[:GLA_W + NAT_W], gqa.reshape(GQA_QW, D_MODEL)], 0).astype(BF16)


def _state_to_blockdiag(s):
    out = jnp.zeros(s.shape[:-3] + (GLA_W, GLA_W), F32)
    for h in range(H_GLA):
        out = out.at[..., h * GLA_DV:(h + 1) * GLA_DV, h * GLA_DK:(h + 1) * GLA_DK].set(
            jnp.swapaxes(s[..., h, :, :], -1, -2))
    return out


def _blockdiag_to_state(st):
    blocks = [st[..., h * GLA_DV:(h + 1) * GLA_DV, h * GLA_DK:(h + 1) * GLA_DK] for h in range(H_GLA)]
    return jnp.swapaxes(jnp.stack(blocks, axis=-3), -1, -2)


def kernel(x_prompt, x_sample, state_gla, cache_nat_k, cache_nat_v, cache_gqa_k, cache_gqa_v, c, c_ctx, w_mod, b_mod, w_in, gla_w_a2, gla_b_a, gla_norm_g, nat_rpb, gqa_q_norm_g, gqa_k_norm_g, w_out, ln1_g, ln1_b, w_router, router_bias, w_expert_gu, w_expert_down, w_shared_gu, w_shared_down, ln2_g, ln2_b):
    x = jnp.concatenate([x_prompt.reshape(N_CTX, D_MODEL), x_sample.reshape(N_LAT, D_MODEL)], axis=0)
    cond = jnp.concatenate([c_ctx[None, :], c, jnp.zeros((N_COND - 1 - DEC_BATCH, D_MODEL), F32)], axis=0)
    mods = _modulation(cond, w_mod, b_mod)
    cos_t, sin_t = _rope_tables()
    row = lambda a: a.reshape(1, -1)

    st_gla, st_nk, st_nv, st_gk, st_gv = [], [], [], [], []
    for l in range(DEPTH):
        sh1, sc1, g1, sh2, sc2, g2 = [mods[l, :, j * D_MODEL:(j + 1) * D_MODEL] for j in range(6)]
        wa, ba = _prep_w_a(gla_w_a2[l], gla_b_a[l])
        qg = row(jnp.tile(gqa_q_norm_g[l], LANES // HEAD_DIM))
        kg = row(jnp.tile(gqa_k_norm_g[l], LANES // HEAD_DIM))
        pgla, la, pnat, q, k, v = _in_proj(x, sh1, sc1, _prep_w_in(w_in[l]), wa, ba, qg, kg, cos_t, sin_t)

        zero_st = jnp.zeros((BATCH, 2, GLA_W, GLA_W), F32)
        of_c, ob_c, st_c = _gla(pgla, la, zero_st, 0, BATCH, SEQ)
        of_l, ob_l, _ = _gla(pgla, la, _state_to_blockdiag(state_gla[:, l]), N_CTX, DEC_BATCH, DEC_SEQ)

        onat_c, ogqa_c = _ctx_attention(pnat, q, k, v)
        k_all = jnp.concatenate([k[N_CTX:].reshape(DEC_BATCH, DEC_SEQ, GQA_KW),
                                 cache_gqa_k[:, l].reshape(DEC_BATCH, PAST_LEN, GQA_KW)], axis=1).astype(BF16)
        v_all = jnp.concatenate([v[N_CTX:].reshape(DEC_BATCH, DEC_SEQ, GQA_KW),
                                 cache_gqa_v[:, l].reshape(DEC_BATCH, PAST_LEN, GQA_KW)], axis=1).astype(BF16)
        ogqa_l = _gqa_latent(q, k_all, v_all)
        onat_l = _nat_latent(pnat, cache_nat_k[:, l].reshape(DEC_BATCH, PAST_LEN, NAT_W),
                             cache_nat_v[:, l].reshape(DEC_BATCH, PAST_LEN, NAT_W),
                             _nat_bias_table(nat_rpb[l]))

        x1, xm, row_t, gate_t, pieces = _out_proj(
            x, (of_c, of_l), (ob_c, ob_l), pgla, (onat_c, onat_l), (ogqa_c, ogqa_l),
            _prep_w_out(w_out[l]), row(jnp.tile(gla_norm_g[l], H_GLA)), g1, row(ln1_g[l]), row(ln1_b[l]),
            sh2, sc2, w_router[l].T.astype(BF16), router_bias[l].reshape(N_EXPERTS, 1))
        plan = _moe_plan(pieces[:, ::LANES].T)
        y = _moe_experts(l, plan, _moe_dispatch(plan, xm, row_t), w_expert_gu, w_expert_down)
        x = _moe_combine(plan, y, row_t, gate_t, xm, x1, g2, w_shared_gu[l].astype(BF16),
                         w_shared_down[l].astype(BF16), row(ln2_g[l]), row(ln2_b[l]))

        st_gla.append(_blockdiag_to_state(st_c))
        st_nk.append(pnat[:N_CTX, NAT_W:2 * NAT_W].astype(F32).reshape(BATCH, SEQ, H_NAT, HEAD_DIM))
        st_nv.append(pnat[:N_CTX, 2 * NAT_W:].astype(F32).reshape(BATCH, SEQ, H_NAT, HEAD_DIM))
        st_gk.append(k[:N_CTX].reshape(BATCH, SEQ, KV_GQA, HEAD_DIM))
        st_gv.append(v[:N_CTX].reshape(BATCH, SEQ, KV_GQA, HEAD_DIM))

    y_prompt = x[:N_CTX].reshape(BATCH, SEQ, D_MODEL)
    y_sample = x[N_CTX:].reshape(DEC_BATCH, DEC_SEQ, D_MODEL)
    return (y_prompt, y_sample, jnp.stack(st_gla, axis=1), jnp.stack(st_nk, axis=1), jnp.stack(st_nv, axis=1),
            jnp.stack(st_gk, axis=1), jnp.stack(st_gv, axis=1))
```

```python
import functools

import numpy as np
import jax
import jax.numpy as jnp
from jax import lax
from jax.experimental import pallas as pl
from jax.experimental.pallas import tpu as pltpu

D_MODEL = 1024
BATCH = 16
SEQ = 256
DEPTH = 2
DEC_BATCH = 2
DEC_SEQ = 4096
PAST_LEN = 512
GRID_W = 64
HEAD_DIM = 64
H_GLA = 4
GLA_DK = 64
GLA_DV = 64
GLA_LOWRANK = 16
GLA_TAU = 16.0
H_NAT = 4
NAT_KH = 8
NAT_KW = 16
H_GQA = 8
KV_GQA = 2
ROPE_THETA = 10000.0
N_EXPERTS = 64
TOP_K = 8
N_GROUPS = 8
TOPK_GROUPS = 4
D_EXPERT = 256
D_SHARED = 256
ROUTE_SCALE = 2.5
EPS = 1e-6
ALPHA = (2 * DEPTH) ** 0.25
IN_SPLITS = (H_GLA * GLA_DK, H_GLA * GLA_DK, H_GLA * GLA_DV, H_GLA * GLA_DV, 2 * GLA_LOWRANK,
             H_NAT * HEAD_DIM, H_NAT * HEAD_DIM, H_NAT * HEAD_DIM,
             H_GQA * HEAD_DIM, KV_GQA * HEAD_DIM, KV_GQA * HEAD_DIM)

F32 = jnp.float32
BF16 = jnp.bfloat16

N_CTX = BATCH * SEQ
N_LAT = DEC_BATCH * DEC_SEQ
N_TOK = N_CTX + N_LAT
N_COND = 8
GLA_W = H_GLA * GLA_DK
NAT_W = H_NAT * HEAD_DIM
GQA_QW = H_GQA * HEAD_DIM
GQA_KW = KV_GQA * HEAD_DIM
LANES = 128
W_IN_COLS = 4 * GLA_W + 3 * NAT_W + GQA_QW + 2 * GQA_KW + LANES
GA_COL = W_IN_COLS - LANES
GQA_SLOT_HEADS = (0, 4, 1, 5, 2, 6, 3, 7)
GLA_CHUNK = 32
GLA_TB = 256
ROWS = DEC_SEQ // GRID_W
NAT_ROWS_PER_STEP = 8
VMEM_LIMIT = 56 * 1024 * 1024

TM_PROJ = 256
MOE_TILE = 256
MOE_NT = N_TOK // MOE_TILE
PIECE = 16
COPY_ROWS = 2 * PIECE
MOE_CAP = MOE_TILE * TOP_K + N_EXPERTS * PIECE
MAX_BIG = MOE_CAP // COPY_ROWS
MOE_CHUNK = 512
LOOP_UNROLL = 8
ROW_TILE = 512
XS_ROWS = N_TOK * TOP_K + MOE_NT * N_EXPERTS * (PIECE - 1) + N_EXPERTS * (ROW_TILE - PIECE)
assert XS_ROWS % ROW_TILE == 0
MAX_ZERO = N_EXPERTS * (ROW_TILE // PIECE - 1)
TQ_GQA = 256


def _dot(a, b):
    return jnp.dot(a.astype(BF16), b.astype(BF16), preferred_element_type=F32)


def _dot_nt(a, b):
    return lax.dot_general(a.astype(BF16), b.astype(BF16), (((1,), (1,)), ((), ())),
                           preferred_element_type=F32)


def _dot_tn(a, b):
    return lax.dot_general(a.astype(BF16), b.astype(BF16), (((0,), (0,)), ((), ())),
                           preferred_element_type=F32)


def _dot_split(a, b_bf16):
    hi = a.astype(BF16)
    lo = (a - hi.astype(F32)).astype(BF16)
    return (jnp.dot(hi, b_bf16, preferred_element_type=F32)
            + jnp.dot(lo, b_bf16, preferred_element_type=F32))


def _sigmoid(x):
    return 1.0 / (1.0 + jnp.exp(-x))


def _silu(x):
    return x * _sigmoid(x)


def _ln(x):
    xc = x - jnp.mean(x, axis=-1, keepdims=True)
    return xc * lax.rsqrt(jnp.mean(xc * xc, axis=-1, keepdims=True) + EPS)


def _lane_group(shape, axis, width):
    return lax.shift_right_logical(lax.broadcasted_iota(jnp.int32, shape, axis), int(np.log2(width)))


def _head_mean_sq(x):
    w = x.shape[-1]
    bmat = jnp.where(_lane_group((w, w), 0, HEAD_DIM) == _lane_group((w, w), 1, HEAD_DIM),
                     1.0 / HEAD_DIM, 0.0).astype(BF16)
    return _dot_split(x * x, bmat)


def _swap_half_heads(x):
    w = x.shape[-1]
    lane = lax.broadcasted_iota(jnp.int32, x.shape, x.ndim - 1)
    first = (lane & (HEAD_DIM - 1)) < HEAD_DIM // 2
    return jnp.where(first, pltpu.roll(x, w - HEAD_DIM // 2, x.ndim - 1),
                     pltpu.roll(x, HEAD_DIM // 2, x.ndim - 1))


def _cond_row(tile, tm):
    r = tile * tm
    return jnp.where(r < N_CTX, 0, 1 + (r - N_CTX) // DEC_SEQ)


def _stack_heads(x, n, width):
    head = _lane_group(x.shape, 1, width)
    return jnp.concatenate([jnp.where(head == h, x, jnp.zeros_like(x)) for h in range(n)], axis=0)


def _unstack_heads(o, n, width):
    m = o.shape[0] // n
    head = _lane_group((m, o.shape[1]), 1, width)
    out = jnp.zeros((m, o.shape[1]), o.dtype)
    for h in range(n):
        out = jnp.where(head == h, o[h * m:(h + 1) * m], out)
    return out


def _mod_kernel(c_ref, w_ref, b_ref, o_ref):
    o_ref[0] = _dot(_silu(c_ref[...]), w_ref[0]) + b_ref[0]


def _modulation(cond, w_mod, b_mod):
    tn = 1536
    return pl.pallas_call(
        _mod_kernel,
        grid=(DEPTH, 6 * D_MODEL // tn),
        in_specs=[pl.BlockSpec((N_COND, D_MODEL), lambda l, j: (0, 0)),
                  pl.BlockSpec((1, D_MODEL, tn), lambda l, j: (l, 0, j)),
                  pl.BlockSpec((1, 1, tn), lambda l, j: (l, 0, j))],
        out_specs=pl.BlockSpec((1, N_COND, tn), lambda l, j: (l, 0, j)),
        out_shape=jax.ShapeDtypeStruct((DEPTH, N_COND, 6 * D_MODEL), F32),
        compiler_params=pltpu.CompilerParams(dimension_semantics=("arbitrary", "arbitrary"),
                                             vmem_limit_bytes=VMEM_LIMIT),
        name="modulation",
    )(cond, w_mod, b_mod.reshape(DEPTH, 1, 6 * D_MODEL))


def _inproj_kernel(x_ref, sh_ref, sc_ref, w_ref, wa_ref, ba_ref, qg_ref, kg_ref, cos_ref, sin_ref,
                   pgla_ref, la_ref, pnat_ref, q_ref, k_ref, v_ref):
    ci = _cond_row(pl.program_id(0), TM_PROJ)
    xn = _ln(x_ref[...])
    xm = xn * (1.0 + sc_ref[pl.ds(ci, 1), :]) + sh_ref[pl.ds(ci, 1), :]
    acc = _dot(xm, w_ref[...])
    pgla_ref[...] = acc[:, 0:4 * GLA_W]
    pnat_ref[...] = acc[:, 4 * GLA_W:4 * GLA_W + 3 * NAT_W].astype(BF16)
    z = _dot(acc[:, GA_COL:GA_COL + LANES], wa_ref[...]) + ba_ref[...]
    la_ref[...] = (jnp.minimum(z, 0.0) - jnp.log(1.0 + jnp.exp(-jnp.abs(z)))) * (1.0 / GLA_TAU)

    c0 = 4 * GLA_W + 3 * NAT_W
    cos = cos_ref[...]
    sin = sin_ref[...]

    def norm_rope(a, g):
        an = a * lax.rsqrt(_head_mean_sq(a) + EPS) * g
        return an * cos + _swap_half_heads(an) * sin

    qs = [norm_rope(acc[:, c0 + j * LANES:c0 + (j + 1) * LANES], qg_ref[...]) for j in range(GQA_QW // LANES)]
    q_ref[...] = (jnp.concatenate(qs, axis=1) * (HEAD_DIM ** -0.5)).astype(BF16)
    k_ref[...] = norm_rope(acc[:, c0 + GQA_QW:c0 + GQA_QW + GQA_KW], kg_ref[...])
    v_ref[...] = acc[:, c0 + GQA_QW + GQA_KW:c0 + GQA_QW + 2 * GQA_KW]


def _in_proj(x, sh, sc, w, wa, ba, qg, kg, cos_t, sin_t):
    tm = TM_PROJ
    full = lambda a: pl.BlockSpec(a.shape, lambda i: (0,) * a.ndim)
    rows = lambda w_: pl.BlockSpec((tm, w_), lambda i: (i, 0))
    return pl.pallas_call(
        _inproj_kernel,
        grid=(N_TOK // tm,),
        in_specs=[rows(D_MODEL), full(sh), full(sc), full(w), full(wa), full(ba), full(qg), full(kg),
                  rows(LANES), rows(LANES)],
        out_specs=[rows(4 * GLA_W), rows(2 * GLA_W), rows(3 * NAT_W), rows(GQA_QW), rows(GQA_KW),
                   rows(GQA_KW)],
        out_shape=[jax.ShapeDtypeStruct((N_TOK, 4 * GLA_W), F32),
                   jax.ShapeDtypeStruct((N_TOK, 2 * GLA_W), F32),
                   jax.ShapeDtypeStruct((N_TOK, 3 * NAT_W), BF16),
                   jax.ShapeDtypeStruct((N_TOK, GQA_QW), BF16),
                   jax.ShapeDtypeStruct((N_TOK, GQA_KW), F32),
                   jax.ShapeDtypeStruct((N_TOK, GQA_KW), F32)],
        compiler_params=pltpu.CompilerParams(dimension_semantics=("arbitrary",),
                                             vmem_limit_bytes=VMEM_LIMIT),
        name="in_proj",
    )(x, sh, sc, w, wa, ba, qg, kg, cos_t, sin_t)


def _gla_kernel(qf_ref, kf_ref, vf_ref, laf_ref, qb_ref, kb_ref, vb_ref, lab_ref, st0_ref,
                of_ref, ob_ref, stout_ref, st_ref):
    t = pl.program_id(1)
    last_t = pl.num_programs(1) - 1
    C = GLA_CHUNK
    nchunk = GLA_TB // C

    @pl.when(t == 0)
    def _():
        st_ref[...] = st0_ref[0]

    tb = GLA_TB
    shift = int(np.log2(C))
    r = lax.broadcasted_iota(jnp.int32, (tb, tb), 0)
    c = lax.broadcasted_iota(jnp.int32, (tb, tb), 1)
    same = lax.shift_right_logical(r, shift) == lax.shift_right_logical(c, shift)
    ra = lax.broadcasted_iota(jnp.int32, (tb, H_GLA * tb), 0)
    ca = lax.broadcasted_iota(jnp.int32, (tb, H_GLA * tb), 1) & (tb - 1)
    same_a = lax.shift_right_logical(ra, shift) == lax.shift_right_logical(ca, shift)
    blk = _lane_group((H_GLA * tb, GLA_W), 0, tb) == _lane_group((H_GLA * tb, GLA_W), 1, GLA_DK)
    diag = _lane_group((GLA_W, GLA_W), 0, GLA_DV) == _lane_group((GLA_W, GLA_W), 1, GLA_DK)
    one = lambda m: jnp.where(m, 1.0, 0.0).astype(BF16)
    chunk_ones = one(same)
    ref_pick = one(c == (r & ~(C - 1)) + C // 2)

    def direction(q_ref, k_ref, v_ref, la_ref, o_ref, d, causal):
        q = q_ref[...] * (GLA_DK ** -0.5)
        k = k_ref[...]
        v = v_ref[...]
        la = la_ref[...]
        la_hi = la.astype(BF16)
        la_lo = (la - la_hi.astype(F32)).astype(BF16)
        csum = lambda m: (jnp.dot(m, la_hi, preferred_element_type=F32)
                          + jnp.dot(m, la_lo, preferred_element_type=F32))
        b = csum(one(same & ((r >= c) if causal else (r <= c))))
        btot = csum(chunk_ones)
        bref = jnp.dot(ref_pick, b.astype(BF16), preferred_element_type=F32)
        q_att = q * jnp.exp(b - bref)
        k_att = k * jnp.exp(bref - b)
        kblk = jnp.where(blk, jnp.concatenate([k_att] * H_GLA, axis=0), 0.0)
        att = _dot_nt(q_att, kblk)
        att = jnp.where(same_a & ((ra >= ca) if causal else (ra <= ca)), att, 0.0)
        vblk = jnp.where(blk, jnp.concatenate([v] * H_GLA, axis=0), 0.0)
        o_intra = _dot(att, vblk)

        q_dec = (q * jnp.exp(b)).astype(BF16)
        k_dec = (k * jnp.exp(btot - b)).astype(BF16)
        g = jnp.exp(btot)
        vb = v.astype(BF16)
        st = st_ref[d]
        o_inter = [None] * nchunk
        for ci in (range(nchunk) if causal else reversed(range(nchunk))):
            rows = slice(ci * C, (ci + 1) * C)
            o_inter[ci] = _dot_nt(q_dec[rows], st)
            u = _dot_tn(vb[rows], k_dec[rows])
            st = st * g[ci * C:ci * C + 1, :] + jnp.where(diag, u, 0.0)
        st_ref[d] = st
        o_ref[...] = o_intra + jnp.concatenate(o_inter, axis=0)

    direction(qf_ref, kf_ref, vf_ref, laf_ref, of_ref, 0, True)
    direction(qb_ref, kb_ref, vb_ref, lab_ref, ob_ref, 1, False)

    @pl.when(t == last_t)
    def _():
        stout_ref[0] = st_ref[...]


def _gla(pgla, la, st0, row0, n_seq, seq_len):
    tb = GLA_TB
    nt = seq_len // tb
    b0 = row0 // tb
    fwd = lambda col: pl.BlockSpec((tb, GLA_W), lambda s, t: (b0 + s * nt + t, col))
    bwd = lambda col: pl.BlockSpec((tb, GLA_W), lambda s, t: (b0 + s * nt + nt - 1 - t, col))
    st_spec = pl.BlockSpec((1, 2, GLA_W, GLA_W), lambda s, t: (s, 0, 0, 0))
    n = n_seq * seq_len
    return pl.pallas_call(
        _gla_kernel,
        grid=(n_seq, nt),
        in_specs=[fwd(0), fwd(1), fwd(2), fwd(0), bwd(0), bwd(1), bwd(2), bwd(1), st_spec],
        out_specs=[pl.BlockSpec((tb, GLA_W), lambda s, t: (s * nt + t, 0)),
                   pl.BlockSpec((tb, GLA_W), lambda s, t: (s * nt + nt - 1 - t, 0)),
                   st_spec],
        out_shape=[jax.ShapeDtypeStruct((n, GLA_W), F32), jax.ShapeDtypeStruct((n, GLA_W), F32),
                   jax.ShapeDtypeStruct((n_seq, 2, GLA_W, GLA_W), F32)],
        scratch_shapes=[pltpu.VMEM((2, GLA_W, GLA_W), F32)],
        compiler_params=pltpu.CompilerParams(dimension_semantics=("arbitrary", "arbitrary"),
                                             vmem_limit_bytes=VMEM_LIMIT),
        name="gla_scan",
    )(pgla, pgla, pgla, la, pgla, pgla, pgla, la, st0)


def _softmax_pv(s_list, v_list):
    m = s_list[0].max(axis=-1, keepdims=True)
    for s in s_list[1:]:
        m = jnp.maximum(m, s.max(axis=-1, keepdims=True))
    acc = None
    l = None
    for s, v in zip(s_list, v_list):
        p = jnp.exp(s - m)
        pl_ = p.sum(axis=-1, keepdims=True)
        pv = _dot(p, v)
        acc = pv if acc is None else acc + pv
        l = pl_ if l is None else l + pl_
    return acc * (1.0 / l)


def _ctx_attn_kernel(pnat_ref, q_ref, k_ref, v_ref, onat_ref, ogqa_ref):
    nq = pnat_ref[:, 0:NAT_W] * (HEAD_DIM ** -0.5)
    nk = pnat_ref[:, NAT_W:2 * NAT_W].astype(BF16)
    nv = pnat_ref[:, 2 * NAT_W:3 * NAT_W].astype(BF16)
    o = _softmax_pv([_dot_nt(_stack_heads(nq, H_NAT, HEAD_DIM), nk)], [nv])
    onat_ref[...] = _unstack_heads(o, H_NAT, HEAD_DIM).astype(BF16)

    k = k_ref[...].astype(BF16)
    v = v_ref[...].astype(BF16)
    k2 = jnp.concatenate([k, k], axis=1)
    v2 = jnp.concatenate([v, v], axis=1)
    outs = []
    for half in range(2):
        q = q_ref[:, half * 256:(half + 1) * 256]
        o = _softmax_pv([_dot_nt(_stack_heads(q, 4, HEAD_DIM), k2)], [v2])
        outs.append(_unstack_heads(o, 4, HEAD_DIM))
    ogqa_ref[...] = jnp.concatenate(outs, axis=1).astype(BF16)


def _ctx_attention(pnat, q, k, v):
    rows = lambda w_: pl.BlockSpec((SEQ, w_), lambda i: (i, 0))
    return pl.pallas_call(
        _ctx_attn_kernel,
        grid=(BATCH,),
        in_specs=[rows(3 * NAT_W), rows(GQA_QW), rows(GQA_KW), rows(GQA_KW)],
        out_specs=[rows(NAT_W), rows(GQA_QW)],
        out_shape=[jax.ShapeDtypeStruct((N_CTX, NAT_W), BF16),
                   jax.ShapeDtypeStruct((N_CTX, GQA_QW), BF16)],
        compiler_params=pltpu.CompilerParams(dimension_semantics=("arbitrary",),
                                             vmem_limit_bytes=VMEM_LIMIT),
        name="ctx_attention",
    )(pnat, q, k, v)


def _gqa_lat_kernel(q_ref, k_ref, v_ref, o_ref):
    k = k_ref[0]
    v = v_ref[0]
    k2 = jnp.concatenate([k, k], axis=1)
    v2 = jnp.concatenate([v, v], axis=1)
    slot = _lane_group((TQ_GQA, 256), 1, HEAD_DIM)
    outs = []
    for half in range(2):
        q = q_ref[:, half * 256:(half + 1) * 256]
        out = jnp.zeros((TQ_GQA, 256), F32)
        for s in range(4):
            qs = jnp.where(slot == s, q, jnp.zeros_like(q))
            o = _softmax_pv([_dot_nt(qs, k2)], [v2])
            out = jnp.where(slot == s, o, out)
        outs.append(out)
    o_ref[...] = jnp.concatenate(outs, axis=1).astype(BF16)


def _gqa_latent(q, k_all, v_all):
    tq = TQ_GQA
    nq = DEC_SEQ // tq
    tk = k_all.shape[1]
    return pl.pallas_call(
        _gqa_lat_kernel,
        grid=(DEC_BATCH, nq),
        in_specs=[pl.BlockSpec((tq, GQA_QW), lambda b, i: (N_CTX // tq + b * nq + i, 0)),
                  pl.BlockSpec((1, tk, GQA_KW), lambda b, i: (b, 0, 0)),
                  pl.BlockSpec((1, tk, GQA_KW), lambda b, i: (b, 0, 0))],
        out_specs=pl.BlockSpec((tq, GQA_QW), lambda b, i: (b * nq + i, 0)),
        out_shape=jax.ShapeDtypeStruct((N_LAT, GQA_QW), BF16),
        compiler_params=pltpu.CompilerParams(dimension_semantics=("arbitrary", "arbitrary"),
                                             vmem_limit_bytes=VMEM_LIMIT),
        name="gqa_latent",
    )(q, k_all, v_all)


def _nat_lat_kernel(q_ref, k_ref, v_ref, kc_ref, vc_ref, bias_ref, o_ref):
    j = pl.program_id(1)
    kc = kc_ref[0].astype(BF16)
    vc = vc_ref[0].astype(BF16)
    for i in range(NAT_ROWS_PER_STEP):
        r = j * NAT_ROWS_PER_STEP + i
        r0 = jnp.clip(r - NAT_KH // 2, 0, ROWS - NAT_KH)
        win = pl.ds(pl.multiple_of(r0 * GRID_W, GRID_W), NAT_KH * GRID_W)
        q = q_ref[i * GRID_W:(i + 1) * GRID_W, :] * (HEAD_DIM ** -0.5)
        qs = _stack_heads(q, H_NAT, HEAD_DIM)
        s_loc = _dot_nt(qs, k_ref[win, :]) + bias_ref[r - r0]
        s_ctx = _dot_nt(qs, kc)
        o = _softmax_pv([s_loc, s_ctx], [v_ref[win, :], vc])
        o_ref[i * GRID_W:(i + 1) * GRID_W, :] = _unstack_heads(o, H_NAT, HEAD_DIM).astype(BF16)


def _nat_latent(pnat, kc, vc, bias):
    tq = NAT_ROWS_PER_STEP * GRID_W
    nq = DEC_SEQ // tq
    lat_blk = N_CTX // DEC_SEQ
    return pl.pallas_call(
        _nat_lat_kernel,
        grid=(DEC_BATCH, nq),
        in_specs=[pl.BlockSpec((tq, NAT_W), lambda b, j: (N_CTX // tq + b * nq + j, 0)),
                  pl.BlockSpec((DEC_SEQ, NAT_W), lambda b, j: (lat_blk + b, 1)),
                  pl.BlockSpec((DEC_SEQ, NAT_W), lambda b, j: (lat_blk + b, 2)),
                  pl.BlockSpec((1, PAST_LEN, NAT_W), lambda b, j: (b, 0, 0)),
                  pl.BlockSpec((1, PAST_LEN, NAT_W), lambda b, j: (b, 0, 0)),
                  pl.BlockSpec(bias.shape, lambda b, j: (0, 0, 0))],
        out_specs=pl.BlockSpec((tq, NAT_W), lambda b, j: (b * nq + j, 0)),
        out_shape=jax.ShapeDtypeStruct((N_LAT, NAT_W), BF16),
        compiler_params=pltpu.CompilerParams(dimension_semantics=("arbitrary", "arbitrary"),
                                             vmem_limit_bytes=VMEM_LIMIT),
        name="nat_latent",
    )(pnat, pnat, pnat, kc, vc, bias)


def _route_gates(logits_t, bias_ref):
    per = N_EXPERTS // N_GROUPS
    t = logits_t.shape[1]
    neg = -jnp.inf
    pos = lax.broadcasted_iota(jnp.int32, (per, t), 0)
    scores = [_sigmoid(logits_t[g * per:(g + 1) * per, :]) for g in range(N_GROUPS)]
    biased = [scores[g] + bias_ref[g * per:(g + 1) * per, :] for g in range(N_GROUPS)]

    grp = []
    for v in biased:
        m1 = jnp.max(v, axis=0, keepdims=True)
        i1 = jnp.min(jnp.where(v == m1, pos, per), axis=0, keepdims=True)
        m2 = jnp.max(jnp.where(pos == i1, neg, v), axis=0, keepdims=True)
        grp.append(m1 + m2)

    keep = [jnp.zeros((1, t), jnp.bool_) for _ in range(N_GROUPS)]
    for _ in range(TOPK_GROUPS):
        best = functools.reduce(jnp.maximum, grp)
        first = jnp.full((1, t), N_GROUPS, jnp.int32)
        for g in reversed(range(N_GROUPS)):
            first = jnp.where(grp[g] == best, g, first)
        for g in range(N_GROUPS):
            hit = first == g
            keep[g] = keep[g] | hit
            grp[g] = jnp.where(hit, neg, grp[g])

    cand = [jnp.where(keep[g], biased[g], neg) for g in range(N_GROUPS)]
    flat = [pos + g * per for g in range(N_GROUPS)]
    picks, weights = [], []
    for _ in range(TOP_K):
        best = functools.reduce(jnp.maximum, [jnp.max(v, axis=0, keepdims=True) for v in cand])
        first = functools.reduce(jnp.minimum, [
            jnp.min(jnp.where(cand[g] == best, flat[g], N_EXPERTS), axis=0, keepdims=True)
            for g in range(N_GROUPS)])
        score = jnp.zeros((1, t), F32)
        for g in range(N_GROUPS):
            hit = flat[g] == first
            score = score + jnp.sum(jnp.where(hit, scores[g], 0.0), axis=0, keepdims=True)
            cand[g] = jnp.where(hit, neg, cand[g])
        picks.append(first)
        weights.append(score)

    total = functools.reduce(lambda a, b: a + b, weights)
    gates = jnp.concatenate([v / total * ROUTE_SCALE for v in weights], axis=0)

    sel = [functools.reduce(lambda a, b: a | b, [flat[g] == k for k in picks]) for g in range(N_GROUPS)]
    sel = jnp.where(jnp.concatenate(sel, axis=0), 1.0, 0.0).astype(BF16)
    before = (lax.broadcasted_iota(jnp.int32, (t, t), 0) < lax.broadcasted_iota(jnp.int32, (t, t), 1))
    rank = jnp.dot(sel, jnp.where(before, 1.0, 0.0).astype(BF16), preferred_element_type=F32)
    count = jnp.sum(sel.astype(F32), axis=1, keepdims=True)
    pieces = jnp.floor((count + (PIECE - 1)) * (1.0 / PIECE))
    lower = (lax.broadcasted_iota(jnp.int32, (N_EXPERTS, N_EXPERTS), 1)
             < lax.broadcasted_iota(jnp.int32, (N_EXPERTS, N_EXPERTS), 0))
    start = PIECE * jnp.dot(jnp.where(lower, 1.0, 0.0).astype(BF16),
                            jnp.broadcast_to(pieces, (N_EXPERTS, LANES)).astype(BF16),
                            preferred_element_type=F32)[:, 0:1]
    row = rank + start
    rows = []
    for k in picks:
        r = jnp.zeros((1, t), F32)
        for g in range(N_GROUPS):
            r = r + jnp.sum(jnp.where(flat[g] == k, row[g * per:(g + 1) * per, :], 0.0), axis=0, keepdims=True)
        rows.append(r)
    return (jnp.concatenate(rows, axis=0).astype(jnp.int32), gates,
            jnp.broadcast_to(pieces, (N_EXPERTS, LANES)).astype(jnp.int32))


def _outproj_kernel(x_ref, ofc_ref, ofl_ref, obc_ref, obl_ref, gg_ref, onatc_ref, onatl_ref, ogqac_ref, ogqal_ref,
                    w_ref, ng_ref, g1_ref, lng_ref, lnb_ref, sh2_ref, sc2_ref, wr_ref, rb_ref, x1_ref, xm_ref,
                    row_ref, gate_ref, pieces_ref):
    ci = _cond_row(pl.program_id(0), TM_PROJ)
    is_ctx = pl.program_id(0) < N_CTX // TM_PROJ
    either = lambda c_ref, l_ref: jnp.where(is_ctx, c_ref[...], l_ref[...])
    og = either(ofc_ref, ofl_ref) + either(obc_ref, obl_ref)
    halves = [og[:, j * LANES:(j + 1) * LANES] for j in range(GLA_W // LANES)]
    ms = jnp.concatenate([_head_mean_sq(h) for h in halves], axis=1)
    ogla = og * lax.rsqrt(ms + EPS) * ng_ref[...] * _silu(gg_ref[...])
    mix = (_dot(ogla, w_ref[0:GLA_W, :]) + _dot(either(onatc_ref, onatl_ref), w_ref[GLA_W:GLA_W + NAT_W, :])
           + _dot(either(ogqac_ref, ogqal_ref), w_ref[GLA_W + NAT_W:, :]))
    x1 = _ln(ALPHA * x_ref[...] + g1_ref[pl.ds(ci, 1), :] * mix) * lng_ref[...] + lnb_ref[...]
    x1_ref[...] = x1
    xm = (_ln(x1) * (1.0 + sc2_ref[pl.ds(ci, 1), :]) + sh2_ref[pl.ds(ci, 1), :]).astype(BF16)
    xm_ref[...] = xm
    row_ref[...], gate_ref[...], pieces_ref[...] = _route_gates(_dot_nt(wr_ref[...], xm), rb_ref)


def _out_proj(x, of, ob, pgla, onat, ogqa, w, ng, g1, lng, lnb, sh2, sc2, wr, rb):
    tm = TM_PROJ
    n_ctx = N_CTX // tm
    full = lambda a: pl.BlockSpec(a.shape, lambda i: (0,) * a.ndim)
    rows = lambda w_: pl.BlockSpec((tm, w_), lambda i: (i, 0))
    ctx = lambda w_: pl.BlockSpec((tm, w_), lambda i: (jnp.minimum(i, n_ctx - 1), 0))
    lat = lambda w_: pl.BlockSpec((tm, w_), lambda i: (jnp.maximum(i - n_ctx, 0), 0))
    return pl.pallas_call(
        _outproj_kernel,
        grid=(N_TOK // tm,),
        in_specs=[rows(D_MODEL), ctx(GLA_W), lat(GLA_W), ctx(GLA_W), lat(GLA_W),
                  pl.BlockSpec((tm, GLA_W), lambda i: (i, 3)),
                  ctx(NAT_W), lat(NAT_W), ctx(GQA_QW), lat(GQA_QW), full(w), full(ng), full(g1), full(lng),
                  full(lnb), full(sh2), full(sc2), full(wr), full(rb)],
        out_specs=[rows(D_MODEL), rows(D_MODEL), pl.BlockSpec((TOP_K, tm), lambda i: (0, i)),
                   pl.BlockSpec((TOP_K, tm), lambda i: (0, i)),
                   pl.BlockSpec((N_EXPERTS, LANES), lambda i: (0, i))],
        out_shape=[jax.ShapeDtypeStruct((N_TOK, D_MODEL), F32),
                   jax.ShapeDtypeStruct((N_TOK, D_MODEL), BF16),
                   jax.ShapeDtypeStruct((TOP_K, N_TOK), jnp.int32),
                   jax.ShapeDtypeStruct((TOP_K, N_TOK), F32),
                   jax.ShapeDtypeStruct((N_EXPERTS, MOE_NT * LANES), jnp.int32)],
        compiler_params=pltpu.CompilerParams(dimension_semantics=("arbitrary",),
                                             vmem_limit_bytes=VMEM_LIMIT),
        name="out_proj",
    )(x, *of, *ob, pgla, *onat, *ogqa, w, ng, g1, lng, lnb, sh2, sc2, wr, rb)


def _plan_specs(n_prefetch, **kw):
    return pltpu.PrefetchScalarGridSpec(num_scalar_prefetch=n_prefetch, **kw)


def _for_each_copy(t, copies, fn):
    bsrc_ref, bdst_ref, ssrc_ref, sdst_ref, nbig_ref, nsmall_ref = copies

    def big(i, lane):
        j = t * MAX_BIG + i
        fn(pl.multiple_of(bsrc_ref[j], PIECE), pl.multiple_of(bdst_ref[j], PIECE), COPY_ROWS, lane)

    def small(i, lane):
        j = t * N_EXPERTS + i
        fn(pl.multiple_of(ssrc_ref[j], PIECE), pl.multiple_of(sdst_ref[j], PIECE), PIECE, lane)

    _unrolled_loop(nbig_ref[t], big)
    _unrolled_loop(nsmall_ref[t], small)


def _unrolled_loop(n, body):
    def group(g, c):
        for u in range(LOOP_UNROLL):
            body(g * LOOP_UNROLL + u, u % 2)
        return c

    def single(i, c):
        body(i, 0)
        return c

    n_group = n // LOOP_UNROLL
    lax.fori_loop(0, n_group, group, 0)
    lax.fori_loop(n_group * LOOP_UNROLL, n, single, 0)


def _wait_copies(n_big, n_small, make):
    _unrolled_loop(n_big, lambda i, lane: make(COPY_ROWS).wait())
    _unrolled_loop(n_small, lambda i, lane: make(PIECE).wait())


def _dispatch_kernel(bsrc_ref, bdst_ref, ssrc_ref, sdst_ref, nbig_ref, nsmall_ref, zdst_ref, tot_ref, xm_ref, ldt_ref,
                     xs_ref, stage_ref, zero_ref, sem_big, sem_small, tail_sem):
    copies = (bsrc_ref, bdst_ref, ssrc_ref, sdst_ref, nbig_ref, nsmall_ref)
    t = pl.program_id(0)
    nt = pl.num_programs(0)
    slot = lax.rem(t, 2)

    def run_copy(local_row, xs_row, rows, s):
        sem = sem_big if rows == COPY_ROWS else sem_small
        return pltpu.make_async_copy(stage_ref.at[s, pl.ds(local_row, rows)], xs_ref.at[pl.ds(xs_row, rows)],
                                     sem.at[s])

    def wait_tile(tt, s):
        _wait_copies(nbig_ref[tt], nsmall_ref[tt], lambda rows: run_copy(0, 0, rows, s))

    n_piece = tot_ref[1]
    first_free_tile = tot_ref[0] // ROW_TILE
    n_tile = XS_ROWS // ROW_TILE - first_free_tile

    def zero_piece(i):
        return pltpu.make_async_copy(zero_ref.at[pl.ds(0, PIECE)],
                                     xs_ref.at[pl.ds(pl.multiple_of(zdst_ref[i], PIECE), PIECE)], tail_sem.at[0])

    def zero_tile(i):
        row = pl.multiple_of((first_free_tile + i) * ROW_TILE, ROW_TILE)
        return pltpu.make_async_copy(zero_ref, xs_ref.at[pl.ds(row, ROW_TILE)], tail_sem.at[1])

    def run(n, make, op):
        _unrolled_loop(n, lambda i, lane: op(make(i)))

    @pl.when(t == 0)
    def _():
        zero_ref[...] = jnp.zeros_like(zero_ref)
        run(n_piece, zero_piece, lambda cp: cp.start())
        run(n_tile, zero_tile, lambda cp: cp.start())

    @pl.when(t >= 2)
    def _():
        wait_tile(t - 2, slot)

    xm = xm_ref[...]
    ld = ldt_ref[...].astype(jnp.int16)
    for c in range(MOE_CAP // MOE_CHUNK):
        p = (lax.broadcasted_iota(jnp.int32, (MOE_CHUNK, MOE_TILE), 0) + c * MOE_CHUNK).astype(jnp.int16)
        hit = p == ld[0:1, :]
        for k in range(1, TOP_K):
            hit = hit | (p == ld[k:k + 1, :])
        stage_ref[slot, c * MOE_CHUNK:(c + 1) * MOE_CHUNK, :] = jnp.dot(
            jnp.where(hit, jnp.ones((), BF16), jnp.zeros((), BF16)), xm, preferred_element_type=F32).astype(BF16)
    _for_each_copy(t, copies, lambda a, b, rows, lane: run_copy(a, b, rows, slot).start(priority=lane))

    @pl.when(t == nt - 1)
    def _():
        run(n_piece, zero_piece, lambda cp: cp.wait())
        run(n_tile, zero_tile, lambda cp: cp.wait())
        wait_tile(t - 1, 1 - slot)
        wait_tile(t, slot)


def _moe_dispatch(plan, xm, row_t):
    return pl.pallas_call(
        _dispatch_kernel,
        grid_spec=_plan_specs(
            8, grid=(MOE_NT,),
            in_specs=[pl.BlockSpec((MOE_TILE, D_MODEL), lambda t, *_: (t, 0)),
                      pl.BlockSpec((TOP_K, MOE_TILE), lambda t, *_: (0, t))],
            out_specs=pl.BlockSpec(memory_space=pl.ANY),
            scratch_shapes=[pltpu.VMEM((2, MOE_CAP, D_MODEL), BF16), pltpu.VMEM((ROW_TILE, D_MODEL), BF16),
                            pltpu.SemaphoreType.DMA((2,)), pltpu.SemaphoreType.DMA((2,)),
                            pltpu.SemaphoreType.DMA((2,))]),
        out_shape=jax.ShapeDtypeStruct((XS_ROWS, D_MODEL), BF16),
        compiler_params=pltpu.CompilerParams(dimension_semantics=("arbitrary",),
                                             vmem_limit_bytes=VMEM_LIMIT),
        name="moe_dispatch",
    )(*plan["copies"], plan["zero_dst"], plan["total"], xm, row_t)


def _experts_kernel(exp_ref, fresh_ref, used_ref, src_ref, xs_ref, wgu_ref, wd_ref, y_ref, wgu_bf_ref, wd_bf_ref):
    w = pl.program_id(0)

    @pl.when(fresh_ref[w] == 1)
    def _():
        wgu_bf_ref[...] = wgu_ref[0].astype(BF16)
        wd_bf_ref[...] = wd_ref[0].astype(BF16)

    @pl.when(used_ref[w] == 1)
    def _():
        ab = jnp.dot(xs_ref[...], wgu_bf_ref[...], preferred_element_type=F32)
        h = (_silu(ab[:, :D_EXPERT]) * ab[:, D_EXPERT:]).astype(BF16)
        y_ref[...] = jnp.dot(h, wd_bf_ref[...], preferred_element_type=F32).astype(BF16)

    @pl.when(used_ref[w] == 0)
    def _():
        y_ref[...] = jnp.zeros_like(y_ref)


def _moe_experts(layer, plan, xs, wgu, wd):
    return pl.pallas_call(
        _experts_kernel,
        grid_spec=_plan_specs(
            4, grid=(XS_ROWS // ROW_TILE,),
            in_specs=[pl.BlockSpec((ROW_TILE, D_MODEL), lambda w, ex, fresh, used, src: (src[w], 0)),
                      pl.BlockSpec((None, 1, D_MODEL, 2 * D_EXPERT), lambda w, ex, *_: (layer, ex[w], 0, 0)),
                      pl.BlockSpec((None, 1, D_EXPERT, D_MODEL), lambda w, ex, *_: (layer, ex[w], 0, 0))],
            out_specs=pl.BlockSpec((ROW_TILE, D_MODEL), lambda w, *_: (w, 0)),
            scratch_shapes=[pltpu.VMEM((D_MODEL, 2 * D_EXPERT), BF16), pltpu.VMEM((D_EXPERT, D_MODEL), BF16)]),
        out_shape=jax.ShapeDtypeStruct((XS_ROWS, D_MODEL), BF16),
        compiler_params=pltpu.CompilerParams(dimension_semantics=("arbitrary",),
                                             vmem_limit_bytes=VMEM_LIMIT),
        name="moe_experts",
    )(plan["tile_expert"], plan["tile_fresh"], plan["tile_used"], plan["tile_src"], xs, wgu, wd)


def _combine_kernel(bsrc_ref, bdst_ref, ssrc_ref, sdst_ref, nbig_ref, nsmall_ref, y_ref, ldt_ref, gate_ref, xm_ref,
                    x1_ref, g2_ref, wsgu_ref, wsd_ref, lng_ref, lnb_ref, out_ref, ybuf_ref, sem_big, sem_small):
    copies = (bsrc_ref, bdst_ref, ssrc_ref, sdst_ref, nbig_ref, nsmall_ref)
    t = pl.program_id(0)
    nt = pl.num_programs(0)
    slot = lax.rem(t, 2)
    ci = _cond_row(t, MOE_TILE)

    def run_copy(local_row, y_row, rows, s):
        sem = sem_big if rows == COPY_ROWS else sem_small
        return pltpu.make_async_copy(y_ref.at[pl.ds(y_row, rows)], ybuf_ref.at[s, pl.ds(local_row, rows)], sem.at[s])

    def fetch(tt, s):
        _for_each_copy(tt, copies, lambda a, b, rows, lane: run_copy(a, b, rows, s).start(priority=lane))

    @pl.when(t == 0)
    def _():
        ybuf_ref[...] = jnp.zeros_like(ybuf_ref)
        fetch(0, 0)

    @pl.when(t + 1 < nt)
    def _():
        fetch(t + 1, 1 - slot)

    _wait_copies(nbig_ref[t], nsmall_ref[t], lambda rows: run_copy(0, 0, rows, slot))

    routed = jnp.zeros((MOE_TILE, D_MODEL), F32)
    ld = ldt_ref[...].astype(jnp.int16)
    gate = gate_ref[...].astype(BF16)
    for c in range(MOE_CAP // MOE_CHUNK):
        p = (lax.broadcasted_iota(jnp.int32, (MOE_CHUNK, MOE_TILE), 0) + c * MOE_CHUNK).astype(jnp.int16)
        wmat = jnp.zeros((MOE_CHUNK, MOE_TILE), BF16)
        for k in range(TOP_K):
            wmat = jnp.where(p == ld[k:k + 1, :], gate[k:k + 1, :], wmat)
        routed = routed + _dot_tn(wmat, ybuf_ref[slot, c * MOE_CHUNK:(c + 1) * MOE_CHUNK, :])

    sab = jnp.dot(xm_ref[...], wsgu_ref[...], preferred_element_type=F32)
    shared = _dot(_silu(sab[:, :D_SHARED]) * sab[:, D_SHARED:], wsd_ref[...])
    z = ALPHA * x1_ref[...] + g2_ref[pl.ds(ci, 1), :] * (routed + shared)
    out_ref[...] = _ln(z) * lng_ref[...] + lnb_ref[...]


def _moe_combine(plan, y, row_t, gate_t, xm, x1, g2, wsgu, wsd, lng, lnb):
    full = lambda a: pl.BlockSpec(a.shape, lambda t, *_: (0,) * a.ndim)
    rows = lambda w_: pl.BlockSpec((MOE_TILE, w_), lambda t, *_: (t, 0))
    picks = pl.BlockSpec((TOP_K, MOE_TILE), lambda t, *_: (0, t))
    return pl.pallas_call(
        _combine_kernel,
        grid_spec=_plan_specs(
            6, grid=(MOE_NT,),
            in_specs=[pl.BlockSpec(memory_space=pl.ANY), picks, picks, rows(D_MODEL), rows(D_MODEL),
                      full(g2), full(wsgu), full(wsd), full(lng), full(lnb)],
            out_specs=rows(D_MODEL),
            scratch_shapes=[pltpu.VMEM((2, MOE_CAP, D_MODEL), BF16), pltpu.SemaphoreType.DMA((2,)),
                            pltpu.SemaphoreType.DMA((2,))]),
        out_shape=jax.ShapeDtypeStruct((N_TOK, D_MODEL), F32),
        compiler_params=pltpu.CompilerParams(dimension_semantics=("arbitrary",),
                                             vmem_limit_bytes=VMEM_LIMIT),
        name="moe_combine",
    )(*plan["copies"], y, row_t, gate_t, xm, x1, g2, wsgu, wsd, lng, lnb)


def _moe_plan(npc):
    i32 = jnp.int32
    run = npc * PIECE
    lo = jnp.cumsum(run, axis=1) - run
    tot_e = jnp.sum(run, axis=0)
    region = (tot_e + ROW_TILE - 1) // ROW_TILE * ROW_TILE
    region_end = jnp.cumsum(region)
    off = region_end - region
    hs = off[None, :] + jnp.cumsum(run, axis=0) - run

    per_copy = COPY_ROWS // PIECE
    n_big = npc // per_copy
    odd = npc - n_big * per_copy

    def expand(count, width):
        end = jnp.cumsum(count, axis=1)
        i = jnp.arange(width, dtype=i32)[None, :, None]
        mine = (i >= (end - count)[:, None, :]) & (i < end[:, None, :])
        pick = lambda a: jnp.sum(jnp.where(mine, a[:, None, :], 0), axis=2)
        return pick, i[:, :, 0] - pick(end - count)

    pick_b, k_b = expand(n_big, MAX_BIG)
    pick_s, _ = expand(odd, N_EXPERTS)
    copies = [pick_b(lo) + k_b * COPY_ROWS, pick_b(hs) + k_b * COPY_ROWS,
              pick_s(lo + n_big * COPY_ROWS), pick_s(hs + n_big * COPY_ROWS)]
    copies = [c.reshape(-1).astype(i32) for c in copies] + [jnp.sum(n_big, axis=1).astype(i32),
                                                            jnp.sum(odd, axis=1).astype(i32)]

    n_zero = ((region - tot_e) // PIECE)[None, :]
    pick_z, k_z = expand(n_zero, MAX_ZERO)
    zero_dst = (pick_z((off + tot_e)[None, :]) + k_z * PIECE).reshape(-1).astype(i32)

    start = jnp.arange(XS_ROWS // ROW_TILE, dtype=i32) * ROW_TILE
    ex = jnp.minimum(jnp.sum((start[:, None] >= region_end[None, :]).astype(i32), axis=1), N_EXPERTS - 1)
    prev_ex = jnp.concatenate([jnp.full((1,), -1, i32), ex[:-1]])
    return dict(copies=copies, zero_dst=zero_dst,
                total=jnp.stack([region_end[-1], jnp.sum(n_zero)]).astype(i32),
                tile_expert=ex.astype(i32), tile_fresh=(ex != prev_ex).astype(i32),
                tile_used=(start < region_end[-1]).astype(i32),
                tile_src=(jnp.minimum(start, region_end[-1] - ROW_TILE) // ROW_TILE).astype(i32))


def _rope_tables():
    t = jnp.arange(DEC_SEQ)
    n_freq = HEAD_DIM // 4
    freqs = ROPE_THETA ** (-jnp.arange(n_freq, dtype=F32) / n_freq)
    ang = jnp.concatenate([(t // GRID_W).astype(F32)[:, None] * freqs,
                           (t % GRID_W).astype(F32)[:, None] * freqs], -1)
    cos, sin = jnp.cos(ang), jnp.sin(ang)
    cos_h = jnp.concatenate([cos, cos], -1)
    sin_h = jnp.concatenate([-sin, sin], -1)
    lat = lambda a: jnp.tile(a, (DEC_BATCH, LANES // HEAD_DIM))
    cos_t = jnp.concatenate([jnp.ones((N_CTX, LANES), F32), lat(cos_h)], 0)
    sin_t = jnp.concatenate([jnp.zeros((N_CTX, LANES), F32), lat(sin_h)], 0)
    return cos_t, sin_t


def _nat_bias_table(rpb):
    n_dr, n_dc = 2 * NAT_KH - 1, 2 * NAT_KW - 1
    cidx = np.arange(GRID_W)
    dc_idx = np.clip(cidx[None, :] - cidx[:, None] + NAT_KW - 1, 0, n_dc - 1)
    col_start = np.clip(cidx - NAT_KW // 2, 0, GRID_W - NAT_KW)
    col_in = (cidx[None, :] >= col_start[:, None]) & (cidx[None, :] < col_start[:, None] + NAT_KW)
    onehot = (dc_idx.reshape(1, -1) == np.arange(n_dc)[:, None]).astype(np.float32)
    t = jnp.dot(rpb.reshape(H_NAT * n_dr, n_dc), onehot, precision=lax.Precision.HIGHEST)
    t = jnp.where(col_in[None, None], t.reshape(H_NAT, n_dr, GRID_W, GRID_W), -jnp.inf)
    bias = jnp.stack([t[:, NAT_KH - 1 - p:2 * NAT_KH - 1 - p] for p in range(NAT_KH)], axis=0)
    return bias.transpose(0, 1, 3, 2, 4).reshape(NAT_KH, H_NAT * GRID_W, NAT_KH * GRID_W)


def _prep_w_in(w_in):
    cuts = np.cumsum((0,) + IN_SPLITS)
    seg = [w_in[:, cuts[i]:cuts[i + 1]] for i in range(len(IN_SPLITS))]
    gq, gk, gv, gg, ga, nq, nk, nv, aq, ak, av = seg
    aq = aq.reshape(D_MODEL, H_GQA, HEAD_DIM)[:, np.array(GQA_SLOT_HEADS), :].reshape(D_MODEL, GQA_QW)
    ga = jnp.pad(ga, ((0, 0), (0, LANES - 2 * GLA_LOWRANK)))
    return jnp.concatenate([gq, gk, gv, gg, nq, nk, nv, aq, ak, av, ga], axis=1).astype(BF16)


def _prep_w_a(w_a2, b_a):
    wa = jnp.zeros((LANES, 2 * GLA_W), F32)
    wa = wa.at[0:GLA_LOWRANK, 0:GLA_W].set(w_a2[0])
    wa = wa.at[GLA_LOWRANK:2 * GLA_LOWRANK, GLA_W:].set(w_a2[1])
    return wa.astype(BF16), b_a.reshape(1, 2 * GLA_W)


def _prep_w_out(w_out):
    gqa = w_out[GLA_W + NAT_W:].reshape(H_GQA, HEAD_DIM, D_MODEL)[np.array(GQA_SLOT_HEADS)]
    return jnp.concatenate([w_out[:GLA_W + NAT_W], gqa.reshape(GQA_QW, D_MODEL)], 0).astype(BF16)


def _state_to_blockdiag(s):
    out = jnp.zeros(s.shape[:-3] + (GLA_W, GLA_W), F32)
    for h in range(H_GLA):
        out = out.at[..., h * GLA_DV:(h + 1) * GLA_DV, h * GLA_DK:(h + 1) * GLA_DK].set(
            jnp.swapaxes(s[..., h, :, :], -1, -2))
    return out


def _blockdiag_to_state(st):
    blocks = [st[..., h * GLA_DV:(h + 1) * GLA_DV, h * GLA_DK:(h + 1) * GLA_DK] for h in range(H_GLA)]
    return jnp.swapaxes(jnp.stack(blocks, axis=-3), -1, -2)


def kernel(x_prompt, x_sample, state_gla, cache_nat_k, cache_nat_v, cache_gqa_k, cache_gqa_v, c, c_ctx, w_mod, b_mod, w_in, gla_w_a2, gla_b_a, gla_norm_g, nat_rpb, gqa_q_norm_g, gqa_k_norm_g, w_out, ln1_g, ln1_b, w_router, router_bias, w_expert_gu, w_expert_down, w_shared_gu, w_shared_down, ln2_g, ln2_b):
    x = jnp.concatenate([x_prompt.reshape(N_CTX, D_MODEL), x_sample.reshape(N_LAT, D_MODEL)], axis=0)
    cond = jnp.concatenate([c_ctx[None, :], c, jnp.zeros((N_COND - 1 - DEC_BATCH, D_MODEL), F32)], axis=0)
    mods = _modulation(cond, w_mod, b_mod)
    cos_t, sin_t = _rope_tables()
    row = lambda a: a.reshape(1, -1)

    st_gla, st_nk, st_nv, st_gk, st_gv = [], [], [], [], []
    for l in range(DEPTH):
        sh1, sc1, g1, sh2, sc2, g2 = [mods[l, :, j * D_MODEL:(j + 1) * D_MODEL] for j in range(6)]
        wa, ba = _prep_w_a(gla_w_a2[l], gla_b_a[l])
        qg = row(jnp.tile(gqa_q_norm_g[l], LANES // HEAD_DIM))
        kg = row(jnp.tile(gqa_k_norm_g[l], LANES // HEAD_DIM))
        pgla, la, pnat, q, k, v = _in_proj(x, sh1, sc1, _prep_w_in(w_in[l]), wa, ba, qg, kg, cos_t, sin_t)

        zero_st = jnp.zeros((BATCH, 2, GLA_W, GLA_W), F32)
        of_c, ob_c, st_c = _gla(pgla, la, zero_st, 0, BATCH, SEQ)
        of_l, ob_l, _ = _gla(pgla, la, _state_to_blockdiag(state_gla[:, l]), N_CTX, DEC_BATCH, DEC_SEQ)

        onat_c, ogqa_c = _ctx_attention(pnat, q, k, v)
        k_all = jnp.concatenate([k[N_CTX:].reshape(DEC_BATCH, DEC_SEQ, GQA_KW),
                                 cache_gqa_k[:, l].reshape(DEC_BATCH, PAST_LEN, GQA_KW)], axis=1).astype(BF16)
        v_all = jnp.concatenate([v[N_CTX:].reshape(DEC_BATCH, DEC_SEQ, GQA_KW),
                                 cache_gqa_v[:, l].reshape(DEC_BATCH, PAST_LEN, GQA_KW)], axis=1).astype(BF16)
        ogqa_l = _gqa_latent(q, k_all, v_all)
        onat_l = _nat_latent(pnat, cache_nat_k[:, l].reshape(DEC_BATCH, PAST_LEN, NAT_W),
                             cache_nat_v[:, l].reshape(DEC_BATCH, PAST_LEN, NAT_W),
                             _nat_bias_table(nat_rpb[l]))

        x1, xm, row_t, gate_t, pieces = _out_proj(
            x, (of_c, of_l), (ob_c, ob_l), pgla, (onat_c, onat_l), (ogqa_c, ogqa_l),
            _prep_w_out(w_out[l]), row(jnp.tile(gla_norm_g[l], H_GLA)), g1, row(ln1_g[l]), row(ln1_b[l]),
            sh2, sc2, w_router[l].T.astype(BF16), router_bias[l].reshape(N_EXPERTS, 1))
        plan = _moe_plan(pieces[:, ::LANES].T)
        y = _moe_experts(l, plan, _moe_dispatch(plan, xm, row_t), w_expert_gu, w_expert_down)
        x = _moe_combine(plan, y, row_t, gate_t, xm, x1, g2, w_shared_gu[l].astype(BF16),
                         w_shared_down[l].astype(BF16), row(ln2_g[l]), row(ln2_b[l]))

        st_gla.append(_blockdiag_to_state(st_c))
        st_nk.append(pnat[:N_CTX, NAT_W:2 * NAT_W].astype(F32).reshape(BATCH, SEQ, H_NAT, HEAD_DIM))
        st_nv.append(pnat[:N_CTX, 2 * NAT_W:].astype(F32).reshape(BATCH, SEQ, H_NAT, HEAD_DIM))
        st_gk.append(k[:N_CTX].reshape(BATCH, SEQ, KV_GQA, HEAD_DIM))
        st_gv.append(v[:N_CTX].reshape(BATCH, SEQ, KV_GQA, HEAD_DIM))

    y_prompt = x[:N_CTX].reshape(BATCH, SEQ, D_MODEL)
    y_sample = x[N_CTX:].reshape(DEC_BATCH, DEC_SEQ, D_MODEL)
    return (y_prompt, y_sample, jnp.stack(st_gla, axis=1), jnp.stack(st_nk, axis=1), jnp.stack(st_nv, axis=1),
            jnp.stack(st_gk, axis=1), jnp.stack(st_gv, axis=1))
```
